```python
import math
import jax, jax.numpy as jnp
from jax import lax
import numpy as np

D_MODEL = 1024
BATCH = 8
SEQ = 4096
DEPTH = 2

SB_HEADS = 8
SB_HEAD_DIM = 64
SB_WIDTH = SB_HEADS * SB_HEAD_DIM
BLOCK_Q = 128
LRU_WIDTH = 512
LRU_BLOCKS = 8
LRU_BLOCK = LRU_WIDTH // LRU_BLOCKS
LRU_CONV = 4
LRU_C = 8.0
SC_WIDTH = 512
SC_CONV = 3
N_BRANCH = 3
N_GROUPS = 4
EXP_PER_GROUP = 4
N_EXPERTS = N_GROUPS * EXP_PER_GROUP
TOP_K = 2
D_EXPERT = 512
ALPHA = (2 * DEPTH) ** 0.25
BETA = (8 * DEPTH) ** -0.25
LN_EPS = 1e-5

PROJ_SPLITS = [SB_WIDTH] * 3 + [LRU_WIDTH] * 2 + [SC_WIDTH] * 3 + [D_MODEL] * N_BRANCH
PROJ_COLS = sum(PROJ_SPLITS)
SPLIT_POINTS = [sum(PROJ_SPLITS[:i + 1]) for i in range(len(PROJ_SPLITS) - 1)]

kernel_name = "hybrid_sb_rglru_shortconv_hmoe_deepnorm"


def layer_norm(x, g, b):
    xf = x.astype(jnp.float32)
    mu = jnp.mean(xf, axis=-1, keepdims=True)
    var = jnp.mean(jnp.square(xf - mu), axis=-1, keepdims=True)
    y = (xf - mu) * lax.rsqrt(var + LN_EPS) * g.astype(jnp.float32) + b.astype(jnp.float32)
    return y.astype(x.dtype)


def causal_dwconv(u, w):
    K = w.shape[0]
    S = u.shape[1]
    up = jnp.pad(u, ((0, 0), (K - 1, 0), (0, 0)))
    return sum(w[k] * up[:, k:k + S] for k in range(K))


def stick_breaking_attention(q, k, v):
    B, S, H, Dh = q.shape
    nb = S // BLOCK_Q
    qf = q.astype(jnp.float32) * (Dh ** -0.5)
    kf = k.astype(jnp.float32)
    vf = v.astype(jnp.float32)
    q_blocks = qf.reshape(B, nb, BLOCK_Q, H, Dh).transpose(1, 0, 3, 2, 4)
    starts = jnp.arange(nb, dtype=jnp.int32) * BLOCK_Q
    key_pos = jnp.arange(S, dtype=jnp.int32)

    def one_block(args):
        qb, start = args
        z = jnp.einsum('bhqd,bkhd->bhqk', qb, kf)
        q_pos = start + jnp.arange(BLOCK_Q, dtype=jnp.int32)
        before = key_pos[None, :] < q_pos[:, None]
        log_keep = jnp.where(before, jax.nn.log_sigmoid(-z), 0.0)
        log_pass = lax.cumsum(log_keep, axis=3, reverse=True) - log_keep
        w = jnp.where(before, jnp.exp(jax.nn.log_sigmoid(z) + log_pass), 0.0)
        return jnp.einsum('bhqk,bkhd->bqhd', w, vf)

    o = lax.map(one_block, (q_blocks, starts))
    return o.transpose(1, 0, 2, 3, 4).reshape(B, S, H * Dh).astype(q.dtype)


def rg_lru(u, w_a, b_a, w_x, b_x, lam):
    B, S, W = u.shape
    uf = u.astype(jnp.float32)
    ub = uf.reshape(B, S, LRU_BLOCKS, LRU_BLOCK)
    r = jax.nn.sigmoid(jnp.einsum('bshi,hij->bshj', ub, w_a.astype(jnp.float32)).reshape(B, S, W)
                       + b_a.astype(jnp.float32))
    i = jax.nn.sigmoid(jnp.einsum('bshi,hij->bshj', ub, w_x.astype(jnp.float32)).reshape(B, S, W)
                       + b_x.astype(jnp.float32))
    log_a = -LRU_C * r * jax.nn.softplus(-lam.astype(jnp.float32))
    a = jnp.exp(log_a)
    drive = jnp.sqrt(-jnp.expm1(2.0 * log_a)) * (i * uf)

    def combine(left, right):
        a1, b1 = left
        a2, b2 = right
        return a1 * a2, a2 * b1 + b2

    _, h = lax.associative_scan(combine, (a, drive), axis=1)
    return h.astype(u.dtype)


def hybrid_mixer(x, w_in, gate_b, lru_conv_w, lru_conv_b, lru_wa, lru_ba, lru_wx, lru_bx,
                 lru_lambda, sc_conv_w, w_branch_sb, w_branch_lru, w_branch_sc, w_out):
    B, S, _ = x.shape
    zc = jnp.einsum('bsd,de->bse', x, w_in)
    (q, k, v, lru_in, lru_gate, sc_b, sc_c, sc_h,
     g_sb, g_lru, g_sc) = jnp.split(zc, SPLIT_POINTS, axis=-1)
    hs = (B, S, SB_HEADS, SB_HEAD_DIM)
    y_sb = stick_breaking_attention(q.reshape(hs), k.reshape(hs), v.reshape(hs))
    u = causal_dwconv(lru_in, lru_conv_w) + lru_conv_b
    y_lru = jax.nn.gelu(lru_gate) * rg_lru(u, lru_wa, lru_ba, lru_wx, lru_bx, lru_lambda)
    y_sc = sc_b * causal_dwconv(sc_c * sc_h, sc_conv_w)
    merged = (jax.nn.sigmoid(g_sb + gate_b[0]) * jnp.einsum('bsw,wd->bsd', y_sb, w_branch_sb)
              + jax.nn.sigmoid(g_lru + gate_b[1]) * jnp.einsum('bsw,wd->bsd', y_lru, w_branch_lru)
              + jax.nn.sigmoid(g_sc + gate_b[2]) * jnp.einsum('bsw,wd->bsd', y_sc, w_branch_sc))
    return jnp.einsum('bsd,de->bse', merged, w_out)


def hierarchical_moe(x, w_group, group_bias, w_expert_router, expert_bias, w_gate, w_up, w_down):
    B, S, D = x.shape
    xt = x.reshape(-1, D)
    N = xt.shape[0]
    g_logits = jnp.einsum('nd,dg->ng', xt, w_group).astype(jnp.float32)
    g_prob = jax.nn.softmax(g_logits, axis=-1)
    g_sel = jnp.argmax(g_logits + group_bias.astype(jnp.float32), axis=-1)
    e_logits = jnp.einsum('nd,de->ne', xt, w_expert_router).astype(jnp.float32)
    e_logits = e_logits.reshape(N, N_GROUPS, EXP_PER_GROUP)
    e_in_group = jnp.take_along_axis(e_logits, g_sel[:, None, None], axis=1)[:, 0]
    e_bias = expert_bias.astype(jnp.float32).reshape(N_GROUPS, EXP_PER_GROUP)[g_sel]
    _, top_idx = lax.top_k(e_in_group + e_bias, TOP_K)
    top_w = jax.nn.softmax(jnp.take_along_axis(e_in_group, top_idx, axis=1), axis=-1)
    top_w = top_w * jnp.take_along_axis(g_prob, g_sel[:, None], axis=1)
    expert_id = g_sel[:, None] * EXP_PER_GROUP + top_idx
    combine = jnp.sum(jax.nn.one_hot(expert_id, N_EXPERTS, dtype=jnp.float32) * top_w[..., None],
                      axis=1).astype(x.dtype)
    out = jnp.zeros((N, D), x.dtype)
    for grp in range(N_GROUPS):
        sl = slice(grp * EXP_PER_GROUP, (grp + 1) * EXP_PER_GROUP)
        h = (jax.nn.silu(jnp.einsum('nd,edf->nef', xt, w_gate[sl]))
             * jnp.einsum('nd,edf->nef', xt, w_up[sl]))
        out = out + jnp.einsum('nef,efd->nd', h * combine[:, sl, None], w_down[sl])
    return out.reshape(B, S, D)


def setup_inputs(seed: int = 0) -> dict:
    key = jax.random.key(seed)
    ks = iter(jax.random.split(key, 40))
    f32 = jnp.float32

    def nrm(shape, scale):
        return jax.random.normal(next(ks), shape, f32) * scale

    u = jax.random.uniform(next(ks), (DEPTH, LRU_WIDTH), f32, 0.9, 0.999)
    a0 = u ** (1.0 / LRU_C)
    lru_lambda = jnp.log(a0) - jnp.log1p(-a0)
    return {
        "x": nrm((BATCH, SEQ, D_MODEL), 1.0),
        "ln_in_g": 1.0 + nrm((D_MODEL,), 0.02),
        "ln_in_b": nrm((D_MODEL,), 0.02),
        "w_in": nrm((DEPTH, D_MODEL, PROJ_COLS), D_MODEL ** -0.5),
        "gate_b": nrm((DEPTH, N_BRANCH, D_MODEL), 0.1),
        "lru_conv_w": nrm((DEPTH, LRU_CONV, LRU_WIDTH), LRU_CONV ** -0.5),
        "lru_conv_b": nrm((DEPTH, LRU_WIDTH), 0.02),
        "lru_wa": nrm((DEPTH, LRU_BLOCKS, LRU_BLOCK, LRU_BLOCK), LRU_BLOCK ** -0.5),
        "lru_ba": nrm((DEPTH, LRU_WIDTH), 0.1),
        "lru_wx": nrm((DEPTH, LRU_BLOCKS, LRU_BLOCK, LRU_BLOCK), LRU_BLOCK ** -0.5),
        "lru_bx": nrm((DEPTH, LRU_WIDTH), 0.1),
        "lru_lambda": lru_lambda,
        "sc_conv_w": nrm((DEPTH, SC_CONV, SC_WIDTH), SC_CONV ** -0.5),
        "w_branch_sb": nrm((DEPTH, SB_WIDTH, D_MODEL), BETA * SB_WIDTH ** -0.5),
        "w_branch_lru": nrm((DEPTH, LRU_WIDTH, D_MODEL), BETA * LRU_WIDTH ** -0.5),
        "w_branch_sc": nrm((DEPTH, SC_WIDTH, D_MODEL), BETA * SC_WIDTH ** -0.5),
        "w_out": nrm((DEPTH, D_MODEL, D_MODEL), BETA * D_MODEL ** -0.5),
        "ln1_g": 1.0 + nrm((DEPTH, D_MODEL), 0.02),
        "ln1_b": nrm((DEPTH, D_MODEL), 0.02),
        "w_group": nrm((DEPTH, D_MODEL, N_GROUPS), D_MODEL ** -0.5),
        "group_bias": nrm((DEPTH, N_GROUPS), 0.01),
        "w_expert_router": nrm((DEPTH, D_MODEL, N_EXPERTS), D_MODEL ** -0.5),
        "expert_bias": nrm((DEPTH, N_EXPERTS), 0.01),
        "w_gate": nrm((DEPTH, N_EXPERTS, D_MODEL, D_EXPERT), D_MODEL ** -0.5),
        "w_up": nrm((DEPTH, N_EXPERTS, D_MODEL, D_EXPERT), D_MODEL ** -0.5),
        "w_down": nrm((DEPTH, N_EXPERTS, D_EXPERT, D_MODEL), BETA * D_EXPERT ** -0.5),
        "ln2_g": 1.0 + nrm((DEPTH, D_MODEL), 0.02),
        "ln2_b": nrm((DEPTH, D_MODEL), 0.02),
    }


def reference(x, ln_in_g, ln_in_b, w_in, gate_b, lru_conv_w, lru_conv_b, lru_wa, lru_ba,
              lru_wx, lru_bx, lru_lambda, sc_conv_w, w_branch_sb, w_branch_lru, w_branch_sc,
              w_out, ln1_g, ln1_b, w_group, group_bias, w_expert_router, expert_bias,
              w_gate, w_up, w_down, ln2_g, ln2_b):
    h = layer_norm(x, ln_in_g, ln_in_b)
    for l in range(DEPTH):
        mix = hybrid_mixer(h, w_in[l], gate_b[l], lru_conv_w[l], lru_conv_b[l], lru_wa[l],
                           lru_ba[l], lru_wx[l], lru_bx[l], lru_lambda[l], sc_conv_w[l],
                           w_branch_sb[l], w_branch_lru[l], w_branch_sc[l], w_out[l])
        h = layer_norm(ALPHA * h + mix, ln1_g[l], ln1_b[l])
        ffn = hierarchical_moe(h, w_group[l], group_bias[l], w_expert_router[l], expert_bias[l],
                               w_gate[l], w_up[l], w_down[l])
        h = layer_norm(ALPHA * h + ffn, ln2_g[l], ln2_b[l])
    return h
```

```python
import functools
import math

import jax
import jax.numpy as jnp
from jax import lax
from jax.experimental import pallas as pl
from jax.experimental.pallas import tpu as pltpu

F32 = jnp.float32
BF16 = jnp.bfloat16

SB_HEADS = 8
SB_HEAD_DIM = 64
SB_WIDTH = SB_HEADS * SB_HEAD_DIM
LRU_WIDTH = 512
LRU_BLOCKS = 8
LRU_C = 8.0
SC_WIDTH = 512
N_GROUPS = 4
EXP_PER_GROUP = 4
N_EXPERTS = N_GROUPS * EXP_PER_GROUP
D_EXPERT = 512
DEPTH = 2
ALPHA = (2 * DEPTH) ** 0.25
LN_EPS = 1e-5

LANES = 128
SUBLANES = 8
VMEM_LIMIT_BYTES = 56 * 1024 * 1024

LN_ROWS = 512
PROJ_ROWS = 512
PROJ_CHUNK = 512
ATT_Q = 256
ATT_K = 256
MIX_ROWS = 256
ROUTE_ROWS = 512
MOE_ROWS = 1024


def _cparams(sem):
    return pltpu.CompilerParams(dimension_semantics=sem, vmem_limit_bytes=VMEM_LIMIT_BYTES)


def _const_spec(shape):
    nd = len(shape)
    return pl.BlockSpec(shape, lambda *_: (0,) * nd)


def _layer_norm(y, g, b):
    mu = jnp.mean(y, axis=-1, keepdims=True)
    d = y - mu
    var = jnp.mean(d * d, axis=-1, keepdims=True)
    return d * lax.rsqrt(var + LN_EPS) * g + b


def _ln_kernel(x_ref, g_ref, b_ref, o_ref):
    o_ref[...] = _layer_norm(x_ref[...], g_ref[...], b_ref[...])


def _entry_ln(x2, g, b):
    n, d = x2.shape
    return pl.pallas_call(
        _ln_kernel,
        out_shape=jax.ShapeDtypeStruct((n, d), F32),
        grid=(n // LN_ROWS,),
        in_specs=[pl.BlockSpec((LN_ROWS, d), lambda i: (i, 0)),
                  _const_spec((1, d)), _const_spec((1, d))],
        out_specs=pl.BlockSpec((LN_ROWS, d), lambda i: (i, 0)),
        compiler_params=_cparams(("parallel",)),
        name="entry_ln",
    )(x2, g.reshape(1, d), b.reshape(1, d))


QKV_COLS = 3 * SB_WIDTH
BRANCH_COLS = 2 * LRU_WIDTH + 3 * SC_WIDTH


def _inproj_kernel(h_ref, w_ref, qkv_ref, branch_ref, gates_ref):
    hb = h_ref[...].astype(BF16)
    col = 0
    for out_ref in (qkv_ref, branch_ref, gates_ref):
        for j in range(out_ref.shape[1] // PROJ_CHUNK):
            acc = jnp.dot(hb, w_ref[:, col:col + PROJ_CHUNK], preferred_element_type=F32)
            if col < SB_WIDTH:
                acc = acc * (SB_HEAD_DIM ** -0.5)
            out_ref[:, j * PROJ_CHUNK:(j + 1) * PROJ_CHUNK] = acc.astype(out_ref.dtype)
            col += PROJ_CHUNK


def _inproj(h2, w_in_b):
    n, d = h2.shape
    cols = w_in_b.shape[1]
    widths = (QKV_COLS, BRANCH_COLS, cols - QKV_COLS - BRANCH_COLS)
    dtypes = (BF16, F32, F32)
    return pl.pallas_call(
        _inproj_kernel,
        out_shape=tuple(jax.ShapeDtypeStruct((n, w), t) for w, t in zip(widths, dtypes)),
        grid=(n // PROJ_ROWS,),
        in_specs=[pl.BlockSpec((PROJ_ROWS, d), lambda i: (i, 0)),
                  pl.BlockSpec((d, cols), lambda i: (0, 0), pipeline_mode=pl.Buffered(1))],
        out_specs=tuple(pl.BlockSpec((PROJ_ROWS, w), lambda i: (i, 0)) for w in widths),
        compiler_params=_cparams(("parallel",)),
        name="in_proj",
    )(h2, w_in_b)


def _sb_block(q_h, k_j, v_j, u2, carry, mask):
    z = lax.dot_general(q_h, k_j, (((1,), (1,)), ((), ())), preferred_element_type=F32)
    soft = jnp.log(1.0 + jnp.exp(-jnp.abs(z)))
    log_beta = jnp.minimum(z, 0.0) - soft
    log_keep = log_beta - z
    if mask is not None:
        log_keep = jnp.where(mask, log_keep, 0.0)
    hi = log_keep.astype(BF16)
    lo = (log_keep - hi.astype(F32)).astype(BF16)
    log_pass = jnp.dot(jnp.concatenate([hi, lo], axis=1), u2, preferred_element_type=F32)
    w = jnp.exp(log_beta + log_pass + carry)
    if mask is not None:
        w = jnp.where(mask, w, 0.0)
    pv = jnp.dot(w.astype(BF16), v_j, preferred_element_type=F32)
    return pv, carry + jnp.sum(log_keep, axis=1, keepdims=True)


def _attn_kernel(q_ref, k_ref, v_ref, u2_ref, o_ref):
    qi = pl.program_id(2)
    u2 = u2_ref[...]
    row = lax.broadcasted_iota(jnp.int32, (ATT_Q, ATT_K), 0)
    col = lax.broadcasted_iota(jnp.int32, (ATT_Q, ATT_K), 1)
    diag_mask = col < row
    outs = []
    for hh in range(LANES // SB_HEAD_DIM):
        lanes = slice(hh * SB_HEAD_DIM, (hh + 1) * SB_HEAD_DIM)
        q_h = q_ref[0, :, lanes]
        diag = pl.ds(pl.multiple_of(qi * ATT_K, ATT_K), ATT_K)
        acc, carry = _sb_block(q_h, k_ref[0, diag, lanes], v_ref[0, diag, lanes], u2,
                               jnp.zeros((ATT_Q, 1), F32), diag_mask)

        def body(i, state, q_h=q_h, lanes=lanes):
            acc, carry = state
            rows = pl.ds(pl.multiple_of((qi - 1 - i) * ATT_K, ATT_K), ATT_K)
            pv, carry = _sb_block(q_h, k_ref[0, rows, lanes], v_ref[0, rows, lanes], u2,
                                  carry, None)
            return acc + pv, carry

        acc, _ = lax.fori_loop(0, qi, body, (acc, carry))
        outs.append(acc)
    o_ref[0] = jnp.concatenate(outs, axis=1).astype(o_ref.dtype)


def _sb_attention(qkv3):
    b, s, _ = qkv3.shape
    pairs = SB_WIDTH // LANES
    tri = (jnp.arange(ATT_K)[:, None] > jnp.arange(ATT_K)[None, :]).astype(BF16)
    u2 = jnp.concatenate([tri, tri], axis=0)
    return pl.pallas_call(
        _attn_kernel,
        out_shape=jax.ShapeDtypeStruct((b, s, SB_WIDTH), BF16),
        grid=(b, pairs, s // ATT_Q),
        in_specs=[pl.BlockSpec((1, ATT_Q, LANES), lambda bi, p, qi: (bi, qi, p)),
                  pl.BlockSpec((1, s, LANES), lambda bi, p, qi: (bi, 0, pairs + p)),
                  pl.BlockSpec((1, s, LANES), lambda bi, p, qi: (bi, 0, 2 * pairs + p)),
                  _const_spec((2 * ATT_K, ATT_K))],
        out_specs=pl.BlockSpec((1, ATT_Q, LANES), lambda bi, p, qi: (bi, qi, p)),
        compiler_params=_cparams(("parallel", "parallel", "arbitrary")),
        name="sb_attention",
    )(qkv3, qkv3, qkv3, u2)


def _shift_rows(x, tail, j):
    if j == 0:
        return x
    xr = pltpu.roll(x, j, axis=0)
    tr = pltpu.roll(tail, j, axis=0)
    rows = lax.broadcasted_iota(jnp.int32, tail.shape, 0)
    head = jnp.where(rows < j, tr, xr[:SUBLANES])
    return jnp.concatenate([head, xr[SUBLANES:]], axis=0)


def _causal_conv(x, tail, w):
    k = w.shape[0]
    out = w[k - 1:k] * x
    for j in range(1, k):
        out = out + w[k - 1 - j:k - j] * _shift_rows(x, tail, j)
    return out


def _sigmoid(x):
    return 1.0 / (1.0 + jnp.exp(-x))


def _gelu_tanh(x):
    return 0.5 * x * (1.0 + jnp.tanh(math.sqrt(2.0 / math.pi) * (x + 0.044715 * (x * x * x))))


def _linear_scan(a, b):
    t = a.shape[0]
    rows = lax.broadcasted_iota(jnp.int32, a.shape, 0)
    d = 1
    while d < t:
        keep = rows >= d
        a_prev = jnp.where(keep, pltpu.roll(a, d, axis=0), 1.0)
        b_prev = jnp.where(keep, pltpu.roll(b, d, axis=0), 0.0)
        b = a * b_prev + b
        a = a * a_prev
        d *= 2
    return a, b


def _mix_kernel(h_ref, ysb_ref, lin_ref, lgate_ref, scb_ref, scc_ref, sch_ref,
                gsb_ref, glru_ref, gsc_ref, gate_b_ref, lconv_w_ref, lconv_b_ref,
                wa_ref, ba_ref, wx_ref, bx_ref, lam_ref, scw_ref,
                wsb_ref, wlru_ref, wsc_ref, wout_ref, ln_g_ref, ln_b_ref,
                out_ref, lin_tail, sc_tail, h_state):
    t = MIX_ROWS

    @pl.when(pl.program_id(1) == 0)
    def _():
        lin_tail[...] = jnp.zeros_like(lin_tail)
        sc_tail[...] = jnp.zeros_like(sc_tail)
        h_state[...] = jnp.zeros_like(h_state)

    x = lin_ref[0]
    u = _causal_conv(x, lin_tail[...], lconv_w_ref[...]) + lconv_b_ref[...]
    lin_tail[...] = x[t - SUBLANES:]
    ub = u.astype(BF16)
    r = _sigmoid(jnp.dot(ub, wa_ref[...], preferred_element_type=F32) + ba_ref[...])
    i = _sigmoid(jnp.dot(ub, wx_ref[...], preferred_element_type=F32) + bx_ref[...])
    lam = lam_ref[...]
    softplus_neg_lam = jnp.maximum(-lam, 0.0) + jnp.log(1.0 + jnp.exp(-jnp.abs(lam)))
    log_a = (-LRU_C) * r * softplus_neg_lam
    a = jnp.exp(log_a)
    drive = jnp.sqrt(1.0 - jnp.exp(2.0 * log_a)) * (i * u)
    a_cum, hs = _linear_scan(a, drive)
    hs = hs + a_cum * h_state[...]
    h_state[...] = hs[t - 1:]
    y_lru = _gelu_tanh(lgate_ref[0]) * hs

    p = scc_ref[0] * sch_ref[0]
    y_sc = scb_ref[0] * _causal_conv(p, sc_tail[...], scw_ref[...])
    sc_tail[...] = p[t - SUBLANES:]

    gb = gate_b_ref[...]
    merged = (_sigmoid(gsb_ref[0] + gb[0:1])
              * jnp.dot(ysb_ref[0], wsb_ref[...], preferred_element_type=F32)
              + _sigmoid(glru_ref[0] + gb[1:2])
              * jnp.dot(y_lru.astype(BF16), wlru_ref[...], preferred_element_type=F32)
              + _sigmoid(gsc_ref[0] + gb[2:3])
              * jnp.dot(y_sc.astype(BF16), wsc_ref[...], preferred_element_type=F32))
    mix = jnp.dot(merged.astype(BF16), wout_ref[...], preferred_element_type=F32)
    out_ref[0] = _layer_norm(ALPHA * h_ref[0] + mix, ln_g_ref[...], ln_b_ref[...])


def _block_diag(w):
    hh, ii, jj = w.shape
    eye = jnp.eye(hh, dtype=w.dtype)
    return (eye[:, None, :, None] * w[:, :, None, :]).reshape(hh * ii, hh * jj)


def _mixer_tail(h3, ysb3, branch3, gates3, p, l):
    b, s, d = h3.shape
    t = MIX_ROWS
    assert LRU_WIDTH == SC_WIDTH

    def row_spec(width, col):
        return pl.BlockSpec((1, t, width), lambda bi, si: (bi, si, col))

    vec = lambda v: v.reshape(1, -1).astype(F32)
    in_specs = [row_spec(d, 0), row_spec(SB_WIDTH, 0)]
    in_specs += [row_spec(LRU_WIDTH, c) for c in range(5)]
    in_specs += [row_spec(d, g) for g in range(3)]
    weights = [
        p["gate_b"][l].astype(F32),
        p["lru_conv_w"][l].astype(F32), vec(p["lru_conv_b"][l]),
        _block_diag(p["lru_wa"][l]).astype(BF16), vec(p["lru_ba"][l]),
        _block_diag(p["lru_wx"][l]).astype(BF16), vec(p["lru_bx"][l]),
        vec(p["lru_lambda"][l]), p["sc_conv_w"][l].astype(F32),
        p["w_branch_sb"][l].astype(BF16), p["w_branch_lru"][l].astype(BF16),
        p["w_branch_sc"][l].astype(BF16), p["w_out"][l].astype(BF16),
        vec(p["ln1_g"][l]), vec(p["ln1_b"][l]),
    ]
    in_specs += [_const_spec(w.shape) for w in weights]
    return pl.pallas_call(
        _mix_kernel,
        out_shape=jax.ShapeDtypeStruct((b, s, d), F32),
        grid=(b, s // t),
        in_specs=in_specs,
        out_specs=pl.BlockSpec((1, t, d), lambda bi, si: (bi, si, 0)),
        scratch_shapes=[pltpu.VMEM((SUBLANES, LRU_WIDTH), F32),
                        pltpu.VMEM((SUBLANES, SC_WIDTH), F32),
                        pltpu.VMEM((1, LRU_WIDTH), F32)],
        compiler_params=_cparams(("parallel", "arbitrary")),
        name="mixer_tail",
    )(h3, ysb3, *([branch3] * 5), *([gates3] * 3), *weights)


def _first_argmax(vals, lane):
    m = jnp.max(vals, axis=-1, keepdims=True)
    idx = jnp.min(jnp.where(vals == m, lane, LANES), axis=-1, keepdims=True)
    return m, idx


def _route_kernel(h_ref, w_ref, bias_ref, c_ref):
    logits = jnp.dot(h_ref[...], w_ref[...], preferred_element_type=F32,
                     precision=lax.Precision.HIGHEST)
    lane = lax.broadcasted_iota(jnp.int32, logits.shape, 1)
    neg = -jnp.inf
    is_group = (lane >= N_EXPERTS) & (lane < N_EXPERTS + N_GROUPS)
    biased = logits + bias_ref[...]
    g_max = jnp.max(jnp.where(is_group, logits, neg), axis=-1, keepdims=True)
    g_exp = jnp.where(is_group, jnp.exp(logits - g_max), 0.0)
    g_den = jnp.sum(g_exp, axis=-1, keepdims=True)
    _, g_lane = _first_argmax(jnp.where(is_group, biased, neg), lane)
    g_sel = g_lane - N_EXPERTS
    g_prob = jnp.sum(jnp.where(lane == g_lane, g_exp, 0.0), axis=-1, keepdims=True) / g_den
    in_group = (lane >= g_sel * EXP_PER_GROUP) & (lane < (g_sel + 1) * EXP_PER_GROUP)
    cand = jnp.where(in_group, biased, neg)
    _, e1 = _first_argmax(cand, lane)
    _, e2 = _first_argmax(jnp.where(lane == e1, neg, cand), lane)
    l1 = jnp.sum(jnp.where(lane == e1, logits, 0.0), axis=-1, keepdims=True)
    l2 = jnp.sum(jnp.where(lane == e2, logits, 0.0), axis=-1, keepdims=True)
    m = jnp.maximum(l1, l2)
    x1 = jnp.exp(l1 - m)
    x2 = jnp.exp(l2 - m)
    scale = g_prob / (x1 + x2)
    c_ref[...] = jnp.where(lane == e1, x1 * scale, 0.0) + jnp.where(lane == e2, x2 * scale, 0.0)


def _route(h2, w_router, bias):
    n, d = h2.shape
    return pl.pallas_call(
        _route_kernel,
        out_shape=jax.ShapeDtypeStruct((n, LANES), F32),
        grid=(n // ROUTE_ROWS,),
        in_specs=[pl.BlockSpec((ROUTE_ROWS, d), lambda i: (i, 0)),
                  _const_spec((d, LANES)), _const_spec((1, LANES))],
        out_specs=pl.BlockSpec((ROUTE_ROWS, LANES), lambda i: (i, 0)),
        compiler_params=_cparams(("parallel",)),
        name="moe_route",
    )(h2, w_router, bias)


def _moe_kernel(h_ref, c_ref, wg_ref, wu_ref, wd_ref, ln_g_ref, ln_b_ref, o_ref, acc_ref):
    e = pl.program_id(1)

    @pl.when(e == 0)
    def _():
        acc_ref[...] = jnp.zeros_like(acc_ref)

    hb = h_ref[...].astype(BF16)
    gate = jnp.dot(hb, wg_ref[0], preferred_element_type=F32)
    up = jnp.dot(hb, wu_ref[0], preferred_element_type=F32)
    lane = lax.broadcasted_iota(jnp.int32, c_ref.shape, 1)
    c = jnp.sum(jnp.where(lane == e, c_ref[...], 0.0), axis=-1, keepdims=True)
    act = gate * _sigmoid(gate) * up * c
    acc_ref[...] += jnp.dot(act.astype(BF16), wd_ref[0], preferred_element_type=F32)

    @pl.when(e == pl.num_programs(1) - 1)
    def _():
        o_ref[...] = _layer_norm(ALPHA * h_ref[...] + acc_ref[...], ln_g_ref[...], ln_b_ref[...])


def _moe(h2, combine, w_gate_b, w_up_b, w_down_b, ln_g, ln_b):
    n, d = h2.shape
    ne, _, f = w_gate_b.shape
    return pl.pallas_call(
        _moe_kernel,
        out_shape=jax.ShapeDtypeStruct((n, d), F32),
        grid=(n // MOE_ROWS, ne),
        in_specs=[pl.BlockSpec((MOE_ROWS, d), lambda i, e: (i, 0)),
                  pl.BlockSpec((MOE_ROWS, LANES), lambda i, e: (i, 0)),
                  pl.BlockSpec((1, d, f), lambda i, e: (e, 0, 0)),
                  pl.BlockSpec((1, d, f), lambda i, e: (e, 0, 0)),
                  pl.BlockSpec((1, f, d), lambda i, e: (e, 0, 0)),
                  _const_spec((1, d)), _const_spec((1, d))],
        out_specs=pl.BlockSpec((MOE_ROWS, d), lambda i, e: (i, 0)),
        scratch_shapes=[pltpu.VMEM((MOE_ROWS, d), F32)],
        compiler_params=_cparams(("parallel", "arbitrary")),
        name="moe_experts",
    )(h2, combine, w_gate_b, w_up_b, w_down_b, ln_g.reshape(1, d), ln_b.reshape(1, d))


def kernel(x, ln_in_g, ln_in_b, w_in, gate_b, lru_conv_w, lru_conv_b, lru_wa, lru_ba, lru_wx, lru_bx, lru_lambda, sc_conv_w, w_branch_sb, w_branch_lru, w_branch_sc, w_out, ln1_g, ln1_b, w_group, group_bias, w_expert_router, expert_bias, w_gate, w_up, w_down, ln2_g, ln2_b):
    b, s, d = x.shape
    n = b * s
    params = dict(gate_b=gate_b, lru_conv_w=lru_conv_w, lru_conv_b=lru_conv_b, lru_wa=lru_wa,
                  lru_ba=lru_ba, lru_wx=lru_wx, lru_bx=lru_bx, lru_lambda=lru_lambda,
                  sc_conv_w=sc_conv_w, w_branch_sb=w_branch_sb, w_branch_lru=w_branch_lru,
                  w_branch_sc=w_branch_sc, w_out=w_out, ln1_g=ln1_g, ln1_b=ln1_b)
    h = _entry_ln(x.reshape(n, d), ln_in_g, ln_in_b)
    for l in range(w_in.shape[0]):
        qkv, branch, gates = _inproj(h, w_in[l].astype(BF16))
        y_sb = _sb_attention(qkv.reshape(b, s, -1))
        h1 = _mixer_tail(h.reshape(b, s, d), y_sb, branch.reshape(b, s, -1),
                         gates.reshape(b, s, -1), params, l)
        h1 = h1.reshape(n, d)
        pad = LANES - N_EXPERTS - N_GROUPS
        w_router = jnp.concatenate(
            [w_expert_router[l], w_group[l], jnp.zeros((d, pad), F32)], axis=1).astype(F32)
        bias = jnp.concatenate(
            [expert_bias[l], group_bias[l], jnp.zeros((pad,), F32)]).reshape(1, LANES).astype(F32)
        combine = _route(h1, w_router, bias)
        h = _moe(h1, combine, w_gate[l].astype(BF16), w_up[l].astype(BF16),
                 w_down[l].astype(BF16), ln2_g[l], ln2_b[l])
    return h.reshape(b, s, d)
```

```python
import functools
import math

import jax
import jax.numpy as jnp
from jax import lax
from jax.experimental import pallas as pl
from jax.experimental.pallas import tpu as pltpu

F32 = jnp.float32
BF16 = jnp.bfloat16

SB_HEADS = 8
SB_HEAD_DIM = 64
SB_WIDTH = SB_HEADS * SB_HEAD_DIM
LRU_WIDTH = 512
LRU_BLOCKS = 8
LRU_C = 8.0
SC_WIDTH = 512
N_GROUPS = 4
EXP_PER_GROUP = 4
N_EXPERTS = N_GROUPS * EXP_PER_GROUP
D_EXPERT = 512
DEPTH = 2
ALPHA = (2 * DEPTH) ** 0.25
LN_EPS = 1e-5

LANES = 128
SUBLANES = 8
VMEM_LIMIT_BYTES = 56 * 1024 * 1024

LN_ROWS = 512
PROJ_ROWS = 512
PROJ_CHUNK = 512
ATT_Q = 256
ATT_K = 256
MIX_ROWS = 256
ROUTE_ROWS = 512
MOE_ROWS = 1024


def _cparams(sem):
    return pltpu.CompilerParams(dimension_semantics=sem, vmem_limit_bytes=VMEM_LIMIT_BYTES)


def _const_spec(shape):
    nd = len(shape)
    return pl.BlockSpec(shape, lambda *_: (0,) * nd)


def _layer_norm(y, g, b):
    mu = jnp.mean(y, axis=-1, keepdims=True)
    d = y - mu
    var = jnp.mean(d * d, axis=-1, keepdims=True)
    return d * lax.rsqrt(var + LN_EPS) * g + b


def _ln_kernel(x_ref, g_ref, b_ref, o_ref):
    o_ref[...] = _layer_norm(x_ref[...], g_ref[...], b_ref[...])


def _entry_ln(x2, g, b):
    n, d = x2.shape
    return pl.pallas_call(
        _ln_kernel,
        out_shape=jax.ShapeDtypeStruct((n, d), F32),
        grid=(n // LN_ROWS,),
        in_specs=[pl.BlockSpec((LN_ROWS, d), lambda i: (i, 0)),
                  _const_spec((1, d)), _const_spec((1, d))],
        out_specs=pl.BlockSpec((LN_ROWS, d), lambda i: (i, 0)),
        compiler_params=_cparams(("parallel",)),
        name="entry_ln",
    )(x2, g.reshape(1, d), b.reshape(1, d))


QKV_COLS = 3 * SB_WIDTH
BRANCH_COLS = 2 * LRU_WIDTH + 3 * SC_WIDTH


def _inproj_kernel(h_ref, w_ref, qkv_ref, branch_ref, gates_ref):
    hb = h_ref[...].astype(BF16)
    col = 0
    for out_ref in (qkv_ref, branch_ref, gates_ref):
        for j in range(out_ref.shape[1] // PROJ_CHUNK):
            acc = jnp.dot(hb, w_ref[:, col:col + PROJ_CHUNK], preferred_element_type=F32)
            if col < SB_WIDTH:
                acc = acc * (SB_HEAD_DIM ** -0.5 * LOG2_E)
            out_ref[:, j * PROJ_CHUNK:(j + 1) * PROJ_CHUNK] = acc.astype(out_ref.dtype)
            col += PROJ_CHUNK


def _inproj(h2, w_in_b):
    n, d = h2.shape
    cols = w_in_b.shape[1]
    widths = (QKV_COLS, BRANCH_COLS, cols - QKV_COLS - BRANCH_COLS)
    dtypes = (BF16, F32, F32)
    return pl.pallas_call(
        _inproj_kernel,
        out_shape=tuple(jax.ShapeDtypeStruct((n, w), t) for w, t in zip(widths, dtypes)),
        grid=(n // PROJ_ROWS,),
        in_specs=[pl.BlockSpec((PROJ_ROWS, d), lambda i: (i, 0)),
                  pl.BlockSpec((d, cols), lambda i: (0, 0), pipeline_mode=pl.Buffered(1))],
        out_specs=tuple(pl.BlockSpec((PROJ_ROWS, w), lambda i: (i, 0)) for w in widths),
        compiler_params=_cparams(("parallel",)),
        name="in_proj",
    )(h2, w_in_b)


LOG2_E = math.log2(math.e)
PASS_LOG2_FLOOR = -151.0


def _sb_block(q_h, k_j, v_j, u2, carry, mask):
    z = lax.dot_general(q_h, k_j, (((1,), (1,)), ((), ())), preferred_element_type=F32)
    soft = jnp.log2(1.0 + jnp.exp2(-jnp.abs(z)))
    log_beta = jnp.minimum(z, 0.0) - soft
    log_keep = log_beta - z
    if mask is not None:
        log_keep = jnp.where(mask, log_keep, 0.0)
    hi = log_keep.astype(BF16)
    lo = (log_keep - hi.astype(F32)).astype(BF16)
    log_pass = jnp.dot(jnp.concatenate([hi, lo], axis=1), u2, preferred_element_type=F32)
    w = jnp.exp2(log_beta + log_pass + carry)
    if mask is not None:
        w = jnp.where(mask, w, 0.0)
    pv = jnp.dot(w.astype(BF16), v_j, preferred_element_type=F32)
    return pv, carry + jnp.sum(log_keep, axis=1, keepdims=True)


def _attn_kernel(q_ref, k_ref, v_ref, u2_ref, o_ref):
    qi = pl.program_id(2)
    u2 = u2_ref[...]
    row = lax.broadcasted_iota(jnp.int32, (ATT_Q, ATT_K), 0)
    col = lax.broadcasted_iota(jnp.int32, (ATT_Q, ATT_K), 1)
    diag_mask = col < row
    head_lanes = [slice(hh * SB_HEAD_DIM, (hh + 1) * SB_HEAD_DIM)
                  for hh in range(LANES // SB_HEAD_DIM)]
    q = [q_ref[0, :, lanes] for lanes in head_lanes]

    def sweep(j, accs, carries, mask):
        rows = pl.ds(pl.multiple_of(j * ATT_K, ATT_K), ATT_K)
        out = [_sb_block(q_h, k_ref[0, rows, lanes], v_ref[0, rows, lanes], u2, carry, mask)
               for q_h, lanes, carry in zip(q, head_lanes, carries)]
        return ([acc + pv for acc, (pv, _) in zip(accs, out)], [c for _, c in out])

    zeros = [jnp.zeros((ATT_Q, SB_HEAD_DIM), F32) for _ in q]
    accs, carries = sweep(qi, zeros, [jnp.zeros((ATT_Q, 1), F32) for _ in q], diag_mask)

    def cond(state):
        j, _, carries = state
        live = functools.reduce(jnp.maximum, carries)
        return jnp.logical_and(j >= 0, jnp.max(live) > PASS_LOG2_FLOOR)

    def body(state):
        j, accs, carries = state
        accs, carries = sweep(j, accs, carries, None)
        return j - 1, accs, carries

    _, accs, _ = lax.while_loop(cond, body, (qi - 1, accs, carries))
    o_ref[0] = jnp.concatenate(accs, axis=1).astype(o_ref.dtype)


def _sb_attention(qkv3):
    b, s, _ = qkv3.shape
    pairs = SB_WIDTH // LANES
    tri = (jnp.arange(ATT_K)[:, None] > jnp.arange(ATT_K)[None, :]).astype(BF16)
    u2 = jnp.concatenate([tri, tri], axis=0)
    return pl.pallas_call(
        _attn_kernel,
        out_shape=jax.ShapeDtypeStruct((b, s, SB_WIDTH), BF16),
        grid=(b, pairs, s // ATT_Q),
        in_specs=[pl.BlockSpec((1, ATT_Q, LANES), lambda bi, p, qi: (bi, qi, p)),
                  pl.BlockSpec((1, s, LANES), lambda bi, p, qi: (bi, 0, pairs + p)),
                  pl.BlockSpec((1, s, LANES), lambda bi, p, qi: (bi, 0, 2 * pairs + p)),
                  _const_spec((2 * ATT_K, ATT_K))],
        out_specs=pl.BlockSpec((1, ATT_Q, LANES), lambda bi, p, qi: (bi, qi, p)),
        compiler_params=_cparams(("parallel", "parallel", "arbitrary")),
        name="sb_attention",
    )(qkv3, qkv3, qkv3, u2)


def _shift_rows(x, tail, j):
    if j == 0:
        return x
    xr = pltpu.roll(x, j, axis=0)
    tr = pltpu.roll(tail, j, axis=0)
    rows = lax.broadcasted_iota(jnp.int32, tail.shape, 0)
    head = jnp.where(rows < j, tr, xr[:SUBLANES])
    return jnp.concatenate([head, xr[SUBLANES:]], axis=0)


def _causal_conv(x, tail, w):
    k = w.shape[0]
    out = w[k - 1:k] * x
    for j in range(1, k):
        out = out + w[k - 1 - j:k - j] * _shift_rows(x, tail, j)
    return out


def _sigmoid(x):
    return 1.0 / (1.0 + jnp.exp(-x))


def _gelu_tanh(x):
    return 0.5 * x * (1.0 + jnp.tanh(math.sqrt(2.0 / math.pi) * (x + 0.044715 * (x * x * x))))


def _linear_scan(a, b):
    t = a.shape[0]
    rows = lax.broadcasted_iota(jnp.int32, a.shape, 0)
    d = 1
    while d < t:
        keep = rows >= d
        a_prev = jnp.where(keep, pltpu.roll(a, d, axis=0), 1.0)
        b_prev = jnp.where(keep, pltpu.roll(b, d, axis=0), 0.0)
        b = a * b_prev + b
        a = a * a_prev
        d *= 2
    return a, b


def _mix_kernel(h_ref, ysb_ref, lin_ref, lgate_ref, scb_ref, scc_ref, sch_ref,
                gsb_ref, glru_ref, gsc_ref, gate_b_ref, lconv_w_ref, lconv_b_ref,
                wa_ref, ba_ref, wx_ref, bx_ref, lam_ref, scw_ref,
                wsb_ref, wlru_ref, wsc_ref, wout_ref, ln_g_ref, ln_b_ref,
                out_ref, lin_tail, sc_tail, h_state):
    t = MIX_ROWS

    @pl.when(pl.program_id(1) == 0)
    def _():
        lin_tail[...] = jnp.zeros_like(lin_tail)
        sc_tail[...] = jnp.zeros_like(sc_tail)
        h_state[...] = jnp.zeros_like(h_state)

    x = lin_ref[0]
    u = _causal_conv(x, lin_tail[...], lconv_w_ref[...]) + lconv_b_ref[...]
    lin_tail[...] = x[t - SUBLANES:]
    ub = u.astype(BF16)
    r = _sigmoid(jnp.dot(ub, wa_ref[...], preferred_element_type=F32) + ba_ref[...])
    i = _sigmoid(jnp.dot(ub, wx_ref[...], preferred_element_type=F32) + bx_ref[...])
    lam = lam_ref[...]
    softplus_neg_lam = jnp.maximum(-lam, 0.0) + jnp.log(1.0 + jnp.exp(-jnp.abs(lam)))
    log_a = (-LRU_C) * r * softplus_neg_lam
    a = jnp.exp(log_a)
    drive = jnp.sqrt(1.0 - jnp.exp(2.0 * log_a)) * (i * u)
    a_cum, hs = _linear_scan(a, drive)
    hs = hs + a_cum * h_state[...]
    h_state[...] = hs[t - 1:]
    y_lru = _gelu_tanh(lgate_ref[0]) * hs

    p = scc_ref[0] * sch_ref[0]
    y_sc = scb_ref[0] * _causal_conv(p, sc_tail[...], scw_ref[...])
    sc_tail[...] = p[t - SUBLANES:]

    gb = gate_b_ref[...]
    merged = (_sigmoid(gsb_ref[0] + gb[0:1])
              * jnp.dot(ysb_ref[0], wsb_ref[...], preferred_element_type=F32)
              + _sigmoid(glru_ref[0] + gb[1:2])
              * jnp.dot(y_lru.astype(BF16), wlru_ref[...], preferred_element_type=F32)
              + _sigmoid(gsc_ref[0] + gb[2:3])
              * jnp.dot(y_sc.astype(BF16), wsc_ref[...], preferred_element_type=F32))
    mix = jnp.dot(merged.astype(BF16), wout_ref[...], preferred_element_type=F32)
    out_ref[0] = _layer_norm(ALPHA * h_ref[0] + mix, ln_g_ref[...], ln_b_ref[...])


def _block_diag(w):
    hh, ii, jj = w.shape
    eye = jnp.eye(hh, dtype=w.dtype)
    return (eye[:, None, :, None] * w[:, :, None, :]).reshape(hh * ii, hh * jj)


def _mixer_tail(h3, ysb3, branch3, gates3, p, l):
    b, s, d = h3.shape
    t = MIX_ROWS
    assert LRU_WIDTH == SC_WIDTH

    def row_spec(width, col):
        return pl.BlockSpec((1, t, width), lambda bi, si: (bi, si, col))

    vec = lambda v: v.reshape(1, -1).astype(F32)
    in_specs = [row_spec(d, 0), row_spec(SB_WIDTH, 0)]
    in_specs += [row_spec(LRU_WIDTH, c) for c in range(5)]
    in_specs += [row_spec(d, g) for g in range(3)]
    weights = [
        p["gate_b"][l].astype(F32),
        p["lru_conv_w"][l].astype(F32), vec(p["lru_conv_b"][l]),
        _block_diag(p["lru_wa"][l]).astype(BF16), vec(p["lru_ba"][l]),
        _block_diag(p["lru_wx"][l]).astype(BF16), vec(p["lru_bx"][l]),
        vec(p["lru_lambda"][l]), p["sc_conv_w"][l].astype(F32),
        p["w_branch_sb"][l].astype(BF16), p["w_branch_lru"][l].astype(BF16),
        p["w_branch_sc"][l].astype(BF16), p["w_out"][l].astype(BF16),
        vec(p["ln1_g"][l]), vec(p["ln1_b"][l]),
    ]
    in_specs += [_const_spec(w.shape) for w in weights]
    return pl.pallas_call(
        _mix_kernel,
        out_shape=jax.ShapeDtypeStruct((b, s, d), F32),
        grid=(b, s // t),
        in_specs=in_specs,
        out_specs=pl.BlockSpec((1, t, d), lambda bi, si: (bi, si, 0)),
        scratch_shapes=[pltpu.VMEM((SUBLANES, LRU_WIDTH), F32),
                        pltpu.VMEM((SUBLANES, SC_WIDTH), F32),
                        pltpu.VMEM((1, LRU_WIDTH), F32)],
        compiler_params=_cparams(("parallel", "arbitrary")),
        name="mixer_tail",
    )(h3, ysb3, *([branch3] * 5), *([gates3] * 3), *weights)


def _first_argmax(vals, lane):
    m = jnp.max(vals, axis=-1, keepdims=True)
    idx = jnp.min(jnp.where(vals == m, lane, LANES), axis=-1, keepdims=True)
    return m, idx


def _route_kernel(h_ref, w_ref, bias_ref, c_ref):
    logits = jnp.dot(h_ref[...], w_ref[...], preferred_element_type=F32,
                     precision=lax.Precision.HIGHEST)
    lane = lax.broadcasted_iota(jnp.int32, logits.shape, 1)
    neg = -jnp.inf
    is_group = (lane >= N_EXPERTS) & (lane < N_EXPERTS + N_GROUPS)
    biased = logits + bias_ref[...]
    g_max = jnp.max(jnp.where(is_group, logits, neg), axis=-1, keepdims=True)
    g_exp = jnp.where(is_group, jnp.exp(logits - g_max), 0.0)
    g_den = jnp.sum(g_exp, axis=-1, keepdims=True)
    _, g_lane = _first_argmax(jnp.where(is_group, biased, neg), lane)
    g_sel = g_lane - N_EXPERTS
    g_prob = jnp.sum(jnp.where(lane == g_lane, g_exp, 0.0), axis=-1, keepdims=True) / g_den
    in_group = (lane >= g_sel * EXP_PER_GROUP) & (lane < (g_sel + 1) * EXP_PER_GROUP)
    cand = jnp.where(in_group, biased, neg)
    _, e1 = _first_argmax(cand, lane)
    _, e2 = _first_argmax(jnp.where(lane == e1, neg, cand), lane)
    l1 = jnp.sum(jnp.where(lane == e1, logits, 0.0), axis=-1, keepdims=True)
    l2 = jnp.sum(jnp.where(lane == e2, logits, 0.0), axis=-1, keepdims=True)
    m = jnp.maximum(l1, l2)
    x1 = jnp.exp(l1 - m)
    x2 = jnp.exp(l2 - m)
    scale = g_prob / (x1 + x2)
    c_ref[...] = jnp.where(lane == e1, x1 * scale, 0.0) + jnp.where(lane == e2, x2 * scale, 0.0)


def _route(h2, w_router, bias):
    n, d = h2.shape
    return pl.pallas_call(
        _route_kernel,
        out_shape=jax.ShapeDtypeStruct((n, LANES), F32),
        grid=(n // ROUTE_ROWS,),
        in_specs=[pl.BlockSpec((ROUTE_ROWS, d), lambda i: (i, 0)),
                  _const_spec((d, LANES)), _const_spec((1, LANES))],
        out_specs=pl.BlockSpec((ROUTE_ROWS, LANES), lambda i: (i, 0)),
        compiler_params=_cparams(("parallel",)),
        name="moe_route",
    )(h2, w_router, bias)


def _moe_kernel(h_ref, c_ref, wg_ref, wu_ref, wd_ref, ln_g_ref, ln_b_ref, o_ref, acc_ref):
    e = pl.program_id(1)

    @pl.when(e == 0)
    def _():
        acc_ref[...] = jnp.zeros_like(acc_ref)

    hb = h_ref[...].astype(BF16)
    gate = jnp.dot(hb, wg_ref[0], preferred_element_type=F32)
    up = jnp.dot(hb, wu_ref[0], preferred_element_type=F32)
    lane = lax.broadcasted_iota(jnp.int32, c_ref.shape, 1)
    c = jnp.sum(jnp.where(lane == e, c_ref[...], 0.0), axis=-1, keepdims=True)
    act = gate * _sigmoid(gate) * up * c
    acc_ref[...] += jnp.dot(act.astype(BF16), wd_ref[0], preferred_element_type=F32)

    @pl.when(e == pl.num_programs(1) - 1)
    def _():
        o_ref[...] = _layer_norm(ALPHA * h_ref[...] + acc_ref[...], ln_g_ref[...], ln_b_ref[...])


def _moe(h2, combine, w_gate_b, w_up_b, w_down_b, ln_g, ln_b):
    n, d = h2.shape
    ne, _, f = w_gate_b.shape
    return pl.pallas_call(
        _moe_kernel,
        out_shape=jax.ShapeDtypeStruct((n, d), F32),
        grid=(n // MOE_ROWS, ne),
        in_specs=[pl.BlockSpec((MOE_ROWS, d), lambda i, e: (i, 0)),
                  pl.BlockSpec((MOE_ROWS, LANES), lambda i, e: (i, 0)),
                  pl.BlockSpec((1, d, f), lambda i, e: (e, 0, 0)),
                  pl.BlockSpec((1, d, f), lambda i, e: (e, 0, 0)),
                  pl.BlockSpec((1, f, d), lambda i, e: (e, 0, 0)),
                  _const_spec((1, d)), _const_spec((1, d))],
        out_specs=pl.BlockSpec((MOE_ROWS, d), lambda i, e: (i, 0)),
        scratch_shapes=[pltpu.VMEM((MOE_ROWS, d), F32)],
        compiler_params=_cparams(("parallel", "arbitrary")),
        name="moe_experts",
    )(h2, combine, w_gate_b, w_up_b, w_down_b, ln_g.reshape(1, d), ln_b.reshape(1, d))


def kernel(x, ln_in_g, ln_in_b, w_in, gate_b, lru_conv_w, lru_conv_b, lru_wa, lru_ba, lru_wx, lru_bx, lru_lambda, sc_conv_w, w_branch_sb, w_branch_lru, w_branch_sc, w_out, ln1_g, ln1_b, w_group, group_bias, w_expert_router, expert_bias, w_gate, w_up, w_down, ln2_g, ln2_b):
    b, s, d = x.shape
    n = b * s
    params = dict(gate_b=gate_b, lru_conv_w=lru_conv_w, lru_conv_b=lru_conv_b, lru_wa=lru_wa,
                  lru_ba=lru_ba, lru_wx=lru_wx, lru_bx=lru_bx, lru_lambda=lru_lambda,
                  sc_conv_w=sc_conv_w, w_branch_sb=w_branch_sb, w_branch_lru=w_branch_lru,
                  w_branch_sc=w_branch_sc, w_out=w_out, ln1_g=ln1_g, ln1_b=ln1_b)
    h = _entry_ln(x.reshape(n, d), ln_in_g, ln_in_b)
    for l in range(w_in.shape[0]):
        qkv, branch, gates = _inproj(h, w_in[l].astype(BF16))
        y_sb = _sb_attention(qkv.reshape(b, s, -1))
        h1 = _mixer_tail(h.reshape(b, s, d), y_sb, branch.reshape(b, s, -1),
                         gates.reshape(b, s, -1), params, l)
        h1 = h1.reshape(n, d)
        pad = LANES - N_EXPERTS - N_GROUPS
        w_router = jnp.concatenate(
            [w_expert_router[l], w_group[l], jnp.zeros((d, pad), F32)], axis=1).astype(F32)
        bias = jnp.concatenate(
            [expert_bias[l], group_bias[l], jnp.zeros((pad,), F32)]).reshape(1, LANES).astype(F32)
        combine = _route(h1, w_router, bias)
        h = _moe(h1, combine, w_gate[l].astype(BF16), w_up[l].astype(BF16),
                 w_down[l].astype(BF16), ln2_g[l], ln2_b[l])
    return h.reshape(b, s, d)
```

```python
import functools
import math

import jax
import jax.numpy as jnp
from jax import lax
from jax.experimental import pallas as pl
from jax.experimental.pallas import tpu as pltpu

F32 = jnp.float32
BF16 = jnp.bfloat16

SB_HEADS = 8
SB_HEAD_DIM = 64
SB_WIDTH = SB_HEADS * SB_HEAD_DIM
LRU_WIDTH = 512
LRU_BLOCKS = 8
LRU_C = 8.0
SC_WIDTH = 512
N_GROUPS = 4
EXP_PER_GROUP = 4
N_EXPERTS = N_GROUPS * EXP_PER_GROUP
D_EXPERT = 512
DEPTH = 2
ALPHA = (2 * DEPTH) ** 0.25
LN_EPS = 1e-5

LANES = 128
SUBLANES = 8
VMEM_LIMIT_BYTES = 56 * 1024 * 1024

LN_ROWS = 512
PROJ_ROWS = 512
PROJ_CHUNK = 512
ATT_Q = 256
ATT_K = 256
MIX_ROWS = 256
ROUTE_ROWS = 512
DISPATCH_ROWS = 512
EXPERT_ROWS = 256
ROW_COPY_UNROLL = 8

PAIRS_PER_GROUP = EXP_PER_GROUP * (EXP_PER_GROUP - 1) // 2
N_CLASSES = N_GROUPS * PAIRS_PER_GROUP
INFO_CLASS, INFO_RANK, INFO_W_LOW, INFO_W_HIGH = 0, 1, 2, 3


def _cparams(sem):
    return pltpu.CompilerParams(dimension_semantics=sem, vmem_limit_bytes=VMEM_LIMIT_BYTES)


def _const_spec(shape):
    nd = len(shape)
    return pl.BlockSpec(shape, lambda *_: (0,) * nd)


def _layer_norm(y, g, b):
    mu = jnp.mean(y, axis=-1, keepdims=True)
    d = y - mu
    var = jnp.mean(d * d, axis=-1, keepdims=True)
    return d * lax.rsqrt(var + LN_EPS) * g + b


def _ln_kernel(x_ref, g_ref, b_ref, o_ref):
    o_ref[...] = _layer_norm(x_ref[...], g_ref[...], b_ref[...])


def _entry_ln(x2, g, b):
    n, d = x2.shape
    return pl.pallas_call(
        _ln_kernel,
        out_shape=jax.ShapeDtypeStruct((n, d), F32),
        grid=(n // LN_ROWS,),
        in_specs=[pl.BlockSpec((LN_ROWS, d), lambda i: (i, 0)),
                  _const_spec((1, d)), _const_spec((1, d))],
        out_specs=pl.BlockSpec((LN_ROWS, d), lambda i: (i, 0)),
        compiler_params=_cparams(("parallel",)),
        name="entry_ln",
    )(x2, g.reshape(1, d), b.reshape(1, d))


QKV_COLS = 3 * SB_WIDTH
BRANCH_COLS = 2 * LRU_WIDTH + 3 * SC_WIDTH


def _inproj_kernel(h_ref, w_ref, qkv_ref, branch_ref, gates_ref):
    hb = h_ref[...].astype(BF16)
    col = 0
    for out_ref in (qkv_ref, branch_ref, gates_ref):
        for j in range(out_ref.shape[1] // PROJ_CHUNK):
            acc = jnp.dot(hb, w_ref[:, col:col + PROJ_CHUNK], preferred_element_type=F32)
            if col < SB_WIDTH:
                acc = acc * (SB_HEAD_DIM ** -0.5 * LOG2_E)
            out_ref[:, j * PROJ_CHUNK:(j + 1) * PROJ_CHUNK] = acc.astype(out_ref.dtype)
            col += PROJ_CHUNK


def _inproj(h2, w_in_b):
    n, d = h2.shape
    cols = w_in_b.shape[1]
    widths = (QKV_COLS, BRANCH_COLS, cols - QKV_COLS - BRANCH_COLS)
    dtypes = (BF16, F32, F32)
    return pl.pallas_call(
        _inproj_kernel,
        out_shape=tuple(jax.ShapeDtypeStruct((n, w), t) for w, t in zip(widths, dtypes)),
        grid=(n // PROJ_ROWS,),
        in_specs=[pl.BlockSpec((PROJ_ROWS, d), lambda i: (i, 0)),
                  pl.BlockSpec((d, cols), lambda i: (0, 0), pipeline_mode=pl.Buffered(1))],
        out_specs=tuple(pl.BlockSpec((PROJ_ROWS, w), lambda i: (i, 0)) for w in widths),
        compiler_params=_cparams(("parallel",)),
        name="in_proj",
    )(h2, w_in_b)


LOG2_E = math.log2(math.e)
PASS_LOG2_FLOOR = -151.0


def _sb_block(q_h, k_j, v_j, u2, carry, mask):
    z = lax.dot_general(q_h, k_j, (((1,), (1,)), ((), ())), preferred_element_type=F32)
    soft = jnp.log2(1.0 + jnp.exp2(-jnp.abs(z)))
    log_beta = jnp.minimum(z, 0.0) - soft
    log_keep = log_beta - z
    if mask is not None:
        log_keep = jnp.where(mask, log_keep, 0.0)
    hi = log_keep.astype(BF16)
    lo = (log_keep - hi.astype(F32)).astype(BF16)
    log_pass = jnp.dot(jnp.concatenate([hi, lo], axis=1), u2, preferred_element_type=F32)
    w = jnp.exp2(log_beta + log_pass + carry)
    if mask is not None:
        w = jnp.where(mask, w, 0.0)
    pv = jnp.dot(w.astype(BF16), v_j, preferred_element_type=F32)
    return pv, carry + jnp.sum(log_keep, axis=1, keepdims=True)


def _attn_kernel(q_ref, k_ref, v_ref, u2_ref, o_ref):
    qi = pl.program_id(2)
    u2 = u2_ref[...]
    row = lax.broadcasted_iota(jnp.int32, (ATT_Q, ATT_K), 0)
    col = lax.broadcasted_iota(jnp.int32, (ATT_Q, ATT_K), 1)
    diag_mask = col < row
    head_lanes = [slice(hh * SB_HEAD_DIM, (hh + 1) * SB_HEAD_DIM)
                  for hh in range(LANES // SB_HEAD_DIM)]
    q = [q_ref[0, :, lanes] for lanes in head_lanes]

    def sweep(j, accs, carries, mask):
        rows = pl.ds(pl.multiple_of(j * ATT_K, ATT_K), ATT_K)
        out = [_sb_block(q_h, k_ref[0, rows, lanes], v_ref[0, rows, lanes], u2, carry, mask)
               for q_h, lanes, carry in zip(q, head_lanes, carries)]
        return ([acc + pv for acc, (pv, _) in zip(accs, out)], [c for _, c in out])

    zeros = [jnp.zeros((ATT_Q, SB_HEAD_DIM), F32) for _ in q]
    accs, carries = sweep(qi, zeros, [jnp.zeros((ATT_Q, 1), F32) for _ in q], diag_mask)

    def cond(state):
        j, _, carries = state
        live = functools.reduce(jnp.maximum, carries)
        return jnp.logical_and(j >= 0, jnp.max(live) > PASS_LOG2_FLOOR)

    def body(state):
        j, accs, carries = state
        accs, carries = sweep(j, accs, carries, None)
        return j - 1, accs, carries

    _, accs, _ = lax.while_loop(cond, body, (qi - 1, accs, carries))
    o_ref[0] = jnp.concatenate(accs, axis=1).astype(o_ref.dtype)


def _sb_attention(qkv3):
    b, s, _ = qkv3.shape
    pairs = SB_WIDTH // LANES
    tri = (jnp.arange(ATT_K)[:, None] > jnp.arange(ATT_K)[None, :]).astype(BF16)
    u2 = jnp.concatenate([tri, tri], axis=0)
    return pl.pallas_call(
        _attn_kernel,
        out_shape=jax.ShapeDtypeStruct((b, s, SB_WIDTH), BF16),
        grid=(b, pairs, s // ATT_Q),
        in_specs=[pl.BlockSpec((1, ATT_Q, LANES), lambda bi, p, qi: (bi, qi, p)),
                  pl.BlockSpec((1, s, LANES), lambda bi, p, qi: (bi, 0, pairs + p)),
                  pl.BlockSpec((1, s, LANES), lambda bi, p, qi: (bi, 0, 2 * pairs + p)),
                  _const_spec((2 * ATT_K, ATT_K))],
        out_specs=pl.BlockSpec((1, ATT_Q, LANES), lambda bi, p, qi: (bi, qi, p)),
        compiler_params=_cparams(("parallel", "parallel", "arbitrary")),
        name="sb_attention",
    )(qkv3, qkv3, qkv3, u2)


def _shift_rows(x, tail, j):
    if j == 0:
        return x
    xr = pltpu.roll(x, j, axis=0)
    tr = pltpu.roll(tail, j, axis=0)
    rows = lax.broadcasted_iota(jnp.int32, tail.shape, 0)
    head = jnp.where(rows < j, tr, xr[:SUBLANES])
    return jnp.concatenate([head, xr[SUBLANES:]], axis=0)


def _causal_conv(x, tail, w):
    k = w.shape[0]
    out = w[k - 1:k] * x
    for j in range(1, k):
        out = out + w[k - 1 - j:k - j] * _shift_rows(x, tail, j)
    return out


def _sigmoid(x):
    return 1.0 / (1.0 + jnp.exp(-x))


def _gelu_tanh(x):
    return 0.5 * x * (1.0 + jnp.tanh(math.sqrt(2.0 / math.pi) * (x + 0.044715 * (x * x * x))))


def _linear_scan(a, b):
    t = a.shape[0]
    rows = lax.broadcasted_iota(jnp.int32, a.shape, 0)
    d = 1
    while d < t:
        keep = rows >= d
        a_prev = jnp.where(keep, pltpu.roll(a, d, axis=0), 1.0)
        b_prev = jnp.where(keep, pltpu.roll(b, d, axis=0), 0.0)
        b = a * b_prev + b
        a = a * a_prev
        d *= 2
    return a, b


def _mix_kernel(h_ref, ysb_ref, lin_ref, lgate_ref, scb_ref, scc_ref, sch_ref,
                gsb_ref, glru_ref, gsc_ref, gate_b_ref, lconv_w_ref, lconv_b_ref,
                wa_ref, ba_ref, wx_ref, bx_ref, lam_ref, scw_ref,
                wsb_ref, wlru_ref, wsc_ref, wout_ref, ln_g_ref, ln_b_ref,
                out_ref, lin_tail, sc_tail, h_state):
    t = MIX_ROWS

    @pl.when(pl.program_id(1) == 0)
    def _():
        lin_tail[...] = jnp.zeros_like(lin_tail)
        sc_tail[...] = jnp.zeros_like(sc_tail)
        h_state[...] = jnp.zeros_like(h_state)

    x = lin_ref[0]
    u = _causal_conv(x, lin_tail[...], lconv_w_ref[...]) + lconv_b_ref[...]
    lin_tail[...] = x[t - SUBLANES:]
    ub = u.astype(BF16)
    r = _sigmoid(jnp.dot(ub, wa_ref[...], preferred_element_type=F32) + ba_ref[...])
    i = _sigmoid(jnp.dot(ub, wx_ref[...], preferred_element_type=F32) + bx_ref[...])
    lam = lam_ref[...]
    softplus_neg_lam = jnp.maximum(-lam, 0.0) + jnp.log(1.0 + jnp.exp(-jnp.abs(lam)))
    log_a = (-LRU_C) * r * softplus_neg_lam
    a = jnp.exp(log_a)
    drive = jnp.sqrt(1.0 - jnp.exp(2.0 * log_a)) * (i * u)
    a_cum, hs = _linear_scan(a, drive)
    hs = hs + a_cum * h_state[...]
    h_state[...] = hs[t - 1:]
    y_lru = _gelu_tanh(lgate_ref[0]) * hs

    p = scc_ref[0] * sch_ref[0]
    y_sc = scb_ref[0] * _causal_conv(p, sc_tail[...], scw_ref[...])
    sc_tail[...] = p[t - SUBLANES:]

    gb = gate_b_ref[...]
    merged = (_sigmoid(gsb_ref[0] + gb[0:1])
              * jnp.dot(ysb_ref[0], wsb_ref[...], preferred_element_type=F32)
              + _sigmoid(glru_ref[0] + gb[1:2])
              * jnp.dot(y_lru.astype(BF16), wlru_ref[...], preferred_element_type=F32)
              + _sigmoid(gsc_ref[0] + gb[2:3])
              * jnp.dot(y_sc.astype(BF16), wsc_ref[...], preferred_element_type=F32))
    mix = jnp.dot(merged.astype(BF16), wout_ref[...], preferred_element_type=F32)
    out_ref[0] = _layer_norm(ALPHA * h_ref[0] + mix, ln_g_ref[...], ln_b_ref[...])


def _block_diag(w):
    hh, ii, jj = w.shape
    eye = jnp.eye(hh, dtype=w.dtype)
    return (eye[:, None, :, None] * w[:, :, None, :]).reshape(hh * ii, hh * jj)


def _mixer_tail(h3, ysb3, branch3, gates3, p, l):
    b, s, d = h3.shape
    t = MIX_ROWS
    assert LRU_WIDTH == SC_WIDTH

    def row_spec(width, col):
        return pl.BlockSpec((1, t, width), lambda bi, si: (bi, si, col))

    vec = lambda v: v.reshape(1, -1).astype(F32)
    in_specs = [row_spec(d, 0), row_spec(SB_WIDTH, 0)]
    in_specs += [row_spec(LRU_WIDTH, c) for c in range(5)]
    in_specs += [row_spec(d, g) for g in range(3)]
    weights = [
        p["gate_b"][l].astype(F32),
        p["lru_conv_w"][l].astype(F32), vec(p["lru_conv_b"][l]),
        _block_diag(p["lru_wa"][l]).astype(BF16), vec(p["lru_ba"][l]),
        _block_diag(p["lru_wx"][l]).astype(BF16), vec(p["lru_bx"][l]),
        vec(p["lru_lambda"][l]), p["sc_conv_w"][l].astype(F32),
        p["w_branch_sb"][l].astype(BF16), p["w_branch_lru"][l].astype(BF16),
        p["w_branch_sc"][l].astype(BF16), p["w_out"][l].astype(BF16),
        vec(p["ln1_g"][l]), vec(p["ln1_b"][l]),
    ]
    in_specs += [_const_spec(w.shape) for w in weights]
    return pl.pallas_call(
        _mix_kernel,
        out_shape=jax.ShapeDtypeStruct((b, s, d), F32),
        grid=(b, s // t),
        in_specs=in_specs,
        out_specs=pl.BlockSpec((1, t, d), lambda bi, si: (bi, si, 0)),
        scratch_shapes=[pltpu.VMEM((SUBLANES, LRU_WIDTH), F32),
                        pltpu.VMEM((SUBLANES, SC_WIDTH), F32),
                        pltpu.VMEM((1, LRU_WIDTH), F32)],
        compiler_params=_cparams(("parallel", "arbitrary")),
        name="mixer_tail",
    )(h3, ysb3, *([branch3] * 5), *([gates3] * 3), *weights)


def _first_argmax(vals, lane):
    m = jnp.max(vals, axis=-1, keepdims=True)
    idx = jnp.min(jnp.where(vals == m, lane, LANES), axis=-1, keepdims=True)
    return m, idx


def _route_kernel(h_ref, w_ref, bias_ref, tri_ref, info_ref, counts_ref, running):
    @pl.when(pl.program_id(0) == 0)
    def _():
        running[...] = jnp.zeros_like(running)

    logits = jnp.dot(h_ref[...], w_ref[...], preferred_element_type=F32,
                     precision=lax.Precision.HIGHEST)
    lane = lax.broadcasted_iota(jnp.int32, logits.shape, 1)
    neg = -jnp.inf
    is_group = (lane >= N_EXPERTS) & (lane < N_EXPERTS + N_GROUPS)
    biased = logits + bias_ref[...]
    g_max = jnp.max(jnp.where(is_group, logits, neg), axis=-1, keepdims=True)
    g_exp = jnp.where(is_group, jnp.exp(logits - g_max), 0.0)
    g_den = jnp.sum(g_exp, axis=-1, keepdims=True)
    _, g_lane = _first_argmax(jnp.where(is_group, biased, neg), lane)
    g_sel = g_lane - N_EXPERTS
    g_prob = jnp.sum(jnp.where(lane == g_lane, g_exp, 0.0), axis=-1, keepdims=True) / g_den
    in_group = (lane >= g_sel * EXP_PER_GROUP) & (lane < (g_sel + 1) * EXP_PER_GROUP)
    cand = jnp.where(in_group, biased, neg)
    _, e1 = _first_argmax(cand, lane)
    _, e2 = _first_argmax(jnp.where(lane == e1, neg, cand), lane)
    l1 = jnp.sum(jnp.where(lane == e1, logits, 0.0), axis=-1, keepdims=True)
    l2 = jnp.sum(jnp.where(lane == e2, logits, 0.0), axis=-1, keepdims=True)
    m = jnp.maximum(l1, l2)
    x1 = jnp.exp(l1 - m)
    x2 = jnp.exp(l2 - m)
    scale = g_prob / (x1 + x2)
    w1 = x1 * scale
    w2 = x2 * scale
    first_low = e1 < e2
    a = jnp.where(first_low, e1, e2) - g_sel * EXP_PER_GROUP
    b = jnp.where(first_low, e2, e1) - g_sel * EXP_PER_GROUP
    cls = g_sel * PAIRS_PER_GROUP + ((a * (2 * EXP_PER_GROUP - 1 - a)) >> 1) + (b - a - 1)
    onehot = lane == cls
    earlier = jnp.dot(tri_ref[...], jnp.where(onehot, 1.0, 0.0).astype(BF16),
                      preferred_element_type=F32) + running[...]
    rank = jnp.sum(jnp.where(onehot, earlier, 0.0), axis=-1, keepdims=True)
    last = ROUTE_ROWS - 1
    running[...] = earlier[last:] + jnp.where(onehot[last:], 1.0, 0.0)
    counts_ref[...] = running[...]
    info = jnp.where(lane == INFO_CLASS, cls.astype(F32), 0.0)
    info = jnp.where(lane == INFO_RANK, rank, info)
    info = jnp.where(lane == INFO_W_LOW, jnp.where(first_low, w1, w2), info)
    info_ref[...] = jnp.where(lane == INFO_W_HIGH, jnp.where(first_low, w2, w1), info)


def _route(h2, w_router, bias):
    n, d = h2.shape
    t = ROUTE_ROWS
    tri = (jnp.arange(t)[:, None] > jnp.arange(t)[None, :]).astype(BF16)
    return pl.pallas_call(
        _route_kernel,
        out_shape=(jax.ShapeDtypeStruct((n, LANES), F32), jax.ShapeDtypeStruct((1, LANES), F32)),
        grid=(n // t,),
        in_specs=[pl.BlockSpec((t, d), lambda i: (i, 0)),
                  _const_spec((d, LANES)), _const_spec((1, LANES)), _const_spec((t, t))],
        out_specs=(pl.BlockSpec((t, LANES), lambda i: (i, 0)), _const_spec((1, LANES))),
        scratch_shapes=[pltpu.VMEM((1, LANES), F32)],
        compiler_params=_cparams(("arbitrary",)),
        name="moe_route",
    )(h2, w_router, bias, tri)


def _row_copies(n_rows, make_copy):
    def start(r, c):
        make_copy(r).start()
        return c

    def wait(r, c):
        make_copy(r).wait()
        return c

    lax.fori_loop(0, n_rows, start, 0, unroll=ROW_COPY_UNROLL)
    lax.fori_loop(0, n_rows, wait, 0, unroll=ROW_COPY_UNROLL)


def _dispatch_kernel(slot_ref, h_ref, info_ref, xs_in_ref, xs_ref, rows, sem):
    del xs_in_ref
    d = h_ref.shape[1]
    rows[:, :d] = h_ref[...]
    rows[:, d:] = info_ref[...]

    def copy(r):
        return pltpu.make_async_copy(rows.at[pl.ds(r, 1)],
                                     xs_ref.at[pl.ds(slot_ref[0, 0, r], 1)], sem)

    _row_copies(DISPATCH_ROWS, copy)


def _dispatch(slot3, h2, info, n_slots):
    n, d = h2.shape
    t = DISPATCH_ROWS
    xs0 = jnp.zeros((n_slots, d + LANES), F32)
    return pl.pallas_call(
        _dispatch_kernel,
        out_shape=jax.ShapeDtypeStruct(xs0.shape, F32),
        grid=(n // t,),
        in_specs=[pl.BlockSpec((1, 1, t), lambda i: (i, 0, 0), memory_space=pltpu.SMEM),
                  pl.BlockSpec((t, d), lambda i: (i, 0)),
                  pl.BlockSpec((t, LANES), lambda i: (i, 0)),
                  pl.BlockSpec(memory_space=pl.ANY)],
        out_specs=pl.BlockSpec(memory_space=pl.ANY),
        scratch_shapes=[pltpu.VMEM((t, d + LANES), F32), pltpu.SemaphoreType.DMA(())],
        input_output_aliases={3: 0},
        compiler_params=_cparams(("arbitrary",)),
        name="moe_dispatch",
    )(slot3, h2, info, xs0)


def _expert_kernel(n_used_ref, src_ref, lo_ref, hi_ref, xs_ref, wg_lo, wg_hi, wu_lo, wu_hi,
                   wd_lo, wd_hi, ln_g_ref, ln_b_ref, ys_ref):
    del src_ref, lo_ref, hi_ref

    @pl.when(pl.program_id(0) < n_used_ref[0])
    def _():
        d = ys_ref.shape[1]
        x = xs_ref[:, :d]
        xb = x.astype(BF16)
        y = None
        for wg, wu, wd, lane in ((wg_lo, wu_lo, wd_lo, INFO_W_LOW), (wg_hi, wu_hi, wd_hi, INFO_W_HIGH)):
            gate = jnp.dot(xb, wg[0], preferred_element_type=F32)
            up = jnp.dot(xb, wu[0], preferred_element_type=F32)
            act = gate * _sigmoid(gate) * up * xs_ref[:, d + lane:d + lane + 1]
            part = jnp.dot(act.astype(BF16), wd[0], preferred_element_type=F32)
            y = part if y is None else y + part
        ys_ref[...] = _layer_norm(ALPHA * x + y, ln_g_ref[...], ln_b_ref[...])

    @pl.when(pl.program_id(0) >= n_used_ref[0])
    def _():
        ys_ref[...] = jnp.zeros_like(ys_ref)


def _experts(tile_meta, xs, w_gate_b, w_up_b, w_down_b, ln_g, ln_b):
    n_slots, cols = xs.shape
    d = cols - LANES
    f = w_gate_b.shape[2]
    t = EXPERT_ROWS
    x_map = lambda i, n_used, src, lo, hi: (src[i], 0)
    lo_map = lambda i, n_used, src, lo, hi: (lo[i], 0, 0)
    hi_map = lambda i, n_used, src, lo, hi: (hi[i], 0, 0)
    const = lambda i, *_: (0, 0)
    grid_spec = pltpu.PrefetchScalarGridSpec(
        num_scalar_prefetch=4,
        grid=(n_slots // t,),
        in_specs=[pl.BlockSpec((t, cols), x_map),
                  pl.BlockSpec((1, d, f), lo_map), pl.BlockSpec((1, d, f), hi_map),
                  pl.BlockSpec((1, d, f), lo_map), pl.BlockSpec((1, d, f), hi_map),
                  pl.BlockSpec((1, f, d), lo_map), pl.BlockSpec((1, f, d), hi_map),
                  pl.BlockSpec((1, d), const), pl.BlockSpec((1, d), const)],
        out_specs=pl.BlockSpec((t, d), lambda i, *_: (i, 0)),
    )
    return pl.pallas_call(
        _expert_kernel,
        out_shape=jax.ShapeDtypeStruct((n_slots, d), F32),
        grid_spec=grid_spec,
        compiler_params=_cparams(("arbitrary",)),
        name="moe_experts",
    )(*tile_meta, xs, w_gate_b, w_gate_b, w_up_b, w_up_b, w_down_b, w_down_b,
      ln_g.reshape(1, d), ln_b.reshape(1, d))


def _combine_kernel(slot_ref, ys_ref, o_ref, sem):
    def copy(r):
        return pltpu.make_async_copy(ys_ref.at[pl.ds(slot_ref[0, 0, r], 1)],
                                     o_ref.at[pl.ds(r, 1)], sem)

    _row_copies(DISPATCH_ROWS, copy)


def _combine(slot3, ys, n):
    d = ys.shape[1]
    t = DISPATCH_ROWS
    return pl.pallas_call(
        _combine_kernel,
        out_shape=jax.ShapeDtypeStruct((n, d), F32),
        grid=(n // t,),
        in_specs=[pl.BlockSpec((1, 1, t), lambda i: (i, 0, 0), memory_space=pltpu.SMEM),
                  pl.BlockSpec(memory_space=pl.ANY)],
        out_specs=pl.BlockSpec((t, d), lambda i: (i, 0)),
        scratch_shapes=[pltpu.SemaphoreType.DMA(())],
        compiler_params=_cparams(("arbitrary",)),
        name="moe_combine",
    )(slot3, ys)


def _pair_tables():
    lo, hi = [], []
    for g in range(N_GROUPS):
        for a in range(EXP_PER_GROUP):
            for b in range(a + 1, EXP_PER_GROUP):
                lo.append(g * EXP_PER_GROUP + a)
                hi.append(g * EXP_PER_GROUP + b)
    return jnp.asarray(lo, jnp.int32), jnp.asarray(hi, jnp.int32)


def _slot_plan(info, counts, n_tiles):
    t = EXPERT_ROWS
    counts = counts[0, :N_CLASSES].astype(jnp.int32)
    tiles_per_class = (counts + t - 1) // t
    tile_end = jnp.cumsum(tiles_per_class)
    class_start = (tile_end - tiles_per_class) * t
    cls = info[:, INFO_CLASS].astype(jnp.int32)
    rank = info[:, INFO_RANK].astype(jnp.int32)
    slot = class_start[cls] + rank
    n_used = tile_end[-1]
    src = jnp.minimum(jnp.arange(n_tiles, dtype=jnp.int32), n_used - 1)
    tile_class = jnp.minimum(jnp.searchsorted(tile_end, src, side="right"), N_CLASSES - 1)
    pair_lo, pair_hi = _pair_tables()
    meta = (n_used.reshape(1).astype(jnp.int32), src, pair_lo[tile_class], pair_hi[tile_class])
    return slot.reshape(-1, 1, DISPATCH_ROWS), meta


def kernel(x, ln_in_g, ln_in_b, w_in, gate_b, lru_conv_w, lru_conv_b, lru_wa, lru_ba, lru_wx, lru_bx, lru_lambda, sc_conv_w, w_branch_sb, w_branch_lru, w_branch_sc, w_out, ln1_g, ln1_b, w_group, group_bias, w_expert_router, expert_bias, w_gate, w_up, w_down, ln2_g, ln2_b):
    b, s, d = x.shape
    n = b * s
    params = dict(gate_b=gate_b, lru_conv_w=lru_conv_w, lru_conv_b=lru_conv_b, lru_wa=lru_wa,
                  lru_ba=lru_ba, lru_wx=lru_wx, lru_bx=lru_bx, lru_lambda=lru_lambda,
                  sc_conv_w=sc_conv_w, w_branch_sb=w_branch_sb, w_branch_lru=w_branch_lru,
                  w_branch_sc=w_branch_sc, w_out=w_out, ln1_g=ln1_g, ln1_b=ln1_b)
    h = _entry_ln(x.reshape(n, d), ln_in_g, ln_in_b)
    for l in range(w_in.shape[0]):
        qkv, branch, gates = _inproj(h, w_in[l].astype(BF16))
        y_sb = _sb_attention(qkv.reshape(b, s, -1))
        h1 = _mixer_tail(h.reshape(b, s, d), y_sb, branch.reshape(b, s, -1),
                         gates.reshape(b, s, -1), params, l)
        h1 = h1.reshape(n, d)
        pad = LANES - N_EXPERTS - N_GROUPS
        w_router = jnp.concatenate(
            [w_expert_router[l], w_group[l], jnp.zeros((d, pad), F32)], axis=1).astype(F32)
        bias = jnp.concatenate(
            [expert_bias[l], group_bias[l], jnp.zeros((pad,), F32)]).reshape(1, LANES).astype(F32)
        info, counts = _route(h1, w_router, bias)
        n_tiles = n // EXPERT_ROWS + N_CLASSES
        slot3, tile_meta = _slot_plan(info, counts, n_tiles)
        xs = _dispatch(slot3, h1, info, n_tiles * EXPERT_ROWS)
        ys = _experts(tile_meta, xs, w_gate[l].astype(BF16), w_up[l].astype(BF16),
                      w_down[l].astype(BF16), ln2_g[l], ln2_b[l])
        h = _combine(slot3, ys, n)
    return h.reshape(b, s, d)
```

```python
import functools
import math

import jax
import jax.numpy as jnp
from jax import lax
from jax.experimental import pallas as pl
from jax.experimental.pallas import tpu as pltpu

F32 = jnp.float32
BF16 = jnp.bfloat16

SB_HEADS = 8
SB_HEAD_DIM = 64
SB_WIDTH = SB_HEADS * SB_HEAD_DIM
LRU_WIDTH = 512
LRU_BLOCKS = 8
LRU_C = 8.0
SC_WIDTH = 512
N_GROUPS = 4
EXP_PER_GROUP = 4
N_EXPERTS = N_GROUPS * EXP_PER_GROUP
D_EXPERT = 512
DEPTH = 2
ALPHA = (2 * DEPTH) ** 0.25
LN_EPS = 1e-5

LANES = 128
SUBLANES = 8
VMEM_LIMIT_BYTES = 56 * 1024 * 1024

LN_ROWS = 512
PROJ_ROWS = 512
PROJ_CHUNK = 512
ATT_Q = 256
ATT_K = 256
MIX_ROWS = 256
ROUTE_ROWS = 512
DISPATCH_ROWS = 512
EXPERT_ROWS = 256
ROW_COPY_UNROLL = 8

PAIRS_PER_GROUP = EXP_PER_GROUP * (EXP_PER_GROUP - 1) // 2
N_CLASSES = N_GROUPS * PAIRS_PER_GROUP
INFO_W_LOW, INFO_W_HIGH = 0, 1
KEY_DIGIT = 256
KEY_SHIFT = 16


def _cparams(sem):
    return pltpu.CompilerParams(dimension_semantics=sem, vmem_limit_bytes=VMEM_LIMIT_BYTES)


def _const_spec(shape):
    nd = len(shape)
    return pl.BlockSpec(shape, lambda *_: (0,) * nd)


def _layer_norm(y, g, b):
    mu = jnp.mean(y, axis=-1, keepdims=True)
    d = y - mu
    var = jnp.mean(d * d, axis=-1, keepdims=True)
    return d * lax.rsqrt(var + LN_EPS) * g + b


def _ln_kernel(x_ref, g_ref, b_ref, o_ref):
    o_ref[...] = _layer_norm(x_ref[...], g_ref[...], b_ref[...])


def _entry_ln(x2, g, b):
    n, d = x2.shape
    return pl.pallas_call(
        _ln_kernel,
        out_shape=jax.ShapeDtypeStruct((n, d), F32),
        grid=(n // LN_ROWS,),
        in_specs=[pl.BlockSpec((LN_ROWS, d), lambda i: (i, 0)),
                  _const_spec((1, d)), _const_spec((1, d))],
        out_specs=pl.BlockSpec((LN_ROWS, d), lambda i: (i, 0)),
        compiler_params=_cparams(("parallel",)),
        name="entry_ln",
    )(x2, g.reshape(1, d), b.reshape(1, d))


QKV_COLS = 3 * SB_WIDTH
BRANCH_COLS = 2 * LRU_WIDTH + 3 * SC_WIDTH


def _inproj_kernel(h_ref, w_ref, qkv_ref, branch_ref, gates_ref):
    hb = h_ref[...].astype(BF16)
    col = 0
    for out_ref in (qkv_ref, branch_ref, gates_ref):
        for j in range(out_ref.shape[1] // PROJ_CHUNK):
            acc = jnp.dot(hb, w_ref[:, col:col + PROJ_CHUNK], preferred_element_type=F32)
            if col < SB_WIDTH:
                acc = acc * (SB_HEAD_DIM ** -0.5 * LOG2_E)
            out_ref[:, j * PROJ_CHUNK:(j + 1) * PROJ_CHUNK] = acc.astype(out_ref.dtype)
            col += PROJ_CHUNK


def _inproj(h2, w_in_b):
    n, d = h2.shape
    cols = w_in_b.shape[1]
    widths = (QKV_COLS, BRANCH_COLS, cols - QKV_COLS - BRANCH_COLS)
    dtypes = (BF16, F32, F32)
    return pl.pallas_call(
        _inproj_kernel,
        out_shape=tuple(jax.ShapeDtypeStruct((n, w), t) for w, t in zip(widths, dtypes)),
        grid=(n // PROJ_ROWS,),
        in_specs=[pl.BlockSpec((PROJ_ROWS, d), lambda i: (i, 0)),
                  pl.BlockSpec((d, cols), lambda i: (0, 0), pipeline_mode=pl.Buffered(1))],
        out_specs=tuple(pl.BlockSpec((PROJ_ROWS, w), lambda i: (i, 0)) for w in widths),
        compiler_params=_cparams(("parallel",)),
        name="in_proj",
    )(h2, w_in_b)


LOG2_E = math.log2(math.e)
PASS_LOG2_FLOOR = -151.0


def _sb_block(q_h, k_j, v_j, u2, carry, mask):
    z = lax.dot_general(q_h, k_j, (((1,), (1,)), ((), ())), preferred_element_type=F32)
    soft = jnp.log2(1.0 + jnp.exp2(-jnp.abs(z)))
    log_beta = jnp.minimum(z, 0.0) - soft
    log_keep = log_beta - z
    if mask is not None:
        log_keep = jnp.where(mask, log_keep, 0.0)
    hi = log_keep.astype(BF16)
    lo = (log_keep - hi.astype(F32)).astype(BF16)
    log_pass = jnp.dot(jnp.concatenate([hi, lo], axis=1), u2, preferred_element_type=F32)
    w = jnp.exp2(log_beta + log_pass + carry)
    if mask is not None:
        w = jnp.where(mask, w, 0.0)
    pv = jnp.dot(w.astype(BF16), v_j, preferred_element_type=F32)
    return pv, carry + jnp.sum(log_keep, axis=1, keepdims=True)


def _attn_kernel(q_ref, k_ref, v_ref, u2_ref, o_ref):
    qi = pl.program_id(2)
    u2 = u2_ref[...]
    row = lax.broadcasted_iota(jnp.int32, (ATT_Q, ATT_K), 0)
    col = lax.broadcasted_iota(jnp.int32, (ATT_Q, ATT_K), 1)
    diag_mask = col < row
    head_lanes = [slice(hh * SB_HEAD_DIM, (hh + 1) * SB_HEAD_DIM)
                  for hh in range(LANES // SB_HEAD_DIM)]
    q = [q_ref[0, :, lanes] for lanes in head_lanes]

    def sweep(j, accs, carries, mask):
        rows = pl.ds(pl.multiple_of(j * ATT_K, ATT_K), ATT_K)
        out = [_sb_block(q_h, k_ref[0, rows, lanes], v_ref[0, rows, lanes], u2, carry, mask)
               for q_h, lanes, carry in zip(q, head_lanes, carries)]
        return ([acc + pv for acc, (pv, _) in zip(accs, out)], [c for _, c in out])

    zeros = [jnp.zeros((ATT_Q, SB_HEAD_DIM), F32) for _ in q]
    accs, carries = sweep(qi, zeros, [jnp.zeros((ATT_Q, 1), F32) for _ in q], diag_mask)

    def cond(state):
        j, _, carries = state
        live = functools.reduce(jnp.maximum, carries)
        return jnp.logical_and(j >= 0, jnp.max(live) > PASS_LOG2_FLOOR)

    def body(state):
        j, accs, carries = state
        accs, carries = sweep(j, accs, carries, None)
        return j - 1, accs, carries

    _, accs, _ = lax.while_loop(cond, body, (qi - 1, accs, carries))
    o_ref[0] = jnp.concatenate(accs, axis=1).astype(o_ref.dtype)


def _sb_attention(qkv3):
    b, s, _ = qkv3.shape
    pairs = SB_WIDTH // LANES
    tri = (jnp.arange(ATT_K)[:, None] > jnp.arange(ATT_K)[None, :]).astype(BF16)
    u2 = jnp.concatenate([tri, tri], axis=0)
    return pl.pallas_call(
        _attn_kernel,
        out_shape=jax.ShapeDtypeStruct((b, s, SB_WIDTH), BF16),
        grid=(b, pairs, s // ATT_Q),
        in_specs=[pl.BlockSpec((1, ATT_Q, LANES), lambda bi, p, qi: (bi, qi, p)),
                  pl.BlockSpec((1, s, LANES), lambda bi, p, qi: (bi, 0, pairs + p)),
                  pl.BlockSpec((1, s, LANES), lambda bi, p, qi: (bi, 0, 2 * pairs + p)),
                  _const_spec((2 * ATT_K, ATT_K))],
        out_specs=pl.BlockSpec((1, ATT_Q, LANES), lambda bi, p, qi: (bi, qi, p)),
        compiler_params=_cparams(("parallel", "parallel", "arbitrary")),
        name="sb_attention",
    )(qkv3, qkv3, qkv3, u2)


def _shift_rows(x, tail, j):
    if j == 0:
        return x
    xr = pltpu.roll(x, j, axis=0)
    tr = pltpu.roll(tail, j, axis=0)
    rows = lax.broadcasted_iota(jnp.int32, tail.shape, 0)
    head = jnp.where(rows < j, tr, xr[:SUBLANES])
    return jnp.concatenate([head, xr[SUBLANES:]], axis=0)


def _causal_conv(x, tail, w):
    k = w.shape[0]
    out = w[k - 1:k] * x
    for j in range(1, k):
        out = out + w[k - 1 - j:k - j] * _shift_rows(x, tail, j)
    return out


def _sigmoid(x):
    return 1.0 / (1.0 + jnp.exp(-x))


def _gelu_tanh(x):
    return 0.5 * x * (1.0 + jnp.tanh(math.sqrt(2.0 / math.pi) * (x + 0.044715 * (x * x * x))))


def _linear_scan(a, b):
    t = a.shape[0]
    rows = lax.broadcasted_iota(jnp.int32, a.shape, 0)
    d = 1
    while d < t:
        keep = rows >= d
        a_prev = jnp.where(keep, pltpu.roll(a, d, axis=0), 1.0)
        b_prev = jnp.where(keep, pltpu.roll(b, d, axis=0), 0.0)
        b = a * b_prev + b
        a = a * a_prev
        d *= 2
    return a, b


def _mix_kernel(h_ref, ysb_ref, lin_ref, lgate_ref, scb_ref, scc_ref, sch_ref,
                gsb_ref, glru_ref, gsc_ref, gate_b_ref, lconv_w_ref, lconv_b_ref,
                wa_ref, ba_ref, wx_ref, bx_ref, lam_ref, scw_ref,
                wsb_ref, wlru_ref, wsc_ref, wout_ref, ln_g_ref, ln_b_ref,
                out_ref, lin_tail, sc_tail, h_state):
    t = MIX_ROWS

    @pl.when(pl.program_id(1) == 0)
    def _():
        lin_tail[...] = jnp.zeros_like(lin_tail)
        sc_tail[...] = jnp.zeros_like(sc_tail)
        h_state[...] = jnp.zeros_like(h_state)

    x = lin_ref[0]
    u = _causal_conv(x, lin_tail[...], lconv_w_ref[...]) + lconv_b_ref[...]
    lin_tail[...] = x[t - SUBLANES:]
    ub = u.astype(BF16)
    r = _sigmoid(jnp.dot(ub, wa_ref[...], preferred_element_type=F32) + ba_ref[...])
    i = _sigmoid(jnp.dot(ub, wx_ref[...], preferred_element_type=F32) + bx_ref[...])
    lam = lam_ref[...]
    softplus_neg_lam = jnp.maximum(-lam, 0.0) + jnp.log(1.0 + jnp.exp(-jnp.abs(lam)))
    log_a = (-LRU_C) * r * softplus_neg_lam
    a = jnp.exp(log_a)
    drive = jnp.sqrt(1.0 - jnp.exp(2.0 * log_a)) * (i * u)
    a_cum, hs = _linear_scan(a, drive)
    hs = hs + a_cum * h_state[...]
    h_state[...] = hs[t - 1:]
    y_lru = _gelu_tanh(lgate_ref[0]) * hs

    p = scc_ref[0] * sch_ref[0]
    y_sc = scb_ref[0] * _causal_conv(p, sc_tail[...], scw_ref[...])
    sc_tail[...] = p[t - SUBLANES:]

    gb = gate_b_ref[...]
    merged = (_sigmoid(gsb_ref[0] + gb[0:1])
              * jnp.dot(ysb_ref[0], wsb_ref[...], preferred_element_type=F32)
              + _sigmoid(glru_ref[0] + gb[1:2])
              * jnp.dot(y_lru.astype(BF16), wlru_ref[...], preferred_element_type=F32)
              + _sigmoid(gsc_ref[0] + gb[2:3])
              * jnp.dot(y_sc.astype(BF16), wsc_ref[...], preferred_element_type=F32))
    mix = jnp.dot(merged.astype(BF16), wout_ref[...], preferred_element_type=F32)
    out_ref[0] = _layer_norm(ALPHA * h_ref[0] + mix, ln_g_ref[...], ln_b_ref[...])


def _block_diag(w):
    hh, ii, jj = w.shape
    eye = jnp.eye(hh, dtype=w.dtype)
    return (eye[:, None, :, None] * w[:, :, None, :]).reshape(hh * ii, hh * jj)


def _mixer_tail(h3, ysb3, branch3, gates3, p, l):
    b, s, d = h3.shape
    t = MIX_ROWS
    assert LRU_WIDTH == SC_WIDTH

    def row_spec(width, col):
        return pl.BlockSpec((1, t, width), lambda bi, si: (bi, si, col))

    vec = lambda v: v.reshape(1, -1).astype(F32)
    in_specs = [row_spec(d, 0), row_spec(SB_WIDTH, 0)]
    in_specs += [row_spec(LRU_WIDTH, c) for c in range(5)]
    in_specs += [row_spec(d, g) for g in range(3)]
    weights = [
        p["gate_b"][l].astype(F32),
        p["lru_conv_w"][l].astype(F32), vec(p["lru_conv_b"][l]),
        _block_diag(p["lru_wa"][l]).astype(BF16), vec(p["lru_ba"][l]),
        _block_diag(p["lru_wx"][l]).astype(BF16), vec(p["lru_bx"][l]),
        vec(p["lru_lambda"][l]), p["sc_conv_w"][l].astype(F32),
        p["w_branch_sb"][l].astype(BF16), p["w_branch_lru"][l].astype(BF16),
        p["w_branch_sc"][l].astype(BF16), p["w_out"][l].astype(BF16),
        vec(p["ln1_g"][l]), vec(p["ln1_b"][l]),
    ]
    in_specs += [_const_spec(w.shape) for w in weights]
    return pl.pallas_call(
        _mix_kernel,
        out_shape=jax.ShapeDtypeStruct((b, s, d), F32),
        grid=(b, s // t),
        in_specs=in_specs,
        out_specs=pl.BlockSpec((1, t, d), lambda bi, si: (bi, si, 0)),
        scratch_shapes=[pltpu.VMEM((SUBLANES, LRU_WIDTH), F32),
                        pltpu.VMEM((SUBLANES, SC_WIDTH), F32),
                        pltpu.VMEM((1, LRU_WIDTH), F32)],
        compiler_params=_cparams(("parallel", "arbitrary")),
        name="mixer_tail",
    )(h3, ysb3, *([branch3] * 5), *([gates3] * 3), *weights)


def _first_argmax(vals, lane):
    m = jnp.max(vals, axis=-1, keepdims=True)
    idx = jnp.min(jnp.where(vals == m, lane, LANES), axis=-1, keepdims=True)
    return m, idx


def _route_kernel(h_ref, w_ref, bias_ref, tri_ref, pick_ref, info_ref, key_ref, counts_ref, running):
    @pl.when(pl.program_id(0) == 0)
    def _():
        running[...] = jnp.zeros_like(running)

    h = h_ref[...]
    h_hi = h.astype(BF16)
    h_lo = (h - h_hi.astype(F32)).astype(BF16)
    w = w_ref[...]
    w_hi = w.astype(BF16)
    w_lo = (w - w_hi.astype(F32)).astype(BF16)
    logits = jnp.dot(jnp.concatenate([h_hi, h_lo, h_hi], axis=1),
                     jnp.concatenate([w_hi, w_hi, w_lo], axis=0), preferred_element_type=F32)
    lane = lax.broadcasted_iota(jnp.int32, logits.shape, 1)
    neg = -jnp.inf
    is_group = (lane >= N_EXPERTS) & (lane < N_EXPERTS + N_GROUPS)
    biased = logits + bias_ref[...]
    g_max = jnp.max(jnp.where(is_group, logits, neg), axis=-1, keepdims=True)
    g_exp = jnp.where(is_group, jnp.exp(logits - g_max), 0.0)
    g_den = jnp.sum(g_exp, axis=-1, keepdims=True)
    _, g_lane = _first_argmax(jnp.where(is_group, biased, neg), lane)
    g_sel = g_lane - N_EXPERTS
    g_prob = jnp.sum(jnp.where(lane == g_lane, g_exp, 0.0), axis=-1, keepdims=True) / g_den
    in_group = (lane >= g_sel * EXP_PER_GROUP) & (lane < (g_sel + 1) * EXP_PER_GROUP)
    cand = jnp.where(in_group, biased, neg)
    _, e1 = _first_argmax(cand, lane)
    _, e2 = _first_argmax(jnp.where(lane == e1, neg, cand), lane)
    l1 = jnp.sum(jnp.where(lane == e1, logits, 0.0), axis=-1, keepdims=True)
    l2 = jnp.sum(jnp.where(lane == e2, logits, 0.0), axis=-1, keepdims=True)
    m = jnp.maximum(l1, l2)
    x1 = jnp.exp(l1 - m)
    x2 = jnp.exp(l2 - m)
    scale = g_prob / (x1 + x2)
    w1 = x1 * scale
    w2 = x2 * scale
    first_low = e1 < e2
    a = jnp.where(first_low, e1, e2) - g_sel * EXP_PER_GROUP
    b = jnp.where(first_low, e2, e1) - g_sel * EXP_PER_GROUP
    cls = g_sel * PAIRS_PER_GROUP + ((a * (2 * EXP_PER_GROUP - 1 - a)) >> 1) + (b - a - 1)
    onehot = lane == cls
    earlier = jnp.dot(tri_ref[...], jnp.where(onehot, 1.0, 0.0).astype(BF16),
                      preferred_element_type=F32) + running[...]
    rank = jnp.sum(jnp.where(onehot, earlier, 0.0), axis=-1, keepdims=True)
    last = ROUTE_ROWS - 1
    running[...] = earlier[last:] + jnp.where(onehot[last:], 1.0, 0.0)
    counts_ref[...] = running[...]
    info = jnp.where(lane == INFO_W_LOW, jnp.where(first_low, w1, w2), 0.0)
    info_ref[...] = jnp.where(lane == INFO_W_HIGH, jnp.where(first_low, w2, w1), info)
    rank_hi = jnp.floor(rank * (1.0 / KEY_DIGIT))
    digits = jnp.where(lane == 0, rank - rank_hi * KEY_DIGIT, 0.0)
    digits = jnp.where(lane == 1, rank_hi, digits)
    digits = jnp.where(lane == 2, cls.astype(F32), digits)
    keys = lax.dot_general(pick_ref[...], digits.astype(BF16), (((1,), (1,)), ((), ())),
                           preferred_element_type=F32)
    key_ref[0] = keys[0:1].astype(jnp.int32)


def _route(h2, w_router, bias):
    n, d = h2.shape
    t = ROUTE_ROWS
    tri = (jnp.arange(t)[:, None] > jnp.arange(t)[None, :]).astype(BF16)
    pick = jnp.zeros((SUBLANES, LANES), F32).at[0, :3].set(
        jnp.asarray([1.0, KEY_DIGIT, KEY_DIGIT * KEY_DIGIT], F32)).astype(BF16)
    return pl.pallas_call(
        _route_kernel,
        out_shape=(jax.ShapeDtypeStruct((n, LANES), F32),
                   jax.ShapeDtypeStruct((n // t, 1, t), jnp.int32),
                   jax.ShapeDtypeStruct((1, LANES), F32)),
        grid=(n // t,),
        in_specs=[pl.BlockSpec((t, d), lambda i: (i, 0)),
                  _const_spec((d, LANES)), _const_spec((1, LANES)), _const_spec((t, t)),
                  _const_spec((SUBLANES, LANES))],
        out_specs=(pl.BlockSpec((t, LANES), lambda i: (i, 0)),
                   pl.BlockSpec((1, 1, t), lambda i: (i, 0, 0)),
                   _const_spec((1, LANES))),
        scratch_shapes=[pltpu.VMEM((1, LANES), F32)],
        compiler_params=_cparams(("arbitrary",)),
        name="moe_route",
    )(h2, w_router, bias, tri, pick)


def _slot_kernel(start_ref, key_ref, slot_ref):
    key = key_ref[...]
    cls = key >> KEY_SHIFT
    slot = key & (KEY_DIGIT * KEY_DIGIT - 1)
    for c in range(N_CLASSES):
        slot = slot + jnp.where(cls == c, start_ref[c], 0)
    slot_ref[...] = slot


def _slots(class_start, key3):
    g, _, t = key3.shape
    return pl.pallas_call(
        _slot_kernel,
        out_shape=jax.ShapeDtypeStruct(key3.shape, jnp.int32),
        grid_spec=pltpu.PrefetchScalarGridSpec(
            num_scalar_prefetch=1, grid=(1,),
            in_specs=[pl.BlockSpec((g, 1, t), lambda i, s: (0, 0, 0))],
            out_specs=pl.BlockSpec((g, 1, t), lambda i, s: (0, 0, 0))),
        compiler_params=_cparams(("arbitrary",)),
        name="moe_slots",
    )(class_start, key3)


def _for_rows(n_rows, fn):
    def body(r, c):
        fn(r)
        return c

    lax.fori_loop(0, n_rows, body, 0, unroll=ROW_COPY_UNROLL)


def _dispatch_kernel(tile_end_ref, n_used_ref, slot_ref, h_ref, info_ref, xs_ref, rows, sems, zsem):
    i = pl.program_id(0)
    last = pl.num_programs(0) - 1
    p = i % 2
    d = h_ref.shape[1]
    t = DISPATCH_ROWS
    te = EXPERT_ROWS

    def row_copy(buf, r):
        return pltpu.make_async_copy(rows.at[buf, pl.ds(r, 1)],
                                     xs_ref.at[pl.ds(slot_ref[0, 0, r], 1)], sems.at[buf])

    @pl.when(i == 0)
    def _():
        rows[1, :te] = jnp.zeros((te, d + LANES), F32)

        def zero_tile(tile):
            return pltpu.make_async_copy(rows.at[1, pl.ds(0, te)],
                                         xs_ref.at[pl.ds(tile * te, te)], zsem)

        prev_end = 0
        for c in range(N_CLASSES):
            end = tile_end_ref[c]

            @pl.when(end > prev_end)
            def _(end=end):
                cp = zero_tile(end - 1)
                cp.start()
                cp.wait()
            prev_end = end

        def zero_tail(tile, c):
            cp = zero_tile(tile)
            cp.start()
            cp.wait()
            return c

        lax.fori_loop(n_used_ref[0], xs_ref.shape[0] // te, zero_tail, 0)

    rows[p, :, :d] = h_ref[...]
    rows[p, :, d:] = info_ref[...]
    _for_rows(t, lambda r: row_copy(p, r).start())

    def row_wait(buf):
        pltpu.make_async_copy(rows.at[buf, pl.ds(0, 1)], xs_ref.at[pl.ds(0, 1)],
                              sems.at[buf]).wait()

    @pl.when(i > 0)
    def _():
        _for_rows(t, lambda r: row_wait(1 - p))

    @pl.when(i == last)
    def _():
        _for_rows(t, lambda r: row_wait(p))


def _dispatch(tile_end, n_used, slot3, h2, info, n_slots):
    n, d = h2.shape
    t = DISPATCH_ROWS
    return pl.pallas_call(
        _dispatch_kernel,
        out_shape=jax.ShapeDtypeStruct((n_slots, d + LANES), F32),
        grid_spec=pltpu.PrefetchScalarGridSpec(
            num_scalar_prefetch=2, grid=(n // t,),
            in_specs=[pl.BlockSpec((1, 1, t), lambda i, *_: (i, 0, 0), memory_space=pltpu.SMEM),
                      pl.BlockSpec((t, d), lambda i, *_: (i, 0)),
                      pl.BlockSpec((t, LANES), lambda i, *_: (i, 0))],
            out_specs=pl.BlockSpec(memory_space=pl.ANY),
            scratch_shapes=[pltpu.VMEM((2, t, d + LANES), F32), pltpu.SemaphoreType.DMA((2,)),
                            pltpu.SemaphoreType.DMA(())]),
        compiler_params=_cparams(("arbitrary",)),
        name="moe_dispatch",
    )(tile_end, n_used, slot3, h2, info)


def _expert_kernel(n_used_ref, src_ref, lo_ref, hi_ref, xs_ref, wg_lo, wg_hi, wu_lo, wu_hi,
                   wd_lo, wd_hi, ln_g_ref, ln_b_ref, ys_ref):
    del src_ref, lo_ref, hi_ref

    @pl.when(pl.program_id(0) < n_used_ref[0])
    def _():
        d = ys_ref.shape[1]
        x = xs_ref[:, :d]
        xb = x.astype(BF16)
        y = None
        for wg, wu, wd, lane in ((wg_lo, wu_lo, wd_lo, INFO_W_LOW), (wg_hi, wu_hi, wd_hi, INFO_W_HIGH)):
            gate = jnp.dot(xb, wg[0], preferred_element_type=F32)
            up = jnp.dot(xb, wu[0], preferred_element_type=F32)
            act = gate * _sigmoid(gate) * up * xs_ref[:, d + lane:d + lane + 1]
            part = jnp.dot(act.astype(BF16), wd[0], preferred_element_type=F32)
            y = part if y is None else y + part
        ys_ref[...] = _layer_norm(ALPHA * x + y, ln_g_ref[...], ln_b_ref[...])

    @pl.when(pl.program_id(0) >= n_used_ref[0])
    def _():
        ys_ref[...] = jnp.zeros_like(ys_ref)


def _experts(tile_meta, xs, w_gate_b, w_up_b, w_down_b, ln_g, ln_b):
    n_slots, cols = xs.shape
    d = cols - LANES
    f = w_gate_b.shape[2]
    t = EXPERT_ROWS
    x_map = lambda i, n_used, src, lo, hi: (src[i], 0)
    lo_map = lambda i, n_used, src, lo, hi: (lo[i], 0, 0)
    hi_map = lambda i, n_used, src, lo, hi: (hi[i], 0, 0)
    const = lambda i, *_: (0, 0)
    grid_spec = pltpu.PrefetchScalarGridSpec(
        num_scalar_prefetch=4,
        grid=(n_slots // t,),
        in_specs=[pl.BlockSpec((t, cols), x_map),
                  pl.BlockSpec((1, d, f), lo_map), pl.BlockSpec((1, d, f), hi_map),
                  pl.BlockSpec((1, d, f), lo_map), pl.BlockSpec((1, d, f), hi_map),
                  pl.BlockSpec((1, f, d), lo_map), pl.BlockSpec((1, f, d), hi_map),
                  pl.BlockSpec((1, d), const), pl.BlockSpec((1, d), const)],
        out_specs=pl.BlockSpec((t, d), lambda i, *_: (i, 0)),
    )
    return pl.pallas_call(
        _expert_kernel,
        out_shape=jax.ShapeDtypeStruct((n_slots, d), F32),
        grid_spec=grid_spec,
        compiler_params=_cparams(("arbitrary",)),
        name="moe_experts",
    )(*tile_meta, xs, w_gate_b, w_gate_b, w_up_b, w_up_b, w_down_b, w_down_b,
      ln_g.reshape(1, d), ln_b.reshape(1, d))


def _combine_kernel(slot_ref, ys_ref, o_ref, buf, gsems, osems):
    i = pl.program_id(0)
    last = pl.num_programs(0) - 1
    p = i % 2
    t = DISPATCH_ROWS

    def tile_out(b, step):
        return pltpu.make_async_copy(buf.at[b], o_ref.at[pl.ds(step * t, t)], osems.at[b])

    def row_wait(b):
        pltpu.make_async_copy(ys_ref.at[pl.ds(0, 1)], buf.at[b, pl.ds(0, 1)], gsems.at[b]).wait()

    @pl.when(i >= 2)
    def _():
        tile_out(p, i - 2).wait()

    _for_rows(t, lambda r: pltpu.make_async_copy(
        ys_ref.at[pl.ds(slot_ref[0, 0, r], 1)], buf.at[p, pl.ds(r, 1)], gsems.at[p]).start())

    @pl.when(i > 0)
    def _():
        _for_rows(t, lambda r: row_wait(1 - p))
        tile_out(1 - p, i - 1).start()

    @pl.when(i == last)
    def _():
        _for_rows(t, lambda r: row_wait(p))
        tile_out(p, i).start()
        tile_out(p, i).wait()

        @pl.when(i > 0)
        def _():
            tile_out(1 - p, i - 1).wait()


def _combine(slot3, ys, n):
    d = ys.shape[1]
    t = DISPATCH_ROWS
    return pl.pallas_call(
        _combine_kernel,
        out_shape=jax.ShapeDtypeStruct((n, d), F32),
        grid=(n // t,),
        in_specs=[pl.BlockSpec((1, 1, t), lambda i: (i, 0, 0), memory_space=pltpu.SMEM),
                  pl.BlockSpec(memory_space=pl.ANY)],
        out_specs=pl.BlockSpec(memory_space=pl.ANY),
        scratch_shapes=[pltpu.VMEM((2, t, d), F32), pltpu.SemaphoreType.DMA((2,)),
                        pltpu.SemaphoreType.DMA((2,))],
        compiler_params=_cparams(("arbitrary",)),
        name="moe_combine",
    )(slot3, ys)


def _pair_tables():
    lo, hi = [], []
    for g in range(N_GROUPS):
        for a in range(EXP_PER_GROUP):
            for b in range(a + 1, EXP_PER_GROUP):
                lo.append(g * EXP_PER_GROUP + a)
                hi.append(g * EXP_PER_GROUP + b)
    return jnp.asarray(lo, jnp.int32), jnp.asarray(hi, jnp.int32)


def _tile_plan(counts, n_tiles):
    t = EXPERT_ROWS
    counts = counts[0, :N_CLASSES].astype(jnp.int32)
    tiles_per_class = (counts + t - 1) // t
    tile_end = jnp.cumsum(tiles_per_class)
    class_start = (tile_end - tiles_per_class) * t
    n_used = tile_end[-1:]
    src = jnp.minimum(jnp.arange(n_tiles, dtype=jnp.int32), n_used - 1)
    tile_class = jnp.sum((tile_end[None, :] <= src[:, None]).astype(jnp.int32), axis=1)
    pair_lo, pair_hi = _pair_tables()
    return class_start, tile_end, (n_used, src, pair_lo[tile_class], pair_hi[tile_class])


def kernel(x, ln_in_g, ln_in_b, w_in, gate_b, lru_conv_w, lru_conv_b, lru_wa, lru_ba, lru_wx, lru_bx, lru_lambda, sc_conv_w, w_branch_sb, w_branch_lru, w_branch_sc, w_out, ln1_g, ln1_b, w_group, group_bias, w_expert_router, expert_bias, w_gate, w_up, w_down, ln2_g, ln2_b):
    b, s, d = x.shape
    n = b * s
    params = dict(gate_b=gate_b, lru_conv_w=lru_conv_w, lru_conv_b=lru_conv_b, lru_wa=lru_wa,
                  lru_ba=lru_ba, lru_wx=lru_wx, lru_bx=lru_bx, lru_lambda=lru_lambda,
                  sc_conv_w=sc_conv_w, w_branch_sb=w_branch_sb, w_branch_lru=w_branch_lru,
                  w_branch_sc=w_branch_sc, w_out=w_out, ln1_g=ln1_g, ln1_b=ln1_b)
    h = _entry_ln(x.reshape(n, d), ln_in_g, ln_in_b)
    for l in range(w_in.shape[0]):
        qkv, branch, gates = _inproj(h, w_in[l].astype(BF16))
        y_sb = _sb_attention(qkv.reshape(b, s, -1))
        h1 = _mixer_tail(h.reshape(b, s, d), y_sb, branch.reshape(b, s, -1),
                         gates.reshape(b, s, -1), params, l)
        h1 = h1.reshape(n, d)
        pad = LANES - N_EXPERTS - N_GROUPS
        w_router = jnp.concatenate(
            [w_expert_router[l], w_group[l], jnp.zeros((d, pad), F32)], axis=1).astype(F32)
        bias = jnp.concatenate(
            [expert_bias[l], group_bias[l], jnp.zeros((pad,), F32)]).reshape(1, LANES).astype(F32)
        info, key3, counts = _route(h1, w_router, bias)
        n_tiles = n // EXPERT_ROWS + N_CLASSES
        class_start, tile_end, tile_meta = _tile_plan(counts, n_tiles)
        slot3 = _slots(class_start, key3)
        xs = _dispatch(tile_end, tile_meta[0], slot3, h1, info, n_tiles * EXPERT_ROWS)
        ys = _experts(tile_meta, xs, w_gate[l].astype(BF16), w_up[l].astype(BF16),
                      w_down[l].astype(BF16), ln2_g[l], ln2_b[l])
        h = _combine(slot3, ys, n)
    return h.reshape(b, s, d)
```

```python
import functools
import math

import jax
import jax.numpy as jnp
from jax import lax
from jax.experimental import pallas as pl
from jax.experimental.pallas import tpu as pltpu

F32 = jnp.float32
BF16 = jnp.bfloat16

SB_HEADS = 8
SB_HEAD_DIM = 64
SB_WIDTH = SB_HEADS * SB_HEAD_DIM
LRU_WIDTH = 512
LRU_BLOCKS = 8
LRU_C = 8.0
SC_WIDTH = 512
N_GROUPS = 4
EXP_PER_GROUP = 4
N_EXPERTS = N_GROUPS * EXP_PER_GROUP
D_EXPERT = 512
DEPTH = 2
ALPHA = (2 * DEPTH) ** 0.25
LN_EPS = 1e-5

LANES = 128
SUBLANES = 8
VMEM_LIMIT_BYTES = 56 * 1024 * 1024

LN_ROWS = 512
PROJ_ROWS = 512
PROJ_CHUNK = 512
ATT_Q = 256
ATT_K = 256
MIX_ROWS = 256
ROUTE_ROWS = 512
DISPATCH_ROWS = 512
EXPERT_ROWS = 256
ROW_COPY_UNROLL = 8

PAIRS_PER_GROUP = EXP_PER_GROUP * (EXP_PER_GROUP - 1) // 2
N_CLASSES = N_GROUPS * PAIRS_PER_GROUP
INFO_W_LOW, INFO_W_HIGH = 0, 1
KEY_DIGIT = 256
KEY_SHIFT = 16


def _cparams(sem):
    return pltpu.CompilerParams(dimension_semantics=sem, vmem_limit_bytes=VMEM_LIMIT_BYTES)


def _const_spec(shape):
    nd = len(shape)
    return pl.BlockSpec(shape, lambda *_: (0,) * nd)


def _layer_norm(y, g, b):
    mu = jnp.mean(y, axis=-1, keepdims=True)
    d = y - mu
    var = jnp.mean(d * d, axis=-1, keepdims=True)
    return d * lax.rsqrt(var + LN_EPS) * g + b


def _ln_kernel(x_ref, g_ref, b_ref, o_ref):
    o_ref[...] = _layer_norm(x_ref[...], g_ref[...], b_ref[...])


def _entry_ln(x2, g, b):
    n, d = x2.shape
    return pl.pallas_call(
        _ln_kernel,
        out_shape=jax.ShapeDtypeStruct((n, d), F32),
        grid=(n // LN_ROWS,),
        in_specs=[pl.BlockSpec((LN_ROWS, d), lambda i: (i, 0)),
                  _const_spec((1, d)), _const_spec((1, d))],
        out_specs=pl.BlockSpec((LN_ROWS, d), lambda i: (i, 0)),
        compiler_params=_cparams(("parallel",)),
        name="entry_ln",
    )(x2, g.reshape(1, d), b.reshape(1, d))


QKV_COLS = 3 * SB_WIDTH
BRANCH_COLS = 2 * LRU_WIDTH + 3 * SC_WIDTH


def _inproj_kernel(h_ref, w_ref, qkv_ref, branch_ref, gates_ref):
    hb = h_ref[...].astype(BF16)
    col = 0
    for out_ref in (qkv_ref, branch_ref, gates_ref):
        for j in range(out_ref.shape[1] // PROJ_CHUNK):
            acc = jnp.dot(hb, w_ref[:, col:col + PROJ_CHUNK], preferred_element_type=F32)
            if col < SB_WIDTH:
                acc = acc * (SB_HEAD_DIM ** -0.5 * LOG2_E)
            out_ref[:, j * PROJ_CHUNK:(j + 1) * PROJ_CHUNK] = acc.astype(out_ref.dtype)
            col += PROJ_CHUNK


def _inproj(h2, w_in_b):
    n, d = h2.shape
    cols = w_in_b.shape[1]
    widths = (QKV_COLS, BRANCH_COLS, cols - QKV_COLS - BRANCH_COLS)
    dtypes = (BF16, F32, F32)
    return pl.pallas_call(
        _inproj_kernel,
        out_shape=tuple(jax.ShapeDtypeStruct((n, w), t) for w, t in zip(widths, dtypes)),
        grid=(n // PROJ_ROWS,),
        in_specs=[pl.BlockSpec((PROJ_ROWS, d), lambda i: (i, 0)),
                  pl.BlockSpec((d, cols), lambda i: (0, 0), pipeline_mode=pl.Buffered(1))],
        out_specs=tuple(pl.BlockSpec((PROJ_ROWS, w), lambda i: (i, 0)) for w in widths),
        compiler_params=_cparams(("parallel",)),
        name="in_proj",
    )(h2, w_in_b)


LOG2_E = math.log2(math.e)
PASS_LOG2_FLOOR = -151.0


HEADS_PER_STEP = LANES // SB_HEAD_DIM


def _per_head(x):
    lane = lax.broadcasted_iota(jnp.int32, x.shape, 1)
    return jnp.concatenate(
        [jnp.where((lane >= h * SB_HEAD_DIM) & (lane < (h + 1) * SB_HEAD_DIM), x, jnp.zeros_like(x))
         for h in range(HEADS_PER_STEP)], axis=0)


MASKED_SCORE = -1e30


def _neg_abs(x):
    bits = lax.bitcast_convert_type(x, jnp.uint32) | jnp.uint32(0x80000000)
    return lax.bitcast_convert_type(bits, F32)


def _sb_block(q, k_j, v_j, neg_tri2, carries, mask):
    nq, kk = q.shape[0], k_j.shape[0]
    z = lax.dot_general(q, _per_head(k_j), (((1,), (1,)), ((), ())), preferred_element_type=F32)
    scores, split = [], []
    for h in range(HEADS_PER_STEP):
        z_h = z[:, h * kk:(h + 1) * kk]
        if mask is not None:
            z_h = jnp.where(mask, z_h, MASKED_SCORE)
        drop = jnp.maximum(z_h, 0.0) + jnp.log2(1.0 + jnp.exp2(_neg_abs(z_h)))
        hi = drop.astype(BF16)
        lo = (drop - hi.astype(F32)).astype(BF16)
        scores.append(z_h)
        split.append(jnp.concatenate([hi, lo], axis=1))
    log_pass = jnp.dot(jnp.concatenate(split, axis=0), neg_tri2, preferred_element_type=F32)
    w, new_carries = [], []
    for h in range(HEADS_PER_STEP):
        lp_h = log_pass[h * nq:(h + 1) * nq]
        w.append(jnp.exp2(scores[h] + lp_h + carries[h]).astype(BF16))
        new_carries.append(carries[h] + lp_h[:, 0:1])
    pv = jnp.dot(jnp.concatenate(w, axis=1), _per_head(v_j), preferred_element_type=F32)
    return pv, new_carries


def _attn_kernel(q_ref, k_ref, v_ref, u2_ref, o_ref):
    qi = pl.program_id(2)
    u2 = u2_ref[...]
    q = q_ref[0]
    row = lax.broadcasted_iota(jnp.int32, (ATT_Q, ATT_K), 0)
    col = lax.broadcasted_iota(jnp.int32, (ATT_Q, ATT_K), 1)
    diag_mask = col < row

    def sweep(j, acc, carries, mask):
        rows = pl.ds(pl.multiple_of(j * ATT_K, ATT_K), ATT_K)
        pv, carries = _sb_block(q, k_ref[0, rows, :], v_ref[0, rows, :], u2, carries, mask)
        return acc + pv, carries

    zero = [jnp.zeros((ATT_Q, 1), F32) for _ in range(HEADS_PER_STEP)]
    acc, carries = sweep(qi, jnp.zeros((ATT_Q, LANES), F32), zero, diag_mask)
    no_prev = jnp.where(qi > 0, 0.0, -1e30)
    acc, carries = sweep(jnp.maximum(qi - 1, 0), acc, [c + no_prev for c in carries], None)

    def largest(carries):
        return jnp.max(functools.reduce(jnp.maximum, carries))

    def cond(state):
        j, _, _, live = state
        return jnp.logical_and(j >= 0, live > PASS_LOG2_FLOOR)

    def body(state):
        j, acc, carries, _ = state
        acc, carries = sweep(j, acc, carries, None)
        return j - 1, acc, carries, largest(carries)

    _, acc, _, _ = lax.while_loop(cond, body, (qi - 2, acc, carries, largest(carries)))
    o_ref[0] = acc.astype(o_ref.dtype)


def _sb_attention(qkv3):
    b, s, _ = qkv3.shape
    pairs = SB_WIDTH // LANES
    neg_tri = -(jnp.arange(ATT_K)[:, None] >= jnp.arange(ATT_K)[None, :]).astype(BF16)
    u2 = jnp.concatenate([neg_tri, neg_tri], axis=0)
    return pl.pallas_call(
        _attn_kernel,
        out_shape=jax.ShapeDtypeStruct((b, s, SB_WIDTH), BF16),
        grid=(b, pairs, s // ATT_Q),
        in_specs=[pl.BlockSpec((1, ATT_Q, LANES), lambda bi, p, qi: (bi, qi, p)),
                  pl.BlockSpec((1, s, LANES), lambda bi, p, qi: (bi, 0, pairs + p)),
                  pl.BlockSpec((1, s, LANES), lambda bi, p, qi: (bi, 0, 2 * pairs + p)),
                  _const_spec((2 * ATT_K, ATT_K))],
        out_specs=pl.BlockSpec((1, ATT_Q, LANES), lambda bi, p, qi: (bi, qi, p)),
        compiler_params=_cparams(("parallel", "parallel", "arbitrary")),
        name="sb_attention",
    )(qkv3, qkv3, qkv3, u2)


def _shift_rows(x, tail, j):
    if j == 0:
        return x
    xr = pltpu.roll(x, j, axis=0)
    tr = pltpu.roll(tail, j, axis=0)
    rows = lax.broadcasted_iota(jnp.int32, tail.shape, 0)
    head = jnp.where(rows < j, tr, xr[:SUBLANES])
    return jnp.concatenate([head, xr[SUBLANES:]], axis=0)


def _causal_conv(x, tail, w):
    k = w.shape[0]
    out = w[k - 1:k] * x
    for j in range(1, k):
        out = out + w[k - 1 - j:k - j] * _shift_rows(x, tail, j)
    return out


def _sigmoid(x):
    return 1.0 / (1.0 + jnp.exp(-x))


def _gelu_tanh(x):
    return 0.5 * x * (1.0 + jnp.tanh(math.sqrt(2.0 / math.pi) * (x + 0.044715 * (x * x * x))))


def _linear_scan(a, b):
    t = a.shape[0]
    rows = lax.broadcasted_iota(jnp.int32, a.shape, 0)
    d = 1
    while d < t:
        keep = rows >= d
        a_prev = jnp.where(keep, pltpu.roll(a, d, axis=0), 1.0)
        b_prev = jnp.where(keep, pltpu.roll(b, d, axis=0), 0.0)
        b = a * b_prev + b
        a = a * a_prev
        d *= 2
    return a, b


def _mix_kernel(h_ref, ysb_ref, lin_ref, lgate_ref, scb_ref, scc_ref, sch_ref,
                gsb_ref, glru_ref, gsc_ref, gate_b_ref, lconv_w_ref, lconv_b_ref,
                wa_ref, ba_ref, wx_ref, bx_ref, lam_ref, scw_ref,
                wsb_ref, wlru_ref, wsc_ref, wout_ref, ln_g_ref, ln_b_ref,
                out_ref, lin_tail, sc_tail, h_state):
    t = MIX_ROWS

    @pl.when(pl.program_id(1) == 0)
    def _():
        lin_tail[...] = jnp.zeros_like(lin_tail)
        sc_tail[...] = jnp.zeros_like(sc_tail)
        h_state[...] = jnp.zeros_like(h_state)

    x = lin_ref[0]
    u = _causal_conv(x, lin_tail[...], lconv_w_ref[...]) + lconv_b_ref[...]
    lin_tail[...] = x[t - SUBLANES:]
    ub = u.astype(BF16)
    r = _sigmoid(jnp.dot(ub, wa_ref[...], preferred_element_type=F32) + ba_ref[...])
    i = _sigmoid(jnp.dot(ub, wx_ref[...], preferred_element_type=F32) + bx_ref[...])
    lam = lam_ref[...]
    softplus_neg_lam = jnp.maximum(-lam, 0.0) + jnp.log(1.0 + jnp.exp(-jnp.abs(lam)))
    log_a = (-LRU_C) * r * softplus_neg_lam
    a = jnp.exp(log_a)
    drive = jnp.sqrt(1.0 - jnp.exp(2.0 * log_a)) * (i * u)
    a_cum, hs = _linear_scan(a, drive)
    hs = hs + a_cum * h_state[...]
    h_state[...] = hs[t - 1:]
    y_lru = _gelu_tanh(lgate_ref[0]) * hs

    p = scc_ref[0] * sch_ref[0]
    y_sc = scb_ref[0] * _causal_conv(p, sc_tail[...], scw_ref[...])
    sc_tail[...] = p[t - SUBLANES:]

    gb = gate_b_ref[...]
    merged = (_sigmoid(gsb_ref[0] + gb[0:1])
              * jnp.dot(ysb_ref[0], wsb_ref[...], preferred_element_type=F32)
              + _sigmoid(glru_ref[0] + gb[1:2])
              * jnp.dot(y_lru.astype(BF16), wlru_ref[...], preferred_element_type=F32)
              + _sigmoid(gsc_ref[0] + gb[2:3])
              * jnp.dot(y_sc.astype(BF16), wsc_ref[...], preferred_element_type=F32))
    mix = jnp.dot(merged.astype(BF16), wout_ref[...], preferred_element_type=F32)
    out_ref[0] = _layer_norm(ALPHA * h_ref[0] + mix, ln_g_ref[...], ln_b_ref[...])


def _block_diag(w):
    hh, ii, jj = w.shape
    eye = jnp.eye(hh, dtype=w.dtype)
    return (eye[:, None, :, None] * w[:, :, None, :]).reshape(hh * ii, hh * jj)


def _mixer_tail(h3, ysb3, branch3, gates3, p, l):
    b, s, d = h3.shape
    t = MIX_ROWS
    assert LRU_WIDTH == SC_WIDTH

    def row_spec(width, col):
        return pl.BlockSpec((1, t, width), lambda bi, si: (bi, si, col))

    vec = lambda v: v.reshape(1, -1).astype(F32)
    in_specs = [row_spec(d, 0), row_spec(SB_WIDTH, 0)]
    in_specs += [row_spec(LRU_WIDTH, c) for c in range(5)]
    in_specs += [row_spec(d, g) for g in range(3)]
    weights = [
        p["gate_b"][l].astype(F32),
        p["lru_conv_w"][l].astype(F32), vec(p["lru_conv_b"][l]),
        _block_diag(p["lru_wa"][l]).astype(BF16), vec(p["lru_ba"][l]),
        _block_diag(p["lru_wx"][l]).astype(BF16), vec(p["lru_bx"][l]),
        vec(p["lru_lambda"][l]), p["sc_conv_w"][l].astype(F32),
        p["w_branch_sb"][l].astype(BF16), p["w_branch_lru"][l].astype(BF16),
        p["w_branch_sc"][l].astype(BF16), p["w_out"][l].astype(BF16),
        vec(p["ln1_g"][l]), vec(p["ln1_b"][l]),
    ]
    in_specs += [_const_spec(w.shape) for w in weights]
    return pl.pallas_call(
        _mix_kernel,
        out_shape=jax.ShapeDtypeStruct((b, s, d), F32),
        grid=(b, s // t),
        in_specs=in_specs,
        out_specs=pl.BlockSpec((1, t, d), lambda bi, si: (bi, si, 0)),
        scratch_shapes=[pltpu.VMEM((SUBLANES, LRU_WIDTH), F32),
                        pltpu.VMEM((SUBLANES, SC_WIDTH), F32),
                        pltpu.VMEM((1, LRU_WIDTH), F32)],
        compiler_params=_cparams(("parallel", "arbitrary")),
        name="mixer_tail",
    )(h3, ysb3, *([branch3] * 5), *([gates3] * 3), *weights)


def _first_argmax(vals, lane):
    m = jnp.max(vals, axis=-1, keepdims=True)
    idx = jnp.min(jnp.where(vals == m, lane, LANES), axis=-1, keepdims=True)
    return m, idx


def _route_kernel(h_ref, w_ref, bias_ref, tri_ref, pick_ref, info_ref, key_ref, counts_ref, running):
    @pl.when(pl.program_id(0) == 0)
    def _():
        running[...] = jnp.zeros_like(running)

    h = h_ref[...]
    h_hi = h.astype(BF16)
    h_lo = (h - h_hi.astype(F32)).astype(BF16)
    w = w_ref[...]
    w_hi = w.astype(BF16)
    w_lo = (w - w_hi.astype(F32)).astype(BF16)
    logits = jnp.dot(jnp.concatenate([h_hi, h_lo, h_hi], axis=1),
                     jnp.concatenate([w_hi, w_hi, w_lo], axis=0), preferred_element_type=F32)
    lane = lax.broadcasted_iota(jnp.int32, logits.shape, 1)
    neg = -jnp.inf
    is_group = (lane >= N_EXPERTS) & (lane < N_EXPERTS + N_GROUPS)
    biased = logits + bias_ref[...]
    g_max = jnp.max(jnp.where(is_group, logits, neg), axis=-1, keepdims=True)
    g_exp = jnp.where(is_group, jnp.exp(logits - g_max), 0.0)
    g_den = jnp.sum(g_exp, axis=-1, keepdims=True)
    _, g_lane = _first_argmax(jnp.where(is_group, biased, neg), lane)
    g_sel = g_lane - N_EXPERTS
    g_prob = jnp.sum(jnp.where(lane == g_lane, g_exp, 0.0), axis=-1, keepdims=True) / g_den
    in_group = (lane >= g_sel * EXP_PER_GROUP) & (lane < (g_sel + 1) * EXP_PER_GROUP)
    cand = jnp.where(in_group, biased, neg)
    _, e1 = _first_argmax(cand, lane)
    _, e2 = _first_argmax(jnp.where(lane == e1, neg, cand), lane)
    l1 = jnp.sum(jnp.where(lane == e1, logits, 0.0), axis=-1, keepdims=True)
    l2 = jnp.sum(jnp.where(lane == e2, logits, 0.0), axis=-1, keepdims=True)
    m = jnp.maximum(l1, l2)
    x1 = jnp.exp(l1 - m)
    x2 = jnp.exp(l2 - m)
    scale = g_prob / (x1 + x2)
    w1 = x1 * scale
    w2 = x2 * scale
    first_low = e1 < e2
    a = jnp.where(first_low, e1, e2) - g_sel * EXP_PER_GROUP
    b = jnp.where(first_low, e2, e1) - g_sel * EXP_PER_GROUP
    cls = g_sel * PAIRS_PER_GROUP + ((a * (2 * EXP_PER_GROUP - 1 - a)) >> 1) + (b - a - 1)
    onehot = lane == cls
    earlier = jnp.dot(tri_ref[...], jnp.where(onehot, 1.0, 0.0).astype(BF16),
                      preferred_element_type=F32) + running[...]
    rank = jnp.sum(jnp.where(onehot, earlier, 0.0), axis=-1, keepdims=True)
    last = ROUTE_ROWS - 1
    running[...] = earlier[last:] + jnp.where(onehot[last:], 1.0, 0.0)
    counts_ref[...] = running[...]
    info = jnp.where(lane == INFO_W_LOW, jnp.where(first_low, w1, w2), 0.0)
    info_ref[...] = jnp.where(lane == INFO_W_HIGH, jnp.where(first_low, w2, w1), info)
    rank_hi = jnp.floor(rank * (1.0 / KEY_DIGIT))
    digits = jnp.where(lane == 0, rank - rank_hi * KEY_DIGIT, 0.0)
    digits = jnp.where(lane == 1, rank_hi, digits)
    digits = jnp.where(lane == 2, cls.astype(F32), digits)
    keys = lax.dot_general(pick_ref[...], digits.astype(BF16), (((1,), (1,)), ((), ())),
                           preferred_element_type=F32)
    key_ref[0] = keys[0:1].astype(jnp.int32)


def _route(h2, w_router, bias):
    n, d = h2.shape
    t = ROUTE_ROWS
    tri = (jnp.arange(t)[:, None] > jnp.arange(t)[None, :]).astype(BF16)
    pick = jnp.zeros((SUBLANES, LANES), F32).at[0, :3].set(
        jnp.asarray([1.0, KEY_DIGIT, KEY_DIGIT * KEY_DIGIT], F32)).astype(BF16)
    return pl.pallas_call(
        _route_kernel,
        out_shape=(jax.ShapeDtypeStruct((n, LANES), F32),
                   jax.ShapeDtypeStruct((n // t, 1, t), jnp.int32),
                   jax.ShapeDtypeStruct((1, LANES), F32)),
        grid=(n // t,),
        in_specs=[pl.BlockSpec((t, d), lambda i: (i, 0)),
                  _const_spec((d, LANES)), _const_spec((1, LANES)), _const_spec((t, t)),
                  _const_spec((SUBLANES, LANES))],
        out_specs=(pl.BlockSpec((t, LANES), lambda i: (i, 0)),
                   pl.BlockSpec((1, 1, t), lambda i: (i, 0, 0)),
                   _const_spec((1, LANES))),
        scratch_shapes=[pltpu.VMEM((1, LANES), F32)],
        compiler_params=_cparams(("arbitrary",)),
        name="moe_route",
    )(h2, w_router, bias, tri, pick)


def _slot_kernel(start_ref, key_ref, slot_ref):
    key = key_ref[...]
    cls = key >> KEY_SHIFT
    slot = key & (KEY_DIGIT * KEY_DIGIT - 1)
    for c in range(N_CLASSES):
        slot = slot + jnp.where(cls == c, start_ref[c], 0)
    slot_ref[...] = slot


def _slots(class_start, key3):
    g, _, t = key3.shape
    return pl.pallas_call(
        _slot_kernel,
        out_shape=jax.ShapeDtypeStruct(key3.shape, jnp.int32),
        grid_spec=pltpu.PrefetchScalarGridSpec(
            num_scalar_prefetch=1, grid=(1,),
            in_specs=[pl.BlockSpec((g, 1, t), lambda i, s: (0, 0, 0))],
            out_specs=pl.BlockSpec((g, 1, t), lambda i, s: (0, 0, 0))),
        compiler_params=_cparams(("arbitrary",)),
        name="moe_slots",
    )(class_start, key3)


def _for_rows(n_rows, fn):
    def body(g, c):
        for j in range(ROW_COPY_UNROLL):
            fn(g * ROW_COPY_UNROLL + j, j % 2)
        return c

    lax.fori_loop(0, n_rows // ROW_COPY_UNROLL, body, 0)


def _dispatch_kernel(tile_end_ref, n_used_ref, slot_ref, h_ref, info_ref, xs_ref, rows, sems, zsem):
    i = pl.program_id(0)
    last = pl.num_programs(0) - 1
    p = i % 2
    d = h_ref.shape[1]
    t = DISPATCH_ROWS
    te = EXPERT_ROWS

    def row_copy(buf, r):
        return pltpu.make_async_copy(rows.at[buf, pl.ds(r, 1)],
                                     xs_ref.at[pl.ds(slot_ref[0, 0, r], 1)], sems.at[buf])

    @pl.when(i == 0)
    def _():
        rows[1, :te] = jnp.zeros((te, d + LANES), F32)

        def zero_tile(tile):
            return pltpu.make_async_copy(rows.at[1, pl.ds(0, te)],
                                         xs_ref.at[pl.ds(tile * te, te)], zsem)

        prev_end = 0
        for c in range(N_CLASSES):
            end = tile_end_ref[c]

            @pl.when(end > prev_end)
            def _(end=end):
                cp = zero_tile(end - 1)
                cp.start()
                cp.wait()
            prev_end = end

        def zero_tail(tile, c):
            cp = zero_tile(tile)
            cp.start()
            cp.wait()
            return c

        lax.fori_loop(n_used_ref[0], xs_ref.shape[0] // te, zero_tail, 0)

    rows[p, :, :d] = h_ref[...]
    rows[p, :, d:] = info_ref[...]
    _for_rows(t, lambda r, q: row_copy(p, r).start(priority=q))

    def row_wait(buf):
        pltpu.make_async_copy(rows.at[buf, pl.ds(0, 1)], xs_ref.at[pl.ds(0, 1)],
                              sems.at[buf]).wait()

    @pl.when(i > 0)
    def _():
        _for_rows(t, lambda r, q: row_wait(1 - p))

    @pl.when(i == last)
    def _():
        _for_rows(t, lambda r, q: row_wait(p))


def _dispatch(tile_end, n_used, slot3, h2, info, n_slots):
    n, d = h2.shape
    t = DISPATCH_ROWS
    return pl.pallas_call(
        _dispatch_kernel,
        out_shape=jax.ShapeDtypeStruct((n_slots, d + LANES), F32),
        grid_spec=pltpu.PrefetchScalarGridSpec(
            num_scalar_prefetch=2, grid=(n // t,),
            in_specs=[pl.BlockSpec((1, 1, t), lambda i, *_: (i, 0, 0), memory_space=pltpu.SMEM),
                      pl.BlockSpec((t, d), lambda i, *_: (i, 0)),
                      pl.BlockSpec((t, LANES), lambda i, *_: (i, 0))],
            out_specs=pl.BlockSpec(memory_space=pl.ANY),
            scratch_shapes=[pltpu.VMEM((2, t, d + LANES), F32), pltpu.SemaphoreType.DMA((2,)),
                            pltpu.SemaphoreType.DMA(())]),
        compiler_params=_cparams(("arbitrary",)),
        name="moe_dispatch",
    )(tile_end, n_used, slot3, h2, info)


def _expert_kernel(n_used_ref, src_ref, lo_ref, hi_ref, xs_ref, wg_lo, wg_hi, wu_lo, wu_hi,
                   wd_lo, wd_hi, ln_g_ref, ln_b_ref, ys_ref):
    del src_ref, lo_ref, hi_ref

    @pl.when(pl.program_id(0) < n_used_ref[0])
    def _():
        d = ys_ref.shape[1]
        x = xs_ref[:, :d]
        xb = x.astype(BF16)
        y = None
        for wg, wu, wd, lane in ((wg_lo, wu_lo, wd_lo, INFO_W_LOW), (wg_hi, wu_hi, wd_hi, INFO_W_HIGH)):
            gate = jnp.dot(xb, wg[0], preferred_element_type=F32)
            up = jnp.dot(xb, wu[0], preferred_element_type=F32)
            act = gate * _sigmoid(gate) * up * xs_ref[:, d + lane:d + lane + 1]
            part = jnp.dot(act.astype(BF16), wd[0], preferred_element_type=F32)
            y = part if y is None else y + part
        ys_ref[...] = _layer_norm(ALPHA * x + y, ln_g_ref[...], ln_b_ref[...])

    @pl.when(pl.program_id(0) >= n_used_ref[0])
    def _():
        ys_ref[...] = jnp.zeros_like(ys_ref)


def _experts(tile_meta, xs, w_gate_b, w_up_b, w_down_b, ln_g, ln_b):
    n_slots, cols = xs.shape
    d = cols - LANES
    f = w_gate_b.shape[2]
    t = EXPERT_ROWS
    x_map = lambda i, n_used, src, lo, hi: (src[i], 0)
    lo_map = lambda i, n_used, src, lo, hi: (lo[i], 0, 0)
    hi_map = lambda i, n_used, src, lo, hi: (hi[i], 0, 0)
    const = lambda i, *_: (0, 0)
    grid_spec = pltpu.PrefetchScalarGridSpec(
        num_scalar_prefetch=4,
        grid=(n_slots // t,),
        in_specs=[pl.BlockSpec((t, cols), x_map),
                  pl.BlockSpec((1, d, f), lo_map), pl.BlockSpec((1, d, f), hi_map),
                  pl.BlockSpec((1, d, f), lo_map), pl.BlockSpec((1, d, f), hi_map),
                  pl.BlockSpec((1, f, d), lo_map), pl.BlockSpec((1, f, d), hi_map),
                  pl.BlockSpec((1, d), const), pl.BlockSpec((1, d), const)],
        out_specs=pl.BlockSpec((t, d), lambda i, *_: (i, 0)),
    )
    return pl.pallas_call(
        _expert_kernel,
        out_shape=jax.ShapeDtypeStruct((n_slots, d), F32),
        grid_spec=grid_spec,
        compiler_params=_cparams(("arbitrary",)),
        name="moe_experts",
    )(*tile_meta, xs, w_gate_b, w_gate_b, w_up_b, w_up_b, w_down_b, w_down_b,
      ln_g.reshape(1, d), ln_b.reshape(1, d))


def _combine_kernel(slot_ref, ys_ref, o_ref, buf, gsems, osems):
    i = pl.program_id(0)
    last = pl.num_programs(0) - 1
    p = i % 2
    t = DISPATCH_ROWS

    def tile_out(b, step):
        return pltpu.make_async_copy(buf.at[b], o_ref.at[pl.ds(step * t, t)], osems.at[b])

    def row_wait(b):
        pltpu.make_async_copy(ys_ref.at[pl.ds(0, 1)], buf.at[b, pl.ds(0, 1)], gsems.at[b]).wait()

    @pl.when(i >= 2)
    def _():
        tile_out(p, i - 2).wait()

    _for_rows(t, lambda r, q: pltpu.make_async_copy(
        ys_ref.at[pl.ds(slot_ref[0, 0, r], 1)], buf.at[p, pl.ds(r, 1)],
        gsems.at[p]).start(priority=q))

    @pl.when(i > 0)
    def _():
        _for_rows(t, lambda r, q: row_wait(1 - p))
        tile_out(1 - p, i - 1).start()

    @pl.when(i == last)
    def _():
        _for_rows(t, lambda r, q: row_wait(p))
        tile_out(p, i).start()
        tile_out(p, i).wait()

        @pl.when(i > 0)
        def _():
            tile_out(1 - p, i - 1).wait()


def _combine(slot3, ys, n):
    d = ys.shape[1]
    t = DISPATCH_ROWS
    return pl.pallas_call(
        _combine_kernel,
        out_shape=jax.ShapeDtypeStruct((n, d), F32),
        grid=(n // t,),
        in_specs=[pl.BlockSpec((1, 1, t), lambda i: (i, 0, 0), memory_space=pltpu.SMEM),
                  pl.BlockSpec(memory_space=pl.ANY)],
        out_specs=pl.BlockSpec(memory_space=pl.ANY),
        scratch_shapes=[pltpu.VMEM((2, t, d), F32), pltpu.SemaphoreType.DMA((2,)),
                        pltpu.SemaphoreType.DMA((2,))],
        compiler_params=_cparams(("arbitrary",)),
        name="moe_combine",
    )(slot3, ys)


def _pair_tables():
    lo, hi = [], []
    for g in range(N_GROUPS):
        for a in range(EXP_PER_GROUP):
            for b in range(a + 1, EXP_PER_GROUP):
                lo.append(g * EXP_PER_GROUP + a)
                hi.append(g * EXP_PER_GROUP + b)
    return jnp.asarray(lo, jnp.int32), jnp.asarray(hi, jnp.int32)


def _tile_plan(counts, n_tiles):
    t = EXPERT_ROWS
    counts = counts[0, :N_CLASSES].astype(jnp.int32)
    tiles_per_class = (counts + t - 1) // t
    tile_end = jnp.cumsum(tiles_per_class)
    class_start = (tile_end - tiles_per_class) * t
    n_used = tile_end[-1:]
    src = jnp.minimum(jnp.arange(n_tiles, dtype=jnp.int32), n_used - 1)
    tile_class = jnp.sum((tile_end[None, :] <= src[:, None]).astype(jnp.int32), axis=1)
    pair_lo, pair_hi = _pair_tables()
    return class_start, tile_end, (n_used, src, pair_lo[tile_class], pair_hi[tile_class])


def kernel(x, ln_in_g, ln_in_b, w_in, gate_b, lru_conv_w, lru_conv_b, lru_wa, lru_ba, lru_wx, lru_bx, lru_lambda, sc_conv_w, w_branch_sb, w_branch_lru, w_branch_sc, w_out, ln1_g, ln1_b, w_group, group_bias, w_expert_router, expert_bias, w_gate, w_up, w_down, ln2_g, ln2_b):
    b, s, d = x.shape
    n = b * s
    params = dict(gate_b=gate_b, lru_conv_w=lru_conv_w, lru_conv_b=lru_conv_b, lru_wa=lru_wa,
                  lru_ba=lru_ba, lru_wx=lru_wx, lru_bx=lru_bx, lru_lambda=lru_lambda,
                  sc_conv_w=sc_conv_w, w_branch_sb=w_branch_sb, w_branch_lru=w_branch_lru,
                  w_branch_sc=w_branch_sc, w_out=w_out, ln1_g=ln1_g, ln1_b=ln1_b)
    h = _entry_ln(x.reshape(n, d), ln_in_g, ln_in_b)
    for l in range(w_in.shape[0]):
        qkv, branch, gates = _inproj(h, w_in[l].astype(BF16))
        y_sb = _sb_attention(qkv.reshape(b, s, -1))
        h1 = _mixer_tail(h.reshape(b, s, d), y_sb, branch.reshape(b, s, -1),
                         gates.reshape(b, s, -1), params, l)
        h1 = h1.reshape(n, d)
        pad = LANES - N_EXPERTS - N_GROUPS
        w_router = jnp.concatenate(
            [w_expert_router[l], w_group[l], jnp.zeros((d, pad), F32)], axis=1).astype(F32)
        bias = jnp.concatenate(
            [expert_bias[l], group_bias[l], jnp.zeros((pad,), F32)]).reshape(1, LANES).astype(F32)
        info, key3, counts = _route(h1, w_router, bias)
        n_tiles = n // EXPERT_ROWS + N_CLASSES
        class_start, tile_end, tile_meta = _tile_plan(counts, n_tiles)
        slot3 = _slots(class_start, key3)
        xs = _dispatch(tile_end, tile_meta[0], slot3, h1, info, n_tiles * EXPERT_ROWS)
        ys = _experts(tile_meta, xs, w_gate[l].astype(BF16), w_up[l].astype(BF16),
                      w_down[l].astype(BF16), ln2_g[l], ln2_b[l])
        h = _combine(slot3, ys, n)
    return h.reshape(b, s, d)
```

```python
import functools
import math

import jax
import jax.numpy as jnp
from jax import lax
from jax.experimental import pallas as pl
from jax.experimental.pallas import tpu as pltpu

F32 = jnp.float32
BF16 = jnp.bfloat16

SB_HEADS = 8
SB_HEAD_DIM = 64
SB_WIDTH = SB_HEADS * SB_HEAD_DIM
LRU_WIDTH = 512
LRU_BLOCKS = 8
LRU_C = 8.0
SC_WIDTH = 512
N_GROUPS = 4
EXP_PER_GROUP = 4
N_EXPERTS = N_GROUPS * EXP_PER_GROUP
D_EXPERT = 512
DEPTH = 2
ALPHA = (2 * DEPTH) ** 0.25
LN_EPS = 1e-5

LANES = 128
SUBLANES = 8
VMEM_LIMIT_BYTES = 56 * 1024 * 1024

LN_ROWS = 512
PROJ_ROWS = 512
PROJ_CHUNK = 512
ATT_Q = 256
ATT_K = 256
MIX_ROWS = 256
ROUTE_ROWS = 512
DISPATCH_ROWS = 512
EXPERT_ROWS = 256
ROW_COPY_UNROLL = 8

PAIRS_PER_GROUP = EXP_PER_GROUP * (EXP_PER_GROUP - 1) // 2
N_CLASSES = N_GROUPS * PAIRS_PER_GROUP
INFO_W_LOW, INFO_W_HIGH = 0, 1
KEY_DIGIT = 256
KEY_SHIFT = 16


def _cparams(sem):
    return pltpu.CompilerParams(dimension_semantics=sem, vmem_limit_bytes=VMEM_LIMIT_BYTES)


def _const_spec(shape):
    nd = len(shape)
    return pl.BlockSpec(shape, lambda *_: (0,) * nd)


def _layer_norm(y, g, b):
    mu = jnp.mean(y, axis=-1, keepdims=True)
    d = y - mu
    var = jnp.mean(d * d, axis=-1, keepdims=True)
    return d * lax.rsqrt(var + LN_EPS) * g + b


def _ln_kernel(x_ref, g_ref, b_ref, o_ref):
    o_ref[...] = _layer_norm(x_ref[...], g_ref[...], b_ref[...])


def _entry_ln(x2, g, b):
    n, d = x2.shape
    return pl.pallas_call(
        _ln_kernel,
        out_shape=jax.ShapeDtypeStruct((n, d), F32),
        grid=(n // LN_ROWS,),
        in_specs=[pl.BlockSpec((LN_ROWS, d), lambda i: (i, 0)),
                  _const_spec((1, d)), _const_spec((1, d))],
        out_specs=pl.BlockSpec((LN_ROWS, d), lambda i: (i, 0)),
        compiler_params=_cparams(("parallel",)),
        name="entry_ln",
    )(x2, g.reshape(1, d), b.reshape(1, d))


QKV_COLS = 3 * SB_WIDTH
BRANCH_COLS = 2 * LRU_WIDTH + 3 * SC_WIDTH


def _sigmoid(x):
    return 0.5 * jnp.tanh(0.5 * x) + 0.5


def _gelu_tanh(x):
    return 0.5 * x * (1.0 + jnp.tanh(math.sqrt(2.0 / math.pi) * (x + 0.044715 * (x * x * x))))


def _inproj_kernel(h_ref, w_ref, gate_b_ref, qkv_ref, conv_ref, act_ref):
    hb = h_ref[...].astype(BF16)
    c = PROJ_CHUNK

    def proj(chunk):
        return jnp.dot(hb, w_ref[:, chunk * c:(chunk + 1) * c], preferred_element_type=F32)

    n_qkv = QKV_COLS // c
    for j in range(n_qkv):
        acc = proj(j)
        if j * c < SB_WIDTH:
            acc = acc * (SB_HEAD_DIM ** -0.5 * LOG2_E)
        qkv_ref[:, j * c:(j + 1) * c] = acc.astype(BF16)
    lru_in, lru_gate, sc_b, sc_c, sc_h = range(n_qkv, n_qkv + 5)
    conv_ref[:, 0:c] = proj(lru_in)
    conv_ref[:, c:2 * c] = proj(sc_b)
    conv_ref[:, 2 * c:3 * c] = proj(sc_c) * proj(sc_h)
    for j in range(gate_b_ref.shape[1] // c):
        acc = proj(sc_h + 1 + j) + gate_b_ref[:, j * c:(j + 1) * c]
        act_ref[:, j * c:(j + 1) * c] = _sigmoid(acc).astype(BF16)
    act_ref[:, gate_b_ref.shape[1]:] = _gelu_tanh(proj(lru_gate)).astype(BF16)


def _inproj(h2, w_in_b, layer, gate_b):
    n, d = h2.shape
    cols = w_in_b.shape[2]
    gate_cols = gate_b.size
    assert LRU_WIDTH == SC_WIDTH == PROJ_CHUNK and cols == QKV_COLS + BRANCH_COLS + gate_cols
    widths = (QKV_COLS, 3 * PROJ_CHUNK, gate_cols + LRU_WIDTH)
    dtypes = (BF16, F32, BF16)
    return pl.pallas_call(
        _inproj_kernel,
        out_shape=tuple(jax.ShapeDtypeStruct((n, w), t) for w, t in zip(widths, dtypes)),
        grid=(n // PROJ_ROWS,),
        in_specs=[pl.BlockSpec((PROJ_ROWS, d), lambda i: (i, 0)),
                  pl.BlockSpec((None, d, cols), lambda i: (layer, 0, 0),
                               pipeline_mode=pl.Buffered(1)),
                  _const_spec((1, gate_cols))],
        out_specs=tuple(pl.BlockSpec((PROJ_ROWS, w), lambda i: (i, 0)) for w in widths),
        compiler_params=_cparams(("parallel",)),
        name="in_proj",
    )(h2, w_in_b, gate_b.reshape(1, gate_cols).astype(F32))


LOG2_E = math.log2(math.e)
PASS_LOG2_FLOOR = -151.0


HEADS_PER_STEP = LANES // SB_HEAD_DIM


def _per_head(x):
    lane = lax.broadcasted_iota(jnp.int32, x.shape, 1)
    return jnp.concatenate(
        [jnp.where((lane >= h * SB_HEAD_DIM) & (lane < (h + 1) * SB_HEAD_DIM), x, jnp.zeros_like(x))
         for h in range(HEADS_PER_STEP)], axis=0)


MASKED_SCORE = -1e30


def _neg_abs(x):
    bits = lax.bitcast_convert_type(x, jnp.uint32) | jnp.uint32(0x80000000)
    return lax.bitcast_convert_type(bits, F32)


def _sb_block(q, k_j, v_j, neg_tri2, carries, mask):
    nq, kk = q.shape[0], k_j.shape[0]
    z = lax.dot_general(q, _per_head(k_j), (((1,), (1,)), ((), ())), preferred_element_type=F32)
    scores, split = [], []
    for h in range(HEADS_PER_STEP):
        z_h = z[:, h * kk:(h + 1) * kk]
        if mask is not None:
            z_h = jnp.where(mask, z_h, MASKED_SCORE)
        drop = jnp.maximum(z_h, 0.0) + jnp.log2(1.0 + jnp.exp2(_neg_abs(z_h)))
        hi = drop.astype(BF16)
        lo = (drop - hi.astype(F32)).astype(BF16)
        scores.append(z_h)
        split.append(jnp.concatenate([hi, lo], axis=1))
    log_pass = jnp.dot(jnp.concatenate(split, axis=0), neg_tri2, preferred_element_type=F32)
    w, new_carries = [], []
    for h in range(HEADS_PER_STEP):
        lp_h = log_pass[h * nq:(h + 1) * nq]
        w.append(jnp.exp2(scores[h] + lp_h + carries[h]).astype(BF16))
        new_carries.append(carries[h] + lp_h[:, 0:1])
    pv = jnp.dot(jnp.concatenate(w, axis=1), _per_head(v_j), preferred_element_type=F32)
    return pv, new_carries


def _attn_kernel(q_ref, k_ref, v_ref, u2_ref, o_ref):
    qi = pl.program_id(2)
    u2 = u2_ref[...]
    q = q_ref[0]
    row = lax.broadcasted_iota(jnp.int32, (ATT_Q, ATT_K), 0)
    col = lax.broadcasted_iota(jnp.int32, (ATT_Q, ATT_K), 1)
    diag_mask = col < row

    def sweep(j, acc, carries, mask):
        rows = pl.ds(pl.multiple_of(j * ATT_K, ATT_K), ATT_K)
        pv, carries = _sb_block(q, k_ref[0, rows, :], v_ref[0, rows, :], u2, carries, mask)
        return acc + pv, carries

    zero = [jnp.zeros((ATT_Q, 1), F32) for _ in range(HEADS_PER_STEP)]
    acc, carries = sweep(qi, jnp.zeros((ATT_Q, LANES), F32), zero, diag_mask)
    no_prev = jnp.where(qi > 0, 0.0, -1e30)
    acc, carries = sweep(jnp.maximum(qi - 1, 0), acc, [c + no_prev for c in carries], None)

    def largest(carries):
        return jnp.max(functools.reduce(jnp.maximum, carries))

    def cond(state):
        j, _, _, live = state
        return jnp.logical_and(j >= 0, live > PASS_LOG2_FLOOR)

    def body(state):
        j, acc, carries, _ = state
        acc, carries = sweep(j, acc, carries, None)
        return j - 1, acc, carries, largest(carries)

    _, acc, _, _ = lax.while_loop(cond, body, (qi - 2, acc, carries, largest(carries)))
    o_ref[0] = acc.astype(o_ref.dtype)


def _sb_attention(qkv3):
    b, s, _ = qkv3.shape
    pairs = SB_WIDTH // LANES
    neg_tri = -(jnp.arange(ATT_K)[:, None] >= jnp.arange(ATT_K)[None, :]).astype(BF16)
    u2 = jnp.concatenate([neg_tri, neg_tri], axis=0)
    return pl.pallas_call(
        _attn_kernel,
        out_shape=jax.ShapeDtypeStruct((b, s, SB_WIDTH), BF16),
        grid=(b, pairs, s // ATT_Q),
        in_specs=[pl.BlockSpec((1, ATT_Q, LANES), lambda bi, p, qi: (bi, qi, p)),
                  pl.BlockSpec((1, s, LANES), lambda bi, p, qi: (bi, 0, pairs + p)),
                  pl.BlockSpec((1, s, LANES), lambda bi, p, qi: (bi, 0, 2 * pairs + p)),
                  _const_spec((2 * ATT_K, ATT_K))],
        out_specs=pl.BlockSpec((1, ATT_Q, LANES), lambda bi, p, qi: (bi, qi, p)),
        compiler_params=_cparams(("parallel", "parallel", "arbitrary")),
        name="sb_attention",
    )(qkv3, qkv3, qkv3, u2)


def _causal_conv(x, stage, w):
    t = x.shape[0]
    k = w.shape[0]
    stage[SUBLANES:, :] = x
    out = w[k - 1:k] * x
    for j in range(1, k):
        out = out + w[k - 1 - j:k - j] * stage[pl.ds(SUBLANES - j, t), :]
    stage[:SUBLANES, :] = x[t - SUBLANES:]
    return out


def _linear_scan(a, b, h0):
    t = a.shape[0]
    sub = lax.broadcasted_iota(jnp.int32, a.shape, 0) % SUBLANES
    d = 1
    while d < SUBLANES:
        keep = sub >= d
        a_prev = jnp.where(keep, pltpu.roll(a, d, axis=0), 1.0)
        b_prev = jnp.where(keep, pltpu.roll(b, d, axis=0), 0.0)
        b = a * b_prev + b
        a = a * a_prev
        d *= 2
    h = h0
    groups = []
    for g in range(t // SUBLANES):
        rows = slice(g * SUBLANES, (g + 1) * SUBLANES)
        seg = b[rows] + a[rows] * h
        h = seg[SUBLANES - 1:]
        groups.append(seg)
    return jnp.concatenate(groups, axis=0)


def _mix_kernel(h_ref, ysb_ref, lin_ref, scb_ref, scp_ref, gsb_ref, glru_ref, gsc_ref, gelu_ref,
                lconv_w_ref, lconv_b_ref, wa_ref, ba_ref, wx_ref, bx_ref, lam_ref, scw_ref,
                wsb_ref, wlru_ref, wsc_ref, wout_ref, ln_g_ref, ln_b_ref,
                out_ref, lin_stage, sc_stage, h_state):
    t = MIX_ROWS

    @pl.when(pl.program_id(1) == 0)
    def _():
        lin_stage[:SUBLANES, :] = jnp.zeros((SUBLANES, LRU_WIDTH), F32)
        sc_stage[:SUBLANES, :] = jnp.zeros((SUBLANES, SC_WIDTH), F32)
        h_state[...] = jnp.zeros_like(h_state)

    u = _causal_conv(lin_ref[0], lin_stage, lconv_w_ref[...]) + lconv_b_ref[...]
    ub = u.astype(BF16)
    r = _sigmoid(jnp.dot(ub, wa_ref[...], preferred_element_type=F32) + ba_ref[...])
    i = _sigmoid(jnp.dot(ub, wx_ref[...], preferred_element_type=F32) + bx_ref[...])
    lam = lam_ref[...]
    softplus_neg_lam = jnp.maximum(-lam, 0.0) + jnp.log(1.0 + jnp.exp(-jnp.abs(lam)))
    log_a = (-LRU_C) * r * softplus_neg_lam
    a = jnp.exp(log_a)
    gap = 1.0 - a * a
    drive = jnp.where(gap > 0.0, gap * lax.rsqrt(gap), 0.0) * (i * u)
    hs = _linear_scan(a, drive, h_state[...])
    h_state[...] = hs[t - 1:]
    y_lru = gelu_ref[0] * hs.astype(BF16)

    y_sc = (scb_ref[0] * _causal_conv(scp_ref[0], sc_stage, scw_ref[...])).astype(BF16)

    def branch(y, w_ref):
        return jnp.dot(y, w_ref[...], preferred_element_type=F32).astype(BF16)

    merged = (gsb_ref[0] * branch(ysb_ref[0], wsb_ref) + glru_ref[0] * branch(y_lru, wlru_ref)
              + gsc_ref[0] * branch(y_sc, wsc_ref))
    mix = jnp.dot(merged, wout_ref[...], preferred_element_type=F32)
    out_ref[0] = _layer_norm(ALPHA * h_ref[0] + mix, ln_g_ref[...], ln_b_ref[...])


def _block_diag(w):
    hh, ii, jj = w.shape
    eye = jnp.eye(hh, dtype=w.dtype)
    return (eye[:, None, :, None] * w[:, :, None, :]).reshape(hh * ii, hh * jj)


def _mixer_tail(h3, ysb3, conv3, act3, p, l):
    b, s, d = h3.shape
    t = MIX_ROWS
    assert LRU_WIDTH == SC_WIDTH

    def row_spec(width, col):
        return pl.BlockSpec((1, t, width), lambda bi, si: (bi, si, col))

    vec = lambda v: v.reshape(1, -1).astype(F32)
    in_specs = [row_spec(d, 0), row_spec(SB_WIDTH, 0)]
    in_specs += [row_spec(LRU_WIDTH, c) for c in range(3)]
    in_specs += [row_spec(d, g) for g in range(3)]
    in_specs += [row_spec(LRU_WIDTH, 3 * d // LRU_WIDTH)]
    weights = [
        p["lru_conv_w"][l].astype(F32), vec(p["lru_conv_b"][l]),
        _block_diag(p["lru_wa"][l]).astype(BF16), vec(p["lru_ba"][l]),
        _block_diag(p["lru_wx"][l]).astype(BF16), vec(p["lru_bx"][l]),
        vec(p["lru_lambda"][l]), p["sc_conv_w"][l].astype(F32),
        p["w_branch_sb"][l].astype(BF16), p["w_branch_lru"][l].astype(BF16),
        p["w_branch_sc"][l].astype(BF16), p["w_out"][l].astype(BF16),
        vec(p["ln1_g"][l]), vec(p["ln1_b"][l]),
    ]
    in_specs += [_const_spec(w.shape) for w in weights]
    return pl.pallas_call(
        _mix_kernel,
        out_shape=jax.ShapeDtypeStruct((b, s, d), F32),
        grid=(b, s // t),
        in_specs=in_specs,
        out_specs=pl.BlockSpec((1, t, d), lambda bi, si: (bi, si, 0)),
        scratch_shapes=[pltpu.VMEM((SUBLANES + t, LRU_WIDTH), F32),
                        pltpu.VMEM((SUBLANES + t, SC_WIDTH), F32),
                        pltpu.VMEM((1, LRU_WIDTH), F32)],
        compiler_params=_cparams(("parallel", "arbitrary")),
        name="mixer_tail",
    )(h3, ysb3, *([conv3] * 3), *([act3] * 4), *weights)


def _first_argmax(vals, lane):
    m = jnp.max(vals, axis=-1, keepdims=True)
    idx = jnp.min(jnp.where(vals == m, lane, LANES), axis=-1, keepdims=True)
    return m, idx


def _route_kernel(h_ref, w_ref, bias_ref, tri_ref, pick_ref, info_ref, key_ref, counts_ref, running):
    @pl.when(pl.program_id(0) == 0)
    def _():
        running[...] = jnp.zeros_like(running)

    h = h_ref[...]
    h_hi = h.astype(BF16)
    h_lo = (h - h_hi.astype(F32)).astype(BF16)
    w = w_ref[...]
    w_hi = w.astype(BF16)
    w_lo = (w - w_hi.astype(F32)).astype(BF16)
    logits = jnp.dot(jnp.concatenate([h_hi, h_lo, h_hi], axis=1),
                     jnp.concatenate([w_hi, w_hi, w_lo], axis=0), preferred_element_type=F32)
    lane = lax.broadcasted_iota(jnp.int32, logits.shape, 1)
    neg = -jnp.inf
    is_group = (lane >= N_EXPERTS) & (lane < N_EXPERTS + N_GROUPS)
    biased = logits + bias_ref[...]
    g_max = jnp.max(jnp.where(is_group, logits, neg), axis=-1, keepdims=True)
    g_exp = jnp.where(is_group, jnp.exp(logits - g_max), 0.0)
    g_den = jnp.sum(g_exp, axis=-1, keepdims=True)
    _, g_lane = _first_argmax(jnp.where(is_group, biased, neg), lane)
    g_sel = g_lane - N_EXPERTS
    g_prob = jnp.sum(jnp.where(lane == g_lane, g_exp, 0.0), axis=-1, keepdims=True) / g_den
    in_group = (lane >= g_sel * EXP_PER_GROUP) & (lane < (g_sel + 1) * EXP_PER_GROUP)
    cand = jnp.where(in_group, biased, neg)
    _, e1 = _first_argmax(cand, lane)
    _, e2 = _first_argmax(jnp.where(lane == e1, neg, cand), lane)
    l1 = jnp.sum(jnp.where(lane == e1, logits, 0.0), axis=-1, keepdims=True)
    l2 = jnp.sum(jnp.where(lane == e2, logits, 0.0), axis=-1, keepdims=True)
    m = jnp.maximum(l1, l2)
    x1 = jnp.exp(l1 - m)
    x2 = jnp.exp(l2 - m)
    scale = g_prob / (x1 + x2)
    w1 = x1 * scale
    w2 = x2 * scale
    first_low = e1 < e2
    a = jnp.where(first_low, e1, e2) - g_sel * EXP_PER_GROUP
    b = jnp.where(first_low, e2, e1) - g_sel * EXP_PER_GROUP
    cls = g_sel * PAIRS_PER_GROUP + ((a * (2 * EXP_PER_GROUP - 1 - a)) >> 1) + (b - a - 1)
    onehot = lane == cls
    earlier = jnp.dot(tri_ref[...], jnp.where(onehot, 1.0, 0.0).astype(BF16),
                      preferred_element_type=F32) + running[...]
    rank = jnp.sum(jnp.where(onehot, earlier, 0.0), axis=-1, keepdims=True)
    last = ROUTE_ROWS - 1
    running[...] = earlier[last:] + jnp.where(onehot[last:], 1.0, 0.0)
    counts_ref[...] = running[...]
    info = jnp.where(lane == INFO_W_LOW, jnp.where(first_low, w1, w2), 0.0)
    info_ref[...] = jnp.where(lane == INFO_W_HIGH, jnp.where(first_low, w2, w1), info)
    rank_hi = jnp.floor(rank * (1.0 / KEY_DIGIT))
    digits = jnp.where(lane == 0, rank - rank_hi * KEY_DIGIT, 0.0)
    digits = jnp.where(lane == 1, rank_hi, digits)
    digits = jnp.where(lane == 2, cls.astype(F32), digits)
    keys = lax.dot_general(pick_ref[...], digits.astype(BF16), (((1,), (1,)), ((), ())),
                           preferred_element_type=F32)
    key_ref[0] = keys[0:1].astype(jnp.int32)


def _route(h2, w_router, bias):
    n, d = h2.shape
    t = ROUTE_ROWS
    tri = (jnp.arange(t)[:, None] > jnp.arange(t)[None, :]).astype(BF16)
    pick = jnp.zeros((SUBLANES, LANES), F32).at[0, :3].set(
        jnp.asarray([1.0, KEY_DIGIT, KEY_DIGIT * KEY_DIGIT], F32)).astype(BF16)
    return pl.pallas_call(
        _route_kernel,
        out_shape=(jax.ShapeDtypeStruct((n, LANES), F32),
                   jax.ShapeDtypeStruct((n // t, 1, t), jnp.int32),
                   jax.ShapeDtypeStruct((1, LANES), F32)),
        grid=(n // t,),
        in_specs=[pl.BlockSpec((t, d), lambda i: (i, 0)),
                  _const_spec((d, LANES)), _const_spec((1, LANES)), _const_spec((t, t)),
                  _const_spec((SUBLANES, LANES))],
        out_specs=(pl.BlockSpec((t, LANES), lambda i: (i, 0)),
                   pl.BlockSpec((1, 1, t), lambda i: (i, 0, 0)),
                   _const_spec((1, LANES))),
        scratch_shapes=[pltpu.VMEM((1, LANES), F32)],
        compiler_params=_cparams(("arbitrary",)),
        name="moe_route",
    )(h2, w_router, bias, tri, pick)


def _slot_kernel(start_ref, key_ref, slot_ref):
    key = key_ref[...]
    cls = key >> KEY_SHIFT
    slot = key & (KEY_DIGIT * KEY_DIGIT - 1)
    for c in range(N_CLASSES):
        slot = slot + jnp.where(cls == c, start_ref[c], 0)
    slot_ref[...] = slot


def _slots(class_start, key3):
    g, _, t = key3.shape
    return pl.pallas_call(
        _slot_kernel,
        out_shape=jax.ShapeDtypeStruct(key3.shape, jnp.int32),
        grid_spec=pltpu.PrefetchScalarGridSpec(
            num_scalar_prefetch=1, grid=(1,),
            in_specs=[pl.BlockSpec((g, 1, t), lambda i, s: (0, 0, 0))],
            out_specs=pl.BlockSpec((g, 1, t), lambda i, s: (0, 0, 0))),
        compiler_params=_cparams(("arbitrary",)),
        name="moe_slots",
    )(class_start, key3)


def _for_rows(n_rows, fn):
    def body(g, c):
        for j in range(ROW_COPY_UNROLL):
            fn(g * ROW_COPY_UNROLL + j, j % 2)
        return c

    lax.fori_loop(0, n_rows // ROW_COPY_UNROLL, body, 0)


def _dispatch_kernel(tile_end_ref, n_used_ref, slot_ref, h_ref, info_ref, xs_ref, rows, sems, zsem):
    i = pl.program_id(0)
    last = pl.num_programs(0) - 1
    p = i % 2
    d = h_ref.shape[1]
    t = DISPATCH_ROWS
    te = EXPERT_ROWS

    def row_copy(buf, r):
        return pltpu.make_async_copy(rows.at[buf, pl.ds(r, 1)],
                                     xs_ref.at[pl.ds(slot_ref[0, 0, r], 1)], sems.at[buf])

    @pl.when(i == 0)
    def _():
        rows[1, :te] = jnp.zeros((te, d + LANES), F32)

        def zero_tile(tile):
            return pltpu.make_async_copy(rows.at[1, pl.ds(0, te)],
                                         xs_ref.at[pl.ds(tile * te, te)], zsem)

        prev_end = 0
        for c in range(N_CLASSES):
            end = tile_end_ref[c]

            @pl.when(end > prev_end)
            def _(end=end):
                cp = zero_tile(end - 1)
                cp.start()
                cp.wait()
            prev_end = end

        def zero_tail(tile, c):
            cp = zero_tile(tile)
            cp.start()
            cp.wait()
            return c

        lax.fori_loop(n_used_ref[0], xs_ref.shape[0] // te, zero_tail, 0)

    rows[p, :, :d] = h_ref[...]
    rows[p, :, d:] = info_ref[...]
    _for_rows(t, lambda r, q: row_copy(p, r).start(priority=q))

    def row_wait(buf):
        pltpu.make_async_copy(rows.at[buf, pl.ds(0, 1)], xs_ref.at[pl.ds(0, 1)],
                              sems.at[buf]).wait()

    @pl.when(i > 0)
    def _():
        _for_rows(t, lambda r, q: row_wait(1 - p))

    @pl.when(i == last)
    def _():
        _for_rows(t, lambda r, q: row_wait(p))


def _dispatch(tile_end, n_used, slot3, h2, info, n_slots):
    n, d = h2.shape
    t = DISPATCH_ROWS
    return pl.pallas_call(
        _dispatch_kernel,
        out_shape=jax.ShapeDtypeStruct((n_slots, d + LANES), F32),
        grid_spec=pltpu.PrefetchScalarGridSpec(
            num_scalar_prefetch=2, grid=(n // t,),
            in_specs=[pl.BlockSpec((1, 1, t), lambda i, *_: (i, 0, 0), memory_space=pltpu.SMEM),
                      pl.BlockSpec((t, d), lambda i, *_: (i, 0)),
                      pl.BlockSpec((t, LANES), lambda i, *_: (i, 0))],
            out_specs=pl.BlockSpec(memory_space=pl.ANY),
            scratch_shapes=[pltpu.VMEM((2, t, d + LANES), F32), pltpu.SemaphoreType.DMA((2,)),
                            pltpu.SemaphoreType.DMA(())]),
        compiler_params=_cparams(("arbitrary",)),
        name="moe_dispatch",
    )(tile_end, n_used, slot3, h2, info)


def _expert_kernel(n_used_ref, src_ref, lo_ref, hi_ref, xs_ref, wg_lo, wg_hi, wu_lo, wu_hi,
                   wd_lo, wd_hi, ln_g_ref, ln_b_ref, ys_ref):
    del src_ref, lo_ref, hi_ref

    @pl.when(pl.program_id(0) < n_used_ref[0])
    def _():
        d = ys_ref.shape[1]
        x = xs_ref[:, :d]
        xb = x.astype(BF16)
        y = None
        for wg, wu, wd, lane in ((wg_lo, wu_lo, wd_lo, INFO_W_LOW), (wg_hi, wu_hi, wd_hi, INFO_W_HIGH)):
            gate = jnp.dot(xb, wg[0], preferred_element_type=F32)
            up = jnp.dot(xb, wu[0], preferred_element_type=F32)
            act = gate * _sigmoid(gate) * up * xs_ref[:, d + lane:d + lane + 1]
            part = jnp.dot(act.astype(BF16), wd[0], preferred_element_type=F32)
            y = part if y is None else y + part
        ys_ref[...] = _layer_norm(ALPHA * x + y, ln_g_ref[...], ln_b_ref[...])

    @pl.when(pl.program_id(0) >= n_used_ref[0])
    def _():
        ys_ref[...] = jnp.zeros_like(ys_ref)


def _experts(tile_meta, xs, layer, w_gate_b, w_up_b, w_down_b, ln_g, ln_b):
    n_slots, cols = xs.shape
    d = cols - LANES
    f = w_gate_b.shape[3]
    t = EXPERT_ROWS
    x_map = lambda i, n_used, src, lo, hi: (src[i], 0)
    lo_map = lambda i, n_used, src, lo, hi: (layer, lo[i], 0, 0)
    hi_map = lambda i, n_used, src, lo, hi: (layer, hi[i], 0, 0)
    const = lambda i, *_: (0, 0)
    grid_spec = pltpu.PrefetchScalarGridSpec(
        num_scalar_prefetch=4,
        grid=(n_slots // t,),
        in_specs=[pl.BlockSpec((t, cols), x_map),
                  pl.BlockSpec((None, 1, d, f), lo_map), pl.BlockSpec((None, 1, d, f), hi_map),
                  pl.BlockSpec((None, 1, d, f), lo_map), pl.BlockSpec((None, 1, d, f), hi_map),
                  pl.BlockSpec((None, 1, f, d), lo_map), pl.BlockSpec((None, 1, f, d), hi_map),
                  pl.BlockSpec((1, d), const), pl.BlockSpec((1, d), const)],
        out_specs=pl.BlockSpec((t, d), lambda i, *_: (i, 0)),
    )
    return pl.pallas_call(
        _expert_kernel,
        out_shape=jax.ShapeDtypeStruct((n_slots, d), F32),
        grid_spec=grid_spec,
        compiler_params=_cparams(("arbitrary",)),
        name="moe_experts",
    )(*tile_meta, xs, w_gate_b, w_gate_b, w_up_b, w_up_b, w_down_b, w_down_b,
      ln_g.reshape(1, d), ln_b.reshape(1, d))


def _combine_kernel(slot_ref, ys_ref, o_ref, buf, gsems, osems):
    i = pl.program_id(0)
    last = pl.num_programs(0) - 1
    p = i % 2
    t = DISPATCH_ROWS

    def tile_out(b, step):
        return pltpu.make_async_copy(buf.at[b], o_ref.at[pl.ds(step * t, t)], osems.at[b])

    def row_wait(b):
        pltpu.make_async_copy(ys_ref.at[pl.ds(0, 1)], buf.at[b, pl.ds(0, 1)], gsems.at[b]).wait()

    @pl.when(i >= 2)
    def _():
        tile_out(p, i - 2).wait()

    _for_rows(t, lambda r, q: pltpu.make_async_copy(
        ys_ref.at[pl.ds(slot_ref[0, 0, r], 1)], buf.at[p, pl.ds(r, 1)],
        gsems.at[p]).start(priority=q))

    @pl.when(i > 0)
    def _():
        _for_rows(t, lambda r, q: row_wait(1 - p))
        tile_out(1 - p, i - 1).start()

    @pl.when(i == last)
    def _():
        _for_rows(t, lambda r, q: row_wait(p))
        tile_out(p, i).start()
        tile_out(p, i).wait()

        @pl.when(i > 0)
        def _():
            tile_out(1 - p, i - 1).wait()


def _combine(slot3, ys, n):
    d = ys.shape[1]
    t = DISPATCH_ROWS
    return pl.pallas_call(
        _combine_kernel,
        out_shape=jax.ShapeDtypeStruct((n, d), F32),
        grid=(n // t,),
        in_specs=[pl.BlockSpec((1, 1, t), lambda i: (i, 0, 0), memory_space=pltpu.SMEM),
                  pl.BlockSpec(memory_space=pl.ANY)],
        out_specs=pl.BlockSpec(memory_space=pl.ANY),
        scratch_shapes=[pltpu.VMEM((2, t, d), F32), pltpu.SemaphoreType.DMA((2,)),
                        pltpu.SemaphoreType.DMA((2,))],
        compiler_params=_cparams(("arbitrary",)),
        name="moe_combine",
    )(slot3, ys)


def _pair_tables():
    lo, hi = [], []
    for g in range(N_GROUPS):
        for a in range(EXP_PER_GROUP):
            for b in range(a + 1, EXP_PER_GROUP):
                lo.append(g * EXP_PER_GROUP + a)
                hi.append(g * EXP_PER_GROUP + b)
    return jnp.asarray(lo, jnp.int32), jnp.asarray(hi, jnp.int32)


def _tile_plan(counts, n_tiles):
    t = EXPERT_ROWS
    counts = counts[0, :N_CLASSES].astype(jnp.int32)
    tiles_per_class = (counts + t - 1) // t
    tile_end = jnp.cumsum(tiles_per_class)
    class_start = (tile_end - tiles_per_class) * t
    n_used = tile_end[-1:]
    src = jnp.minimum(jnp.arange(n_tiles, dtype=jnp.int32), n_used - 1)
    tile_class = jnp.sum((tile_end[None, :] <= src[:, None]).astype(jnp.int32), axis=1)
    pair_lo, pair_hi = _pair_tables()
    return class_start, tile_end, (n_used, src, pair_lo[tile_class], pair_hi[tile_class])


def kernel(x, ln_in_g, ln_in_b, w_in, gate_b, lru_conv_w, lru_conv_b, lru_wa, lru_ba, lru_wx, lru_bx, lru_lambda, sc_conv_w, w_branch_sb, w_branch_lru, w_branch_sc, w_out, ln1_g, ln1_b, w_group, group_bias, w_expert_router, expert_bias, w_gate, w_up, w_down, ln2_g, ln2_b):
    b, s, d = x.shape
    n = b * s
    params = dict(gate_b=gate_b, lru_conv_w=lru_conv_w, lru_conv_b=lru_conv_b, lru_wa=lru_wa,
                  lru_ba=lru_ba, lru_wx=lru_wx, lru_bx=lru_bx, lru_lambda=lru_lambda,
                  sc_conv_w=sc_conv_w, w_branch_sb=w_branch_sb, w_branch_lru=w_branch_lru,
                  w_branch_sc=w_branch_sc, w_out=w_out, ln1_g=ln1_g, ln1_b=ln1_b)
    h = _entry_ln(x.reshape(n, d), ln_in_g, ln_in_b)
    w_in_b, w_gate_b, w_up_b, w_down_b = (w.astype(BF16) for w in (w_in, w_gate, w_up, w_down))
    for l in range(w_in.shape[0]):
        qkv, conv_in, act = _inproj(h, w_in_b, l, gate_b[l])
        y_sb = _sb_attention(qkv.reshape(b, s, -1))
        h1 = _mixer_tail(h.reshape(b, s, d), y_sb, conv_in.reshape(b, s, -1),
                         act.reshape(b, s, -1), params, l)
        h1 = h1.reshape(n, d)
        pad = LANES - N_EXPERTS - N_GROUPS
        w_router = jnp.concatenate(
            [w_expert_router[l], w_group[l], jnp.zeros((d, pad), F32)], axis=1).astype(F32)
        bias = jnp.concatenate(
            [expert_bias[l], group_bias[l], jnp.zeros((pad,), F32)]).reshape(1, LANES).astype(F32)
        info, key3, counts = _route(h1, w_router, bias)
        n_tiles = n // EXPERT_ROWS + N_CLASSES
        class_start, tile_end, tile_meta = _tile_plan(counts, n_tiles)
        slot3 = _slots(class_start, key3)
        xs = _dispatch(tile_end, tile_meta[0], slot3, h1, info, n_tiles * EXPERT_ROWS)
        ys = _experts(tile_meta, xs, l, w_gate_b, w_up_b, w_down_b, ln2_g[l], ln2_b[l])
        h = _combine(slot3, ys, n)
    return h.reshape(b, s, d)
```

```python
import functools
import math

import jax
import jax.numpy as jnp
from jax import lax
from jax.experimental import pallas as pl
from jax.experimental.pallas import tpu as pltpu

F32 = jnp.float32
BF16 = jnp.bfloat16

SB_HEADS = 8
SB_HEAD_DIM = 64
SB_WIDTH = SB_HEADS * SB_HEAD_DIM
LRU_WIDTH = 512
LRU_BLOCKS = 8
LRU_C = 8.0
SC_WIDTH = 512
N_GROUPS = 4
EXP_PER_GROUP = 4
N_EXPERTS = N_GROUPS * EXP_PER_GROUP
D_EXPERT = 512
DEPTH = 2
ALPHA = (2 * DEPTH) ** 0.25
LN_EPS = 1e-5

LANES = 128
SUBLANES = 8
VMEM_LIMIT_BYTES = 56 * 1024 * 1024

PROJ_ROWS = 512
PROJ_CHUNK = 512
ATT_Q = 256
ATT_K = 256
ATT_LANE_BLOCKS = 2
MIX_ROWS = 256
ROUTE_ROWS = 512
DISPATCH_ROWS = 512
EXPERT_ROWS = 256
ROW_COPY_UNROLL = 8

PAIRS_PER_GROUP = EXP_PER_GROUP * (EXP_PER_GROUP - 1) // 2
N_CLASSES = N_GROUPS * PAIRS_PER_GROUP
INFO_W_LOW, INFO_W_HIGH = 0, 1
KEY_DIGIT = 256
KEY_SHIFT = 16


def _cparams(sem):
    return pltpu.CompilerParams(dimension_semantics=sem, vmem_limit_bytes=VMEM_LIMIT_BYTES)


def _const_spec(shape):
    nd = len(shape)
    return pl.BlockSpec(shape, lambda *_: (0,) * nd)


def _layer_norm(y, g, b):
    mu = jnp.mean(y, axis=-1, keepdims=True)
    d = y - mu
    var = jnp.mean(d * d, axis=-1, keepdims=True)
    return d * lax.rsqrt(var + LN_EPS) * g + b


QKV_COLS = 3 * SB_WIDTH
BRANCH_COLS = 2 * LRU_WIDTH + 3 * SC_WIDTH


def _sigmoid(x):
    return 0.5 * jnp.tanh(0.5 * x) + 0.5


def _gelu_tanh(x):
    return 0.5 * x * (1.0 + jnp.tanh(math.sqrt(2.0 / math.pi) * (x + 0.044715 * (x * x * x))))


def _project(h, w_ref, gate_b_ref, qkv_ref, conv_ref, act_ref):
    hb = h.astype(BF16)
    c = PROJ_CHUNK

    def proj(chunk):
        return jnp.dot(hb, w_ref[:, chunk * c:(chunk + 1) * c], preferred_element_type=F32)

    n_qkv = QKV_COLS // c
    for j in range(n_qkv):
        acc = proj(j)
        if j * c < SB_WIDTH:
            acc = acc * (SB_HEAD_DIM ** -0.5 * LOG2_E)
        qkv_ref[:, j * c:(j + 1) * c] = acc.astype(BF16)
    lru_in, lru_gate, sc_b, sc_c, sc_h = range(n_qkv, n_qkv + 5)
    conv_ref[:, 0:c] = proj(lru_in)
    conv_ref[:, c:2 * c] = proj(sc_b)
    conv_ref[:, 2 * c:3 * c] = proj(sc_c) * proj(sc_h)
    for j in range(gate_b_ref.shape[1] // c):
        acc = proj(sc_h + 1 + j) + gate_b_ref[:, j * c:(j + 1) * c]
        act_ref[:, j * c:(j + 1) * c] = _sigmoid(acc).astype(BF16)
    act_ref[:, gate_b_ref.shape[1]:] = _gelu_tanh(proj(lru_gate)).astype(BF16)


def _inproj_kernel(h_ref, w_ref, gate_b_ref, qkv_ref, conv_ref, act_ref):
    _project(h_ref[...], w_ref, gate_b_ref, qkv_ref, conv_ref, act_ref)


def _ln_inproj_kernel(x_ref, g_ref, b_ref, w_ref, gate_b_ref, h_ref, qkv_ref, conv_ref, act_ref):
    h = _layer_norm(x_ref[...], g_ref[...], b_ref[...])
    h_ref[...] = h
    _project(h, w_ref, gate_b_ref, qkv_ref, conv_ref, act_ref)


def _inproj(h2, w_in_b, layer, gate_b, entry_ln=None):
    n, d = h2.shape
    cols = w_in_b.shape[2]
    gate_cols = gate_b.size
    assert LRU_WIDTH == SC_WIDTH == PROJ_CHUNK and cols == QKV_COLS + BRANCH_COLS + gate_cols
    widths = (QKV_COLS, 3 * PROJ_CHUNK, gate_cols + LRU_WIDTH)
    dtypes = (BF16, F32, BF16)
    rows = pl.BlockSpec((PROJ_ROWS, d), lambda i: (i, 0))
    weight_specs = [pl.BlockSpec((None, d, cols), lambda i: (layer, 0, 0),
                                 pipeline_mode=pl.Buffered(1)),
                    _const_spec((1, gate_cols))]
    weights = (w_in_b, gate_b.reshape(1, gate_cols).astype(F32))
    out_shape = [jax.ShapeDtypeStruct((n, w), t) for w, t in zip(widths, dtypes)]
    out_specs = [pl.BlockSpec((PROJ_ROWS, w), lambda i: (i, 0)) for w in widths]
    if entry_ln is None:
        body, ln_specs, ln_args = _inproj_kernel, [], ()
    else:
        body = _ln_inproj_kernel
        ln_specs = [_const_spec((1, d)), _const_spec((1, d))]
        ln_args = tuple(v.reshape(1, d).astype(F32) for v in entry_ln)
        out_shape.insert(0, jax.ShapeDtypeStruct((n, d), F32))
        out_specs.insert(0, rows)
    return pl.pallas_call(
        body,
        out_shape=tuple(out_shape),
        grid=(n // PROJ_ROWS,),
        in_specs=[rows] + ln_specs + weight_specs,
        out_specs=tuple(out_specs),
        compiler_params=_cparams(("parallel",)),
        name="in_proj",
    )(h2, *ln_args, *weights)


LOG2_E = math.log2(math.e)
PASS_LOG2_FLOOR = -151.0


HEADS_PER_STEP = LANES // SB_HEAD_DIM


def _per_head(x):
    lane = lax.broadcasted_iota(jnp.int32, x.shape, 1)
    return jnp.concatenate(
        [jnp.where((lane >= h * SB_HEAD_DIM) & (lane < (h + 1) * SB_HEAD_DIM), x, jnp.zeros_like(x))
         for h in range(HEADS_PER_STEP)], axis=0)


MASKED_SCORE = -1e30


def _neg_abs(x):
    bits = lax.bitcast_convert_type(x, jnp.uint32) | jnp.uint32(0x80000000)
    return lax.bitcast_convert_type(bits, F32)


def _sb_block(q, k_j, v_j, neg_tri2, carries, mask):
    nq, kk = q.shape[0], k_j.shape[0]
    z = lax.dot_general(q, _per_head(k_j), (((1,), (1,)), ((), ())), preferred_element_type=F32)
    scores, split = [], []
    for h in range(HEADS_PER_STEP):
        z_h = z[:, h * kk:(h + 1) * kk]
        if mask is not None:
            z_h = jnp.where(mask, z_h, MASKED_SCORE)
        drop = jnp.maximum(z_h, 0.0) + jnp.log2(1.0 + jnp.exp2(_neg_abs(z_h)))
        hi = drop.astype(BF16)
        lo = (drop - hi.astype(F32)).astype(BF16)
        scores.append(z_h)
        split.append(jnp.concatenate([hi, lo], axis=1))
    log_pass = jnp.dot(jnp.concatenate(split, axis=0), neg_tri2, preferred_element_type=F32)
    w, new_carries = [], []
    for h in range(HEADS_PER_STEP):
        lp_h = log_pass[h * nq:(h + 1) * nq]
        w.append(jnp.exp2(scores[h] + lp_h + carries[h]).astype(BF16))
        new_carries.append(carries[h] + lp_h[:, 0:1])
    pv = jnp.dot(jnp.concatenate(w, axis=1), _per_head(v_j), preferred_element_type=F32)
    return pv, new_carries


def _attn_kernel(q_ref, k_ref, v_ref, u2_ref, o_ref):
    qi = pl.program_id(2)
    u2 = u2_ref[...]
    row = lax.broadcasted_iota(jnp.int32, (ATT_Q, ATT_K), 0)
    col = lax.broadcasted_iota(jnp.int32, (ATT_Q, ATT_K), 1)
    diag_mask = col < row
    blocks = [slice(g * LANES, (g + 1) * LANES) for g in range(ATT_LANE_BLOCKS)]
    q = [q_ref[0, :, lanes] for lanes in blocks]

    def sweep(j, accs, carries, mask):
        rows = pl.ds(pl.multiple_of(j * ATT_K, ATT_K), ATT_K)
        out = [_sb_block(q_g, k_ref[0, rows, lanes], v_ref[0, rows, lanes], u2, c_g, mask)
               for q_g, lanes, c_g in zip(q, blocks, carries)]
        return [acc + pv for acc, (pv, _) in zip(accs, out)], [c for _, c in out]

    zero = [[jnp.zeros((ATT_Q, 1), F32) for _ in range(HEADS_PER_STEP)] for _ in blocks]
    accs, carries = sweep(qi, [jnp.zeros((ATT_Q, LANES), F32) for _ in blocks], zero, diag_mask)
    no_prev = jnp.where(qi > 0, 0.0, -1e30)
    accs, carries = sweep(jnp.maximum(qi - 1, 0), accs,
                          [[c + no_prev for c in c_g] for c_g in carries], None)

    def largest(carries):
        return jnp.max(functools.reduce(jnp.maximum, [c for c_g in carries for c in c_g]))

    def cond(state):
        j, _, _, live = state
        return jnp.logical_and(j >= 0, live > PASS_LOG2_FLOOR)

    def body(state):
        j, accs, carries, _ = state
        accs, carries = sweep(j, accs, carries, None)
        return j - 1, accs, carries, largest(carries)

    _, accs, _, _ = lax.while_loop(cond, body, (qi - 2, accs, carries, largest(carries)))
    o_ref[0] = jnp.concatenate(accs, axis=1).astype(o_ref.dtype)


def _sb_attention(qkv3):
    b, s, _ = qkv3.shape
    width = ATT_LANE_BLOCKS * LANES
    groups = SB_WIDTH // width
    neg_tri = -(jnp.arange(ATT_K)[:, None] >= jnp.arange(ATT_K)[None, :]).astype(BF16)
    u2 = jnp.concatenate([neg_tri, neg_tri], axis=0)
    return pl.pallas_call(
        _attn_kernel,
        out_shape=jax.ShapeDtypeStruct((b, s, SB_WIDTH), BF16),
        grid=(b, groups, s // ATT_Q),
        in_specs=[pl.BlockSpec((1, ATT_Q, width), lambda bi, p, qi: (bi, qi, p)),
                  pl.BlockSpec((1, s, width), lambda bi, p, qi: (bi, 0, groups + p)),
                  pl.BlockSpec((1, s, width), lambda bi, p, qi: (bi, 0, 2 * groups + p)),
                  _const_spec((2 * ATT_K, ATT_K))],
        out_specs=pl.BlockSpec((1, ATT_Q, width), lambda bi, p, qi: (bi, qi, p)),
        compiler_params=_cparams(("parallel", "parallel", "arbitrary")),
        name="sb_attention",
    )(qkv3, qkv3, qkv3, u2)


def _causal_conv(x, stage, w):
    t = x.shape[0]
    k = w.shape[0]
    stage[SUBLANES:, :] = x
    out = w[k - 1:k] * x
    for j in range(1, k):
        out = out + w[k - 1 - j:k - j] * stage[pl.ds(SUBLANES - j, t), :]
    stage[:SUBLANES, :] = x[t - SUBLANES:]
    return out


def _linear_scan(a, b, h0):
    t = a.shape[0]
    sub = lax.broadcasted_iota(jnp.int32, a.shape, 0) % SUBLANES
    d = 1
    while d < SUBLANES:
        keep = sub >= d
        a_prev = jnp.where(keep, pltpu.roll(a, d, axis=0), 1.0)
        b_prev = jnp.where(keep, pltpu.roll(b, d, axis=0), 0.0)
        b = a * b_prev + b
        a = a * a_prev
        d *= 2
    h = h0
    groups = []
    for g in range(t // SUBLANES):
        rows = slice(g * SUBLANES, (g + 1) * SUBLANES)
        seg = b[rows] + a[rows] * h
        h = seg[SUBLANES - 1:]
        groups.append(seg)
    return jnp.concatenate(groups, axis=0)


def _mix_kernel(h_ref, ysb_ref, lin_ref, scb_ref, scp_ref, gsb_ref, glru_ref, gsc_ref, gelu_ref,
                lconv_w_ref, lconv_b_ref, wa_ref, ba_ref, wx_ref, bx_ref, lam_ref, scw_ref,
                wsb_ref, wlru_ref, wsc_ref, wout_ref, ln_g_ref, ln_b_ref,
                out_ref, lin_stage, sc_stage, h_state):
    t = MIX_ROWS

    @pl.when(pl.program_id(1) == 0)
    def _():
        lin_stage[:SUBLANES, :] = jnp.zeros((SUBLANES, LRU_WIDTH), F32)
        sc_stage[:SUBLANES, :] = jnp.zeros((SUBLANES, SC_WIDTH), F32)
        h_state[...] = jnp.zeros_like(h_state)

    u = _causal_conv(lin_ref[0], lin_stage, lconv_w_ref[...]) + lconv_b_ref[...]
    ub = u.astype(BF16)
    r = _sigmoid(jnp.dot(ub, wa_ref[...], preferred_element_type=F32) + ba_ref[...])
    i = _sigmoid(jnp.dot(ub, wx_ref[...], preferred_element_type=F32) + bx_ref[...])
    lam = lam_ref[...]
    softplus_neg_lam = jnp.maximum(-lam, 0.0) + jnp.log(1.0 + jnp.exp(-jnp.abs(lam)))
    log_a = (-LRU_C) * r * softplus_neg_lam
    a = jnp.exp(log_a)
    gap = 1.0 - a * a
    drive = jnp.where(gap > 0.0, gap * lax.rsqrt(gap), 0.0) * (i * u)
    hs = _linear_scan(a, drive, h_state[...])
    h_state[...] = hs[t - 1:]
    y_lru = gelu_ref[0] * hs.astype(BF16)

    y_sc = (scb_ref[0] * _causal_conv(scp_ref[0], sc_stage, scw_ref[...])).astype(BF16)

    def branch(y, w_ref):
        return jnp.dot(y, w_ref[...], preferred_element_type=F32).astype(BF16)

    merged = (gsb_ref[0] * branch(ysb_ref[0], wsb_ref) + glru_ref[0] * branch(y_lru, wlru_ref)
              + gsc_ref[0] * branch(y_sc, wsc_ref))
    mix = jnp.dot(merged, wout_ref[...], preferred_element_type=F32)
    out_ref[0] = _layer_norm(ALPHA * h_ref[0] + mix, ln_g_ref[...], ln_b_ref[...])


def _block_diag(w):
    hh, ii, jj = w.shape
    eye = jnp.eye(hh, dtype=w.dtype)
    return (eye[:, None, :, None] * w[:, :, None, :]).reshape(hh * ii, hh * jj)


def _mixer_tail(h3, ysb3, conv3, act3, p, l):
    b, s, d = h3.shape
    t = MIX_ROWS
    assert LRU_WIDTH == SC_WIDTH

    def row_spec(width, col):
        return pl.BlockSpec((1, t, width), lambda bi, si: (bi, si, col))

    vec = lambda v: v.reshape(1, -1).astype(F32)
    in_specs = [row_spec(d, 0), row_spec(SB_WIDTH, 0)]
    in_specs += [row_spec(LRU_WIDTH, c) for c in range(3)]
    in_specs += [row_spec(d, g) for g in range(3)]
    in_specs += [row_spec(LRU_WIDTH, 3 * d // LRU_WIDTH)]
    weights = [
        p["lru_conv_w"][l].astype(F32), vec(p["lru_conv_b"][l]),
        _block_diag(p["lru_wa"][l]).astype(BF16), vec(p["lru_ba"][l]),
        _block_diag(p["lru_wx"][l]).astype(BF16), vec(p["lru_bx"][l]),
        vec(p["lru_lambda"][l]), p["sc_conv_w"][l].astype(F32),
        p["w_branch_sb"][l].astype(BF16), p["w_branch_lru"][l].astype(BF16),
        p["w_branch_sc"][l].astype(BF16), p["w_out"][l].astype(BF16),
        vec(p["ln1_g"][l]), vec(p["ln1_b"][l]),
    ]
    in_specs += [_const_spec(w.shape) for w in weights]
    return pl.pallas_call(
        _mix_kernel,
        out_shape=jax.ShapeDtypeStruct((b, s, d), F32),
        grid=(b, s // t),
        in_specs=in_specs,
        out_specs=pl.BlockSpec((1, t, d), lambda bi, si: (bi, si, 0)),
        scratch_shapes=[pltpu.VMEM((SUBLANES + t, LRU_WIDTH), F32),
                        pltpu.VMEM((SUBLANES + t, SC_WIDTH), F32),
                        pltpu.VMEM((1, LRU_WIDTH), F32)],
        compiler_params=_cparams(("parallel", "arbitrary")),
        name="mixer_tail",
    )(h3, ysb3, *([conv3] * 3), *([act3] * 4), *weights)


def _first_argmax(vals, lane):
    m = jnp.max(vals, axis=-1, keepdims=True)
    idx = jnp.min(jnp.where(vals == m, lane, LANES), axis=-1, keepdims=True)
    return m, idx


def _route_kernel(h_ref, w_ref, bias_ref, tri_ref, pick_ref, info_ref, key_ref, counts_ref, running):
    @pl.when(pl.program_id(0) == 0)
    def _():
        running[...] = jnp.zeros_like(running)

    h = h_ref[...]
    h_hi = h.astype(BF16)
    h_lo = (h - h_hi.astype(F32)).astype(BF16)
    w = w_ref[...]
    w_hi = w.astype(BF16)
    w_lo = (w - w_hi.astype(F32)).astype(BF16)
    logits = jnp.dot(jnp.concatenate([h_hi, h_lo, h_hi], axis=1),
                     jnp.concatenate([w_hi, w_hi, w_lo], axis=0), preferred_element_type=F32)
    lane = lax.broadcasted_iota(jnp.int32, logits.shape, 1)
    neg = -jnp.inf
    is_group = (lane >= N_EXPERTS) & (lane < N_EXPERTS + N_GROUPS)
    biased = logits + bias_ref[...]
    g_max = jnp.max(jnp.where(is_group, logits, neg), axis=-1, keepdims=True)
    g_exp = jnp.where(is_group, jnp.exp(logits - g_max), 0.0)
    g_den = jnp.sum(g_exp, axis=-1, keepdims=True)
    _, g_lane = _first_argmax(jnp.where(is_group, biased, neg), lane)
    g_sel = g_lane - N_EXPERTS
    g_prob = jnp.sum(jnp.where(lane == g_lane, g_exp, 0.0), axis=-1, keepdims=True) / g_den
    in_group = (lane >= g_sel * EXP_PER_GROUP) & (lane < (g_sel + 1) * EXP_PER_GROUP)
    cand = jnp.where(in_group, biased, neg)
    _, e1 = _first_argmax(cand, lane)
    _, e2 = _first_argmax(jnp.where(lane == e1, neg, cand), lane)
    l1 = jnp.sum(jnp.where(lane == e1, logits, 0.0), axis=-1, keepdims=True)
    l2 = jnp.sum(jnp.where(lane == e2, logits, 0.0), axis=-1, keepdims=True)
    m = jnp.maximum(l1, l2)
    x1 = jnp.exp(l1 - m)
    x2 = jnp.exp(l2 - m)
    scale = g_prob / (x1 + x2)
    w1 = x1 * scale
    w2 = x2 * scale
    first_low = e1 < e2
    a = jnp.where(first_low, e1, e2) - g_sel * EXP_PER_GROUP
    b = jnp.where(first_low, e2, e1) - g_sel * EXP_PER_GROUP
    cls = g_sel * PAIRS_PER_GROUP + ((a * (2 * EXP_PER_GROUP - 1 - a)) >> 1) + (b - a - 1)
    onehot = lane == cls
    earlier = jnp.dot(tri_ref[...], jnp.where(onehot, 1.0, 0.0).astype(BF16),
                      preferred_element_type=F32) + running[...]
    rank = jnp.sum(jnp.where(onehot, earlier, 0.0), axis=-1, keepdims=True)
    last = ROUTE_ROWS - 1
    running[...] = earlier[last:] + jnp.where(onehot[last:], 1.0, 0.0)
    counts_ref[...] = running[...]
    info = jnp.where(lane == INFO_W_LOW, jnp.where(first_low, w1, w2), 0.0)
    info_ref[...] = jnp.where(lane == INFO_W_HIGH, jnp.where(first_low, w2, w1), info)
    rank_hi = jnp.floor(rank * (1.0 / KEY_DIGIT))
    digits = jnp.where(lane == 0, rank - rank_hi * KEY_DIGIT, 0.0)
    digits = jnp.where(lane == 1, rank_hi, digits)
    digits = jnp.where(lane == 2, cls.astype(F32), digits)
    keys = lax.dot_general(pick_ref[...], digits.astype(BF16), (((1,), (1,)), ((), ())),
                           preferred_element_type=F32)
    key_ref[0] = keys[0:1].astype(jnp.int32)


def _route(h2, w_router, bias):
    n, d = h2.shape
    t = ROUTE_ROWS
    tri = (jnp.arange(t)[:, None] > jnp.arange(t)[None, :]).astype(BF16)
    pick = jnp.zeros((SUBLANES, LANES), F32).at[0, :3].set(
        jnp.asarray([1.0, KEY_DIGIT, KEY_DIGIT * KEY_DIGIT], F32)).astype(BF16)
    return pl.pallas_call(
        _route_kernel,
        out_shape=(jax.ShapeDtypeStruct((n, LANES), F32),
                   jax.ShapeDtypeStruct((n // t, 1, t), jnp.int32),
                   jax.ShapeDtypeStruct((1, LANES), F32)),
        grid=(n // t,),
        in_specs=[pl.BlockSpec((t, d), lambda i: (i, 0)),
                  _const_spec((d, LANES)), _const_spec((1, LANES)), _const_spec((t, t)),
                  _const_spec((SUBLANES, LANES))],
        out_specs=(pl.BlockSpec((t, LANES), lambda i: (i, 0)),
                   pl.BlockSpec((1, 1, t), lambda i: (i, 0, 0)),
                   _const_spec((1, LANES))),
        scratch_shapes=[pltpu.VMEM((1, LANES), F32)],
        compiler_params=_cparams(("arbitrary",)),
        name="moe_route",
    )(h2, w_router, bias, tri, pick)


def _slot_kernel(start_ref, key_ref, slot_ref):
    key = key_ref[...]
    cls = key >> KEY_SHIFT
    slot = key & (KEY_DIGIT * KEY_DIGIT - 1)
    for c in range(N_CLASSES):
        slot = slot + jnp.where(cls == c, start_ref[c], 0)
    slot_ref[...] = slot


def _slots(class_start, key3):
    g, _, t = key3.shape
    return pl.pallas_call(
        _slot_kernel,
        out_shape=jax.ShapeDtypeStruct(key3.shape, jnp.int32),
        grid_spec=pltpu.PrefetchScalarGridSpec(
            num_scalar_prefetch=1, grid=(1,),
            in_specs=[pl.BlockSpec((g, 1, t), lambda i, s: (0, 0, 0))],
            out_specs=pl.BlockSpec((g, 1, t), lambda i, s: (0, 0, 0))),
        compiler_params=_cparams(("arbitrary",)),
        name="moe_slots",
    )(class_start, key3)


def _for_rows(n_rows, fn):
    def body(g, c):
        for j in range(ROW_COPY_UNROLL):
            fn(g * ROW_COPY_UNROLL + j, j % 2)
        return c

    lax.fori_loop(0, n_rows // ROW_COPY_UNROLL, body, 0)


def _dispatch_kernel(tile_end_ref, n_used_ref, slot_ref, h_ref, info_ref, xs_ref, rows, sems, zsem):
    i = pl.program_id(0)
    last = pl.num_programs(0) - 1
    p = i % 2
    d = h_ref.shape[1]
    t = DISPATCH_ROWS
    te = EXPERT_ROWS

    def row_copy(buf, r):
        return pltpu.make_async_copy(rows.at[buf, pl.ds(r, 1)],
                                     xs_ref.at[pl.ds(slot_ref[0, 0, r], 1)], sems.at[buf])

    @pl.when(i == 0)
    def _():
        rows[1, :te] = jnp.zeros((te, d + LANES), F32)

        def zero_tile(tile):
            return pltpu.make_async_copy(rows.at[1, pl.ds(0, te)],
                                         xs_ref.at[pl.ds(tile * te, te)], zsem)

        prev_end = 0
        for c in range(N_CLASSES):
            end = tile_end_ref[c]

            @pl.when(end > prev_end)
            def _(end=end):
                cp = zero_tile(end - 1)
                cp.start()
                cp.wait()
            prev_end = end

        def zero_tail(tile, c):
            cp = zero_tile(tile)
            cp.start()
            cp.wait()
            return c

        lax.fori_loop(n_used_ref[0], xs_ref.shape[0] // te, zero_tail, 0)

    rows[p, :, :d] = h_ref[...]
    rows[p, :, d:] = info_ref[...]
    _for_rows(t, lambda r, q: row_copy(p, r).start(priority=q))

    def row_wait(buf):
        pltpu.make_async_copy(rows.at[buf, pl.ds(0, 1)], xs_ref.at[pl.ds(0, 1)],
                              sems.at[buf]).wait()

    @pl.when(i > 0)
    def _():
        _for_rows(t, lambda r, q: row_wait(1 - p))

    @pl.when(i == last)
    def _():
        _for_rows(t, lambda r, q: row_wait(p))


def _dispatch(tile_end, n_used, slot3, h2, info, n_slots):
    n, d = h2.shape
    t = DISPATCH_ROWS
    return pl.pallas_call(
        _dispatch_kernel,
        out_shape=jax.ShapeDtypeStruct((n_slots, d + LANES), F32),
        grid_spec=pltpu.PrefetchScalarGridSpec(
            num_scalar_prefetch=2, grid=(n // t,),
            in_specs=[pl.BlockSpec((1, 1, t), lambda i, *_: (i, 0, 0), memory_space=pltpu.SMEM),
                      pl.BlockSpec((t, d), lambda i, *_: (i, 0)),
                      pl.BlockSpec((t, LANES), lambda i, *_: (i, 0))],
            out_specs=pl.BlockSpec(memory_space=pl.ANY),
            scratch_shapes=[pltpu.VMEM((2, t, d + LANES), F32), pltpu.SemaphoreType.DMA((2,)),
                            pltpu.SemaphoreType.DMA(())]),
        compiler_params=_cparams(("arbitrary",)),
        name="moe_dispatch",
    )(tile_end, n_used, slot3, h2, info)


def _expert_kernel(n_used_ref, src_ref, lo_ref, hi_ref, xs_ref, wg_lo, wg_hi, wu_lo, wu_hi,
                   wd_lo, wd_hi, ln_g_ref, ln_b_ref, ys_ref):
    del src_ref, lo_ref, hi_ref

    @pl.when(pl.program_id(0) < n_used_ref[0])
    def _():
        d = ys_ref.shape[1]
        x = xs_ref[:, :d]
        xb = x.astype(BF16)
        y = None
        for wg, wu, wd, lane in ((wg_lo, wu_lo, wd_lo, INFO_W_LOW), (wg_hi, wu_hi, wd_hi, INFO_W_HIGH)):
            gate = jnp.dot(xb, wg[0], preferred_element_type=F32)
            up = jnp.dot(xb, wu[0], preferred_element_type=F32)
            act = gate * _sigmoid(gate) * up * xs_ref[:, d + lane:d + lane + 1]
            part = jnp.dot(act.astype(BF16), wd[0], preferred_element_type=F32)
            y = part if y is None else y + part
        ys_ref[...] = _layer_norm(ALPHA * x + y, ln_g_ref[...], ln_b_ref[...])

    @pl.when(pl.program_id(0) >= n_used_ref[0])
    def _():
        ys_ref[...] = jnp.zeros_like(ys_ref)


def _experts(tile_meta, xs, layer, w_gate_b, w_up_b, w_down_b, ln_g, ln_b):
    n_slots, cols = xs.shape
    d = cols - LANES
    f = w_gate_b.shape[3]
    t = EXPERT_ROWS
    x_map = lambda i, n_used, src, lo, hi: (src[i], 0)
    lo_map = lambda i, n_used, src, lo, hi: (layer, lo[i], 0, 0)
    hi_map = lambda i, n_used, src, lo, hi: (layer, hi[i], 0, 0)
    const = lambda i, *_: (0, 0)
    grid_spec = pltpu.PrefetchScalarGridSpec(
        num_scalar_prefetch=4,
        grid=(n_slots // t,),
        in_specs=[pl.BlockSpec((t, cols), x_map),
                  pl.BlockSpec((None, 1, d, f), lo_map), pl.BlockSpec((None, 1, d, f), hi_map),
                  pl.BlockSpec((None, 1, d, f), lo_map), pl.BlockSpec((None, 1, d, f), hi_map),
                  pl.BlockSpec((None, 1, f, d), lo_map), pl.BlockSpec((None, 1, f, d), hi_map),
                  pl.BlockSpec((1, d), const), pl.BlockSpec((1, d), const)],
        out_specs=pl.BlockSpec((t, d), lambda i, *_: (i, 0)),
    )
    return pl.pallas_call(
        _expert_kernel,
        out_shape=jax.ShapeDtypeStruct((n_slots, d), F32),
        grid_spec=grid_spec,
        compiler_params=_cparams(("arbitrary",)),
        name="moe_experts",
    )(*tile_meta, xs, w_gate_b, w_gate_b, w_up_b, w_up_b, w_down_b, w_down_b,
      ln_g.reshape(1, d), ln_b.reshape(1, d))


def _combine_kernel(slot_ref, ys_ref, o_ref, buf, gsems, osems):
    i = pl.program_id(0)
    last = pl.num_programs(0) - 1
    p = i % 2
    t = DISPATCH_ROWS

    def tile_out(b, step):
        return pltpu.make_async_copy(buf.at[b], o_ref.at[pl.ds(step * t, t)], osems.at[b])

    def row_wait(b):
        pltpu.make_async_copy(ys_ref.at[pl.ds(0, 1)], buf.at[b, pl.ds(0, 1)], gsems.at[b]).wait()

    @pl.when(i >= 2)
    def _():
        tile_out(p, i - 2).wait()

    _for_rows(t, lambda r, q: pltpu.make_async_copy(
        ys_ref.at[pl.ds(slot_ref[0, 0, r], 1)], buf.at[p, pl.ds(r, 1)],
        gsems.at[p]).start(priority=q))

    @pl.when(i > 0)
    def _():
        _for_rows(t, lambda r, q: row_wait(1 - p))
        tile_out(1 - p, i - 1).start()

    @pl.when(i == last)
    def _():
        _for_rows(t, lambda r, q: row_wait(p))
        tile_out(p, i).start()
        tile_out(p, i).wait()

        @pl.when(i > 0)
        def _():
            tile_out(1 - p, i - 1).wait()


def _combine(slot3, ys, n):
    d = ys.shape[1]
    t = DISPATCH_ROWS
    return pl.pallas_call(
        _combine_kernel,
        out_shape=jax.ShapeDtypeStruct((n, d), F32),
        grid=(n // t,),
        in_specs=[pl.BlockSpec((1, 1, t), lambda i: (i, 0, 0), memory_space=pltpu.SMEM),
                  pl.BlockSpec(memory_space=pl.ANY)],
        out_specs=pl.BlockSpec(memory_space=pl.ANY),
        scratch_shapes=[pltpu.VMEM((2, t, d), F32), pltpu.SemaphoreType.DMA((2,)),
                        pltpu.SemaphoreType.DMA((2,))],
        compiler_params=_cparams(("arbitrary",)),
        name="moe_combine",
    )(slot3, ys)


def _pair_tables():
    lo, hi = [], []
    for g in range(N_GROUPS):
        for a in range(EXP_PER_GROUP):
            for b in range(a + 1, EXP_PER_GROUP):
                lo.append(g * EXP_PER_GROUP + a)
                hi.append(g * EXP_PER_GROUP + b)
    return jnp.asarray(lo, jnp.int32), jnp.asarray(hi, jnp.int32)


def _tile_plan(counts, n_tiles):
    t = EXPERT_ROWS
    counts = counts[0, :N_CLASSES].astype(jnp.int32)
    tiles_per_class = (counts + t - 1) // t
    tile_end = jnp.cumsum(tiles_per_class)
    class_start = (tile_end - tiles_per_class) * t
    n_used = tile_end[-1:]
    src = jnp.minimum(jnp.arange(n_tiles, dtype=jnp.int32), n_used - 1)
    tile_class = jnp.sum((tile_end[None, :] <= src[:, None]).astype(jnp.int32), axis=1)
    pair_lo, pair_hi = _pair_tables()
    return class_start, tile_end, (n_used, src, pair_lo[tile_class], pair_hi[tile_class])


def kernel(x, ln_in_g, ln_in_b, w_in, gate_b, lru_conv_w, lru_conv_b, lru_wa, lru_ba, lru_wx, lru_bx, lru_lambda, sc_conv_w, w_branch_sb, w_branch_lru, w_branch_sc, w_out, ln1_g, ln1_b, w_group, group_bias, w_expert_router, expert_bias, w_gate, w_up, w_down, ln2_g, ln2_b):
    b, s, d = x.shape
    n = b * s
    params = dict(gate_b=gate_b, lru_conv_w=lru_conv_w, lru_conv_b=lru_conv_b, lru_wa=lru_wa,
                  lru_ba=lru_ba, lru_wx=lru_wx, lru_bx=lru_bx, lru_lambda=lru_lambda,
                  sc_conv_w=sc_conv_w, w_branch_sb=w_branch_sb, w_branch_lru=w_branch_lru,
                  w_branch_sc=w_branch_sc, w_out=w_out, ln1_g=ln1_g, ln1_b=ln1_b)
    w_in_b, w_gate_b, w_up_b, w_down_b = (w.astype(BF16) for w in (w_in, w_gate, w_up, w_down))
    h = x.reshape(n, d)
    for l in range(w_in.shape[0]):
        if l == 0:
            h, qkv, conv_in, act = _inproj(h, w_in_b, l, gate_b[l], entry_ln=(ln_in_g, ln_in_b))
        else:
            qkv, conv_in, act = _inproj(h, w_in_b, l, gate_b[l])
        y_sb = _sb_attention(qkv.reshape(b, s, -1))
        h1 = _mixer_tail(h.reshape(b, s, d), y_sb, conv_in.reshape(b, s, -1),
                         act.reshape(b, s, -1), params, l)
        h1 = h1.reshape(n, d)
        pad = LANES - N_EXPERTS - N_GROUPS
        w_router = jnp.concatenate(
            [w_expert_router[l], w_group[l], jnp.zeros((d, pad), F32)], axis=1).astype(F32)
        bias = jnp.concatenate(
            [expert_bias[l], group_bias[l], jnp.zeros((pad,), F32)]).reshape(1, LANES).astype(F32)
        info, key3, counts = _route(h1, w_router, bias)
        n_tiles = n // EXPERT_ROWS + N_CLASSES
        class_start, tile_end, tile_meta = _tile_plan(counts, n_tiles)
        slot3 = _slots(class_start, key3)
        xs = _dispatch(tile_end, tile_meta[0], slot3, h1, info, n_tiles * EXPERT_ROWS)
        ys = _experts(tile_meta, xs, l, w_gate_b, w_up_b, w_down_b, ln2_g[l], ln2_b[l])
        h = _combine(slot3, ys, n)
    return h.reshape(b, s, d)
```

```python
import functools
import math

import jax
import jax.numpy as jnp
from jax import lax
from jax.experimental import pallas as pl
from jax.experimental.pallas import tpu as pltpu

F32 = jnp.float32
BF16 = jnp.bfloat16

SB_HEADS = 8
SB_HEAD_DIM = 64
SB_WIDTH = SB_HEADS * SB_HEAD_DIM
LRU_WIDTH = 512
LRU_BLOCKS = 8
LRU_C = 8.0
SC_WIDTH = 512
N_GROUPS = 4
EXP_PER_GROUP = 4
N_EXPERTS = N_GROUPS * EXP_PER_GROUP
D_EXPERT = 512
DEPTH = 2
ALPHA = (2 * DEPTH) ** 0.25
LN_EPS = 1e-5

LANES = 128
SUBLANES = 8
VMEM_LIMIT_BYTES = 56 * 1024 * 1024

PROJ_ROWS = 512
PROJ_CHUNK = 512
ATT_Q = 256
ATT_K = 256
ATT_LANE_BLOCKS = 2
MIX_ROWS = 256
ROUTE_ROWS = 512
DISPATCH_ROWS = 512
EXPERT_ROWS = 256
ROW_COPY_UNROLL = 8

PAIRS_PER_GROUP = EXP_PER_GROUP * (EXP_PER_GROUP - 1) // 2
N_CLASSES = N_GROUPS * PAIRS_PER_GROUP
CLASS_ROWS = 32
INFO_W_LOW, INFO_W_HIGH = 0, 1
KEY_SHIFT = 16


def _cparams(sem):
    return pltpu.CompilerParams(dimension_semantics=sem, vmem_limit_bytes=VMEM_LIMIT_BYTES)


def _const_spec(shape):
    nd = len(shape)
    return pl.BlockSpec(shape, lambda *_: (0,) * nd)


def _layer_norm(y, g, b):
    mu = jnp.mean(y, axis=-1, keepdims=True)
    d = y - mu
    var = jnp.mean(d * d, axis=-1, keepdims=True)
    return d * lax.rsqrt(var + LN_EPS) * g + b


QKV_COLS = 3 * SB_WIDTH
BRANCH_COLS = 2 * LRU_WIDTH + 3 * SC_WIDTH


def _sigmoid(x):
    return 0.5 * jnp.tanh(0.5 * x) + 0.5


def _gelu_tanh(x):
    return 0.5 * x * (1.0 + jnp.tanh(math.sqrt(2.0 / math.pi) * (x + 0.044715 * (x * x * x))))


def _project(h, w_ref, gate_b_ref, qkv_ref, conv_ref, act_ref):
    hb = h.astype(BF16)
    c = PROJ_CHUNK

    def proj(chunk):
        return jnp.dot(hb, w_ref[:, chunk * c:(chunk + 1) * c], preferred_element_type=F32)

    n_qkv = QKV_COLS // c
    for j in range(n_qkv):
        acc = proj(j)
        if j * c < SB_WIDTH:
            acc = acc * (SB_HEAD_DIM ** -0.5 * LOG2_E)
        qkv_ref[:, j * c:(j + 1) * c] = acc.astype(BF16)
    lru_in, lru_gate, sc_b, sc_c, sc_h = range(n_qkv, n_qkv + 5)
    conv_ref[:, 0:c] = proj(lru_in)
    conv_ref[:, c:2 * c] = proj(sc_b)
    conv_ref[:, 2 * c:3 * c] = proj(sc_c) * proj(sc_h)
    for j in range(gate_b_ref.shape[1] // c):
        acc = proj(sc_h + 1 + j) + gate_b_ref[:, j * c:(j + 1) * c]
        act_ref[:, j * c:(j + 1) * c] = _sigmoid(acc).astype(BF16)
    act_ref[:, gate_b_ref.shape[1]:] = _gelu_tanh(proj(lru_gate)).astype(BF16)


def _inproj_kernel(h_ref, w_ref, gate_b_ref, qkv_ref, conv_ref, act_ref):
    _project(h_ref[...], w_ref, gate_b_ref, qkv_ref, conv_ref, act_ref)


def _ln_inproj_kernel(x_ref, g_ref, b_ref, w_ref, gate_b_ref, h_ref, qkv_ref, conv_ref, act_ref):
    h = _layer_norm(x_ref[...], g_ref[...], b_ref[...])
    h_ref[...] = h
    _project(h, w_ref, gate_b_ref, qkv_ref, conv_ref, act_ref)


def _inproj(h2, w_in_b, layer, gate_b, entry_ln=None):
    n, d = h2.shape
    cols = w_in_b.shape[2]
    gate_cols = gate_b.size
    assert LRU_WIDTH == SC_WIDTH == PROJ_CHUNK and cols == QKV_COLS + BRANCH_COLS + gate_cols
    widths = (QKV_COLS, 3 * PROJ_CHUNK, gate_cols + LRU_WIDTH)
    dtypes = (BF16, F32, BF16)
    rows = pl.BlockSpec((PROJ_ROWS, d), lambda i: (i, 0))
    weight_specs = [pl.BlockSpec((None, d, cols), lambda i: (layer, 0, 0),
                                 pipeline_mode=pl.Buffered(1)),
                    _const_spec((1, gate_cols))]
    weights = (w_in_b, gate_b.reshape(1, gate_cols).astype(F32))
    out_shape = [jax.ShapeDtypeStruct((n, w), t) for w, t in zip(widths, dtypes)]
    out_specs = [pl.BlockSpec((PROJ_ROWS, w), lambda i: (i, 0)) for w in widths]
    if entry_ln is None:
        body, ln_specs, ln_args = _inproj_kernel, [], ()
    else:
        body = _ln_inproj_kernel
        ln_specs = [_const_spec((1, d)), _const_spec((1, d))]
        ln_args = tuple(v.reshape(1, d).astype(F32) for v in entry_ln)
        out_shape.insert(0, jax.ShapeDtypeStruct((n, d), F32))
        out_specs.insert(0, rows)
    return pl.pallas_call(
        body,
        out_shape=tuple(out_shape),
        grid=(n // PROJ_ROWS,),
        in_specs=[rows] + ln_specs + weight_specs,
        out_specs=tuple(out_specs),
        compiler_params=_cparams(("parallel",)),
        name="in_proj",
    )(h2, *ln_args, *weights)


LOG2_E = math.log2(math.e)
PASS_LOG2_FLOOR = -151.0


HEADS_PER_STEP = LANES // SB_HEAD_DIM


def _per_head(x):
    lane = lax.broadcasted_iota(jnp.int32, x.shape, 1)
    return jnp.concatenate(
        [jnp.where((lane >= h * SB_HEAD_DIM) & (lane < (h + 1) * SB_HEAD_DIM), x, jnp.zeros_like(x))
         for h in range(HEADS_PER_STEP)], axis=0)


MASKED_SCORE = -1e30


def _neg_abs(x):
    bits = lax.bitcast_convert_type(x, jnp.uint32) | jnp.uint32(0x80000000)
    return lax.bitcast_convert_type(bits, F32)


def _sb_block(q, k_j, v_j, neg_tri2, carries, mask):
    nq, kk = q.shape[0], k_j.shape[0]
    z = lax.dot_general(q, _per_head(k_j), (((1,), (1,)), ((), ())), preferred_element_type=F32)
    scores, split = [], []
    for h in range(HEADS_PER_STEP):
        z_h = z[:, h * kk:(h + 1) * kk]
        if mask is not None:
            z_h = jnp.where(mask, z_h, MASKED_SCORE)
        drop = jnp.maximum(z_h, 0.0) + jnp.log2(1.0 + jnp.exp2(_neg_abs(z_h)))
        hi = drop.astype(BF16)
        lo = (drop - hi.astype(F32)).astype(BF16)
        scores.append(z_h)
        split.append(jnp.concatenate([hi, lo], axis=1))
    log_pass = jnp.dot(jnp.concatenate(split, axis=0), neg_tri2, preferred_element_type=F32)
    w, new_carries = [], []
    for h in range(HEADS_PER_STEP):
        lp_h = log_pass[h * nq:(h + 1) * nq]
        w.append(jnp.exp2(scores[h] + lp_h + carries[h]).astype(BF16))
        new_carries.append(carries[h] + lp_h[:, 0:1])
    pv = jnp.dot(jnp.concatenate(w, axis=1), _per_head(v_j), preferred_element_type=F32)
    return pv, new_carries


def _attn_kernel(q_ref, k_ref, v_ref, u2_ref, o_ref):
    qi = pl.program_id(2)
    u2 = u2_ref[...]
    row = lax.broadcasted_iota(jnp.int32, (ATT_Q, ATT_K), 0)
    col = lax.broadcasted_iota(jnp.int32, (ATT_Q, ATT_K), 1)
    diag_mask = col < row
    blocks = [slice(g * LANES, (g + 1) * LANES) for g in range(ATT_LANE_BLOCKS)]
    q = [q_ref[0, :, lanes] for lanes in blocks]

    def sweep(j, accs, carries, mask):
        rows = pl.ds(pl.multiple_of(j * ATT_K, ATT_K), ATT_K)
        out = [_sb_block(q_g, k_ref[0, rows, lanes], v_ref[0, rows, lanes], u2, c_g, mask)
               for q_g, lanes, c_g in zip(q, blocks, carries)]
        return [acc + pv for acc, (pv, _) in zip(accs, out)], [c for _, c in out]

    zero = [[jnp.zeros((ATT_Q, 1), F32) for _ in range(HEADS_PER_STEP)] for _ in blocks]
    accs, carries = sweep(qi, [jnp.zeros((ATT_Q, LANES), F32) for _ in blocks], zero, diag_mask)
    no_prev = jnp.where(qi > 0, 0.0, -1e30)
    accs, carries = sweep(jnp.maximum(qi - 1, 0), accs,
                          [[c + no_prev for c in c_g] for c_g in carries], None)

    def largest(carries):
        return jnp.max(functools.reduce(jnp.maximum, [c for c_g in carries for c in c_g]))

    def cond(state):
        j, _, _, live = state
        return jnp.logical_and(j >= 0, live > PASS_LOG2_FLOOR)

    def body(state):
        j, accs, carries, _ = state
        accs, carries = sweep(j, accs, carries, None)
        return j - 1, accs, carries, largest(carries)

    _, accs, _, _ = lax.while_loop(cond, body, (qi - 2, accs, carries, largest(carries)))
    o_ref[0] = jnp.concatenate(accs, axis=1).astype(o_ref.dtype)


def _sb_attention(qkv3):
    b, s, _ = qkv3.shape
    width = ATT_LANE_BLOCKS * LANES
    groups = SB_WIDTH // width
    neg_tri = -(jnp.arange(ATT_K)[:, None] >= jnp.arange(ATT_K)[None, :]).astype(BF16)
    u2 = jnp.concatenate([neg_tri, neg_tri], axis=0)
    return pl.pallas_call(
        _attn_kernel,
        out_shape=jax.ShapeDtypeStruct((b, s, SB_WIDTH), BF16),
        grid=(b, groups, s // ATT_Q),
        in_specs=[pl.BlockSpec((1, ATT_Q, width), lambda bi, p, qi: (bi, qi, p)),
                  pl.BlockSpec((1, s, width), lambda bi, p, qi: (bi, 0, groups + p)),
                  pl.BlockSpec((1, s, width), lambda bi, p, qi: (bi, 0, 2 * groups + p)),
                  _const_spec((2 * ATT_K, ATT_K))],
        out_specs=pl.BlockSpec((1, ATT_Q, width), lambda bi, p, qi: (bi, qi, p)),
        compiler_params=_cparams(("parallel", "parallel", "arbitrary")),
        name="sb_attention",
    )(qkv3, qkv3, qkv3, u2)


def _causal_conv(x, stage, w):
    t = x.shape[0]
    k = w.shape[0]
    stage[SUBLANES:, :] = x
    out = w[k - 1:k] * x
    for j in range(1, k):
        out = out + w[k - 1 - j:k - j] * stage[pl.ds(SUBLANES - j, t), :]
    stage[:SUBLANES, :] = x[t - SUBLANES:]
    return out


def _linear_scan(a, b, h0):
    t = a.shape[0]
    sub = lax.broadcasted_iota(jnp.int32, a.shape, 0) % SUBLANES
    d = 1
    while d < SUBLANES:
        keep = sub >= d
        a_prev = jnp.where(keep, pltpu.roll(a, d, axis=0), 1.0)
        b_prev = jnp.where(keep, pltpu.roll(b, d, axis=0), 0.0)
        b = a * b_prev + b
        a = a * a_prev
        d *= 2
    h = h0
    groups = []
    for g in range(t // SUBLANES):
        rows = slice(g * SUBLANES, (g + 1) * SUBLANES)
        seg = b[rows] + a[rows] * h
        h = seg[SUBLANES - 1:]
        groups.append(seg)
    return jnp.concatenate(groups, axis=0)


def _mix_kernel(h_ref, ysb_ref, lin_ref, scb_ref, scp_ref, gsb_ref, glru_ref, gsc_ref, gelu_ref,
                lconv_w_ref, lconv_b_ref, wa_ref, ba_ref, wx_ref, bx_ref, lam_ref, scw_ref,
                wsb_ref, wlru_ref, wsc_ref, wout_ref, ln_g_ref, ln_b_ref,
                out_ref, lin_stage, sc_stage, h_state):
    t = MIX_ROWS

    @pl.when(pl.program_id(1) == 0)
    def _():
        lin_stage[:SUBLANES, :] = jnp.zeros((SUBLANES, LRU_WIDTH), F32)
        sc_stage[:SUBLANES, :] = jnp.zeros((SUBLANES, SC_WIDTH), F32)
        h_state[...] = jnp.zeros_like(h_state)

    u = _causal_conv(lin_ref[0], lin_stage, lconv_w_ref[...]) + lconv_b_ref[...]
    ub = u.astype(BF16)
    r = _sigmoid(jnp.dot(ub, wa_ref[...], preferred_element_type=F32) + ba_ref[...])
    i = _sigmoid(jnp.dot(ub, wx_ref[...], preferred_element_type=F32) + bx_ref[...])
    lam = lam_ref[...]
    softplus_neg_lam = jnp.maximum(-lam, 0.0) + jnp.log(1.0 + jnp.exp(-jnp.abs(lam)))
    log_a = (-LRU_C) * r * softplus_neg_lam
    a = jnp.exp(log_a)
    gap = 1.0 - a * a
    drive = jnp.where(gap > 0.0, gap * lax.rsqrt(gap), 0.0) * (i * u)
    hs = _linear_scan(a, drive, h_state[...])
    h_state[...] = hs[t - 1:]
    y_lru = gelu_ref[0] * hs.astype(BF16)

    y_sc = (scb_ref[0] * _causal_conv(scp_ref[0], sc_stage, scw_ref[...])).astype(BF16)

    def branch(y, w_ref):
        return jnp.dot(y, w_ref[...], preferred_element_type=F32).astype(BF16)

    merged = (gsb_ref[0] * branch(ysb_ref[0], wsb_ref) + glru_ref[0] * branch(y_lru, wlru_ref)
              + gsc_ref[0] * branch(y_sc, wsc_ref))
    mix = jnp.dot(merged, wout_ref[...], preferred_element_type=F32)
    out_ref[0] = _layer_norm(ALPHA * h_ref[0] + mix, ln_g_ref[...], ln_b_ref[...])


def _block_diag(w):
    hh, ii, jj = w.shape
    eye = jnp.eye(hh, dtype=w.dtype)
    return (eye[:, None, :, None] * w[:, :, None, :]).reshape(hh * ii, hh * jj)


def _mixer_tail(h3, ysb3, conv3, act3, p, l):
    b, s, d = h3.shape
    t = MIX_ROWS
    assert LRU_WIDTH == SC_WIDTH

    def row_spec(width, col):
        return pl.BlockSpec((1, t, width), lambda bi, si: (bi, si, col))

    vec = lambda v: v.reshape(1, -1).astype(F32)
    in_specs = [row_spec(d, 0), row_spec(SB_WIDTH, 0)]
    in_specs += [row_spec(LRU_WIDTH, c) for c in range(3)]
    in_specs += [row_spec(d, g) for g in range(3)]
    in_specs += [row_spec(LRU_WIDTH, 3 * d // LRU_WIDTH)]
    weights = [
        p["lru_conv_w"][l].astype(F32), vec(p["lru_conv_b"][l]),
        _block_diag(p["lru_wa"][l]).astype(BF16), vec(p["lru_ba"][l]),
        _block_diag(p["lru_wx"][l]).astype(BF16), vec(p["lru_bx"][l]),
        vec(p["lru_lambda"][l]), p["sc_conv_w"][l].astype(F32),
        p["w_branch_sb"][l].astype(BF16), p["w_branch_lru"][l].astype(BF16),
        p["w_branch_sc"][l].astype(BF16), p["w_out"][l].astype(BF16),
        vec(p["ln1_g"][l]), vec(p["ln1_b"][l]),
    ]
    in_specs += [_const_spec(w.shape) for w in weights]
    return pl.pallas_call(
        _mix_kernel,
        out_shape=jax.ShapeDtypeStruct((b, s, d), F32),
        grid=(b, s // t),
        in_specs=in_specs,
        out_specs=pl.BlockSpec((1, t, d), lambda bi, si: (bi, si, 0)),
        scratch_shapes=[pltpu.VMEM((SUBLANES + t, LRU_WIDTH), F32),
                        pltpu.VMEM((SUBLANES + t, SC_WIDTH), F32),
                        pltpu.VMEM((1, LRU_WIDTH), F32)],
        compiler_params=_cparams(("parallel", "arbitrary")),
        name="mixer_tail",
    )(h3, ysb3, *([conv3] * 3), *([act3] * 4), *weights)


def _first_argmax(vals, row):
    m = jnp.max(vals, axis=0, keepdims=True)
    idx = jnp.min(jnp.where(vals == m, row, vals.shape[0]), axis=0, keepdims=True)
    return m, idx


def _route_kernel(h_ref, w_ref, bias_ref, before_ref, info_ref, key_ref, counts_ref, running):
    @pl.when(pl.program_id(0) == 0)
    def _():
        running[...] = jnp.zeros_like(running)

    h = h_ref[...]
    h_hi = h.astype(BF16)
    h_lo = (h - h_hi.astype(F32)).astype(BF16)
    w = w_ref[...]
    w_hi = w.astype(BF16)
    w_lo = (w - w_hi.astype(F32)).astype(BF16)
    logits = jnp.dot(jnp.concatenate([h_hi, h_lo, h_hi], axis=1),
                     jnp.concatenate([w_hi, w_hi, w_lo], axis=0), preferred_element_type=F32)
    lt = logits.T
    neg = -jnp.inf
    experts = lt[:N_EXPERTS]
    groups = lt[N_EXPERTS:N_EXPERTS + SUBLANES]
    experts_b = experts + bias_ref[:N_EXPERTS]
    groups_b = groups + bias_ref[N_EXPERTS:N_EXPERTS + SUBLANES]
    row_g = lax.broadcasted_iota(jnp.int32, groups.shape, 0)
    row_e = lax.broadcasted_iota(jnp.int32, experts.shape, 0)
    is_group = row_g < N_GROUPS
    g_max = jnp.max(jnp.where(is_group, groups, neg), axis=0, keepdims=True)
    g_exp = jnp.where(is_group, jnp.exp(groups - g_max), 0.0)
    g_den = jnp.sum(g_exp, axis=0, keepdims=True)
    _, g_sel = _first_argmax(jnp.where(is_group, groups_b, neg), row_g)
    g_prob = jnp.sum(jnp.where(row_g == g_sel, g_exp, 0.0), axis=0, keepdims=True) / g_den
    cand = jnp.where((row_e >> 2) == g_sel, experts_b, neg)
    _, e1 = _first_argmax(cand, row_e)
    _, e2 = _first_argmax(jnp.where(row_e == e1, neg, cand), row_e)
    l1 = jnp.sum(jnp.where(row_e == e1, experts, 0.0), axis=0, keepdims=True)
    l2 = jnp.sum(jnp.where(row_e == e2, experts, 0.0), axis=0, keepdims=True)
    m = jnp.maximum(l1, l2)
    x1 = jnp.exp(l1 - m)
    x2 = jnp.exp(l2 - m)
    scale = g_prob / (x1 + x2)
    w1 = x1 * scale
    w2 = x2 * scale
    first_low = e1 < e2
    a = jnp.where(first_low, e1, e2) - g_sel * EXP_PER_GROUP
    b = jnp.where(first_low, e2, e1) - g_sel * EXP_PER_GROUP
    cls = g_sel * PAIRS_PER_GROUP + ((a * (2 * EXP_PER_GROUP - 1 - a)) >> 1) + (b - a - 1)
    row_c = lax.broadcasted_iota(jnp.int32, (CLASS_ROWS, cls.shape[1]), 0)
    onehot = row_c == cls
    ones = jnp.where(onehot, 1.0, 0.0)
    earlier = jnp.dot(ones, before_ref[...], preferred_element_type=F32) + running[...]
    rank = jnp.sum(jnp.where(onehot, earlier, 0.0), axis=0, keepdims=True)
    running[...] += jnp.sum(ones, axis=1, keepdims=True)
    counts_ref[...] = running[...]
    key_ref[0] = (cls << KEY_SHIFT) + rank.astype(jnp.int32)
    row_w = lax.broadcasted_iota(jnp.int32, (LANES, cls.shape[1]), 0)
    w_rows = jnp.where(row_w == INFO_W_LOW, jnp.where(first_low, w1, w2), 0.0)
    w_rows = jnp.where(row_w == INFO_W_HIGH, jnp.where(first_low, w2, w1), w_rows)
    info_ref[...] = w_rows.T


def _route(h2, w_router, bias):
    n, d = h2.shape
    t = ROUTE_ROWS
    before = (jnp.arange(t)[:, None] < jnp.arange(t)[None, :]).astype(F32)
    return pl.pallas_call(
        _route_kernel,
        out_shape=(jax.ShapeDtypeStruct((n, LANES), F32),
                   jax.ShapeDtypeStruct((n // t, 1, t), jnp.int32),
                   jax.ShapeDtypeStruct((CLASS_ROWS, 1), F32)),
        grid=(n // t,),
        in_specs=[pl.BlockSpec((t, d), lambda i: (i, 0)),
                  _const_spec((d, LANES)), _const_spec((LANES, 1)), _const_spec((t, t))],
        out_specs=(pl.BlockSpec((t, LANES), lambda i: (i, 0)),
                   pl.BlockSpec((1, 1, t), lambda i: (i, 0, 0)),
                   _const_spec((CLASS_ROWS, 1))),
        scratch_shapes=[pltpu.VMEM((CLASS_ROWS, 1), F32)],
        compiler_params=_cparams(("arbitrary",)),
        name="moe_route",
    )(h2, w_router, bias, before)


def _slot_kernel(start_ref, key_ref, slot_ref):
    key = key_ref[...]
    cls = key >> KEY_SHIFT
    slot = key & ((1 << KEY_SHIFT) - 1)
    for c in range(N_CLASSES):
        slot = slot + jnp.where(cls == c, start_ref[c], 0)
    slot_ref[...] = slot


def _slots(class_start, key3):
    g, _, t = key3.shape
    return pl.pallas_call(
        _slot_kernel,
        out_shape=jax.ShapeDtypeStruct(key3.shape, jnp.int32),
        grid_spec=pltpu.PrefetchScalarGridSpec(
            num_scalar_prefetch=1, grid=(1,),
            in_specs=[pl.BlockSpec((g, 1, t), lambda i, s: (0, 0, 0))],
            out_specs=pl.BlockSpec((g, 1, t), lambda i, s: (0, 0, 0))),
        compiler_params=_cparams(("arbitrary",)),
        name="moe_slots",
    )(class_start, key3)


def _for_rows(n_rows, fn):
    def body(g, c):
        for j in range(ROW_COPY_UNROLL):
            fn(g * ROW_COPY_UNROLL + j, j % 2)
        return c

    lax.fori_loop(0, n_rows // ROW_COPY_UNROLL, body, 0)


def _dispatch_kernel(tile_end_ref, n_used_ref, slot_ref, h_ref, info_ref, xs_ref, rows, sems, zsem):
    i = pl.program_id(0)
    last = pl.num_programs(0) - 1
    p = i % 2
    d = h_ref.shape[1]
    t = DISPATCH_ROWS
    te = EXPERT_ROWS

    def row_copy(buf, r):
        return pltpu.make_async_copy(rows.at[buf, pl.ds(r, 1)],
                                     xs_ref.at[pl.ds(slot_ref[0, 0, r], 1)], sems.at[buf])

    @pl.when(i == 0)
    def _():
        rows[1, :te] = jnp.zeros((te, d + LANES), F32)

        def zero_tile(tile):
            return pltpu.make_async_copy(rows.at[1, pl.ds(0, te)],
                                         xs_ref.at[pl.ds(tile * te, te)], zsem)

        prev_end = 0
        for c in range(N_CLASSES):
            end = tile_end_ref[c]

            @pl.when(end > prev_end)
            def _(end=end):
                cp = zero_tile(end - 1)
                cp.start()
                cp.wait()
            prev_end = end

        def zero_tail(tile, c):
            cp = zero_tile(tile)
            cp.start()
            cp.wait()
            return c

        lax.fori_loop(n_used_ref[0], xs_ref.shape[0] // te, zero_tail, 0)

    rows[p, :, :d] = h_ref[...]
    rows[p, :, d:] = info_ref[...]
    _for_rows(t, lambda r, q: row_copy(p, r).start(priority=q))

    def row_wait(buf):
        pltpu.make_async_copy(rows.at[buf, pl.ds(0, 1)], xs_ref.at[pl.ds(0, 1)],
                              sems.at[buf]).wait()

    @pl.when(i > 0)
    def _():
        _for_rows(t, lambda r, q: row_wait(1 - p))

    @pl.when(i == last)
    def _():
        _for_rows(t, lambda r, q: row_wait(p))


def _dispatch(tile_end, n_used, slot3, h2, info, n_slots):
    n, d = h2.shape
    t = DISPATCH_ROWS
    return pl.pallas_call(
        _dispatch_kernel,
        out_shape=jax.ShapeDtypeStruct((n_slots, d + LANES), F32),
        grid_spec=pltpu.PrefetchScalarGridSpec(
            num_scalar_prefetch=2, grid=(n // t,),
            in_specs=[pl.BlockSpec((1, 1, t), lambda i, *_: (i, 0, 0), memory_space=pltpu.SMEM),
                      pl.BlockSpec((t, d), lambda i, *_: (i, 0)),
                      pl.BlockSpec((t, LANES), lambda i, *_: (i, 0))],
            out_specs=pl.BlockSpec(memory_space=pl.ANY),
            scratch_shapes=[pltpu.VMEM((2, t, d + LANES), F32), pltpu.SemaphoreType.DMA((2,)),
                            pltpu.SemaphoreType.DMA(())]),
        compiler_params=_cparams(("arbitrary",)),
        name="moe_dispatch",
    )(tile_end, n_used, slot3, h2, info)


def _expert_kernel(n_used_ref, src_ref, lo_ref, hi_ref, xs_ref, wg_lo, wg_hi, wu_lo, wu_hi,
                   wd_lo, wd_hi, ln_g_ref, ln_b_ref, ys_ref):
    del src_ref, lo_ref, hi_ref

    @pl.when(pl.program_id(0) < n_used_ref[0])
    def _():
        d = ys_ref.shape[1]
        x = xs_ref[:, :d]
        xb = x.astype(BF16)
        y = None
        for wg, wu, wd, lane in ((wg_lo, wu_lo, wd_lo, INFO_W_LOW), (wg_hi, wu_hi, wd_hi, INFO_W_HIGH)):
            gate = jnp.dot(xb, wg[0], preferred_element_type=F32)
            up = jnp.dot(xb, wu[0], preferred_element_type=F32)
            act = gate * _sigmoid(gate) * up * xs_ref[:, d + lane:d + lane + 1]
            part = jnp.dot(act.astype(BF16), wd[0], preferred_element_type=F32)
            y = part if y is None else y + part
        ys_ref[...] = _layer_norm(ALPHA * x + y, ln_g_ref[...], ln_b_ref[...])

    @pl.when(pl.program_id(0) >= n_used_ref[0])
    def _():
        ys_ref[...] = jnp.zeros_like(ys_ref)


def _experts(tile_meta, xs, layer, w_gate_b, w_up_b, w_down_b, ln_g, ln_b):
    n_slots, cols = xs.shape
    d = cols - LANES
    f = w_gate_b.shape[3]
    t = EXPERT_ROWS
    x_map = lambda i, n_used, src, lo, hi: (src[i], 0)
    lo_map = lambda i, n_used, src, lo, hi: (layer, lo[i], 0, 0)
    hi_map = lambda i, n_used, src, lo, hi: (layer, hi[i], 0, 0)
    const = lambda i, *_: (0, 0)
    grid_spec = pltpu.PrefetchScalarGridSpec(
        num_scalar_prefetch=4,
        grid=(n_slots // t,),
        in_specs=[pl.BlockSpec((t, cols), x_map),
                  pl.BlockSpec((None, 1, d, f), lo_map), pl.BlockSpec((None, 1, d, f), hi_map),
                  pl.BlockSpec((None, 1, d, f), lo_map), pl.BlockSpec((None, 1, d, f), hi_map),
                  pl.BlockSpec((None, 1, f, d), lo_map), pl.BlockSpec((None, 1, f, d), hi_map),
                  pl.BlockSpec((1, d), const), pl.BlockSpec((1, d), const)],
        out_specs=pl.BlockSpec((t, d), lambda i, *_: (i, 0)),
    )
    return pl.pallas_call(
        _expert_kernel,
        out_shape=jax.ShapeDtypeStruct((n_slots, d), F32),
        grid_spec=grid_spec,
        compiler_params=_cparams(("arbitrary",)),
        name="moe_experts",
    )(*tile_meta, xs, w_gate_b, w_gate_b, w_up_b, w_up_b, w_down_b, w_down_b,
      ln_g.reshape(1, d), ln_b.reshape(1, d))


def _combine_kernel(slot_ref, ys_ref, o_ref, buf, gsems, osems):
    i = pl.program_id(0)
    last = pl.num_programs(0) - 1
    p = i % 2
    t = DISPATCH_ROWS

    def tile_out(b, step):
        return pltpu.make_async_copy(buf.at[b], o_ref.at[pl.ds(step * t, t)], osems.at[b])

    def row_wait(b):
        pltpu.make_async_copy(ys_ref.at[pl.ds(0, 1)], buf.at[b, pl.ds(0, 1)], gsems.at[b]).wait()

    @pl.when(i >= 2)
    def _():
        tile_out(p, i - 2).wait()

    _for_rows(t, lambda r, q: pltpu.make_async_copy(
        ys_ref.at[pl.ds(slot_ref[0, 0, r], 1)], buf.at[p, pl.ds(r, 1)],
        gsems.at[p]).start(priority=q))

    @pl.when(i > 0)
    def _():
        _for_rows(t, lambda r, q: row_wait(1 - p))
        tile_out(1 - p, i - 1).start()

    @pl.when(i == last)
    def _():
        _for_rows(t, lambda r, q: row_wait(p))
        tile_out(p, i).start()
        tile_out(p, i).wait()

        @pl.when(i > 0)
        def _():
            tile_out(1 - p, i - 1).wait()


def _combine(slot3, ys, n):
    d = ys.shape[1]
    t = DISPATCH_ROWS
    return pl.pallas_call(
        _combine_kernel,
        out_shape=jax.ShapeDtypeStruct((n, d), F32),
        grid=(n // t,),
        in_specs=[pl.BlockSpec((1, 1, t), lambda i: (i, 0, 0), memory_space=pltpu.SMEM),
                  pl.BlockSpec(memory_space=pl.ANY)],
        out_specs=pl.BlockSpec(memory_space=pl.ANY),
        scratch_shapes=[pltpu.VMEM((2, t, d), F32), pltpu.SemaphoreType.DMA((2,)),
                        pltpu.SemaphoreType.DMA((2,))],
        compiler_params=_cparams(("arbitrary",)),
        name="moe_combine",
    )(slot3, ys)


def _pair_tables():
    lo, hi = [], []
    for g in range(N_GROUPS):
        for a in range(EXP_PER_GROUP):
            for b in range(a + 1, EXP_PER_GROUP):
                lo.append(g * EXP_PER_GROUP + a)
                hi.append(g * EXP_PER_GROUP + b)
    return jnp.asarray(lo, jnp.int32), jnp.asarray(hi, jnp.int32)


def _tile_plan(counts, n_tiles):
    t = EXPERT_ROWS
    counts = counts[:N_CLASSES, 0].astype(jnp.int32)
    tiles_per_class = (counts + t - 1) // t
    tile_end = jnp.cumsum(tiles_per_class)
    class_start = (tile_end - tiles_per_class) * t
    n_used = tile_end[-1:]
    src = jnp.minimum(jnp.arange(n_tiles, dtype=jnp.int32), n_used - 1)
    tile_class = jnp.sum((tile_end[None, :] <= src[:, None]).astype(jnp.int32), axis=1)
    pair_lo, pair_hi = _pair_tables()
    return class_start, tile_end, (n_used, src, pair_lo[tile_class], pair_hi[tile_class])


def kernel(x, ln_in_g, ln_in_b, w_in, gate_b, lru_conv_w, lru_conv_b, lru_wa, lru_ba, lru_wx, lru_bx, lru_lambda, sc_conv_w, w_branch_sb, w_branch_lru, w_branch_sc, w_out, ln1_g, ln1_b, w_group, group_bias, w_expert_router, expert_bias, w_gate, w_up, w_down, ln2_g, ln2_b):
    b, s, d = x.shape
    n = b * s
    params = dict(gate_b=gate_b, lru_conv_w=lru_conv_w, lru_conv_b=lru_conv_b, lru_wa=lru_wa,
                  lru_ba=lru_ba, lru_wx=lru_wx, lru_bx=lru_bx, lru_lambda=lru_lambda,
                  sc_conv_w=sc_conv_w, w_branch_sb=w_branch_sb, w_branch_lru=w_branch_lru,
                  w_branch_sc=w_branch_sc, w_out=w_out, ln1_g=ln1_g, ln1_b=ln1_b)
    w_in_b, w_gate_b, w_up_b, w_down_b = (w.astype(BF16) for w in (w_in, w_gate, w_up, w_down))
    h = x.reshape(n, d)
    for l in range(w_in.shape[0]):
        if l == 0:
            h, qkv, conv_in, act = _inproj(h, w_in_b, l, gate_b[l], entry_ln=(ln_in_g, ln_in_b))
        else:
            qkv, conv_in, act = _inproj(h, w_in_b, l, gate_b[l])
        y_sb = _sb_attention(qkv.reshape(b, s, -1))
        h1 = _mixer_tail(h.reshape(b, s, d), y_sb, conv_in.reshape(b, s, -1),
                         act.reshape(b, s, -1), params, l)
        h1 = h1.reshape(n, d)
        pad = LANES - N_EXPERTS - N_GROUPS
        w_router = jnp.concatenate(
            [w_expert_router[l], w_group[l], jnp.zeros((d, pad), F32)], axis=1).astype(F32)
        bias = jnp.concatenate(
            [expert_bias[l], group_bias[l], jnp.zeros((pad,), F32)]).reshape(LANES, 1).astype(F32)
        info, key3, counts = _route(h1, w_router, bias)
        n_tiles = n // EXPERT_ROWS + N_CLASSES
        class_start, tile_end, tile_meta = _tile_plan(counts, n_tiles)
        slot3 = _slots(class_start, key3)
        xs = _dispatch(tile_end, tile_meta[0], slot3, h1, info, n_tiles * EXPERT_ROWS)
        ys = _experts(tile_meta, xs, l, w_gate_b, w_up_b, w_down_b, ln2_g[l], ln2_b[l])
        h = _combine(slot3, ys, n)
    return h.reshape(b, s, d)
```

```python
import functools
import math

import jax
import jax.numpy as jnp
from jax import lax
from jax.experimental import pallas as pl
from jax.experimental.pallas import tpu as pltpu

F32 = jnp.float32
BF16 = jnp.bfloat16

SB_HEADS = 8
SB_HEAD_DIM = 64
SB_WIDTH = SB_HEADS * SB_HEAD_DIM
LRU_WIDTH = 512
LRU_BLOCKS = 8
LRU_C = 8.0
SC_WIDTH = 512
N_GROUPS = 4
EXP_PER_GROUP = 4
N_EXPERTS = N_GROUPS * EXP_PER_GROUP
D_EXPERT = 512
DEPTH = 2
ALPHA = (2 * DEPTH) ** 0.25
LN_EPS = 1e-5

LANES = 128
SUBLANES = 8
VMEM_LIMIT_BYTES = 56 * 1024 * 1024

PROJ_ROWS = 512
PROJ_CHUNK = 512
ATT_Q = 256
ATT_K = 256
ATT_LANE_BLOCKS = 2
MIX_ROWS = 256
ROUTE_ROWS = 512
DISPATCH_ROWS = 512
EXPERT_ROWS = 256
ROW_COPY_UNROLL = 8

PAIRS_PER_GROUP = EXP_PER_GROUP * (EXP_PER_GROUP - 1) // 2
N_CLASSES = N_GROUPS * PAIRS_PER_GROUP
CLASS_ROWS = 32
INFO_W_LOW, INFO_W_HIGH = 0, 1
KEY_SHIFT = 16


def _cparams(sem):
    return pltpu.CompilerParams(dimension_semantics=sem, vmem_limit_bytes=VMEM_LIMIT_BYTES)


def _const_spec(shape):
    nd = len(shape)
    return pl.BlockSpec(shape, lambda *_: (0,) * nd)


def _layer_norm(y, g, b):
    mu = jnp.mean(y, axis=-1, keepdims=True)
    d = y - mu
    var = jnp.mean(d * d, axis=-1, keepdims=True)
    return d * lax.rsqrt(var + LN_EPS) * g + b


QKV_COLS = 3 * SB_WIDTH
BRANCH_COLS = 2 * LRU_WIDTH + 3 * SC_WIDTH


def _sigmoid(x):
    return 0.5 * jnp.tanh(0.5 * x) + 0.5


def _gelu_tanh(x):
    return 0.5 * x * (1.0 + jnp.tanh(math.sqrt(2.0 / math.pi) * (x + 0.044715 * (x * x * x))))


def _project(h, w_ref, gate_b_ref, qkv_ref, conv_ref, act_ref):
    hb = h.astype(BF16)
    c = PROJ_CHUNK

    def proj(chunk):
        return jnp.dot(hb, w_ref[:, chunk * c:(chunk + 1) * c], preferred_element_type=F32)

    n_qkv = QKV_COLS // c
    for j in range(n_qkv):
        acc = proj(j)
        if j * c < SB_WIDTH:
            acc = acc * (SB_HEAD_DIM ** -0.5 * LOG2_E)
        qkv_ref[:, j * c:(j + 1) * c] = acc.astype(BF16)
    lru_in, lru_gate, sc_b, sc_c, sc_h = range(n_qkv, n_qkv + 5)
    conv_ref[:, 0:c] = proj(lru_in)
    conv_ref[:, c:2 * c] = proj(sc_b)
    conv_ref[:, 2 * c:3 * c] = proj(sc_c) * proj(sc_h)
    for j in range(gate_b_ref.shape[1] // c):
        acc = proj(sc_h + 1 + j) + gate_b_ref[:, j * c:(j + 1) * c]
        act_ref[:, j * c:(j + 1) * c] = _sigmoid(acc).astype(BF16)
    act_ref[:, gate_b_ref.shape[1]:] = _gelu_tanh(proj(lru_gate)).astype(BF16)


def _inproj_kernel(h_ref, w_ref, gate_b_ref, qkv_ref, conv_ref, act_ref):
    _project(h_ref[...], w_ref, gate_b_ref, qkv_ref, conv_ref, act_ref)


def _ln_inproj_kernel(x_ref, g_ref, b_ref, w_ref, gate_b_ref, h_ref, qkv_ref, conv_ref, act_ref):
    h = _layer_norm(x_ref[...], g_ref[...], b_ref[...])
    h_ref[...] = h
    _project(h, w_ref, gate_b_ref, qkv_ref, conv_ref, act_ref)


def _gather_inproj_kernel(slot_ref, slot_next_ref, ys_ref, w_ref, gate_b_ref,
                          h_ref, qkv_ref, conv_ref, act_ref, rows, sems):
    i = pl.program_id(0)
    last = pl.num_programs(0) - 1
    p = i % 2

    def fetch(slots, buf):
        _for_rows(PROJ_ROWS, lambda r, q: pltpu.make_async_copy(
            ys_ref.at[pl.ds(slots[0, 0, r], 1)], rows.at[buf, pl.ds(r, 1)],
            sems.at[buf]).start(priority=q))

    @pl.when(i == 0)
    def _():
        fetch(slot_ref, 0)

    @pl.when(i < last)
    def _():
        fetch(slot_next_ref, 1 - p)

    _for_rows(PROJ_ROWS, lambda r, q: pltpu.make_async_copy(
        ys_ref.at[pl.ds(0, 1)], rows.at[p, pl.ds(0, 1)], sems.at[p]).wait())
    h = rows[p]
    h_ref[...] = h
    _project(h, w_ref, gate_b_ref, qkv_ref, conv_ref, act_ref)


def _inproj(h2, w_in_b, layer, gate_b, entry_ln=None, slots=None):
    d = h2.shape[1]
    n = h2.shape[0] if slots is None else slots.size
    cols = w_in_b.shape[2]
    gate_cols = gate_b.size
    assert LRU_WIDTH == SC_WIDTH == PROJ_CHUNK and cols == QKV_COLS + BRANCH_COLS + gate_cols
    widths = (QKV_COLS, 3 * PROJ_CHUNK, gate_cols + LRU_WIDTH)
    dtypes = (BF16, F32, BF16)
    rows = pl.BlockSpec((PROJ_ROWS, d), lambda i: (i, 0))
    weight_specs = [pl.BlockSpec((None, d, cols), lambda i: (layer, 0, 0),
                                 pipeline_mode=pl.Buffered(1)),
                    _const_spec((1, gate_cols))]
    weights = (w_in_b, gate_b.reshape(1, gate_cols).astype(F32))
    out_shape = [jax.ShapeDtypeStruct((n, w), t) for w, t in zip(widths, dtypes)]
    out_specs = [pl.BlockSpec((PROJ_ROWS, w), lambda i: (i, 0)) for w in widths]
    if slots is not None:
        assert DISPATCH_ROWS == PROJ_ROWS
        steps = n // PROJ_ROWS
        slot_spec = lambda index: pl.BlockSpec((1, 1, PROJ_ROWS), index, memory_space=pltpu.SMEM)
        out_shape.insert(0, jax.ShapeDtypeStruct((n, d), F32))
        out_specs.insert(0, rows)
        return pl.pallas_call(
            _gather_inproj_kernel,
            out_shape=tuple(out_shape),
            grid=(steps,),
            in_specs=[slot_spec(lambda i: (i, 0, 0)),
                      slot_spec(lambda i: (jnp.minimum(i + 1, steps - 1), 0, 0)),
                      pl.BlockSpec(memory_space=pl.ANY)] + weight_specs,
            out_specs=tuple(out_specs),
            scratch_shapes=[pltpu.VMEM((2, PROJ_ROWS, d), F32), pltpu.SemaphoreType.DMA((2,))],
            compiler_params=_cparams(("arbitrary",)),
            name="in_proj_gather",
        )(slots, slots, h2, *weights)
    if entry_ln is None:
        body, ln_specs, ln_args = _inproj_kernel, [], ()
    else:
        body = _ln_inproj_kernel
        ln_specs = [_const_spec((1, d)), _const_spec((1, d))]
        ln_args = tuple(v.reshape(1, d).astype(F32) for v in entry_ln)
        out_shape.insert(0, jax.ShapeDtypeStruct((n, d), F32))
        out_specs.insert(0, rows)
    return pl.pallas_call(
        body,
        out_shape=tuple(out_shape),
        grid=(n // PROJ_ROWS,),
        in_specs=[rows] + ln_specs + weight_specs,
        out_specs=tuple(out_specs),
        compiler_params=_cparams(("parallel",)),
        name="in_proj",
    )(h2, *ln_args, *weights)


LOG2_E = math.log2(math.e)
PASS_LOG2_FLOOR = -151.0


HEADS_PER_STEP = LANES // SB_HEAD_DIM


def _per_head(x):
    lane = lax.broadcasted_iota(jnp.int32, x.shape, 1)
    return jnp.concatenate(
        [jnp.where((lane >= h * SB_HEAD_DIM) & (lane < (h + 1) * SB_HEAD_DIM), x, jnp.zeros_like(x))
         for h in range(HEADS_PER_STEP)], axis=0)


MASKED_SCORE = -1e30


def _neg_abs(x):
    bits = lax.bitcast_convert_type(x, jnp.uint32) | jnp.uint32(0x80000000)
    return lax.bitcast_convert_type(bits, F32)


def _sb_block(q, k_j, v_j, neg_tri2, carries, mask):
    nq, kk = q.shape[0], k_j.shape[0]
    z = lax.dot_general(q, _per_head(k_j), (((1,), (1,)), ((), ())), preferred_element_type=F32)
    scores, split = [], []
    for h in range(HEADS_PER_STEP):
        z_h = z[:, h * kk:(h + 1) * kk]
        if mask is not None:
            z_h = jnp.where(mask, z_h, MASKED_SCORE)
        drop = jnp.maximum(z_h, 0.0) + jnp.log2(1.0 + jnp.exp2(_neg_abs(z_h)))
        hi = drop.astype(BF16)
        lo = (drop - hi.astype(F32)).astype(BF16)
        scores.append(z_h)
        split.append(jnp.concatenate([hi, lo], axis=1))
    log_pass = jnp.dot(jnp.concatenate(split, axis=0), neg_tri2, preferred_element_type=F32)
    w, new_carries = [], []
    for h in range(HEADS_PER_STEP):
        lp_h = log_pass[h * nq:(h + 1) * nq]
        w.append(jnp.exp2(scores[h] + lp_h + carries[h]).astype(BF16))
        new_carries.append(carries[h] + lp_h[:, 0:1])
    pv = jnp.dot(jnp.concatenate(w, axis=1), _per_head(v_j), preferred_element_type=F32)
    return pv, new_carries


def _attn_kernel(q_ref, k_ref, v_ref, u2_ref, o_ref):
    qi = pl.program_id(2)
    u2 = u2_ref[...]
    row = lax.broadcasted_iota(jnp.int32, (ATT_Q, ATT_K), 0)
    col = lax.broadcasted_iota(jnp.int32, (ATT_Q, ATT_K), 1)
    diag_mask = col < row
    blocks = [slice(g * LANES, (g + 1) * LANES) for g in range(ATT_LANE_BLOCKS)]
    q = [q_ref[0, :, lanes] for lanes in blocks]

    def sweep(j, accs, carries, mask):
        rows = pl.ds(pl.multiple_of(j * ATT_K, ATT_K), ATT_K)
        out = [_sb_block(q_g, k_ref[0, rows, lanes], v_ref[0, rows, lanes], u2, c_g, mask)
               for q_g, lanes, c_g in zip(q, blocks, carries)]
        return [acc + pv for acc, (pv, _) in zip(accs, out)], [c for _, c in out]

    zero = [[jnp.zeros((ATT_Q, 1), F32) for _ in range(HEADS_PER_STEP)] for _ in blocks]
    accs, carries = sweep(qi, [jnp.zeros((ATT_Q, LANES), F32) for _ in blocks], zero, diag_mask)
    no_prev = jnp.where(qi > 0, 0.0, -1e30)
    accs, carries = sweep(jnp.maximum(qi - 1, 0), accs,
                          [[c + no_prev for c in c_g] for c_g in carries], None)

    def largest(carries):
        return jnp.max(functools.reduce(jnp.maximum, [c for c_g in carries for c in c_g]))

    def cond(state):
        j, _, _, live = state
        return jnp.logical_and(j >= 0, live > PASS_LOG2_FLOOR)

    def body(state):
        j, accs, carries, _ = state
        accs, carries = sweep(j, accs, carries, None)
        return j - 1, accs, carries, largest(carries)

    _, accs, _, _ = lax.while_loop(cond, body, (qi - 2, accs, carries, largest(carries)))
    o_ref[0] = jnp.concatenate(accs, axis=1).astype(o_ref.dtype)


def _sb_attention(qkv3):
    b, s, _ = qkv3.shape
    width = ATT_LANE_BLOCKS * LANES
    groups = SB_WIDTH // width
    neg_tri = -(jnp.arange(ATT_K)[:, None] >= jnp.arange(ATT_K)[None, :]).astype(BF16)
    u2 = jnp.concatenate([neg_tri, neg_tri], axis=0)
    return pl.pallas_call(
        _attn_kernel,
        out_shape=jax.ShapeDtypeStruct((b, s, SB_WIDTH), BF16),
        grid=(b, groups, s // ATT_Q),
        in_specs=[pl.BlockSpec((1, ATT_Q, width), lambda bi, p, qi: (bi, qi, p)),
                  pl.BlockSpec((1, s, width), lambda bi, p, qi: (bi, 0, groups + p)),
                  pl.BlockSpec((1, s, width), lambda bi, p, qi: (bi, 0, 2 * groups + p)),
                  _const_spec((2 * ATT_K, ATT_K))],
        out_specs=pl.BlockSpec((1, ATT_Q, width), lambda bi, p, qi: (bi, qi, p)),
        compiler_params=_cparams(("parallel", "parallel", "arbitrary")),
        name="sb_attention",
    )(qkv3, qkv3, qkv3, u2)


def _causal_conv(x, stage, w):
    t = x.shape[0]
    k = w.shape[0]
    stage[SUBLANES:, :] = x
    out = w[k - 1:k] * x
    for j in range(1, k):
        out = out + w[k - 1 - j:k - j] * stage[pl.ds(SUBLANES - j, t), :]
    stage[:SUBLANES, :] = x[t - SUBLANES:]
    return out


def _linear_scan(a, b, h0):
    t = a.shape[0]
    sub = lax.broadcasted_iota(jnp.int32, a.shape, 0) % SUBLANES
    d = 1
    while d < SUBLANES:
        keep = sub >= d
        a_prev = jnp.where(keep, pltpu.roll(a, d, axis=0), 1.0)
        b_prev = jnp.where(keep, pltpu.roll(b, d, axis=0), 0.0)
        b = a * b_prev + b
        a = a * a_prev
        d *= 2
    h = h0
    groups = []
    for g in range(t // SUBLANES):
        rows = slice(g * SUBLANES, (g + 1) * SUBLANES)
        seg = b[rows] + a[rows] * h
        h = seg[SUBLANES - 1:]
        groups.append(seg)
    return jnp.concatenate(groups, axis=0)


def _mix_kernel(h_ref, ysb_ref, lin_ref, scb_ref, scp_ref, gsb_ref, glru_ref, gsc_ref, gelu_ref,
                lconv_w_ref, lconv_b_ref, wa_ref, ba_ref, wx_ref, bx_ref, lam_ref, scw_ref,
                wsb_ref, wlru_ref, wsc_ref, wout_ref, ln_g_ref, ln_b_ref,
                out_ref, lin_stage, sc_stage, h_state):
    t = MIX_ROWS

    @pl.when(pl.program_id(1) == 0)
    def _():
        lin_stage[:SUBLANES, :] = jnp.zeros((SUBLANES, LRU_WIDTH), F32)
        sc_stage[:SUBLANES, :] = jnp.zeros((SUBLANES, SC_WIDTH), F32)
        h_state[...] = jnp.zeros_like(h_state)

    u = _causal_conv(lin_ref[0], lin_stage, lconv_w_ref[...]) + lconv_b_ref[...]
    ub = u.astype(BF16)
    r = _sigmoid(jnp.dot(ub, wa_ref[...], preferred_element_type=F32) + ba_ref[...])
    i = _sigmoid(jnp.dot(ub, wx_ref[...], preferred_element_type=F32) + bx_ref[...])
    lam = lam_ref[...]
    softplus_neg_lam = jnp.maximum(-lam, 0.0) + jnp.log(1.0 + jnp.exp(-jnp.abs(lam)))
    log_a = (-LRU_C) * r * softplus_neg_lam
    a = jnp.exp(log_a)
    gap = 1.0 - a * a
    drive = jnp.where(gap > 0.0, gap * lax.rsqrt(gap), 0.0) * (i * u)
    hs = _linear_scan(a, drive, h_state[...])
    h_state[...] = hs[t - 1:]
    y_lru = gelu_ref[0] * hs.astype(BF16)

    y_sc = (scb_ref[0] * _causal_conv(scp_ref[0], sc_stage, scw_ref[...])).astype(BF16)

    def branch(y, w_ref):
        return jnp.dot(y, w_ref[...], preferred_element_type=F32).astype(BF16)

    merged = (gsb_ref[0] * branch(ysb_ref[0], wsb_ref) + glru_ref[0] * branch(y_lru, wlru_ref)
              + gsc_ref[0] * branch(y_sc, wsc_ref))
    mix = jnp.dot(merged, wout_ref[...], preferred_element_type=F32)
    out_ref[0] = _layer_norm(ALPHA * h_ref[0] + mix, ln_g_ref[...], ln_b_ref[...])


def _block_diag(w):
    hh, ii, jj = w.shape
    eye = jnp.eye(hh, dtype=w.dtype)
    return (eye[:, None, :, None] * w[:, :, None, :]).reshape(hh * ii, hh * jj)


def _mixer_tail(h3, ysb3, conv3, act3, p, l):
    b, s, d = h3.shape
    t = MIX_ROWS
    assert LRU_WIDTH == SC_WIDTH

    def row_spec(width, col):
        return pl.BlockSpec((1, t, width), lambda bi, si: (bi, si, col))

    vec = lambda v: v.reshape(1, -1).astype(F32)
    in_specs = [row_spec(d, 0), row_spec(SB_WIDTH, 0)]
    in_specs += [row_spec(LRU_WIDTH, c) for c in range(3)]
    in_specs += [row_spec(d, g) for g in range(3)]
    in_specs += [row_spec(LRU_WIDTH, 3 * d // LRU_WIDTH)]
    weights = [
        p["lru_conv_w"][l].astype(F32), vec(p["lru_conv_b"][l]),
        _block_diag(p["lru_wa"][l]).astype(BF16), vec(p["lru_ba"][l]),
        _block_diag(p["lru_wx"][l]).astype(BF16), vec(p["lru_bx"][l]),
        vec(p["lru_lambda"][l]), p["sc_conv_w"][l].astype(F32),
        p["w_branch_sb"][l].astype(BF16), p["w_branch_lru"][l].astype(BF16),
        p["w_branch_sc"][l].astype(BF16), p["w_out"][l].astype(BF16),
        vec(p["ln1_g"][l]), vec(p["ln1_b"][l]),
    ]
    in_specs += [_const_spec(w.shape) for w in weights]
    return pl.pallas_call(
        _mix_kernel,
        out_shape=jax.ShapeDtypeStruct((b, s, d), F32),
        grid=(b, s // t),
        in_specs=in_specs,
        out_specs=pl.BlockSpec((1, t, d), lambda bi, si: (bi, si, 0)),
        scratch_shapes=[pltpu.VMEM((SUBLANES + t, LRU_WIDTH), F32),
                        pltpu.VMEM((SUBLANES + t, SC_WIDTH), F32),
                        pltpu.VMEM((1, LRU_WIDTH), F32)],
        compiler_params=_cparams(("parallel", "arbitrary")),
        name="mixer_tail",
    )(h3, ysb3, *([conv3] * 3), *([act3] * 4), *weights)


def _first_argmax(vals, row):
    m = jnp.max(vals, axis=0, keepdims=True)
    idx = jnp.min(jnp.where(vals == m, row, vals.shape[0]), axis=0, keepdims=True)
    return m, idx


def _route_kernel(h_ref, w_ref, bias_ref, before_ref, info_ref, key_ref, counts_ref, running):
    @pl.when(pl.program_id(0) == 0)
    def _():
        running[...] = jnp.zeros_like(running)

    h = h_ref[...]
    h_hi = h.astype(BF16)
    h_lo = (h - h_hi.astype(F32)).astype(BF16)
    w = w_ref[...]
    w_hi = w.astype(BF16)
    w_lo = (w - w_hi.astype(F32)).astype(BF16)
    logits = jnp.dot(jnp.concatenate([h_hi, h_lo, h_hi], axis=1),
                     jnp.concatenate([w_hi, w_hi, w_lo], axis=0), preferred_element_type=F32)
    lt = logits.T
    neg = -jnp.inf
    experts = lt[:N_EXPERTS]
    groups = lt[N_EXPERTS:N_EXPERTS + SUBLANES]
    experts_b = experts + bias_ref[:N_EXPERTS]
    groups_b = groups + bias_ref[N_EXPERTS:N_EXPERTS + SUBLANES]
    row_g = lax.broadcasted_iota(jnp.int32, groups.shape, 0)
    row_e = lax.broadcasted_iota(jnp.int32, experts.shape, 0)
    is_group = row_g < N_GROUPS
    g_max = jnp.max(jnp.where(is_group, groups, neg), axis=0, keepdims=True)
    g_exp = jnp.where(is_group, jnp.exp(groups - g_max), 0.0)
    g_den = jnp.sum(g_exp, axis=0, keepdims=True)
    _, g_sel = _first_argmax(jnp.where(is_group, groups_b, neg), row_g)
    g_prob = jnp.sum(jnp.where(row_g == g_sel, g_exp, 0.0), axis=0, keepdims=True) / g_den
    cand = jnp.where((row_e >> 2) == g_sel, experts_b, neg)
    _, e1 = _first_argmax(cand, row_e)
    _, e2 = _first_argmax(jnp.where(row_e == e1, neg, cand), row_e)
    l1 = jnp.sum(jnp.where(row_e == e1, experts, 0.0), axis=0, keepdims=True)
    l2 = jnp.sum(jnp.where(row_e == e2, experts, 0.0), axis=0, keepdims=True)
    m = jnp.maximum(l1, l2)
    x1 = jnp.exp(l1 - m)
    x2 = jnp.exp(l2 - m)
    scale = g_prob / (x1 + x2)
    w1 = x1 * scale
    w2 = x2 * scale
    first_low = e1 < e2
    a = jnp.where(first_low, e1, e2) - g_sel * EXP_PER_GROUP
    b = jnp.where(first_low, e2, e1) - g_sel * EXP_PER_GROUP
    cls = g_sel * PAIRS_PER_GROUP + ((a * (2 * EXP_PER_GROUP - 1 - a)) >> 1) + (b - a - 1)
    row_c = lax.broadcasted_iota(jnp.int32, (CLASS_ROWS, cls.shape[1]), 0)
    onehot = row_c == cls
    ones = jnp.where(onehot, 1.0, 0.0)
    earlier = jnp.dot(ones, before_ref[...], preferred_element_type=F32) + running[...]
    rank = jnp.sum(jnp.where(onehot, earlier, 0.0), axis=0, keepdims=True)
    running[...] += jnp.sum(ones, axis=1, keepdims=True)
    counts_ref[...] = running[...]
    key_ref[0] = (cls << KEY_SHIFT) + rank.astype(jnp.int32)
    row_w = lax.broadcasted_iota(jnp.int32, (LANES, cls.shape[1]), 0)
    w_rows = jnp.where(row_w == INFO_W_LOW, jnp.where(first_low, w1, w2), 0.0)
    w_rows = jnp.where(row_w == INFO_W_HIGH, jnp.where(first_low, w2, w1), w_rows)
    info_ref[...] = w_rows.T


def _route(h2, w_router, bias):
    n, d = h2.shape
    t = ROUTE_ROWS
    before = (jnp.arange(t)[:, None] < jnp.arange(t)[None, :]).astype(F32)
    return pl.pallas_call(
        _route_kernel,
        out_shape=(jax.ShapeDtypeStruct((n, LANES), F32),
                   jax.ShapeDtypeStruct((n // t, 1, t), jnp.int32),
                   jax.ShapeDtypeStruct((CLASS_ROWS, 1), F32)),
        grid=(n // t,),
        in_specs=[pl.BlockSpec((t, d), lambda i: (i, 0)),
                  _const_spec((d, LANES)), _const_spec((LANES, 1)), _const_spec((t, t))],
        out_specs=(pl.BlockSpec((t, LANES), lambda i: (i, 0)),
                   pl.BlockSpec((1, 1, t), lambda i: (i, 0, 0)),
                   _const_spec((CLASS_ROWS, 1))),
        scratch_shapes=[pltpu.VMEM((CLASS_ROWS, 1), F32)],
        compiler_params=_cparams(("arbitrary",)),
        name="moe_route",
    )(h2, w_router, bias, before)


def _slot_kernel(start_ref, key_ref, slot_ref):
    key = key_ref[...]
    cls = key >> KEY_SHIFT
    slot = key & ((1 << KEY_SHIFT) - 1)
    for c in range(N_CLASSES):
        slot = slot + jnp.where(cls == c, start_ref[c], 0)
    slot_ref[...] = slot


def _slots(class_start, key3):
    g, _, t = key3.shape
    return pl.pallas_call(
        _slot_kernel,
        out_shape=jax.ShapeDtypeStruct(key3.shape, jnp.int32),
        grid_spec=pltpu.PrefetchScalarGridSpec(
            num_scalar_prefetch=1, grid=(1,),
            in_specs=[pl.BlockSpec((g, 1, t), lambda i, s: (0, 0, 0))],
            out_specs=pl.BlockSpec((g, 1, t), lambda i, s: (0, 0, 0))),
        compiler_params=_cparams(("arbitrary",)),
        name="moe_slots",
    )(class_start, key3)


def _for_rows(n_rows, fn):
    def body(g, c):
        for j in range(ROW_COPY_UNROLL):
            fn(g * ROW_COPY_UNROLL + j, j % 2)
        return c

    lax.fori_loop(0, n_rows // ROW_COPY_UNROLL, body, 0)


def _dispatch_kernel(tile_end_ref, n_used_ref, slot_ref, h_ref, info_ref, xs_ref, rows, sems, zsem):
    i = pl.program_id(0)
    last = pl.num_programs(0) - 1
    p = i % 2
    d = h_ref.shape[1]
    t = DISPATCH_ROWS
    te = EXPERT_ROWS

    def row_copy(buf, r):
        return pltpu.make_async_copy(rows.at[buf, pl.ds(r, 1)],
                                     xs_ref.at[pl.ds(slot_ref[0, 0, r], 1)], sems.at[buf])

    @pl.when(i == 0)
    def _():
        rows[1, :te] = jnp.zeros((te, d + LANES), F32)

        def zero_tile(tile):
            return pltpu.make_async_copy(rows.at[1, pl.ds(0, te)],
                                         xs_ref.at[pl.ds(tile * te, te)], zsem)

        prev_end = 0
        for c in range(N_CLASSES):
            end = tile_end_ref[c]

            @pl.when(end > prev_end)
            def _(end=end):
                cp = zero_tile(end - 1)
                cp.start()
                cp.wait()
            prev_end = end

        def zero_tail(tile, c):
            cp = zero_tile(tile)
            cp.start()
            cp.wait()
            return c

        lax.fori_loop(n_used_ref[0], xs_ref.shape[0] // te, zero_tail, 0)

    rows[p, :, :d] = h_ref[...]
    rows[p, :, d:] = info_ref[...]
    _for_rows(t, lambda r, q: row_copy(p, r).start(priority=q))

    def row_wait(buf):
        pltpu.make_async_copy(rows.at[buf, pl.ds(0, 1)], xs_ref.at[pl.ds(0, 1)],
                              sems.at[buf]).wait()

    @pl.when(i > 0)
    def _():
        _for_rows(t, lambda r, q: row_wait(1 - p))

    @pl.when(i == last)
    def _():
        _for_rows(t, lambda r, q: row_wait(p))


def _dispatch(tile_end, n_used, slot3, h2, info, n_slots):
    n, d = h2.shape
    t = DISPATCH_ROWS
    return pl.pallas_call(
        _dispatch_kernel,
        out_shape=jax.ShapeDtypeStruct((n_slots, d + LANES), F32),
        grid_spec=pltpu.PrefetchScalarGridSpec(
            num_scalar_prefetch=2, grid=(n // t,),
            in_specs=[pl.BlockSpec((1, 1, t), lambda i, *_: (i, 0, 0), memory_space=pltpu.SMEM),
                      pl.BlockSpec((t, d), lambda i, *_: (i, 0)),
                      pl.BlockSpec((t, LANES), lambda i, *_: (i, 0))],
            out_specs=pl.BlockSpec(memory_space=pl.ANY),
            scratch_shapes=[pltpu.VMEM((2, t, d + LANES), F32), pltpu.SemaphoreType.DMA((2,)),
                            pltpu.SemaphoreType.DMA(())]),
        compiler_params=_cparams(("arbitrary",)),
        name="moe_dispatch",
    )(tile_end, n_used, slot3, h2, info)


def _expert_kernel(n_used_ref, src_ref, lo_ref, hi_ref, xs_ref, wg_lo, wg_hi, wu_lo, wu_hi,
                   wd_lo, wd_hi, ln_g_ref, ln_b_ref, ys_ref):
    del src_ref, lo_ref, hi_ref

    @pl.when(pl.program_id(0) < n_used_ref[0])
    def _():
        d = ys_ref.shape[1]
        x = xs_ref[:, :d]
        xb = x.astype(BF16)
        y = None
        for wg, wu, wd, lane in ((wg_lo, wu_lo, wd_lo, INFO_W_LOW), (wg_hi, wu_hi, wd_hi, INFO_W_HIGH)):
            gate = jnp.dot(xb, wg[0], preferred_element_type=F32)
            up = jnp.dot(xb, wu[0], preferred_element_type=F32)
            act = gate * _sigmoid(gate) * up * xs_ref[:, d + lane:d + lane + 1]
            part = jnp.dot(act.astype(BF16), wd[0], preferred_element_type=F32)
            y = part if y is None else y + part
        ys_ref[...] = _layer_norm(ALPHA * x + y, ln_g_ref[...], ln_b_ref[...])

    @pl.when(pl.program_id(0) >= n_used_ref[0])
    def _():
        ys_ref[...] = jnp.zeros_like(ys_ref)


def _experts(tile_meta, xs, layer, w_gate_b, w_up_b, w_down_b, ln_g, ln_b):
    n_slots, cols = xs.shape
    d = cols - LANES
    f = w_gate_b.shape[3]
    t = EXPERT_ROWS
    x_map = lambda i, n_used, src, lo, hi: (src[i], 0)
    lo_map = lambda i, n_used, src, lo, hi: (layer, lo[i], 0, 0)
    hi_map = lambda i, n_used, src, lo, hi: (layer, hi[i], 0, 0)
    const = lambda i, *_: (0, 0)
    grid_spec = pltpu.PrefetchScalarGridSpec(
        num_scalar_prefetch=4,
        grid=(n_slots // t,),
        in_specs=[pl.BlockSpec((t, cols), x_map),
                  pl.BlockSpec((None, 1, d, f), lo_map), pl.BlockSpec((None, 1, d, f), hi_map),
                  pl.BlockSpec((None, 1, d, f), lo_map), pl.BlockSpec((None, 1, d, f), hi_map),
                  pl.BlockSpec((None, 1, f, d), lo_map), pl.BlockSpec((None, 1, f, d), hi_map),
                  pl.BlockSpec((1, d), const), pl.BlockSpec((1, d), const)],
        out_specs=pl.BlockSpec((t, d), lambda i, *_: (i, 0)),
    )
    return pl.pallas_call(
        _expert_kernel,
        out_shape=jax.ShapeDtypeStruct((n_slots, d), F32),
        grid_spec=grid_spec,
        compiler_params=_cparams(("arbitrary",)),
        name="moe_experts",
    )(*tile_meta, xs, w_gate_b, w_gate_b, w_up_b, w_up_b, w_down_b, w_down_b,
      ln_g.reshape(1, d), ln_b.reshape(1, d))


def _combine_kernel(slot_ref, ys_ref, o_ref, buf, gsems, osems):
    i = pl.program_id(0)
    last = pl.num_programs(0) - 1
    p = i % 2
    t = DISPATCH_ROWS

    def tile_out(b, step):
        return pltpu.make_async_copy(buf.at[b], o_ref.at[pl.ds(step * t, t)], osems.at[b])

    def row_wait(b):
        pltpu.make_async_copy(ys_ref.at[pl.ds(0, 1)], buf.at[b, pl.ds(0, 1)], gsems.at[b]).wait()

    @pl.when(i >= 2)
    def _():
        tile_out(p, i - 2).wait()

    _for_rows(t, lambda r, q: pltpu.make_async_copy(
        ys_ref.at[pl.ds(slot_ref[0, 0, r], 1)], buf.at[p, pl.ds(r, 1)],
        gsems.at[p]).start(priority=q))

    @pl.when(i > 0)
    def _():
        _for_rows(t, lambda r, q: row_wait(1 - p))
        tile_out(1 - p, i - 1).start()

    @pl.when(i == last)
    def _():
        _for_rows(t, lambda r, q: row_wait(p))
        tile_out(p, i).start()
        tile_out(p, i).wait()

        @pl.when(i > 0)
        def _():
            tile_out(1 - p, i - 1).wait()


def _combine(slot3, ys, n):
    d = ys.shape[1]
    t = DISPATCH_ROWS
    return pl.pallas_call(
        _combine_kernel,
        out_shape=jax.ShapeDtypeStruct((n, d), F32),
        grid=(n // t,),
        in_specs=[pl.BlockSpec((1, 1, t), lambda i: (i, 0, 0), memory_space=pltpu.SMEM),
                  pl.BlockSpec(memory_space=pl.ANY)],
        out_specs=pl.BlockSpec(memory_space=pl.ANY),
        scratch_shapes=[pltpu.VMEM((2, t, d), F32), pltpu.SemaphoreType.DMA((2,)),
                        pltpu.SemaphoreType.DMA((2,))],
        compiler_params=_cparams(("arbitrary",)),
        name="moe_combine",
    )(slot3, ys)


def _pair_tables():
    lo, hi = [], []
    for g in range(N_GROUPS):
        for a in range(EXP_PER_GROUP):
            for b in range(a + 1, EXP_PER_GROUP):
                lo.append(g * EXP_PER_GROUP + a)
                hi.append(g * EXP_PER_GROUP + b)
    return jnp.asarray(lo, jnp.int32), jnp.asarray(hi, jnp.int32)


def _tile_plan(counts, n_tiles):
    t = EXPERT_ROWS
    counts = counts[:N_CLASSES, 0].astype(jnp.int32)
    tiles_per_class = (counts + t - 1) // t
    tile_end = jnp.cumsum(tiles_per_class)
    class_start = (tile_end - tiles_per_class) * t
    n_used = tile_end[-1:]
    src = jnp.minimum(jnp.arange(n_tiles, dtype=jnp.int32), n_used - 1)
    tile_class = jnp.sum((tile_end[None, :] <= src[:, None]).astype(jnp.int32), axis=1)
    pair_lo, pair_hi = _pair_tables()
    return class_start, tile_end, (n_used, src, pair_lo[tile_class], pair_hi[tile_class])


def kernel(x, ln_in_g, ln_in_b, w_in, gate_b, lru_conv_w, lru_conv_b, lru_wa, lru_ba, lru_wx, lru_bx, lru_lambda, sc_conv_w, w_branch_sb, w_branch_lru, w_branch_sc, w_out, ln1_g, ln1_b, w_group, group_bias, w_expert_router, expert_bias, w_gate, w_up, w_down, ln2_g, ln2_b):
    b, s, d = x.shape
    n = b * s
    params = dict(gate_b=gate_b, lru_conv_w=lru_conv_w, lru_conv_b=lru_conv_b, lru_wa=lru_wa,
                  lru_ba=lru_ba, lru_wx=lru_wx, lru_bx=lru_bx, lru_lambda=lru_lambda,
                  sc_conv_w=sc_conv_w, w_branch_sb=w_branch_sb, w_branch_lru=w_branch_lru,
                  w_branch_sc=w_branch_sc, w_out=w_out, ln1_g=ln1_g, ln1_b=ln1_b)
    w_in_b, w_gate_b, w_up_b, w_down_b = (w.astype(BF16) for w in (w_in, w_gate, w_up, w_down))
    h = x.reshape(n, d)
    for l in range(w_in.shape[0]):
        if l == 0:
            h, qkv, conv_in, act = _inproj(h, w_in_b, l, gate_b[l], entry_ln=(ln_in_g, ln_in_b))
        else:
            h, qkv, conv_in, act = _inproj(ys, w_in_b, l, gate_b[l], slots=slot3)
        y_sb = _sb_attention(qkv.reshape(b, s, -1))
        h1 = _mixer_tail(h.reshape(b, s, d), y_sb, conv_in.reshape(b, s, -1),
                         act.reshape(b, s, -1), params, l)
        h1 = h1.reshape(n, d)
        pad = LANES - N_EXPERTS - N_GROUPS
        w_router = jnp.concatenate(
            [w_expert_router[l], w_group[l], jnp.zeros((d, pad), F32)], axis=1).astype(F32)
        bias = jnp.concatenate(
            [expert_bias[l], group_bias[l], jnp.zeros((pad,), F32)]).reshape(LANES, 1).astype(F32)
        info, key3, counts = _route(h1, w_router, bias)
        n_tiles = n // EXPERT_ROWS + N_CLASSES
        class_start, tile_end, tile_meta = _tile_plan(counts, n_tiles)
        slot3 = _slots(class_start, key3)
        xs = _dispatch(tile_end, tile_meta[0], slot3, h1, info, n_tiles * EXPERT_ROWS)
        ys = _experts(tile_meta, xs, l, w_gate_b, w_up_b, w_down_b, ln2_g[l], ln2_b[l])
    return _combine(slot3, ys, n).reshape(b, s, d)
```

```python
import functools
import math

import jax
import jax.numpy as jnp
from jax import lax
from jax.experimental import pallas as pl
from jax.experimental.pallas import tpu as pltpu

F32 = jnp.float32
BF16 = jnp.bfloat16

SB_HEADS = 8
SB_HEAD_DIM = 64
SB_WIDTH = SB_HEADS * SB_HEAD_DIM
LRU_WIDTH = 512
LRU_BLOCKS = 8
LRU_C = 8.0
SC_WIDTH = 512
N_GROUPS = 4
EXP_PER_GROUP = 4
N_EXPERTS = N_GROUPS * EXP_PER_GROUP
D_EXPERT = 512
DEPTH = 2
ALPHA = (2 * DEPTH) ** 0.25
LN_EPS = 1e-5

LANES = 128
SUBLANES = 8
VMEM_LIMIT_BYTES = 56 * 1024 * 1024

PROJ_ROWS = 512
PROJ_CHUNK = 512
ATT_Q = 256
ATT_K = 256
ATT_LANE_BLOCKS = 2
MIX_ROWS = 512
ROUTE_ROWS = 512
DISPATCH_ROWS = 512
EXPERT_ROWS = 256
ROW_COPY_UNROLL = 8

PAIRS_PER_GROUP = EXP_PER_GROUP * (EXP_PER_GROUP - 1) // 2
N_CLASSES = N_GROUPS * PAIRS_PER_GROUP
CLASS_ROWS = 32
INFO_W_LOW, INFO_W_HIGH = 0, 1
KEY_SHIFT = 16


def _cparams(sem):
    return pltpu.CompilerParams(dimension_semantics=sem, vmem_limit_bytes=VMEM_LIMIT_BYTES)


def _const_spec(shape):
    nd = len(shape)
    return pl.BlockSpec(shape, lambda *_: (0,) * nd)


def _layer_norm(y, g, b):
    mu = jnp.mean(y, axis=-1, keepdims=True)
    d = y - mu
    var = jnp.mean(d * d, axis=-1, keepdims=True)
    return d * lax.rsqrt(var + LN_EPS) * g + b


QKV_COLS = 3 * SB_WIDTH
BRANCH_COLS = 2 * LRU_WIDTH + 3 * SC_WIDTH


def _sigmoid(x):
    return 0.5 * jnp.tanh(0.5 * x) + 0.5


def _gelu_tanh(x):
    return 0.5 * x * (1.0 + jnp.tanh(math.sqrt(2.0 / math.pi) * (x + 0.044715 * (x * x * x))))


def _project(h, w_ref, gate_b_ref, qkv_ref, conv_ref, act_ref):
    hb = h.astype(BF16)
    c = PROJ_CHUNK

    def proj(chunk):
        return jnp.dot(hb, w_ref[:, chunk * c:(chunk + 1) * c], preferred_element_type=F32)

    n_qkv = QKV_COLS // c
    for j in range(n_qkv):
        acc = proj(j)
        if j * c < SB_WIDTH:
            acc = acc * (SB_HEAD_DIM ** -0.5 * LOG2_E)
        qkv_ref[:, j * c:(j + 1) * c] = acc.astype(BF16)
    lru_in, lru_gate, sc_b, sc_c, sc_h = range(n_qkv, n_qkv + 5)
    conv_ref[:, 0:c] = proj(lru_in)
    conv_ref[:, c:2 * c] = proj(sc_b)
    conv_ref[:, 2 * c:3 * c] = proj(sc_c) * proj(sc_h)
    for j in range(gate_b_ref.shape[1] // c):
        acc = proj(sc_h + 1 + j) + gate_b_ref[:, j * c:(j + 1) * c]
        act_ref[:, j * c:(j + 1) * c] = _sigmoid(acc).astype(BF16)
    act_ref[:, gate_b_ref.shape[1]:] = _gelu_tanh(proj(lru_gate)).astype(BF16)


def _inproj_kernel(h_ref, w_ref, gate_b_ref, qkv_ref, conv_ref, act_ref):
    _project(h_ref[...], w_ref, gate_b_ref, qkv_ref, conv_ref, act_ref)


def _ln_inproj_kernel(x_ref, g_ref, b_ref, w_ref, gate_b_ref, h_ref, qkv_ref, conv_ref, act_ref):
    h = _layer_norm(x_ref[...], g_ref[...], b_ref[...])
    h_ref[...] = h
    _project(h, w_ref, gate_b_ref, qkv_ref, conv_ref, act_ref)


def _gather_inproj_kernel(slot_ref, slot_next_ref, ys_ref, w_ref, gate_b_ref,
                          h_ref, qkv_ref, conv_ref, act_ref, rows, sems):
    i = pl.program_id(0)
    last = pl.num_programs(0) - 1
    p = i % 2

    def fetch(slots, buf):
        def group(g, c):
            for j in range(SUBLANES):
                pltpu.make_async_copy(ys_ref.at[pl.ds(slots[0, 0, g * SUBLANES + j], 1)],
                                      rows.at[buf, g, pl.ds(j, 1)],
                                      sems.at[buf]).start(priority=j % 2)
            return c

        lax.fori_loop(0, PROJ_ROWS // SUBLANES, group, 0)

    @pl.when(i == 0)
    def _():
        fetch(slot_ref, 0)

    @pl.when(i < last)
    def _():
        fetch(slot_next_ref, 1 - p)

    _for_rows(PROJ_ROWS, lambda r, q: pltpu.make_async_copy(
        ys_ref.at[pl.ds(0, 1)], rows.at[p, 0, pl.ds(0, 1)], sems.at[p]).wait())
    h = rows[p].reshape(PROJ_ROWS, rows.shape[-1])
    h_ref[...] = h
    _project(h, w_ref, gate_b_ref, qkv_ref, conv_ref, act_ref)


def _inproj(h2, w_in_b, layer, gate_b, entry_ln=None, slots=None):
    d = h2.shape[1]
    n = h2.shape[0] if slots is None else slots.size
    cols = w_in_b.shape[2]
    gate_cols = gate_b.size
    assert LRU_WIDTH == SC_WIDTH == PROJ_CHUNK and cols == QKV_COLS + BRANCH_COLS + gate_cols
    widths = (QKV_COLS, 3 * PROJ_CHUNK, gate_cols + LRU_WIDTH)
    dtypes = (BF16, F32, BF16)
    rows = pl.BlockSpec((PROJ_ROWS, d), lambda i: (i, 0))
    weight_specs = [pl.BlockSpec((None, d, cols), lambda i: (layer, 0, 0),
                                 pipeline_mode=pl.Buffered(1)),
                    _const_spec((1, gate_cols))]
    weights = (w_in_b, gate_b.reshape(1, gate_cols).astype(F32))
    out_shape = [jax.ShapeDtypeStruct((n, w), t) for w, t in zip(widths, dtypes)]
    out_specs = [pl.BlockSpec((PROJ_ROWS, w), lambda i: (i, 0)) for w in widths]
    if slots is not None:
        assert DISPATCH_ROWS == PROJ_ROWS
        steps = n // PROJ_ROWS
        slot_spec = lambda index: pl.BlockSpec((1, 1, PROJ_ROWS), index, memory_space=pltpu.SMEM)
        out_shape.insert(0, jax.ShapeDtypeStruct((n, d), F32))
        out_specs.insert(0, rows)
        return pl.pallas_call(
            _gather_inproj_kernel,
            out_shape=tuple(out_shape),
            grid=(steps,),
            in_specs=[slot_spec(lambda i: (i, 0, 0)),
                      slot_spec(lambda i: (jnp.minimum(i + 1, steps - 1), 0, 0)),
                      pl.BlockSpec(memory_space=pl.ANY)] + weight_specs,
            out_specs=tuple(out_specs),
            scratch_shapes=[pltpu.VMEM((2, PROJ_ROWS // SUBLANES, SUBLANES, d), F32),
                            pltpu.SemaphoreType.DMA((2,))],
            compiler_params=_cparams(("arbitrary",)),
            name="in_proj_gather",
        )(slots, slots, h2, *weights)
    if entry_ln is None:
        body, ln_specs, ln_args = _inproj_kernel, [], ()
    else:
        body = _ln_inproj_kernel
        ln_specs = [_const_spec((1, d)), _const_spec((1, d))]
        ln_args = tuple(v.reshape(1, d).astype(F32) for v in entry_ln)
        out_shape.insert(0, jax.ShapeDtypeStruct((n, d), F32))
        out_specs.insert(0, rows)
    return pl.pallas_call(
        body,
        out_shape=tuple(out_shape),
        grid=(n // PROJ_ROWS,),
        in_specs=[rows] + ln_specs + weight_specs,
        out_specs=tuple(out_specs),
        compiler_params=_cparams(("parallel",)),
        name="in_proj",
    )(h2, *ln_args, *weights)


LOG2_E = math.log2(math.e)
PASS_LOG2_FLOOR = -151.0


HEADS_PER_STEP = LANES // SB_HEAD_DIM


def _per_head(x):
    lane = lax.broadcasted_iota(jnp.int32, x.shape, 1)
    return jnp.concatenate(
        [jnp.where((lane >= h * SB_HEAD_DIM) & (lane < (h + 1) * SB_HEAD_DIM), x, jnp.zeros_like(x))
         for h in range(HEADS_PER_STEP)], axis=0)


MASKED_SCORE = -1e30


def _neg_abs(x):
    bits = lax.bitcast_convert_type(x, jnp.uint32) | jnp.uint32(0x80000000)
    return lax.bitcast_convert_type(bits, F32)


def _sb_block(q, k_j, v_j, neg_tri2, carries, mask):
    nq, kk = q.shape[0], k_j.shape[0]
    z = lax.dot_general(q, _per_head(k_j), (((1,), (1,)), ((), ())), preferred_element_type=F32)
    scores, split = [], []
    for h in range(HEADS_PER_STEP):
        z_h = z[:, h * kk:(h + 1) * kk]
        if mask is not None:
            z_h = jnp.where(mask, z_h, MASKED_SCORE)
        drop = jnp.maximum(z_h, 0.0) + jnp.log2(1.0 + jnp.exp2(_neg_abs(z_h)))
        scores.append(z_h)
        split.append(drop.astype(BF16))
    log_pass = jnp.dot(jnp.concatenate(split, axis=0), neg_tri2, preferred_element_type=F32)
    w, new_carries = [], []
    for h in range(HEADS_PER_STEP):
        lp_h = log_pass[h * nq:(h + 1) * nq]
        w.append(jnp.exp2(scores[h] + lp_h + carries[h]).astype(BF16))
        new_carries.append(carries[h] + lp_h[:, 0:1])
    pv = jnp.dot(jnp.concatenate(w, axis=1), _per_head(v_j), preferred_element_type=F32)
    return pv, new_carries


def _attn_kernel(q_ref, k_ref, v_ref, u2_ref, o_ref):
    qi = pl.program_id(2)
    u2 = u2_ref[...]
    row = lax.broadcasted_iota(jnp.int32, (ATT_Q, ATT_K), 0)
    col = lax.broadcasted_iota(jnp.int32, (ATT_Q, ATT_K), 1)
    diag_mask = col < row
    blocks = [slice(g * LANES, (g + 1) * LANES) for g in range(ATT_LANE_BLOCKS)]
    q = [q_ref[0, :, lanes] for lanes in blocks]

    def sweep(j, accs, carries, mask):
        rows = pl.ds(pl.multiple_of(j * ATT_K, ATT_K), ATT_K)
        out = [_sb_block(q_g, k_ref[0, rows, lanes], v_ref[0, rows, lanes], u2, c_g, mask)
               for q_g, lanes, c_g in zip(q, blocks, carries)]
        return [acc + pv for acc, (pv, _) in zip(accs, out)], [c for _, c in out]

    zero = [[jnp.zeros((ATT_Q, 1), F32) for _ in range(HEADS_PER_STEP)] for _ in blocks]
    accs, carries = sweep(qi, [jnp.zeros((ATT_Q, LANES), F32) for _ in blocks], zero, diag_mask)
    no_prev = jnp.where(qi > 0, 0.0, -1e30)
    accs, carries = sweep(jnp.maximum(qi - 1, 0), accs,
                          [[c + no_prev for c in c_g] for c_g in carries], None)

    def largest(carries):
        return jnp.max(functools.reduce(jnp.maximum, [c for c_g in carries for c in c_g]))

    def cond(state):
        j, _, _, live = state
        return jnp.logical_and(j >= 0, live > PASS_LOG2_FLOOR)

    def body(state):
        j, accs, carries, _ = state
        accs, carries = sweep(j, accs, carries, None)
        return j - 1, accs, carries, largest(carries)

    _, accs, _, _ = lax.while_loop(cond, body, (qi - 2, accs, carries, largest(carries)))
    o_ref[0] = jnp.concatenate(accs, axis=1).astype(o_ref.dtype)


def _sb_attention(qkv3):
    b, s, _ = qkv3.shape
    width = ATT_LANE_BLOCKS * LANES
    groups = SB_WIDTH // width
    neg_tri = -(jnp.arange(ATT_K)[:, None] >= jnp.arange(ATT_K)[None, :]).astype(BF16)
    u2 = neg_tri
    return pl.pallas_call(
        _attn_kernel,
        out_shape=jax.ShapeDtypeStruct((b, s, SB_WIDTH), BF16),
        grid=(b, groups, s // ATT_Q),
        in_specs=[pl.BlockSpec((1, ATT_Q, width), lambda bi, p, qi: (bi, qi, p)),
                  pl.BlockSpec((1, s, width), lambda bi, p, qi: (bi, 0, groups + p)),
                  pl.BlockSpec((1, s, width), lambda bi, p, qi: (bi, 0, 2 * groups + p)),
                  _const_spec((ATT_K, ATT_K))],
        out_specs=pl.BlockSpec((1, ATT_Q, width), lambda bi, p, qi: (bi, qi, p)),
        compiler_params=_cparams(("parallel", "parallel", "arbitrary")),
        name="sb_attention",
    )(qkv3, qkv3, qkv3, u2)


def _causal_conv(x, stage, w):
    t = x.shape[0]
    k = w.shape[0]
    stage[SUBLANES:, :] = x
    out = w[k - 1:k] * x
    for j in range(1, k):
        out = out + w[k - 1 - j:k - j] * stage[pl.ds(SUBLANES - j, t), :]
    stage[:SUBLANES, :] = x[t - SUBLANES:]
    return out


def _linear_scan(a, b, h0):
    t = a.shape[0]
    sub = lax.broadcasted_iota(jnp.int32, a.shape, 0) % SUBLANES
    d = 1
    while d < SUBLANES:
        keep = sub >= d
        a_prev = jnp.where(keep, pltpu.roll(a, d, axis=0), 1.0)
        b_prev = jnp.where(keep, pltpu.roll(b, d, axis=0), 0.0)
        b = a * b_prev + b
        a = a * a_prev
        d *= 2
    h = h0
    groups = []
    for g in range(t // SUBLANES):
        rows = slice(g * SUBLANES, (g + 1) * SUBLANES)
        seg = b[rows] + a[rows] * h
        h = seg[SUBLANES - 1:]
        groups.append(seg)
    return jnp.concatenate(groups, axis=0)


def _mix_kernel(h_ref, ysb_ref, lin_ref, scb_ref, scp_ref, gsb_ref, glru_ref, gsc_ref, gelu_ref,
                lconv_w_ref, lconv_b_ref, wa_ref, ba_ref, wx_ref, bx_ref, lam_ref, scw_ref,
                wsb_ref, wlru_ref, wsc_ref, wout_ref, ln_g_ref, ln_b_ref,
                out_ref, lin_stage, sc_stage, h_state):
    t = MIX_ROWS

    @pl.when(pl.program_id(1) == 0)
    def _():
        lin_stage[:SUBLANES, :] = jnp.zeros((SUBLANES, LRU_WIDTH), F32)
        sc_stage[:SUBLANES, :] = jnp.zeros((SUBLANES, SC_WIDTH), F32)
        h_state[...] = jnp.zeros_like(h_state)

    u = _causal_conv(lin_ref[0], lin_stage, lconv_w_ref[...]) + lconv_b_ref[...]
    ub = u.astype(BF16)
    r = _sigmoid(jnp.dot(ub, wa_ref[...], preferred_element_type=F32) + ba_ref[...])
    i = _sigmoid(jnp.dot(ub, wx_ref[...], preferred_element_type=F32) + bx_ref[...])
    lam = lam_ref[...]
    softplus_neg_lam = jnp.maximum(-lam, 0.0) + jnp.log(1.0 + jnp.exp(-jnp.abs(lam)))
    log_a = (-LRU_C) * r * softplus_neg_lam
    a = jnp.exp(log_a)
    gap = 1.0 - a * a
    drive = jnp.where(gap > 0.0, gap * lax.rsqrt(gap), 0.0) * (i * u)
    hs = _linear_scan(a, drive, h_state[...])
    h_state[...] = hs[t - 1:]
    y_lru = gelu_ref[0] * hs.astype(BF16)

    y_sc = (scb_ref[0] * _causal_conv(scp_ref[0], sc_stage, scw_ref[...])).astype(BF16)

    def branch(y, w_ref):
        return jnp.dot(y, w_ref[...], preferred_element_type=F32).astype(BF16)

    merged = (gsb_ref[0] * branch(ysb_ref[0], wsb_ref) + glru_ref[0] * branch(y_lru, wlru_ref)
              + gsc_ref[0] * branch(y_sc, wsc_ref))
    mix = jnp.dot(merged, wout_ref[...], preferred_element_type=F32)
    out_ref[0] = _layer_norm(ALPHA * h_ref[0] + mix, ln_g_ref[...], ln_b_ref[...])


def _block_diag(w):
    hh, ii, jj = w.shape
    eye = jnp.eye(hh, dtype=w.dtype)
    return (eye[:, None, :, None] * w[:, :, None, :]).reshape(hh * ii, hh * jj)


def _mixer_tail(h3, ysb3, conv3, act3, p, l):
    b, s, d = h3.shape
    t = MIX_ROWS
    assert LRU_WIDTH == SC_WIDTH

    def row_spec(width, col):
        return pl.BlockSpec((1, t, width), lambda bi, si: (bi, si, col))

    vec = lambda v: v.reshape(1, -1).astype(F32)
    in_specs = [row_spec(d, 0), row_spec(SB_WIDTH, 0)]
    in_specs += [row_spec(LRU_WIDTH, c) for c in range(3)]
    in_specs += [row_spec(d, g) for g in range(3)]
    in_specs += [row_spec(LRU_WIDTH, 3 * d // LRU_WIDTH)]
    weights = [
        p["lru_conv_w"][l].astype(F32), vec(p["lru_conv_b"][l]),
        _block_diag(p["lru_wa"][l]).astype(BF16), vec(p["lru_ba"][l]),
        _block_diag(p["lru_wx"][l]).astype(BF16), vec(p["lru_bx"][l]),
        vec(p["lru_lambda"][l]), p["sc_conv_w"][l].astype(F32),
        p["w_branch_sb"][l].astype(BF16), p["w_branch_lru"][l].astype(BF16),
        p["w_branch_sc"][l].astype(BF16), p["w_out"][l].astype(BF16),
        vec(p["ln1_g"][l]), vec(p["ln1_b"][l]),
    ]
    in_specs += [_const_spec(w.shape) for w in weights]
    return pl.pallas_call(
        _mix_kernel,
        out_shape=jax.ShapeDtypeStruct((b, s, d), F32),
        grid=(b, s // t),
        in_specs=in_specs,
        out_specs=pl.BlockSpec((1, t, d), lambda bi, si: (bi, si, 0)),
        scratch_shapes=[pltpu.VMEM((SUBLANES + t, LRU_WIDTH), F32),
                        pltpu.VMEM((SUBLANES + t, SC_WIDTH), F32),
                        pltpu.VMEM((1, LRU_WIDTH), F32)],
        compiler_params=_cparams(("parallel", "arbitrary")),
        name="mixer_tail",
    )(h3, ysb3, *([conv3] * 3), *([act3] * 4), *weights)


def _first_argmax(vals, row):
    m = jnp.max(vals, axis=0, keepdims=True)
    idx = jnp.min(jnp.where(vals == m, row, vals.shape[0]), axis=0, keepdims=True)
    return m, idx


def _route_kernel(h_ref, w_ref, bias_ref, before_ref, info_ref, key_ref, counts_ref, running):
    @pl.when(pl.program_id(0) == 0)
    def _():
        running[...] = jnp.zeros_like(running)

    h = h_ref[...]
    h_hi = h.astype(BF16)
    h_lo = (h - h_hi.astype(F32)).astype(BF16)
    w = w_ref[...]
    w_hi = w.astype(BF16)
    w_lo = (w - w_hi.astype(F32)).astype(BF16)
    logits = jnp.dot(jnp.concatenate([h_hi, h_lo, h_hi], axis=1),
                     jnp.concatenate([w_hi, w_hi, w_lo], axis=0), preferred_element_type=F32)
    lt = logits.T
    neg = -jnp.inf
    experts = lt[:N_EXPERTS]
    groups = lt[N_EXPERTS:N_EXPERTS + SUBLANES]
    experts_b = experts + bias_ref[:N_EXPERTS]
    groups_b = groups + bias_ref[N_EXPERTS:N_EXPERTS + SUBLANES]
    row_g = lax.broadcasted_iota(jnp.int32, groups.shape, 0)
    row_e = lax.broadcasted_iota(jnp.int32, experts.shape, 0)
    is_group = row_g < N_GROUPS
    g_max = jnp.max(jnp.where(is_group, groups, neg), axis=0, keepdims=True)
    g_exp = jnp.where(is_group, jnp.exp(groups - g_max), 0.0)
    g_den = jnp.sum(g_exp, axis=0, keepdims=True)
    _, g_sel = _first_argmax(jnp.where(is_group, groups_b, neg), row_g)
    g_prob = jnp.sum(jnp.where(row_g == g_sel, g_exp, 0.0), axis=0, keepdims=True) / g_den
    cand = jnp.where((row_e >> 2) == g_sel, experts_b, neg)
    _, e1 = _first_argmax(cand, row_e)
    _, e2 = _first_argmax(jnp.where(row_e == e1, neg, cand), row_e)
    l1 = jnp.sum(jnp.where(row_e == e1, experts, 0.0), axis=0, keepdims=True)
    l2 = jnp.sum(jnp.where(row_e == e2, experts, 0.0), axis=0, keepdims=True)
    m = jnp.maximum(l1, l2)
    x1 = jnp.exp(l1 - m)
    x2 = jnp.exp(l2 - m)
    scale = g_prob / (x1 + x2)
    w1 = x1 * scale
    w2 = x2 * scale
    first_low = e1 < e2
    a = jnp.where(first_low, e1, e2) - g_sel * EXP_PER_GROUP
    b = jnp.where(first_low, e2, e1) - g_sel * EXP_PER_GROUP
    cls = g_sel * PAIRS_PER_GROUP + ((a * (2 * EXP_PER_GROUP - 1 - a)) >> 1) + (b - a - 1)
    row_c = lax.broadcasted_iota(jnp.int32, (CLASS_ROWS, cls.shape[1]), 0)
    onehot = row_c == cls
    ones = jnp.where(onehot, 1.0, 0.0)
    earlier = jnp.dot(ones, before_ref[...], preferred_element_type=F32) + running[...]
    rank = jnp.sum(jnp.where(onehot, earlier, 0.0), axis=0, keepdims=True)
    running[...] += jnp.sum(ones, axis=1, keepdims=True)
    counts_ref[...] = running[...]
    key_ref[0] = (cls << KEY_SHIFT) + rank.astype(jnp.int32)
    row_w = lax.broadcasted_iota(jnp.int32, (LANES, cls.shape[1]), 0)
    w_rows = jnp.where(row_w == INFO_W_LOW, jnp.where(first_low, w1, w2), 0.0)
    w_rows = jnp.where(row_w == INFO_W_HIGH, jnp.where(first_low, w2, w1), w_rows)
    info_ref[...] = w_rows.T


def _route(h2, w_router, bias):
    n, d = h2.shape
    t = ROUTE_ROWS
    before = (jnp.arange(t)[:, None] < jnp.arange(t)[None, :]).astype(F32)
    return pl.pallas_call(
        _route_kernel,
        out_shape=(jax.ShapeDtypeStruct((n, LANES), F32),
                   jax.ShapeDtypeStruct((n // t, 1, t), jnp.int32),
                   jax.ShapeDtypeStruct((CLASS_ROWS, 1), F32)),
        grid=(n // t,),
        in_specs=[pl.BlockSpec((t, d), lambda i: (i, 0)),
                  _const_spec((d, LANES)), _const_spec((LANES, 1)), _const_spec((t, t))],
        out_specs=(pl.BlockSpec((t, LANES), lambda i: (i, 0)),
                   pl.BlockSpec((1, 1, t), lambda i: (i, 0, 0)),
                   _const_spec((CLASS_ROWS, 1))),
        scratch_shapes=[pltpu.VMEM((CLASS_ROWS, 1), F32)],
        compiler_params=_cparams(("arbitrary",)),
        name="moe_route",
    )(h2, w_router, bias, before)


def _slot_kernel(start_ref, key_ref, slot_ref):
    key = key_ref[...]
    cls = key >> KEY_SHIFT
    slot = key & ((1 << KEY_SHIFT) - 1)
    for c in range(N_CLASSES):
        slot = slot + jnp.where(cls == c, start_ref[c], 0)
    slot_ref[...] = slot


def _slots(class_start, key3):
    g, _, t = key3.shape
    return pl.pallas_call(
        _slot_kernel,
        out_shape=jax.ShapeDtypeStruct(key3.shape, jnp.int32),
        grid_spec=pltpu.PrefetchScalarGridSpec(
            num_scalar_prefetch=1, grid=(1,),
            in_specs=[pl.BlockSpec((g, 1, t), lambda i, s: (0, 0, 0))],
            out_specs=pl.BlockSpec((g, 1, t), lambda i, s: (0, 0, 0))),
        compiler_params=_cparams(("arbitrary",)),
        name="moe_slots",
    )(class_start, key3)


def _for_rows(n_rows, fn):
    def body(g, c):
        for j in range(ROW_COPY_UNROLL):
            fn(g * ROW_COPY_UNROLL + j, j % 2)
        return c

    lax.fori_loop(0, n_rows // ROW_COPY_UNROLL, body, 0)


def _dispatch_kernel(tile_end_ref, n_used_ref, slot_ref, h_ref, info_ref, xs_ref, rows, sems, zsem):
    i = pl.program_id(0)
    last = pl.num_programs(0) - 1
    p = i % 2
    d = h_ref.shape[1]
    t = DISPATCH_ROWS
    te = EXPERT_ROWS

    def row_copy(buf, r):
        return pltpu.make_async_copy(rows.at[buf, pl.ds(r, 1)],
                                     xs_ref.at[pl.ds(slot_ref[0, 0, r], 1)], sems.at[buf])

    @pl.when(i == 0)
    def _():
        rows[1, :te] = jnp.zeros((te, d + LANES), F32)

        def zero_tile(tile):
            return pltpu.make_async_copy(rows.at[1, pl.ds(0, te)],
                                         xs_ref.at[pl.ds(tile * te, te)], zsem)

        prev_end = 0
        for c in range(N_CLASSES):
            end = tile_end_ref[c]

            @pl.when(end > prev_end)
            def _(end=end):
                cp = zero_tile(end - 1)
                cp.start()
                cp.wait()
            prev_end = end

        def zero_tail(tile, c):
            cp = zero_tile(tile)
            cp.start()
            cp.wait()
            return c

        lax.fori_loop(n_used_ref[0], xs_ref.shape[0] // te, zero_tail, 0)

    rows[p, :, :d] = h_ref[...]
    rows[p, :, d:] = info_ref[...]
    _for_rows(t, lambda r, q: row_copy(p, r).start(priority=q))

    def row_wait(buf):
        pltpu.make_async_copy(rows.at[buf, pl.ds(0, 1)], xs_ref.at[pl.ds(0, 1)],
                              sems.at[buf]).wait()

    @pl.when(i > 0)
    def _():
        _for_rows(t, lambda r, q: row_wait(1 - p))

    @pl.when(i == last)
    def _():
        _for_rows(t, lambda r, q: row_wait(p))


def _dispatch(tile_end, n_used, slot3, h2, info, n_slots):
    n, d = h2.shape
    t = DISPATCH_ROWS
    return pl.pallas_call(
        _dispatch_kernel,
        out_shape=jax.ShapeDtypeStruct((n_slots, d + LANES), F32),
        grid_spec=pltpu.PrefetchScalarGridSpec(
            num_scalar_prefetch=2, grid=(n // t,),
            in_specs=[pl.BlockSpec((1, 1, t), lambda i, *_: (i, 0, 0), memory_space=pltpu.SMEM),
                      pl.BlockSpec((t, d), lambda i, *_: (i, 0)),
                      pl.BlockSpec((t, LANES), lambda i, *_: (i, 0))],
            out_specs=pl.BlockSpec(memory_space=pl.ANY),
            scratch_shapes=[pltpu.VMEM((2, t, d + LANES), F32), pltpu.SemaphoreType.DMA((2,)),
                            pltpu.SemaphoreType.DMA(())]),
        compiler_params=_cparams(("arbitrary",)),
        name="moe_dispatch",
    )(tile_end, n_used, slot3, h2, info)


def _expert_kernel(n_used_ref, src_ref, lo_ref, hi_ref, xs_ref, wg_lo, wg_hi, wu_lo, wu_hi,
                   wd_lo, wd_hi, ln_g_ref, ln_b_ref, ys_ref):
    del src_ref, lo_ref, hi_ref

    @pl.when(pl.program_id(0) < n_used_ref[0])
    def _():
        d = ys_ref.shape[1]
        x = xs_ref[:, :d]
        xb = x.astype(BF16)
        y = None
        for wg, wu, wd, lane in ((wg_lo, wu_lo, wd_lo, INFO_W_LOW), (wg_hi, wu_hi, wd_hi, INFO_W_HIGH)):
            gate = jnp.dot(xb, wg[0], preferred_element_type=F32)
            up = jnp.dot(xb, wu[0], preferred_element_type=F32)
            act = gate * _sigmoid(gate) * up * xs_ref[:, d + lane:d + lane + 1]
            part = jnp.dot(act.astype(BF16), wd[0], preferred_element_type=F32)
            y = part if y is None else y + part
        ys_ref[...] = _layer_norm(ALPHA * x + y, ln_g_ref[...], ln_b_ref[...])

    @pl.when(pl.program_id(0) >= n_used_ref[0])
    def _():
        ys_ref[...] = jnp.zeros_like(ys_ref)


def _experts(tile_meta, xs, layer, w_gate_b, w_up_b, w_down_b, ln_g, ln_b):
    n_slots, cols = xs.shape
    d = cols - LANES
    f = w_gate_b.shape[3]
    t = EXPERT_ROWS
    x_map = lambda i, n_used, src, lo, hi: (src[i], 0)
    lo_map = lambda i, n_used, src, lo, hi: (layer, lo[i], 0, 0)
    hi_map = lambda i, n_used, src, lo, hi: (layer, hi[i], 0, 0)
    const = lambda i, *_: (0, 0)
    grid_spec = pltpu.PrefetchScalarGridSpec(
        num_scalar_prefetch=4,
        grid=(n_slots // t,),
        in_specs=[pl.BlockSpec((t, cols), x_map),
                  pl.BlockSpec((None, 1, d, f), lo_map), pl.BlockSpec((None, 1, d, f), hi_map),
                  pl.BlockSpec((None, 1, d, f), lo_map), pl.BlockSpec((None, 1, d, f), hi_map),
                  pl.BlockSpec((None, 1, f, d), lo_map), pl.BlockSpec((None, 1, f, d), hi_map),
                  pl.BlockSpec((1, d), const), pl.BlockSpec((1, d), const)],
        out_specs=pl.BlockSpec((t, d), lambda i, *_: (i, 0)),
    )
    return pl.pallas_call(
        _expert_kernel,
        out_shape=jax.ShapeDtypeStruct((n_slots, d), F32),
        grid_spec=grid_spec,
        compiler_params=_cparams(("arbitrary",)),
        name="moe_experts",
    )(*tile_meta, xs, w_gate_b, w_gate_b, w_up_b, w_up_b, w_down_b, w_down_b,
      ln_g.reshape(1, d), ln_b.reshape(1, d))


def _combine_kernel(slot_ref, ys_ref, o_ref, buf, gsems, osems):
    i = pl.program_id(0)
    last = pl.num_programs(0) - 1
    p = i % 2
    t = DISPATCH_ROWS

    def tile_out(b, step):
        return pltpu.make_async_copy(buf.at[b], o_ref.at[pl.ds(step * t, t)], osems.at[b])

    def row_wait(b):
        pltpu.make_async_copy(ys_ref.at[pl.ds(0, 1)], buf.at[b, pl.ds(0, 1)], gsems.at[b]).wait()

    @pl.when(i >= 2)
    def _():
        tile_out(p, i - 2).wait()

    _for_rows(t, lambda r, q: pltpu.make_async_copy(
        ys_ref.at[pl.ds(slot_ref[0, 0, r], 1)], buf.at[p, pl.ds(r, 1)],
        gsems.at[p]).start(priority=q))

    @pl.when(i > 0)
    def _():
        _for_rows(t, lambda r, q: row_wait(1 - p))
        tile_out(1 - p, i - 1).start()

    @pl.when(i == last)
    def _():
        _for_rows(t, lambda r, q: row_wait(p))
        tile_out(p, i).start()
        tile_out(p, i).wait()

        @pl.when(i > 0)
        def _():
            tile_out(1 - p, i - 1).wait()


def _combine(slot3, ys, n):
    d = ys.shape[1]
    t = DISPATCH_ROWS
    return pl.pallas_call(
        _combine_kernel,
        out_shape=jax.ShapeDtypeStruct((n, d), F32),
        grid=(n // t,),
        in_specs=[pl.BlockSpec((1, 1, t), lambda i: (i, 0, 0), memory_space=pltpu.SMEM),
                  pl.BlockSpec(memory_space=pl.ANY)],
        out_specs=pl.BlockSpec(memory_space=pl.ANY),
        scratch_shapes=[pltpu.VMEM((2, t, d), F32), pltpu.SemaphoreType.DMA((2,)),
                        pltpu.SemaphoreType.DMA((2,))],
        compiler_params=_cparams(("arbitrary",)),
        name="moe_combine",
    )(slot3, ys)


def _pair_tables():
    lo, hi = [], []
    for g in range(N_GROUPS):
        for a in range(EXP_PER_GROUP):
            for b in range(a + 1, EXP_PER_GROUP):
                lo.append(g * EXP_PER_GROUP + a)
                hi.append(g * EXP_PER_GROUP + b)
    return jnp.asarray(lo, jnp.int32), jnp.asarray(hi, jnp.int32)


def _tile_plan(counts, n_tiles):
    t = EXPERT_ROWS
    counts = counts[:N_CLASSES, 0].astype(jnp.int32)
    tiles_per_class = (counts + t - 1) // t
    tile_end = jnp.cumsum(tiles_per_class)
    class_start = (tile_end - tiles_per_class) * t
    n_used = tile_end[-1:]
    src = jnp.minimum(jnp.arange(n_tiles, dtype=jnp.int32), n_used - 1)
    tile_class = jnp.sum((tile_end[None, :] <= src[:, None]).astype(jnp.int32), axis=1)
    pair_lo, pair_hi = _pair_tables()
    return class_start, tile_end, (n_used, src, pair_lo[tile_class], pair_hi[tile_class])


def kernel(x, ln_in_g, ln_in_b, w_in, gate_b, lru_conv_w, lru_conv_b, lru_wa, lru_ba, lru_wx, lru_bx, lru_lambda, sc_conv_w, w_branch_sb, w_branch_lru, w_branch_sc, w_out, ln1_g, ln1_b, w_group, group_bias, w_expert_router, expert_bias, w_gate, w_up, w_down, ln2_g, ln2_b):
    b, s, d = x.shape
    n = b * s
    params = dict(gate_b=gate_b, lru_conv_w=lru_conv_w, lru_conv_b=lru_conv_b, lru_wa=lru_wa,
                  lru_ba=lru_ba, lru_wx=lru_wx, lru_bx=lru_bx, lru_lambda=lru_lambda,
                  sc_conv_w=sc_conv_w, w_branch_sb=w_branch_sb, w_branch_lru=w_branch_lru,
                  w_branch_sc=w_branch_sc, w_out=w_out, ln1_g=ln1_g, ln1_b=ln1_b)
    w_in_b, w_gate_b, w_up_b, w_down_b = (w.astype(BF16) for w in (w_in, w_gate, w_up, w_down))
    h = x.reshape(n, d)
    for l in range(w_in.shape[0]):
        if l == 0:
            h, qkv, conv_in, act = _inproj(h, w_in_b, l, gate_b[l], entry_ln=(ln_in_g, ln_in_b))
        else:
            h, qkv, conv_in, act = _inproj(ys, w_in_b, l, gate_b[l], slots=slot3)
        y_sb = _sb_attention(qkv.reshape(b, s, -1))
        h1 = _mixer_tail(h.reshape(b, s, d), y_sb, conv_in.reshape(b, s, -1),
                         act.reshape(b, s, -1), params, l)
        h1 = h1.reshape(n, d)
        pad = LANES - N_EXPERTS - N_GROUPS
        w_router = jnp.concatenate(
            [w_expert_router[l], w_group[l], jnp.zeros((d, pad), F32)], axis=1).astype(F32)
        bias = jnp.concatenate(
            [expert_bias[l], group_bias[l], jnp.zeros((pad,), F32)]).reshape(LANES, 1).astype(F32)
        info, key3, counts = _route(h1, w_router, bias)
        n_tiles = n // EXPERT_ROWS + N_CLASSES
        class_start, tile_end, tile_meta = _tile_plan(counts, n_tiles)
        slot3 = _slots(class_start, key3)
        xs = _dispatch(tile_end, tile_meta[0], slot3, h1, info, n_tiles * EXPERT_ROWS)
        ys = _experts(tile_meta, xs, l, w_gate_b, w_up_b, w_down_b, ln2_g[l], ln2_b[l])
    return _combine(slot3, ys, n).reshape(b, s, d)
```

```python
import functools
import math

import jax
import jax.numpy as jnp
from jax import lax
from jax.experimental import pallas as pl
from jax.experimental.pallas import tpu as pltpu

F32 = jnp.float32
BF16 = jnp.bfloat16

SB_HEADS = 8
SB_HEAD_DIM = 64
SB_WIDTH = SB_HEADS * SB_HEAD_DIM
LRU_WIDTH = 512
LRU_BLOCKS = 8
LRU_C = 8.0
SC_WIDTH = 512
N_GROUPS = 4
EXP_PER_GROUP = 4
N_EXPERTS = N_GROUPS * EXP_PER_GROUP
D_EXPERT = 512
DEPTH = 2
ALPHA = (2 * DEPTH) ** 0.25
LN_EPS = 1e-5

LANES = 128
SUBLANES = 8
VMEM_LIMIT_BYTES = 56 * 1024 * 1024

PROJ_ROWS = 512
PROJ_CHUNK = 512
ATT_Q = 256
ATT_K = 256
ATT_LANE_BLOCKS = 2
MIX_ROWS = 512
ROUTE_ROWS = 512
DISPATCH_ROWS = 512
EXPERT_ROWS = 256
ROW_COPY_UNROLL = 8

PAIRS_PER_GROUP = EXP_PER_GROUP * (EXP_PER_GROUP - 1) // 2
N_CLASSES = N_GROUPS * PAIRS_PER_GROUP
CLASS_ROWS = 32
INFO_W_LOW, INFO_W_HIGH = 0, 1
KEY_SHIFT = 16


def _cparams(sem):
    return pltpu.CompilerParams(dimension_semantics=sem, vmem_limit_bytes=VMEM_LIMIT_BYTES)


def _const_spec(shape):
    nd = len(shape)
    return pl.BlockSpec(shape, lambda *_: (0,) * nd)


def _layer_norm(y, g, b):
    mu = jnp.mean(y, axis=-1, keepdims=True)
    d = y - mu
    var = jnp.mean(d * d, axis=-1, keepdims=True)
    return d * lax.rsqrt(var + LN_EPS) * g + b


QKV_COLS = 3 * SB_WIDTH
BRANCH_COLS = 2 * LRU_WIDTH + 3 * SC_WIDTH


def _sigmoid(x):
    return 0.5 * jnp.tanh(0.5 * x) + 0.5


def _gelu_tanh(x):
    return 0.5 * x * (1.0 + jnp.tanh(math.sqrt(2.0 / math.pi) * (x + 0.044715 * (x * x * x))))


def _project(h, w_ref, gate_b_ref, qkv_ref, conv_ref, act_ref, after_chunk=lambda k: None):
    hb = h.astype(BF16)
    c = PROJ_CHUNK

    def proj(chunk):
        out = jnp.dot(hb, w_ref[:, chunk * c:(chunk + 1) * c], preferred_element_type=F32)
        after_chunk(chunk)
        return out

    n_qkv = QKV_COLS // c
    for j in range(n_qkv):
        acc = proj(j)
        if j * c < SB_WIDTH:
            acc = acc * (SB_HEAD_DIM ** -0.5 * LOG2_E)
        qkv_ref[:, j * c:(j + 1) * c] = acc.astype(BF16)
    lru_in, lru_gate, sc_b, sc_c, sc_h = range(n_qkv, n_qkv + 5)
    conv_ref[:, 0:c] = proj(lru_in)
    conv_ref[:, c:2 * c] = proj(sc_b)
    conv_ref[:, 2 * c:3 * c] = proj(sc_c) * proj(sc_h)
    for j in range(gate_b_ref.shape[1] // c):
        acc = proj(sc_h + 1 + j) + gate_b_ref[:, j * c:(j + 1) * c]
        act_ref[:, j * c:(j + 1) * c] = _sigmoid(acc).astype(BF16)
    act_ref[:, gate_b_ref.shape[1]:] = _gelu_tanh(proj(lru_gate)).astype(BF16)


def _inproj_kernel(h_ref, w_ref, gate_b_ref, qkv_ref, conv_ref, act_ref):
    _project(h_ref[...], w_ref, gate_b_ref, qkv_ref, conv_ref, act_ref)


def _ln_inproj_kernel(x_ref, g_ref, b_ref, w_ref, gate_b_ref, h_ref, qkv_ref, conv_ref, act_ref):
    h = _layer_norm(x_ref[...], g_ref[...], b_ref[...])
    h_ref[...] = h
    _project(h, w_ref, gate_b_ref, qkv_ref, conv_ref, act_ref)


def _gather_inproj_kernel(slot_ref, slot_next_ref, ys_ref, w_ref, gate_b_ref,
                          h_ref, qkv_ref, conv_ref, act_ref, rows, sems):
    i = pl.program_id(0)
    last = pl.num_programs(0) - 1
    p = i % 2

    def fetch(slots, buf):
        def group(g, c):
            for j in range(SUBLANES):
                pltpu.make_async_copy(ys_ref.at[pl.ds(slots[0, 0, g * SUBLANES + j], 1)],
                                      rows.at[buf, g, pl.ds(j, 1)],
                                      sems.at[buf]).start(priority=j % 2)
            return c

        lax.fori_loop(0, PROJ_ROWS // SUBLANES, group, 0)

    @pl.when(i == 0)
    def _():
        fetch(slot_ref, 0)

    _for_rows(PROJ_ROWS, lambda r, q: pltpu.make_async_copy(
        ys_ref.at[pl.ds(0, 1)], rows.at[p, 0, pl.ds(0, 1)], sems.at[p]).wait())
    h = rows[p].reshape(PROJ_ROWS, rows.shape[-1])
    h_ref[...] = h

    n_groups = PROJ_ROWS // SUBLANES
    n_chunks = w_ref.shape[1] // PROJ_CHUNK

    def start_some(k):
        for g in range(k * n_groups // n_chunks, (k + 1) * n_groups // n_chunks):
            for j in range(SUBLANES):
                pltpu.make_async_copy(
                    ys_ref.at[pl.ds(slot_next_ref[0, 0, g * SUBLANES + j], 1)],
                    rows.at[1 - p, g, pl.ds(j, 1)], sems.at[1 - p]).start(priority=j % 2)

    _project(h, w_ref, gate_b_ref, qkv_ref, conv_ref, act_ref, after_chunk=start_some)

    @pl.when(i == last)
    def _():
        _for_rows(PROJ_ROWS, lambda r, q: pltpu.make_async_copy(
            ys_ref.at[pl.ds(0, 1)], rows.at[1 - p, 0, pl.ds(0, 1)], sems.at[1 - p]).wait())


def _inproj(h2, w_in_b, layer, gate_b, entry_ln=None, slots=None):
    d = h2.shape[1]
    n = h2.shape[0] if slots is None else slots.size
    cols = w_in_b.shape[2]
    gate_cols = gate_b.size
    assert LRU_WIDTH == SC_WIDTH == PROJ_CHUNK and cols == QKV_COLS + BRANCH_COLS + gate_cols
    widths = (QKV_COLS, 3 * PROJ_CHUNK, gate_cols + LRU_WIDTH)
    dtypes = (BF16, F32, BF16)
    rows = pl.BlockSpec((PROJ_ROWS, d), lambda i: (i, 0))
    weight_specs = [pl.BlockSpec((None, d, cols), lambda i: (layer, 0, 0),
                                 pipeline_mode=pl.Buffered(1)),
                    _const_spec((1, gate_cols))]
    weights = (w_in_b, gate_b.reshape(1, gate_cols).astype(F32))
    out_shape = [jax.ShapeDtypeStruct((n, w), t) for w, t in zip(widths, dtypes)]
    out_specs = [pl.BlockSpec((PROJ_ROWS, w), lambda i: (i, 0)) for w in widths]
    if slots is not None:
        assert DISPATCH_ROWS == PROJ_ROWS
        steps = n // PROJ_ROWS
        slot_spec = lambda index: pl.BlockSpec((1, 1, PROJ_ROWS), index, memory_space=pltpu.SMEM)
        out_shape.insert(0, jax.ShapeDtypeStruct((n, d), F32))
        out_specs.insert(0, rows)
        return pl.pallas_call(
            _gather_inproj_kernel,
            out_shape=tuple(out_shape),
            grid=(steps,),
            in_specs=[slot_spec(lambda i: (i, 0, 0)),
                      slot_spec(lambda i: (jnp.minimum(i + 1, steps - 1), 0, 0)),
                      pl.BlockSpec(memory_space=pl.ANY)] + weight_specs,
            out_specs=tuple(out_specs),
            scratch_shapes=[pltpu.VMEM((2, PROJ_ROWS // SUBLANES, SUBLANES, d), F32),
                            pltpu.SemaphoreType.DMA((2,))],
            compiler_params=_cparams(("arbitrary",)),
            name="in_proj_gather",
        )(slots, slots, h2, *weights)
    if entry_ln is None:
        body, ln_specs, ln_args = _inproj_kernel, [], ()
    else:
        body = _ln_inproj_kernel
        ln_specs = [_const_spec((1, d)), _const_spec((1, d))]
        ln_args = tuple(v.reshape(1, d).astype(F32) for v in entry_ln)
        out_shape.insert(0, jax.ShapeDtypeStruct((n, d), F32))
        out_specs.insert(0, rows)
    return pl.pallas_call(
        body,
        out_shape=tuple(out_shape),
        grid=(n // PROJ_ROWS,),
        in_specs=[rows] + ln_specs + weight_specs,
        out_specs=tuple(out_specs),
        compiler_params=_cparams(("parallel",)),
        name="in_proj",
    )(h2, *ln_args, *weights)


LOG2_E = math.log2(math.e)
PASS_LOG2_FLOOR = -151.0


HEADS_PER_STEP = LANES // SB_HEAD_DIM


def _per_head(x):
    lane = lax.broadcasted_iota(jnp.int32, x.shape, 1)
    return jnp.concatenate(
        [jnp.where((lane >= h * SB_HEAD_DIM) & (lane < (h + 1) * SB_HEAD_DIM), x, jnp.zeros_like(x))
         for h in range(HEADS_PER_STEP)], axis=0)


MASKED_SCORE = -1e30


def _neg_abs(x):
    bits = lax.bitcast_convert_type(x, jnp.uint32) | jnp.uint32(0x80000000)
    return lax.bitcast_convert_type(bits, F32)


def _sb_block(q, k_j, v_j, neg_tri2, carries, mask):
    nq, kk = q.shape[0], k_j.shape[0]
    z = lax.dot_general(q, _per_head(k_j), (((1,), (1,)), ((), ())), preferred_element_type=F32)
    scores, split = [], []
    for h in range(HEADS_PER_STEP):
        z_h = z[:, h * kk:(h + 1) * kk]
        if mask is not None:
            z_h = jnp.where(mask, z_h, MASKED_SCORE)
        drop = jnp.maximum(z_h, 0.0) + jnp.log2(1.0 + jnp.exp2(_neg_abs(z_h)))
        scores.append(z_h)
        split.append(drop.astype(BF16))
    log_pass = jnp.dot(jnp.concatenate(split, axis=0), neg_tri2, preferred_element_type=F32)
    w, new_carries = [], []
    for h in range(HEADS_PER_STEP):
        lp_h = log_pass[h * nq:(h + 1) * nq]
        w.append(jnp.exp2(scores[h] + lp_h + carries[h]).astype(BF16))
        new_carries.append(carries[h] + lp_h[:, 0:1])
    pv = jnp.dot(jnp.concatenate(w, axis=1), _per_head(v_j), preferred_element_type=F32)
    return pv, new_carries


def _attn_kernel(q_ref, k_ref, v_ref, u2_ref, o_ref):
    qi = pl.program_id(2)
    u2 = u2_ref[...]
    row = lax.broadcasted_iota(jnp.int32, (ATT_Q, ATT_K), 0)
    col = lax.broadcasted_iota(jnp.int32, (ATT_Q, ATT_K), 1)
    diag_mask = col < row
    blocks = [slice(g * LANES, (g + 1) * LANES) for g in range(ATT_LANE_BLOCKS)]
    q = [q_ref[0, :, lanes] for lanes in blocks]

    def sweep(j, accs, carries, mask):
        rows = pl.ds(pl.multiple_of(j * ATT_K, ATT_K), ATT_K)
        out = [_sb_block(q_g, k_ref[0, rows, lanes], v_ref[0, rows, lanes], u2, c_g, mask)
               for q_g, lanes, c_g in zip(q, blocks, carries)]
        return [acc + pv for acc, (pv, _) in zip(accs, out)], [c for _, c in out]

    zero = [[jnp.zeros((ATT_Q, 1), F32) for _ in range(HEADS_PER_STEP)] for _ in blocks]
    accs, carries = sweep(qi, [jnp.zeros((ATT_Q, LANES), F32) for _ in blocks], zero, diag_mask)
    no_prev = jnp.where(qi > 0, 0.0, -1e30)
    accs, carries = sweep(jnp.maximum(qi - 1, 0), accs,
                          [[c + no_prev for c in c_g] for c_g in carries], None)

    def largest(carries):
        return jnp.max(functools.reduce(jnp.maximum, [c for c_g in carries for c in c_g]))

    def cond(state):
        j, _, _, live = state
        return jnp.logical_and(j >= 0, live > PASS_LOG2_FLOOR)

    def body(state):
        j, accs, carries, _ = state
        accs, carries = sweep(j, accs, carries, None)
        return j - 1, accs, carries, largest(carries)

    _, accs, _, _ = lax.while_loop(cond, body, (qi - 2, accs, carries, largest(carries)))
    o_ref[0] = jnp.concatenate(accs, axis=1).astype(o_ref.dtype)


def _sb_attention(qkv3):
    b, s, _ = qkv3.shape
    width = ATT_LANE_BLOCKS * LANES
    groups = SB_WIDTH // width
    neg_tri = -(jnp.arange(ATT_K)[:, None] >= jnp.arange(ATT_K)[None, :]).astype(BF16)
    u2 = neg_tri
    return pl.pallas_call(
        _attn_kernel,
        out_shape=jax.ShapeDtypeStruct((b, s, SB_WIDTH), BF16),
        grid=(b, groups, s // ATT_Q),
        in_specs=[pl.BlockSpec((1, ATT_Q, width), lambda bi, p, qi: (bi, qi, p)),
                  pl.BlockSpec((1, s, width), lambda bi, p, qi: (bi, 0, groups + p)),
                  pl.BlockSpec((1, s, width), lambda bi, p, qi: (bi, 0, 2 * groups + p)),
                  _const_spec((ATT_K, ATT_K))],
        out_specs=pl.BlockSpec((1, ATT_Q, width), lambda bi, p, qi: (bi, qi, p)),
        compiler_params=_cparams(("parallel", "parallel", "arbitrary")),
        name="sb_attention",
    )(qkv3, qkv3, qkv3, u2)


def _causal_conv(x, stage, w):
    t = x.shape[0]
    k = w.shape[0]
    stage[SUBLANES:, :] = x
    out = w[k - 1:k] * x
    for j in range(1, k):
        out = out + w[k - 1 - j:k - j] * stage[pl.ds(SUBLANES - j, t), :]
    stage[:SUBLANES, :] = x[t - SUBLANES:]
    return out


def _linear_scan(a, b, h0):
    t = a.shape[0]
    sub = lax.broadcasted_iota(jnp.int32, a.shape, 0) % SUBLANES
    d = 1
    while d < SUBLANES:
        keep = sub >= d
        a_prev = jnp.where(keep, pltpu.roll(a, d, axis=0), 1.0)
        b_prev = jnp.where(keep, pltpu.roll(b, d, axis=0), 0.0)
        b = a * b_prev + b
        a = a * a_prev
        d *= 2
    h = h0
    groups = []
    for g in range(t // SUBLANES):
        rows = slice(g * SUBLANES, (g + 1) * SUBLANES)
        seg = b[rows] + a[rows] * h
        h = seg[SUBLANES - 1:]
        groups.append(seg)
    return jnp.concatenate(groups, axis=0)


def _mix_kernel(h_ref, ysb_ref, lin_ref, scb_ref, scp_ref, gsb_ref, glru_ref, gsc_ref, gelu_ref,
                lconv_w_ref, lconv_b_ref, wa_ref, ba_ref, wx_ref, bx_ref, lam_ref, scw_ref,
                wsb_ref, wlru_ref, wsc_ref, wout_ref, ln_g_ref, ln_b_ref,
                out_ref, lin_stage, sc_stage, h_state):
    t = MIX_ROWS

    @pl.when(pl.program_id(1) == 0)
    def _():
        lin_stage[:SUBLANES, :] = jnp.zeros((SUBLANES, LRU_WIDTH), F32)
        sc_stage[:SUBLANES, :] = jnp.zeros((SUBLANES, SC_WIDTH), F32)
        h_state[...] = jnp.zeros_like(h_state)

    u = _causal_conv(lin_ref[0], lin_stage, lconv_w_ref[...]) + lconv_b_ref[...]
    ub = u.astype(BF16)
    r = _sigmoid(jnp.dot(ub, wa_ref[...], preferred_element_type=F32) + ba_ref[...])
    i = _sigmoid(jnp.dot(ub, wx_ref[...], preferred_element_type=F32) + bx_ref[...])
    lam = lam_ref[...]
    softplus_neg_lam = jnp.maximum(-lam, 0.0) + jnp.log(1.0 + jnp.exp(-jnp.abs(lam)))
    log_a = (-LRU_C) * r * softplus_neg_lam
    a = jnp.exp(log_a)
    gap = 1.0 - a * a
    drive = jnp.where(gap > 0.0, gap * lax.rsqrt(gap), 0.0) * (i * u)
    hs = _linear_scan(a, drive, h_state[...])
    h_state[...] = hs[t - 1:]
    y_lru = gelu_ref[0] * hs.astype(BF16)

    y_sc = (scb_ref[0] * _causal_conv(scp_ref[0], sc_stage, scw_ref[...])).astype(BF16)

    def branch(y, w_ref):
        return jnp.dot(y, w_ref[...], preferred_element_type=F32).astype(BF16)

    merged = (gsb_ref[0] * branch(ysb_ref[0], wsb_ref) + glru_ref[0] * branch(y_lru, wlru_ref)
              + gsc_ref[0] * branch(y_sc, wsc_ref))
    mix = jnp.dot(merged, wout_ref[...], preferred_element_type=F32)
    out_ref[0] = _layer_norm(ALPHA * h_ref[0] + mix, ln_g_ref[...], ln_b_ref[...])


def _block_diag(w):
    hh, ii, jj = w.shape
    eye = jnp.eye(hh, dtype=w.dtype)
    return (eye[:, None, :, None] * w[:, :, None, :]).reshape(hh * ii, hh * jj)


def _mixer_tail(h3, ysb3, conv3, act3, p, l):
    b, s, d = h3.shape
    t = MIX_ROWS
    assert LRU_WIDTH == SC_WIDTH

    def row_spec(width, col):
        return pl.BlockSpec((1, t, width), lambda bi, si: (bi, si, col))

    vec = lambda v: v.reshape(1, -1).astype(F32)
    in_specs = [row_spec(d, 0), row_spec(SB_WIDTH, 0)]
    in_specs += [row_spec(LRU_WIDTH, c) for c in range(3)]
    in_specs += [row_spec(d, g) for g in range(3)]
    in_specs += [row_spec(LRU_WIDTH, 3 * d // LRU_WIDTH)]
    weights = [
        p["lru_conv_w"][l].astype(F32), vec(p["lru_conv_b"][l]),
        _block_diag(p["lru_wa"][l]).astype(BF16), vec(p["lru_ba"][l]),
        _block_diag(p["lru_wx"][l]).astype(BF16), vec(p["lru_bx"][l]),
        vec(p["lru_lambda"][l]), p["sc_conv_w"][l].astype(F32),
        p["w_branch_sb"][l].astype(BF16), p["w_branch_lru"][l].astype(BF16),
        p["w_branch_sc"][l].astype(BF16), p["w_out"][l].astype(BF16),
        vec(p["ln1_g"][l]), vec(p["ln1_b"][l]),
    ]
    in_specs += [_const_spec(w.shape) for w in weights]
    return pl.pallas_call(
        _mix_kernel,
        out_shape=jax.ShapeDtypeStruct((b, s, d), F32),
        grid=(b, s // t),
        in_specs=in_specs,
        out_specs=pl.BlockSpec((1, t, d), lambda bi, si: (bi, si, 0)),
        scratch_shapes=[pltpu.VMEM((SUBLANES + t, LRU_WIDTH), F32),
                        pltpu.VMEM((SUBLANES + t, SC_WIDTH), F32),
                        pltpu.VMEM((1, LRU_WIDTH), F32)],
        compiler_params=_cparams(("parallel", "arbitrary")),
        name="mixer_tail",
    )(h3, ysb3, *([conv3] * 3), *([act3] * 4), *weights)


def _first_argmax(vals, row):
    m = jnp.max(vals, axis=0, keepdims=True)
    idx = jnp.min(jnp.where(vals == m, row, vals.shape[0]), axis=0, keepdims=True)
    return m, idx


def _route_kernel(h_ref, w_ref, bias_ref, before_ref, info_ref, key_ref, counts_ref, running):
    @pl.when(pl.program_id(0) == 0)
    def _():
        running[...] = jnp.zeros_like(running)

    h = h_ref[...]
    h_hi = h.astype(BF16)
    h_lo = (h - h_hi.astype(F32)).astype(BF16)
    w = w_ref[...]
    w_hi = w.astype(BF16)
    w_lo = (w - w_hi.astype(F32)).astype(BF16)
    logits = jnp.dot(jnp.concatenate([h_hi, h_lo, h_hi], axis=1),
                     jnp.concatenate([w_hi, w_hi, w_lo], axis=0), preferred_element_type=F32)
    lt = logits.T
    neg = -jnp.inf
    experts = lt[:N_EXPERTS]
    groups = lt[N_EXPERTS:N_EXPERTS + SUBLANES]
    experts_b = experts + bias_ref[:N_EXPERTS]
    groups_b = groups + bias_ref[N_EXPERTS:N_EXPERTS + SUBLANES]
    row_g = lax.broadcasted_iota(jnp.int32, groups.shape, 0)
    row_e = lax.broadcasted_iota(jnp.int32, experts.shape, 0)
    is_group = row_g < N_GROUPS
    g_max = jnp.max(jnp.where(is_group, groups, neg), axis=0, keepdims=True)
    g_exp = jnp.where(is_group, jnp.exp(groups - g_max), 0.0)
    g_den = jnp.sum(g_exp, axis=0, keepdims=True)
    _, g_sel = _first_argmax(jnp.where(is_group, groups_b, neg), row_g)
    g_prob = jnp.sum(jnp.where(row_g == g_sel, g_exp, 0.0), axis=0, keepdims=True) / g_den
    cand = jnp.where((row_e >> 2) == g_sel, experts_b, neg)
    _, e1 = _first_argmax(cand, row_e)
    _, e2 = _first_argmax(jnp.where(row_e == e1, neg, cand), row_e)
    l1 = jnp.sum(jnp.where(row_e == e1, experts, 0.0), axis=0, keepdims=True)
    l2 = jnp.sum(jnp.where(row_e == e2, experts, 0.0), axis=0, keepdims=True)
    m = jnp.maximum(l1, l2)
    x1 = jnp.exp(l1 - m)
    x2 = jnp.exp(l2 - m)
    scale = g_prob / (x1 + x2)
    w1 = x1 * scale
    w2 = x2 * scale
    first_low = e1 < e2
    a = jnp.where(first_low, e1, e2) - g_sel * EXP_PER_GROUP
    b = jnp.where(first_low, e2, e1) - g_sel * EXP_PER_GROUP
    cls = g_sel * PAIRS_PER_GROUP + ((a * (2 * EXP_PER_GROUP - 1 - a)) >> 1) + (b - a - 1)
    row_c = lax.broadcasted_iota(jnp.int32, (CLASS_ROWS, cls.shape[1]), 0)
    onehot = row_c == cls
    ones = jnp.where(onehot, 1.0, 0.0)
    earlier = jnp.dot(ones, before_ref[...], preferred_element_type=F32) + running[...]
    rank = jnp.sum(jnp.where(onehot, earlier, 0.0), axis=0, keepdims=True)
    running[...] += jnp.sum(ones, axis=1, keepdims=True)
    counts_ref[...] = running[...]
    key_ref[0] = (cls << KEY_SHIFT) + rank.astype(jnp.int32)
    row_w = lax.broadcasted_iota(jnp.int32, (LANES, cls.shape[1]), 0)
    w_rows = jnp.where(row_w == INFO_W_LOW, jnp.where(first_low, w1, w2), 0.0)
    w_rows = jnp.where(row_w == INFO_W_HIGH, jnp.where(first_low, w2, w1), w_rows)
    info_ref[...] = w_rows.T


def _route(h2, w_router, bias):
    n, d = h2.shape
    t = ROUTE_ROWS
    before = (jnp.arange(t)[:, None] < jnp.arange(t)[None, :]).astype(F32)
    return pl.pallas_call(
        _route_kernel,
        out_shape=(jax.ShapeDtypeStruct((n, LANES), F32),
                   jax.ShapeDtypeStruct((n // t, 1, t), jnp.int32),
                   jax.ShapeDtypeStruct((CLASS_ROWS, 1), F32)),
        grid=(n // t,),
        in_specs=[pl.BlockSpec((t, d), lambda i: (i, 0)),
                  _const_spec((d, LANES)), _const_spec((LANES, 1)), _const_spec((t, t))],
        out_specs=(pl.BlockSpec((t, LANES), lambda i: (i, 0)),
                   pl.BlockSpec((1, 1, t), lambda i: (i, 0, 0)),
                   _const_spec((CLASS_ROWS, 1))),
        scratch_shapes=[pltpu.VMEM((CLASS_ROWS, 1), F32)],
        compiler_params=_cparams(("arbitrary",)),
        name="moe_route",
    )(h2, w_router, bias, before)


def _slot_kernel(start_ref, key_ref, slot_ref):
    key = key_ref[...]
    cls = key >> KEY_SHIFT
    slot = key & ((1 << KEY_SHIFT) - 1)
    for c in range(N_CLASSES):
        slot = slot + jnp.where(cls == c, start_ref[c], 0)
    slot_ref[...] = slot


def _slots(class_start, key3):
    g, _, t = key3.shape
    return pl.pallas_call(
        _slot_kernel,
        out_shape=jax.ShapeDtypeStruct(key3.shape, jnp.int32),
        grid_spec=pltpu.PrefetchScalarGridSpec(
            num_scalar_prefetch=1, grid=(1,),
            in_specs=[pl.BlockSpec((g, 1, t), lambda i, s: (0, 0, 0))],
            out_specs=pl.BlockSpec((g, 1, t), lambda i, s: (0, 0, 0))),
        compiler_params=_cparams(("arbitrary",)),
        name="moe_slots",
    )(class_start, key3)


def _for_rows(n_rows, fn):
    def body(g, c):
        for j in range(ROW_COPY_UNROLL):
            fn(g * ROW_COPY_UNROLL + j, j % 2)
        return c

    lax.fori_loop(0, n_rows // ROW_COPY_UNROLL, body, 0)


def _dispatch_kernel(tile_end_ref, n_used_ref, slot_ref, h_ref, info_ref, xs_ref, rows, sems, zsem):
    i = pl.program_id(0)
    last = pl.num_programs(0) - 1
    p = i % 2
    d = h_ref.shape[1]
    t = DISPATCH_ROWS
    te = EXPERT_ROWS

    def row_copy(buf, r):
        return pltpu.make_async_copy(rows.at[buf, pl.ds(r, 1)],
                                     xs_ref.at[pl.ds(slot_ref[0, 0, r], 1)], sems.at[buf])

    @pl.when(i == 0)
    def _():
        rows[1, :te] = jnp.zeros((te, d + LANES), F32)

        def zero_tile(tile):
            return pltpu.make_async_copy(rows.at[1, pl.ds(0, te)],
                                         xs_ref.at[pl.ds(tile * te, te)], zsem)

        prev_end = 0
        for c in range(N_CLASSES):
            end = tile_end_ref[c]

            @pl.when(end > prev_end)
            def _(end=end):
                cp = zero_tile(end - 1)
                cp.start()
                cp.wait()
            prev_end = end

        def zero_tail(tile, c):
            cp = zero_tile(tile)
            cp.start()
            cp.wait()
            return c

        lax.fori_loop(n_used_ref[0], xs_ref.shape[0] // te, zero_tail, 0)

    rows[p, :, :d] = h_ref[...]
    rows[p, :, d:] = info_ref[...]
    _for_rows(t, lambda r, q: row_copy(p, r).start(priority=q))

    def row_wait(buf):
        pltpu.make_async_copy(rows.at[buf, pl.ds(0, 1)], xs_ref.at[pl.ds(0, 1)],
                              sems.at[buf]).wait()

    @pl.when(i > 0)
    def _():
        _for_rows(t, lambda r, q: row_wait(1 - p))

    @pl.when(i == last)
    def _():
        _for_rows(t, lambda r, q: row_wait(p))


def _dispatch(tile_end, n_used, slot3, h2, info, n_slots):
    n, d = h2.shape
    t = DISPATCH_ROWS
    return pl.pallas_call(
        _dispatch_kernel,
        out_shape=jax.ShapeDtypeStruct((n_slots, d + LANES), F32),
        grid_spec=pltpu.PrefetchScalarGridSpec(
            num_scalar_prefetch=2, grid=(n // t,),
            in_specs=[pl.BlockSpec((1, 1, t), lambda i, *_: (i, 0, 0), memory_space=pltpu.SMEM),
                      pl.BlockSpec((t, d), lambda i, *_: (i, 0)),
                      pl.BlockSpec((t, LANES), lambda i, *_: (i, 0))],
            out_specs=pl.BlockSpec(memory_space=pl.ANY),
            scratch_shapes=[pltpu.VMEM((2, t, d + LANES), F32), pltpu.SemaphoreType.DMA((2,)),
                            pltpu.SemaphoreType.DMA(())]),
        compiler_params=_cparams(("arbitrary",)),
        name="moe_dispatch",
    )(tile_end, n_used, slot3, h2, info)


def _expert_kernel(n_used_ref, src_ref, lo_ref, hi_ref, xs_ref, wg_lo, wg_hi, wu_lo, wu_hi,
                   wd_lo, wd_hi, ln_g_ref, ln_b_ref, ys_ref):
    del src_ref, lo_ref, hi_ref

    @pl.when(pl.program_id(0) < n_used_ref[0])
    def _():
        d = ys_ref.shape[1]
        x = xs_ref[:, :d]
        xb = x.astype(BF16)
        y = None
        for wg, wu, wd, lane in ((wg_lo, wu_lo, wd_lo, INFO_W_LOW), (wg_hi, wu_hi, wd_hi, INFO_W_HIGH)):
            gate = jnp.dot(xb, wg[0], preferred_element_type=F32)
            up = jnp.dot(xb, wu[0], preferred_element_type=F32)
            act = gate * _sigmoid(gate) * up * xs_ref[:, d + lane:d + lane + 1]
            part = jnp.dot(act.astype(BF16), wd[0], preferred_element_type=F32)
            y = part if y is None else y + part
        ys_ref[...] = _layer_norm(ALPHA * x + y, ln_g_ref[...], ln_b_ref[...])

    @pl.when(pl.program_id(0) >= n_used_ref[0])
    def _():
        ys_ref[...] = jnp.zeros_like(ys_ref)


def _experts(tile_meta, xs, layer, w_gate_b, w_up_b, w_down_b, ln_g, ln_b):
    n_slots, cols = xs.shape
    d = cols - LANES
    f = w_gate_b.shape[3]
    t = EXPERT_ROWS
    x_map = lambda i, n_used, src, lo, hi: (src[i], 0)
    lo_map = lambda i, n_used, src, lo, hi: (layer, lo[i], 0, 0)
    hi_map = lambda i, n_used, src, lo, hi: (layer, hi[i], 0, 0)
    const = lambda i, *_: (0, 0)
    grid_spec = pltpu.PrefetchScalarGridSpec(
        num_scalar_prefetch=4,
        grid=(n_slots // t,),
        in_specs=[pl.BlockSpec((t, cols), x_map),
                  pl.BlockSpec((None, 1, d, f), lo_map), pl.BlockSpec((None, 1, d, f), hi_map),
                  pl.BlockSpec((None, 1, d, f), lo_map), pl.BlockSpec((None, 1, d, f), hi_map),
                  pl.BlockSpec((None, 1, f, d), lo_map), pl.BlockSpec((None, 1, f, d), hi_map),
                  pl.BlockSpec((1, d), const), pl.BlockSpec((1, d), const)],
        out_specs=pl.BlockSpec((t, d), lambda i, *_: (i, 0)),
    )
    return pl.pallas_call(
        _expert_kernel,
        out_shape=jax.ShapeDtypeStruct((n_slots, d), F32),
        grid_spec=grid_spec,
        compiler_params=_cparams(("arbitrary",)),
        name="moe_experts",
    )(*tile_meta, xs, w_gate_b, w_gate_b, w_up_b, w_up_b, w_down_b, w_down_b,
      ln_g.reshape(1, d), ln_b.reshape(1, d))


def _combine_kernel(slot_ref, ys_ref, o_ref, buf, gsems, osems):
    i = pl.program_id(0)
    last = pl.num_programs(0) - 1
    p = i % 2
    t = DISPATCH_ROWS

    def tile_out(b, step):
        return pltpu.make_async_copy(buf.at[b], o_ref.at[pl.ds(step * t, t)], osems.at[b])

    def row_wait(b):
        pltpu.make_async_copy(ys_ref.at[pl.ds(0, 1)], buf.at[b, pl.ds(0, 1)], gsems.at[b]).wait()

    @pl.when(i >= 2)
    def _():
        tile_out(p, i - 2).wait()

    _for_rows(t, lambda r, q: pltpu.make_async_copy(
        ys_ref.at[pl.ds(slot_ref[0, 0, r], 1)], buf.at[p, pl.ds(r, 1)],
        gsems.at[p]).start(priority=q))

    @pl.when(i > 0)
    def _():
        _for_rows(t, lambda r, q: row_wait(1 - p))
        tile_out(1 - p, i - 1).start()

    @pl.when(i == last)
    def _():
        _for_rows(t, lambda r, q: row_wait(p))
        tile_out(p, i).start()
        tile_out(p, i).wait()

        @pl.when(i > 0)
        def _():
            tile_out(1 - p, i - 1).wait()


def _combine(slot3, ys, n):
    d = ys.shape[1]
    t = DISPATCH_ROWS
    return pl.pallas_call(
        _combine_kernel,
        out_shape=jax.ShapeDtypeStruct((n, d), F32),
        grid=(n // t,),
        in_specs=[pl.BlockSpec((1, 1, t), lambda i: (i, 0, 0), memory_space=pltpu.SMEM),
                  pl.BlockSpec(memory_space=pl.ANY)],
        out_specs=pl.BlockSpec(memory_space=pl.ANY),
        scratch_shapes=[pltpu.VMEM((2, t, d), F32), pltpu.SemaphoreType.DMA((2,)),
                        pltpu.SemaphoreType.DMA((2,))],
        compiler_params=_cparams(("arbitrary",)),
        name="moe_combine",
    )(slot3, ys)


def _pair_tables():
    lo, hi = [], []
    for g in range(N_GROUPS):
        for a in range(EXP_PER_GROUP):
            for b in range(a + 1, EXP_PER_GROUP):
                lo.append(g * EXP_PER_GROUP + a)
                hi.append(g * EXP_PER_GROUP + b)
    return jnp.asarray(lo, jnp.int32), jnp.asarray(hi, jnp.int32)


def _tile_plan(counts, n_tiles):
    t = EXPERT_ROWS
    counts = counts[:N_CLASSES, 0].astype(jnp.int32)
    tiles_per_class = (counts + t - 1) // t
    tile_end = jnp.cumsum(tiles_per_class)
    class_start = (tile_end - tiles_per_class) * t
    n_used = tile_end[-1:]
    src = jnp.minimum(jnp.arange(n_tiles, dtype=jnp.int32), n_used - 1)
    tile_class = jnp.sum((tile_end[None, :] <= src[:, None]).astype(jnp.int32), axis=1)
    pair_lo, pair_hi = _pair_tables()
    return class_start, tile_end, (n_used, src, pair_lo[tile_class], pair_hi[tile_class])


def kernel(x, ln_in_g, ln_in_b, w_in, gate_b, lru_conv_w, lru_conv_b, lru_wa, lru_ba, lru_wx, lru_bx, lru_lambda, sc_conv_w, w_branch_sb, w_branch_lru, w_branch_sc, w_out, ln1_g, ln1_b, w_group, group_bias, w_expert_router, expert_bias, w_gate, w_up, w_down, ln2_g, ln2_b):
    b, s, d = x.shape
    n = b * s
    params = dict(gate_b=gate_b, lru_conv_w=lru_conv_w, lru_conv_b=lru_conv_b, lru_wa=lru_wa,
                  lru_ba=lru_ba, lru_wx=lru_wx, lru_bx=lru_bx, lru_lambda=lru_lambda,
                  sc_conv_w=sc_conv_w, w_branch_sb=w_branch_sb, w_branch_lru=w_branch_lru,
                  w_branch_sc=w_branch_sc, w_out=w_out, ln1_g=ln1_g, ln1_b=ln1_b)
    w_in_b, w_gate_b, w_up_b, w_down_b = (w.astype(BF16) for w in (w_in, w_gate, w_up, w_down))
    h = x.reshape(n, d)
    for l in range(w_in.shape[0]):
        if l == 0:
            h, qkv, conv_in, act = _inproj(h, w_in_b, l, gate_b[l], entry_ln=(ln_in_g, ln_in_b))
        else:
            h, qkv, conv_in, act = _inproj(ys, w_in_b, l, gate_b[l], slots=slot3)
        y_sb = _sb_attention(qkv.reshape(b, s, -1))
        h1 = _mixer_tail(h.reshape(b, s, d), y_sb, conv_in.reshape(b, s, -1),
                         act.reshape(b, s, -1), params, l)
        h1 = h1.reshape(n, d)
        pad = LANES - N_EXPERTS - N_GROUPS
        w_router = jnp.concatenate(
            [w_expert_router[l], w_group[l], jnp.zeros((d, pad), F32)], axis=1).astype(F32)
        bias = jnp.concatenate(
            [expert_bias[l], group_bias[l], jnp.zeros((pad,), F32)]).reshape(LANES, 1).astype(F32)
        info, key3, counts = _route(h1, w_router, bias)
        n_tiles = n // EXPERT_ROWS + N_CLASSES
        class_start, tile_end, tile_meta = _tile_plan(counts, n_tiles)
        slot3 = _slots(class_start, key3)
        xs = _dispatch(tile_end, tile_meta[0], slot3, h1, info, n_tiles * EXPERT_ROWS)
        ys = _experts(tile_meta, xs, l, w_gate_b, w_up_b, w_down_b, ln2_g[l], ln2_b[l])
    return _combine(slot3, ys, n).reshape(b, s, d)
```

```python
import functools
import math

import jax
import jax.numpy as jnp
from jax import lax
from jax.experimental import pallas as pl
from jax.experimental.pallas import tpu as pltpu

F32 = jnp.float32
BF16 = jnp.bfloat16

SB_HEADS = 8
SB_HEAD_DIM = 64
SB_WIDTH = SB_HEADS * SB_HEAD_DIM
LRU_WIDTH = 512
LRU_BLOCKS = 8
LRU_C = 8.0
SC_WIDTH = 512
N_GROUPS = 4
EXP_PER_GROUP = 4
N_EXPERTS = N_GROUPS * EXP_PER_GROUP
D_EXPERT = 512
DEPTH = 2
ALPHA = (2 * DEPTH) ** 0.25
LN_EPS = 1e-5

LANES = 128
SUBLANES = 8
VMEM_LIMIT_BYTES = 56 * 1024 * 1024

PROJ_ROWS = 512
PROJ_CHUNK = 512
ATT_Q = 256
ATT_K = 256
ATT_LANE_BLOCKS = 2
MIX_ROWS = 512
ROUTE_ROWS = 512
DISPATCH_ROWS = 512
EXPERT_ROWS = 256
ROW_COPY_UNROLL = 8

PAIRS_PER_GROUP = EXP_PER_GROUP * (EXP_PER_GROUP - 1) // 2
N_CLASSES = N_GROUPS * PAIRS_PER_GROUP
CLASS_ROWS = 32
INFO_W_LOW, INFO_W_HIGH = 0, 1
KEY_SHIFT = 16


def _cparams(sem):
    return pltpu.CompilerParams(dimension_semantics=sem, vmem_limit_bytes=VMEM_LIMIT_BYTES)


def _const_spec(shape):
    nd = len(shape)
    return pl.BlockSpec(shape, lambda *_: (0,) * nd)


def _layer_norm(y, g, b):
    mu = jnp.mean(y, axis=-1, keepdims=True)
    d = y - mu
    var = jnp.mean(d * d, axis=-1, keepdims=True)
    return d * lax.rsqrt(var + LN_EPS) * g + b


QKV_COLS = 3 * SB_WIDTH
BRANCH_COLS = 2 * LRU_WIDTH + 3 * SC_WIDTH


def _sigmoid(x):
    return 0.5 * jnp.tanh(0.5 * x) + 0.5


def _gelu_tanh(x):
    return 0.5 * x * (1.0 + jnp.tanh(math.sqrt(2.0 / math.pi) * (x + 0.044715 * (x * x * x))))


def _project(h, w_ref, gate_b_ref, qkv_ref, conv_ref, act_ref, after_chunk=lambda k: None):
    hb = h.astype(BF16)
    c = PROJ_CHUNK

    def proj(chunk):
        out = jnp.dot(hb, w_ref[:, chunk * c:(chunk + 1) * c], preferred_element_type=F32)
        after_chunk(chunk)
        return out

    n_qkv = QKV_COLS // c
    for j in range(n_qkv):
        acc = proj(j)
        if j * c < SB_WIDTH:
            acc = acc * (SB_HEAD_DIM ** -0.5 * LOG2_E)
        qkv_ref[:, j * c:(j + 1) * c] = acc.astype(BF16)
    lru_in, lru_gate, sc_b, sc_c, sc_h = range(n_qkv, n_qkv + 5)
    conv_ref[:, 0:c] = proj(lru_in)
    conv_ref[:, c:2 * c] = proj(sc_b)
    conv_ref[:, 2 * c:3 * c] = proj(sc_c) * proj(sc_h)
    for j in range(gate_b_ref.shape[1] // c):
        acc = proj(sc_h + 1 + j) + gate_b_ref[:, j * c:(j + 1) * c]
        act_ref[:, j * c:(j + 1) * c] = _sigmoid(acc).astype(BF16)
    act_ref[:, gate_b_ref.shape[1]:] = _gelu_tanh(proj(lru_gate)).astype(BF16)


def _inproj_kernel(h_ref, w_ref, gate_b_ref, qkv_ref, conv_ref, act_ref):
    _project(h_ref[...], w_ref, gate_b_ref, qkv_ref, conv_ref, act_ref)


def _ln_inproj_kernel(x_ref, g_ref, b_ref, w_ref, gate_b_ref, h_ref, qkv_ref, conv_ref, act_ref):
    h = _layer_norm(x_ref[...], g_ref[...], b_ref[...])
    h_ref[...] = h
    _project(h, w_ref, gate_b_ref, qkv_ref, conv_ref, act_ref)


def _gather_inproj_kernel(slot_ref, slot_next_ref, ys_ref, w_ref, gate_b_ref,
                          h_ref, qkv_ref, conv_ref, act_ref, rows, sems):
    i = pl.program_id(0)
    last = pl.num_programs(0) - 1
    p = i % 2

    def fetch(slots, buf):
        def group(g, c):
            for j in range(SUBLANES):
                pltpu.make_async_copy(ys_ref.at[pl.ds(slots[0, 0, g * SUBLANES + j], 1)],
                                      rows.at[buf, g, pl.ds(j, 1)],
                                      sems.at[buf]).start(priority=j % 2)
            return c

        lax.fori_loop(0, PROJ_ROWS // SUBLANES, group, 0)

    @pl.when(i == 0)
    def _():
        fetch(slot_ref, 0)

    _for_rows(PROJ_ROWS, lambda r, q: pltpu.make_async_copy(
        ys_ref.at[pl.ds(0, 1)], rows.at[p, 0, pl.ds(0, 1)], sems.at[p]).wait())
    h = rows[p].reshape(PROJ_ROWS, rows.shape[-1])
    h_ref[...] = h

    n_groups = PROJ_ROWS // SUBLANES
    n_chunks = w_ref.shape[1] // PROJ_CHUNK

    def start_some(k):
        for g in range(k * n_groups // n_chunks, (k + 1) * n_groups // n_chunks):
            for j in range(SUBLANES):
                pltpu.make_async_copy(
                    ys_ref.at[pl.ds(slot_next_ref[0, 0, g * SUBLANES + j], 1)],
                    rows.at[1 - p, g, pl.ds(j, 1)], sems.at[1 - p]).start(priority=j % 2)

    _project(h, w_ref, gate_b_ref, qkv_ref, conv_ref, act_ref, after_chunk=start_some)

    @pl.when(i == last)
    def _():
        _for_rows(PROJ_ROWS, lambda r, q: pltpu.make_async_copy(
            ys_ref.at[pl.ds(0, 1)], rows.at[1 - p, 0, pl.ds(0, 1)], sems.at[1 - p]).wait())


def _inproj(h2, w_in_b, layer, gate_b, entry_ln=None, slots=None):
    d = h2.shape[1]
    n = h2.shape[0] if slots is None else slots.size
    cols = w_in_b.shape[2]
    gate_cols = gate_b.size
    assert LRU_WIDTH == SC_WIDTH == PROJ_CHUNK and cols == QKV_COLS + BRANCH_COLS + gate_cols
    widths = (QKV_COLS, 3 * PROJ_CHUNK, gate_cols + LRU_WIDTH)
    dtypes = (BF16, F32, BF16)
    rows = pl.BlockSpec((PROJ_ROWS, d), lambda i: (i, 0))
    weight_specs = [pl.BlockSpec((None, d, cols), lambda i: (layer, 0, 0),
                                 pipeline_mode=pl.Buffered(1)),
                    _const_spec((1, gate_cols))]
    weights = (w_in_b, gate_b.reshape(1, gate_cols).astype(F32))
    out_shape = [jax.ShapeDtypeStruct((n, w), t) for w, t in zip(widths, dtypes)]
    out_specs = [pl.BlockSpec((PROJ_ROWS, w), lambda i: (i, 0)) for w in widths]
    if slots is not None:
        assert DISPATCH_ROWS == PROJ_ROWS
        steps = n // PROJ_ROWS
        slot_spec = lambda index: pl.BlockSpec((1, 1, PROJ_ROWS), index, memory_space=pltpu.SMEM)
        out_shape.insert(0, jax.ShapeDtypeStruct((n, d), F32))
        out_specs.insert(0, rows)
        return pl.pallas_call(
            _gather_inproj_kernel,
            out_shape=tuple(out_shape),
            grid=(steps,),
            in_specs=[slot_spec(lambda i: (i, 0, 0)),
                      slot_spec(lambda i: (jnp.minimum(i + 1, steps - 1), 0, 0)),
                      pl.BlockSpec(memory_space=pl.ANY)] + weight_specs,
            out_specs=tuple(out_specs),
            scratch_shapes=[pltpu.VMEM((2, PROJ_ROWS // SUBLANES, SUBLANES, d), F32),
                            pltpu.SemaphoreType.DMA((2,))],
            compiler_params=_cparams(("arbitrary",)),
            name="in_proj_gather",
        )(slots, slots, h2, *weights)
    if entry_ln is None:
        body, ln_specs, ln_args = _inproj_kernel, [], ()
    else:
        body = _ln_inproj_kernel
        ln_specs = [_const_spec((1, d)), _const_spec((1, d))]
        ln_args = tuple(v.reshape(1, d).astype(F32) for v in entry_ln)
        out_shape.insert(0, jax.ShapeDtypeStruct((n, d), F32))
        out_specs.insert(0, rows)
    return pl.pallas_call(
        body,
        out_shape=tuple(out_shape),
        grid=(n // PROJ_ROWS,),
        in_specs=[rows] + ln_specs + weight_specs,
        out_specs=tuple(out_specs),
        compiler_params=_cparams(("parallel",)),
        name="in_proj",
    )(h2, *ln_args, *weights)


LOG2_E = math.log2(math.e)
PASS_LOG2_FLOOR = -151.0


HEADS_PER_STEP = LANES // SB_HEAD_DIM


def _per_head(x):
    lane = lax.broadcasted_iota(jnp.int32, x.shape, 1)
    return jnp.concatenate(
        [jnp.where((lane >= h * SB_HEAD_DIM) & (lane < (h + 1) * SB_HEAD_DIM), x, jnp.zeros_like(x))
         for h in range(HEADS_PER_STEP)], axis=0)


MASKED_SCORE = -1e30


def _neg_abs(x):
    bits = lax.bitcast_convert_type(x, jnp.uint32) | jnp.uint32(0x80000000)
    return lax.bitcast_convert_type(bits, F32)


def _sb_block(q, k_j, v_j, neg_tri2, carries, mask):
    nq, kk = q.shape[0], k_j.shape[0]
    z = lax.dot_general(q, _per_head(k_j), (((1,), (1,)), ((), ())), preferred_element_type=F32)
    scores, split = [], []
    for h in range(HEADS_PER_STEP):
        z_h = z[:, h * kk:(h + 1) * kk]
        if mask is not None:
            z_h = jnp.where(mask, z_h, MASKED_SCORE)
        drop = jnp.maximum(z_h, 0.0) + jnp.log2(1.0 + jnp.exp2(_neg_abs(z_h)))
        scores.append(z_h)
        split.append(drop.astype(BF16))
    log_pass = jnp.dot(jnp.concatenate(split, axis=0), neg_tri2, preferred_element_type=F32)
    w, new_carries = [], []
    for h in range(HEADS_PER_STEP):
        lp_h = log_pass[h * nq:(h + 1) * nq]
        w.append(jnp.exp2(scores[h] + lp_h + carries[h]).astype(BF16))
        new_carries.append(carries[h] + lp_h[:, 0:1])
    pv = jnp.dot(jnp.concatenate(w, axis=1), _per_head(v_j), preferred_element_type=F32)
    return pv, new_carries


def _attn_kernel(q_ref, k_ref, v_ref, u2_ref, o_ref):
    qi = pl.program_id(2)
    u2 = u2_ref[...]
    row = lax.broadcasted_iota(jnp.int32, (ATT_Q, ATT_K), 0)
    col = lax.broadcasted_iota(jnp.int32, (ATT_Q, ATT_K), 1)
    diag_mask = col < row
    blocks = [slice(g * LANES, (g + 1) * LANES) for g in range(ATT_LANE_BLOCKS)]
    q = [q_ref[0, :, lanes] for lanes in blocks]

    def sweep(j, accs, carries, mask):
        rows = pl.ds(pl.multiple_of(j * ATT_K, ATT_K), ATT_K)
        out = [_sb_block(q_g, k_ref[0, rows, lanes], v_ref[0, rows, lanes], u2, c_g, mask)
               for q_g, lanes, c_g in zip(q, blocks, carries)]
        return [acc + pv for acc, (pv, _) in zip(accs, out)], [c for _, c in out]

    zero = [[jnp.zeros((ATT_Q, 1), F32) for _ in range(HEADS_PER_STEP)] for _ in blocks]
    accs, carries = sweep(qi, [jnp.zeros((ATT_Q, LANES), F32) for _ in blocks], zero, diag_mask)
    no_prev = jnp.where(qi > 0, 0.0, -1e30)
    accs, carries = sweep(jnp.maximum(qi - 1, 0), accs,
                          [[c + no_prev for c in c_g] for c_g in carries], None)

    def largest(carries):
        return jnp.max(functools.reduce(jnp.maximum, [c for c_g in carries for c in c_g]))

    def cond(state):
        j, _, _, live = state
        return jnp.logical_and(j >= 0, live > PASS_LOG2_FLOOR)

    def body(state):
        j, accs, carries, _ = state
        accs, carries = sweep(j, accs, carries, None)
        return j - 1, accs, carries, largest(carries)

    _, accs, _, _ = lax.while_loop(cond, body, (qi - 2, accs, carries, largest(carries)))
    o_ref[0] = jnp.concatenate(accs, axis=1).astype(o_ref.dtype)


def _sb_attention(qkv3):
    b, s, _ = qkv3.shape
    width = ATT_LANE_BLOCKS * LANES
    groups = SB_WIDTH // width
    neg_tri = -(jnp.arange(ATT_K)[:, None] >= jnp.arange(ATT_K)[None, :]).astype(BF16)
    u2 = neg_tri
    return pl.pallas_call(
        _attn_kernel,
        out_shape=jax.ShapeDtypeStruct((b, s, SB_WIDTH), BF16),
        grid=(b, groups, s // ATT_Q),
        in_specs=[pl.BlockSpec((1, ATT_Q, width), lambda bi, p, qi: (bi, qi, p)),
                  pl.BlockSpec((1, s, width), lambda bi, p, qi: (bi, 0, groups + p)),
                  pl.BlockSpec((1, s, width), lambda bi, p, qi: (bi, 0, 2 * groups + p)),
                  _const_spec((ATT_K, ATT_K))],
        out_specs=pl.BlockSpec((1, ATT_Q, width), lambda bi, p, qi: (bi, qi, p)),
        compiler_params=_cparams(("parallel", "parallel", "arbitrary")),
        name="sb_attention",
    )(qkv3, qkv3, qkv3, u2)


def _causal_conv(x, stage, w):
    t = x.shape[0]
    k = w.shape[0]
    stage[SUBLANES:, :] = x
    out = w[k - 1:k] * x
    for j in range(1, k):
        out = out + w[k - 1 - j:k - j] * stage[pl.ds(SUBLANES - j, t), :]
    stage[:SUBLANES, :] = x[t - SUBLANES:]
    return out


def _linear_scan(a, b, h0):
    t = a.shape[0]
    sub = lax.broadcasted_iota(jnp.int32, a.shape, 0) % SUBLANES
    d = 1
    while d < SUBLANES:
        keep = sub >= d
        a_prev = jnp.where(keep, pltpu.roll(a, d, axis=0), 1.0)
        b_prev = jnp.where(keep, pltpu.roll(b, d, axis=0), 0.0)
        b = a * b_prev + b
        a = a * a_prev
        d *= 2
    h = h0
    groups = []
    for g in range(t // SUBLANES):
        rows = slice(g * SUBLANES, (g + 1) * SUBLANES)
        seg = b[rows] + a[rows] * h
        h = seg[SUBLANES - 1:]
        groups.append(seg)
    return jnp.concatenate(groups, axis=0)


def _mix_kernel(h_ref, ysb_ref, lin_ref, scb_ref, scp_ref, gsb_ref, glru_ref, gsc_ref, gelu_ref,
                lconv_w_ref, lconv_b_ref, wa_ref, ba_ref, wx_ref, bx_ref, lam_ref, scw_ref,
                wsb_ref, wlru_ref, wsc_ref, wout_ref, ln_g_ref, ln_b_ref,
                out_ref, lin_stage, sc_stage, h_state):
    t = MIX_ROWS

    @pl.when(pl.program_id(1) == 0)
    def _():
        lin_stage[:SUBLANES, :] = jnp.zeros((SUBLANES, LRU_WIDTH), F32)
        sc_stage[:SUBLANES, :] = jnp.zeros((SUBLANES, SC_WIDTH), F32)
        h_state[...] = jnp.zeros_like(h_state)

    u = _causal_conv(lin_ref[0], lin_stage, lconv_w_ref[...]) + lconv_b_ref[...]
    ub = u.astype(BF16)
    r = _sigmoid(jnp.dot(ub, wa_ref[...], preferred_element_type=F32) + ba_ref[...])
    i = _sigmoid(jnp.dot(ub, wx_ref[...], preferred_element_type=F32) + bx_ref[...])
    lam = lam_ref[...]
    softplus_neg_lam = jnp.maximum(-lam, 0.0) + jnp.log(1.0 + jnp.exp(-jnp.abs(lam)))
    log_a = (-LRU_C) * r * softplus_neg_lam
    a = jnp.exp(log_a)
    gap = 1.0 - a * a
    drive = jnp.where(gap > 0.0, gap * lax.rsqrt(gap), 0.0) * (i * u)
    hs = _linear_scan(a, drive, h_state[...])
    h_state[...] = hs[t - 1:]
    y_lru = gelu_ref[0] * hs.astype(BF16)

    y_sc = (scb_ref[0] * _causal_conv(scp_ref[0], sc_stage, scw_ref[...])).astype(BF16)

    def branch(y, w_ref):
        return jnp.dot(y, w_ref[...], preferred_element_type=F32).astype(BF16)

    merged = (gsb_ref[0] * branch(ysb_ref[0], wsb_ref) + glru_ref[0] * branch(y_lru, wlru_ref)
              + gsc_ref[0] * branch(y_sc, wsc_ref))
    mix = jnp.dot(merged, wout_ref[...], preferred_element_type=F32)
    out_ref[0] = _layer_norm(ALPHA * h_ref[0] + mix, ln_g_ref[...], ln_b_ref[...])


def _block_diag(w):
    hh, ii, jj = w.shape
    eye = jnp.eye(hh, dtype=w.dtype)
    return (eye[:, None, :, None] * w[:, :, None, :]).reshape(hh * ii, hh * jj)


def _mixer_tail(h3, ysb3, conv3, act3, p, l):
    b, s, d = h3.shape
    t = MIX_ROWS
    assert LRU_WIDTH == SC_WIDTH

    def row_spec(width, col):
        return pl.BlockSpec((1, t, width), lambda bi, si: (bi, si, col))

    vec = lambda v: v.reshape(1, -1).astype(F32)
    in_specs = [row_spec(d, 0), row_spec(SB_WIDTH, 0)]
    in_specs += [row_spec(LRU_WIDTH, c) for c in range(3)]
    in_specs += [row_spec(d, g) for g in range(3)]
    in_specs += [row_spec(LRU_WIDTH, 3 * d // LRU_WIDTH)]
    weights = [
        p["lru_conv_w"][l].astype(F32), vec(p["lru_conv_b"][l]),
        _block_diag(p["lru_wa"][l]).astype(BF16), vec(p["lru_ba"][l]),
        _block_diag(p["lru_wx"][l]).astype(BF16), vec(p["lru_bx"][l]),
        vec(p["lru_lambda"][l]), p["sc_conv_w"][l].astype(F32),
        p["w_branch_sb"][l].astype(BF16), p["w_branch_lru"][l].astype(BF16),
        p["w_branch_sc"][l].astype(BF16), p["w_out"][l].astype(BF16),
        vec(p["ln1_g"][l]), vec(p["ln1_b"][l]),
    ]
    in_specs += [_const_spec(w.shape) for w in weights]
    return pl.pallas_call(
        _mix_kernel,
        out_shape=jax.ShapeDtypeStruct((b, s, d), F32),
        grid=(b, s // t),
        in_specs=in_specs,
        out_specs=pl.BlockSpec((1, t, d), lambda bi, si: (bi, si, 0)),
        scratch_shapes=[pltpu.VMEM((SUBLANES + t, LRU_WIDTH), F32),
                        pltpu.VMEM((SUBLANES + t, SC_WIDTH), F32),
                        pltpu.VMEM((1, LRU_WIDTH), F32)],
        compiler_params=_cparams(("parallel", "arbitrary")),
        name="mixer_tail",
    )(h3, ysb3, *([conv3] * 3), *([act3] * 4), *weights)


def _first_argmax(vals, row):
    m = jnp.max(vals, axis=0, keepdims=True)
    idx = jnp.min(jnp.where(vals == m, row, vals.shape[0]), axis=0, keepdims=True)
    return m, idx


def _route_kernel(h_ref, w_ref, bias_ref, before_ref, hx_ref, key_ref, counts_ref, running):
    @pl.when(pl.program_id(0) == 0)
    def _():
        running[...] = jnp.zeros_like(running)

    h = h_ref[...]
    h_hi = h.astype(BF16)
    h_lo = (h - h_hi.astype(F32)).astype(BF16)
    w = w_ref[...]
    w_hi = w.astype(BF16)
    w_lo = (w - w_hi.astype(F32)).astype(BF16)
    logits = jnp.dot(jnp.concatenate([h_hi, h_lo, h_hi], axis=1),
                     jnp.concatenate([w_hi, w_hi, w_lo], axis=0), preferred_element_type=F32)
    lt = logits.T
    neg = -jnp.inf
    experts = lt[:N_EXPERTS]
    groups = lt[N_EXPERTS:N_EXPERTS + SUBLANES]
    experts_b = experts + bias_ref[:N_EXPERTS]
    groups_b = groups + bias_ref[N_EXPERTS:N_EXPERTS + SUBLANES]
    row_g = lax.broadcasted_iota(jnp.int32, groups.shape, 0)
    row_e = lax.broadcasted_iota(jnp.int32, experts.shape, 0)
    is_group = row_g < N_GROUPS
    g_max = jnp.max(jnp.where(is_group, groups, neg), axis=0, keepdims=True)
    g_exp = jnp.where(is_group, jnp.exp(groups - g_max), 0.0)
    g_den = jnp.sum(g_exp, axis=0, keepdims=True)
    _, g_sel = _first_argmax(jnp.where(is_group, groups_b, neg), row_g)
    g_prob = jnp.sum(jnp.where(row_g == g_sel, g_exp, 0.0), axis=0, keepdims=True) / g_den
    cand = jnp.where((row_e >> 2) == g_sel, experts_b, neg)
    _, e1 = _first_argmax(cand, row_e)
    _, e2 = _first_argmax(jnp.where(row_e == e1, neg, cand), row_e)
    l1 = jnp.sum(jnp.where(row_e == e1, experts, 0.0), axis=0, keepdims=True)
    l2 = jnp.sum(jnp.where(row_e == e2, experts, 0.0), axis=0, keepdims=True)
    m = jnp.maximum(l1, l2)
    x1 = jnp.exp(l1 - m)
    x2 = jnp.exp(l2 - m)
    scale = g_prob / (x1 + x2)
    w1 = x1 * scale
    w2 = x2 * scale
    first_low = e1 < e2
    a = jnp.where(first_low, e1, e2) - g_sel * EXP_PER_GROUP
    b = jnp.where(first_low, e2, e1) - g_sel * EXP_PER_GROUP
    cls = g_sel * PAIRS_PER_GROUP + ((a * (2 * EXP_PER_GROUP - 1 - a)) >> 1) + (b - a - 1)
    row_c = lax.broadcasted_iota(jnp.int32, (CLASS_ROWS, cls.shape[1]), 0)
    onehot = row_c == cls
    ones = jnp.where(onehot, 1.0, 0.0)
    earlier = jnp.dot(ones, before_ref[...], preferred_element_type=F32) + running[...]
    rank = jnp.sum(jnp.where(onehot, earlier, 0.0), axis=0, keepdims=True)
    running[...] += jnp.sum(ones, axis=1, keepdims=True)
    counts_ref[...] = running[...]
    key_ref[0] = (cls << KEY_SHIFT) + rank.astype(jnp.int32)
    row_w = lax.broadcasted_iota(jnp.int32, (LANES, cls.shape[1]), 0)
    w_rows = jnp.where(row_w == INFO_W_LOW, jnp.where(first_low, w1, w2), 0.0)
    w_rows = jnp.where(row_w == INFO_W_HIGH, jnp.where(first_low, w2, w1), w_rows)
    hx_ref[:, :h.shape[1]] = h
    hx_ref[:, h.shape[1]:] = w_rows.T


def _route(h2, w_router, bias):
    n, d = h2.shape
    t = ROUTE_ROWS
    before = (jnp.arange(t)[:, None] < jnp.arange(t)[None, :]).astype(F32)
    return pl.pallas_call(
        _route_kernel,
        out_shape=(jax.ShapeDtypeStruct((n, d + LANES), F32),
                   jax.ShapeDtypeStruct((n // t, 1, t), jnp.int32),
                   jax.ShapeDtypeStruct((CLASS_ROWS, 1), F32)),
        grid=(n // t,),
        in_specs=[pl.BlockSpec((t, d), lambda i: (i, 0)),
                  _const_spec((d, LANES)), _const_spec((LANES, 1)), _const_spec((t, t))],
        out_specs=(pl.BlockSpec((t, d + LANES), lambda i: (i, 0)),
                   pl.BlockSpec((1, 1, t), lambda i: (i, 0, 0)),
                   _const_spec((CLASS_ROWS, 1))),
        scratch_shapes=[pltpu.VMEM((CLASS_ROWS, 1), F32)],
        compiler_params=_cparams(("arbitrary",)),
        name="moe_route",
    )(h2, w_router, bias, before)


def _slot_kernel(start_ref, key_ref, slot_ref):
    key = key_ref[...]
    cls = key >> KEY_SHIFT
    slot = key & ((1 << KEY_SHIFT) - 1)
    for c in range(N_CLASSES):
        slot = slot + jnp.where(cls == c, start_ref[c], 0)
    slot_ref[...] = slot


def _slots(class_start, key3):
    g, _, t = key3.shape
    return pl.pallas_call(
        _slot_kernel,
        out_shape=jax.ShapeDtypeStruct(key3.shape, jnp.int32),
        grid_spec=pltpu.PrefetchScalarGridSpec(
            num_scalar_prefetch=1, grid=(1,),
            in_specs=[pl.BlockSpec((g, 1, t), lambda i, s: (0, 0, 0))],
            out_specs=pl.BlockSpec((g, 1, t), lambda i, s: (0, 0, 0))),
        compiler_params=_cparams(("arbitrary",)),
        name="moe_slots",
    )(class_start, key3)


def _for_rows(n_rows, fn):
    def body(g, c):
        for j in range(ROW_COPY_UNROLL):
            fn(g * ROW_COPY_UNROLL + j, j % 2)
        return c

    lax.fori_loop(0, n_rows // ROW_COPY_UNROLL, body, 0)


def _token_of_slot_kernel(counts_ref, start_ref, slot_ref, tos_ref):
    n_slots = tos_ref.shape[0]
    n = slot_ref.shape[0]

    def clear(lo, hi):
        def one(s, c):
            tos_ref[s] = 0
            return c
        lax.fori_loop(lo, hi, one, 0)

    end = 0
    for c in range(N_CLASSES):
        used = start_ref[c] + counts_ref[c]
        end = start_ref[c] + ((counts_ref[c] + EXPERT_ROWS - 1) // EXPERT_ROWS) * EXPERT_ROWS
        clear(used, end)
    clear(end, n_slots)

    def token(t, c):
        tos_ref[slot_ref[t]] = t
        return c

    lax.fori_loop(0, n, token, 0, unroll=ROW_COPY_UNROLL)


def _token_of_slot(counts, class_start, slot3, n_tiles):
    n_slots = n_tiles * EXPERT_ROWS
    tos = pl.pallas_call(
        _token_of_slot_kernel,
        out_shape=jax.ShapeDtypeStruct((n_slots,), jnp.int32),
        grid_spec=pltpu.PrefetchScalarGridSpec(
            num_scalar_prefetch=2, grid=(1,),
            in_specs=[pl.BlockSpec(memory_space=pltpu.SMEM)],
            out_specs=pl.BlockSpec(memory_space=pltpu.SMEM)),
        name="moe_token_of_slot",
    )(counts, class_start, slot3.reshape(-1))
    return tos.reshape(n_tiles, 1, EXPERT_ROWS)


def _expert_kernel(lo_ref, hi_ref, tos_ref, tos_next_ref, hx_ref,
                   wg_lo, wg_hi, wu_lo, wu_hi, wd_lo, wd_hi, ln_g_ref, ln_b_ref,
                   ys_ref, rows, sems):
    del lo_ref, hi_ref
    i = pl.program_id(0)
    last = pl.num_programs(0) - 1
    p = i % 2
    t = EXPERT_ROWS
    n_groups = t // SUBLANES

    def start_group(tos, buf, g):
        for j in range(SUBLANES):
            pltpu.make_async_copy(hx_ref.at[pl.ds(tos[0, 0, g * SUBLANES + j], 1)],
                                  rows.at[buf, g, pl.ds(j, 1)],
                                  sems.at[buf]).start(priority=j % 2)

    def wait_tile(buf):
        _for_rows(t, lambda r, q: pltpu.make_async_copy(
            hx_ref.at[pl.ds(0, 1)], rows.at[buf, 0, pl.ds(0, 1)], sems.at[buf]).wait())

    @pl.when(i == 0)
    def _():
        def group(g, c):
            start_group(tos_ref, 0, g)
            return c
        lax.fori_loop(0, n_groups, group, 0)

    wait_tile(p)
    d = ys_ref.shape[1]
    x = rows[p, :, :, :d].reshape(t, d)
    info = rows[p, :, :, d:].reshape(t, LANES)
    xb = x.astype(BF16)

    n_stages = 6
    stage = [0]

    def dot(a, w):
        out = jnp.dot(a, w, preferred_element_type=F32)
        k = stage[0]
        for g in range(k * n_groups // n_stages, (k + 1) * n_groups // n_stages):
            start_group(tos_next_ref, 1 - p, g)
        stage[0] = k + 1
        return out

    y = None
    for wg, wu, wd, lane in ((wg_lo, wu_lo, wd_lo, INFO_W_LOW), (wg_hi, wu_hi, wd_hi, INFO_W_HIGH)):
        gate = dot(xb, wg[0])
        up = dot(xb, wu[0])
        act = gate * _sigmoid(gate) * up * info[:, lane:lane + 1]
        part = dot(act.astype(BF16), wd[0])
        y = part if y is None else y + part
    assert stage[0] == n_stages
    ys_ref[...] = _layer_norm(ALPHA * x + y, ln_g_ref[...], ln_b_ref[...])

    @pl.when(i == last)
    def _():
        wait_tile(1 - p)


def _experts(tile_lo, tile_hi, tos3, hx, layer, w_gate_b, w_up_b, w_down_b, ln_g, ln_b):
    d = hx.shape[1] - LANES
    n_tiles = tos3.shape[0]
    f = w_gate_b.shape[3]
    t = EXPERT_ROWS
    lo_map = lambda i, lo, hi: (layer, lo[i], 0, 0)
    hi_map = lambda i, lo, hi: (layer, hi[i], 0, 0)
    const = lambda i, *_: (0, 0)
    tos_spec = lambda index: pl.BlockSpec((1, 1, t), index, memory_space=pltpu.SMEM)
    grid_spec = pltpu.PrefetchScalarGridSpec(
        num_scalar_prefetch=2,
        grid=(n_tiles,),
        in_specs=[tos_spec(lambda i, *_: (i, 0, 0)),
                  tos_spec(lambda i, *_: (jnp.minimum(i + 1, n_tiles - 1), 0, 0)),
                  pl.BlockSpec(memory_space=pl.ANY),
                  pl.BlockSpec((None, 1, d, f), lo_map), pl.BlockSpec((None, 1, d, f), hi_map),
                  pl.BlockSpec((None, 1, d, f), lo_map), pl.BlockSpec((None, 1, d, f), hi_map),
                  pl.BlockSpec((None, 1, f, d), lo_map), pl.BlockSpec((None, 1, f, d), hi_map),
                  pl.BlockSpec((1, d), const), pl.BlockSpec((1, d), const)],
        out_specs=pl.BlockSpec((t, d), lambda i, *_: (i, 0)),
        scratch_shapes=[pltpu.VMEM((2, t // SUBLANES, SUBLANES, d + LANES), F32),
                        pltpu.SemaphoreType.DMA((2,))],
    )
    return pl.pallas_call(
        _expert_kernel,
        out_shape=jax.ShapeDtypeStruct((n_tiles * t, d), F32),
        grid_spec=grid_spec,
        compiler_params=_cparams(("arbitrary",)),
        name="moe_experts",
    )(tile_lo, tile_hi, tos3, tos3, hx, w_gate_b, w_gate_b, w_up_b, w_up_b,
      w_down_b, w_down_b, ln_g.reshape(1, d), ln_b.reshape(1, d))


def _combine_kernel(slot_ref, ys_ref, o_ref, buf, gsems, osems):
    i = pl.program_id(0)
    last = pl.num_programs(0) - 1
    p = i % 2
    t = DISPATCH_ROWS

    def tile_out(b, step):
        return pltpu.make_async_copy(buf.at[b], o_ref.at[pl.ds(step * t, t)], osems.at[b])

    def row_wait(b):
        pltpu.make_async_copy(ys_ref.at[pl.ds(0, 1)], buf.at[b, pl.ds(0, 1)], gsems.at[b]).wait()

    @pl.when(i >= 2)
    def _():
        tile_out(p, i - 2).wait()

    _for_rows(t, lambda r, q: pltpu.make_async_copy(
        ys_ref.at[pl.ds(slot_ref[0, 0, r], 1)], buf.at[p, pl.ds(r, 1)],
        gsems.at[p]).start(priority=q))

    @pl.when(i > 0)
    def _():
        _for_rows(t, lambda r, q: row_wait(1 - p))
        tile_out(1 - p, i - 1).start()

    @pl.when(i == last)
    def _():
        _for_rows(t, lambda r, q: row_wait(p))
        tile_out(p, i).start()
        tile_out(p, i).wait()

        @pl.when(i > 0)
        def _():
            tile_out(1 - p, i - 1).wait()


def _combine(slot3, ys, n):
    d = ys.shape[1]
    t = DISPATCH_ROWS
    return pl.pallas_call(
        _combine_kernel,
        out_shape=jax.ShapeDtypeStruct((n, d), F32),
        grid=(n // t,),
        in_specs=[pl.BlockSpec((1, 1, t), lambda i: (i, 0, 0), memory_space=pltpu.SMEM),
                  pl.BlockSpec(memory_space=pl.ANY)],
        out_specs=pl.BlockSpec(memory_space=pl.ANY),
        scratch_shapes=[pltpu.VMEM((2, t, d), F32), pltpu.SemaphoreType.DMA((2,)),
                        pltpu.SemaphoreType.DMA((2,))],
        compiler_params=_cparams(("arbitrary",)),
        name="moe_combine",
    )(slot3, ys)


def _pair_tables():
    lo, hi = [], []
    for g in range(N_GROUPS):
        for a in range(EXP_PER_GROUP):
            for b in range(a + 1, EXP_PER_GROUP):
                lo.append(g * EXP_PER_GROUP + a)
                hi.append(g * EXP_PER_GROUP + b)
    return jnp.asarray(lo, jnp.int32), jnp.asarray(hi, jnp.int32)


def _tile_plan(counts, n_tiles):
    t = EXPERT_ROWS
    counts = counts[:N_CLASSES, 0].astype(jnp.int32)
    tiles_per_class = (counts + t - 1) // t
    tile_end = jnp.cumsum(tiles_per_class)
    class_start = (tile_end - tiles_per_class) * t
    tile = jnp.arange(n_tiles, dtype=jnp.int32)
    tile_class = jnp.sum((tile_end[None, :] <= tile[:, None]).astype(jnp.int32), axis=1)
    tile_class = jnp.minimum(tile_class, N_CLASSES - 1)
    pair_lo, pair_hi = _pair_tables()
    return counts, class_start, pair_lo[tile_class], pair_hi[tile_class]


def kernel(x, ln_in_g, ln_in_b, w_in, gate_b, lru_conv_w, lru_conv_b, lru_wa, lru_ba, lru_wx, lru_bx, lru_lambda, sc_conv_w, w_branch_sb, w_branch_lru, w_branch_sc, w_out, ln1_g, ln1_b, w_group, group_bias, w_expert_router, expert_bias, w_gate, w_up, w_down, ln2_g, ln2_b):
    b, s, d = x.shape
    n = b * s
    params = dict(gate_b=gate_b, lru_conv_w=lru_conv_w, lru_conv_b=lru_conv_b, lru_wa=lru_wa,
                  lru_ba=lru_ba, lru_wx=lru_wx, lru_bx=lru_bx, lru_lambda=lru_lambda,
                  sc_conv_w=sc_conv_w, w_branch_sb=w_branch_sb, w_branch_lru=w_branch_lru,
                  w_branch_sc=w_branch_sc, w_out=w_out, ln1_g=ln1_g, ln1_b=ln1_b)
    w_in_b, w_gate_b, w_up_b, w_down_b = (w.astype(BF16) for w in (w_in, w_gate, w_up, w_down))
    h = x.reshape(n, d)
    for l in range(w_in.shape[0]):
        if l == 0:
            h, qkv, conv_in, act = _inproj(h, w_in_b, l, gate_b[l], entry_ln=(ln_in_g, ln_in_b))
        else:
            h, qkv, conv_in, act = _inproj(ys, w_in_b, l, gate_b[l], slots=slot3)
        y_sb = _sb_attention(qkv.reshape(b, s, -1))
        h1 = _mixer_tail(h.reshape(b, s, d), y_sb, conv_in.reshape(b, s, -1),
                         act.reshape(b, s, -1), params, l)
        h1 = h1.reshape(n, d)
        pad = LANES - N_EXPERTS - N_GROUPS
        w_router = jnp.concatenate(
            [w_expert_router[l], w_group[l], jnp.zeros((d, pad), F32)], axis=1).astype(F32)
        bias = jnp.concatenate(
            [expert_bias[l], group_bias[l], jnp.zeros((pad,), F32)]).reshape(LANES, 1).astype(F32)
        hx, key3, counts = _route(h1, w_router, bias)
        n_tiles = n // EXPERT_ROWS + N_CLASSES
        counts, class_start, tile_lo, tile_hi = _tile_plan(counts, n_tiles)
        slot3 = _slots(class_start, key3)
        tos3 = _token_of_slot(counts, class_start, slot3, n_tiles)
        ys = _experts(tile_lo, tile_hi, tos3, hx, l, w_gate_b, w_up_b, w_down_b,
                      ln2_g[l], ln2_b[l])
    return _combine(slot3, ys, n).reshape(b, s, d)
```

```python
import functools
import math

import jax
import jax.numpy as jnp
from jax import lax
from jax.experimental import pallas as pl
from jax.experimental.pallas import tpu as pltpu

F32 = jnp.float32
BF16 = jnp.bfloat16

SB_HEADS = 8
SB_HEAD_DIM = 64
SB_WIDTH = SB_HEADS * SB_HEAD_DIM
LRU_WIDTH = 512
LRU_BLOCKS = 8
LRU_C = 8.0
SC_WIDTH = 512
N_GROUPS = 4
EXP_PER_GROUP = 4
N_EXPERTS = N_GROUPS * EXP_PER_GROUP
D_EXPERT = 512
DEPTH = 2
ALPHA = (2 * DEPTH) ** 0.25
LN_EPS = 1e-5

LANES = 128
SUBLANES = 8
VMEM_LIMIT_BYTES = 56 * 1024 * 1024

PROJ_ROWS = 512
PROJ_CHUNK = 512
ATT_Q = 256
ATT_K = 256
ATT_LANE_BLOCKS = 2
MIX_ROWS = 512
ROUTE_ROWS = 512
DISPATCH_ROWS = 512
EXPERT_ROWS = 256
ROW_COPY_UNROLL = 8

PAIRS_PER_GROUP = EXP_PER_GROUP * (EXP_PER_GROUP - 1) // 2
N_CLASSES = N_GROUPS * PAIRS_PER_GROUP
CLASS_ROWS = 32
INFO_W_LOW, INFO_W_HIGH = 0, 1
KEY_SHIFT = 16


def _cparams(sem):
    return pltpu.CompilerParams(dimension_semantics=sem, vmem_limit_bytes=VMEM_LIMIT_BYTES)


def _const_spec(shape):
    nd = len(shape)
    return pl.BlockSpec(shape, lambda *_: (0,) * nd)


def _layer_norm(y, g, b):
    mu = jnp.mean(y, axis=-1, keepdims=True)
    d = y - mu
    var = jnp.mean(d * d, axis=-1, keepdims=True)
    return d * lax.rsqrt(var + LN_EPS) * g + b


QKV_COLS = 3 * SB_WIDTH
BRANCH_COLS = 2 * LRU_WIDTH + 3 * SC_WIDTH


def _sigmoid(x):
    return 0.5 * jnp.tanh(0.5 * x) + 0.5


def _gelu_tanh(x):
    return 0.5 * x * (1.0 + jnp.tanh(math.sqrt(2.0 / math.pi) * (x + 0.044715 * (x * x * x))))


def _project(h, w_ref, gate_b_ref, qkv_ref, conv_ref, act_ref, after_chunk=lambda k: None):
    hb = h.astype(BF16)
    c = PROJ_CHUNK

    def proj(chunk):
        out = jnp.dot(hb, w_ref[:, chunk * c:(chunk + 1) * c], preferred_element_type=F32)
        after_chunk(chunk)
        return out

    n_qkv = QKV_COLS // c
    for j in range(n_qkv):
        acc = proj(j)
        if j * c < SB_WIDTH:
            acc = acc * (SB_HEAD_DIM ** -0.5 * LOG2_E)
        qkv_ref[:, j * c:(j + 1) * c] = acc.astype(BF16)
    lru_in, lru_gate, sc_b, sc_c, sc_h = range(n_qkv, n_qkv + 5)
    conv_ref[:, 0:c] = proj(lru_in)
    conv_ref[:, c:2 * c] = proj(sc_b)
    conv_ref[:, 2 * c:3 * c] = proj(sc_c) * proj(sc_h)
    for j in range(gate_b_ref.shape[1] // c):
        acc = proj(sc_h + 1 + j) + gate_b_ref[:, j * c:(j + 1) * c]
        act_ref[:, j * c:(j + 1) * c] = _sigmoid(acc).astype(BF16)
    act_ref[:, gate_b_ref.shape[1]:] = _gelu_tanh(proj(lru_gate)).astype(BF16)


def _inproj_kernel(h_ref, w_ref, gate_b_ref, qkv_ref, conv_ref, act_ref):
    _project(h_ref[...], w_ref, gate_b_ref, qkv_ref, conv_ref, act_ref)


def _ln_inproj_kernel(x_ref, g_ref, b_ref, w_ref, gate_b_ref, h_ref, qkv_ref, conv_ref, act_ref):
    h = _layer_norm(x_ref[...], g_ref[...], b_ref[...])
    h_ref[...] = h
    _project(h, w_ref, gate_b_ref, qkv_ref, conv_ref, act_ref)


def _gather_inproj_kernel(slot_ref, slot_next_ref, ys_ref, res_ref, ln_g_ref, ln_b_ref,
                          w_ref, gate_b_ref, h_ref, qkv_ref, conv_ref, act_ref, rows, sems):
    i = pl.program_id(0)
    last = pl.num_programs(0) - 1
    p = i % 2

    def fetch(slots, buf):
        def group(g, c):
            for j in range(SUBLANES):
                pltpu.make_async_copy(ys_ref.at[pl.ds(slots[0, 0, g * SUBLANES + j], 1)],
                                      rows.at[buf, g, pl.ds(j, 1)],
                                      sems.at[buf]).start(priority=j % 2)
            return c

        lax.fori_loop(0, PROJ_ROWS // SUBLANES, group, 0)

    @pl.when(i == 0)
    def _():
        fetch(slot_ref, 0)

    _for_rows(PROJ_ROWS, lambda r, q: pltpu.make_async_copy(
        ys_ref.at[pl.ds(0, 1)], rows.at[p, 0, pl.ds(0, 1)], sems.at[p]).wait())
    h = _layer_norm(ALPHA * res_ref[...] + rows[p].reshape(PROJ_ROWS, rows.shape[-1]),
                    ln_g_ref[...], ln_b_ref[...])
    h_ref[...] = h

    n_groups = PROJ_ROWS // SUBLANES
    n_chunks = w_ref.shape[1] // PROJ_CHUNK

    def start_some(k):
        for g in range(k * n_groups // n_chunks, (k + 1) * n_groups // n_chunks):
            for j in range(SUBLANES):
                pltpu.make_async_copy(
                    ys_ref.at[pl.ds(slot_next_ref[0, 0, g * SUBLANES + j], 1)],
                    rows.at[1 - p, g, pl.ds(j, 1)], sems.at[1 - p]).start(priority=j % 2)

    _project(h, w_ref, gate_b_ref, qkv_ref, conv_ref, act_ref, after_chunk=start_some)

    @pl.when(i == last)
    def _():
        _for_rows(PROJ_ROWS, lambda r, q: pltpu.make_async_copy(
            ys_ref.at[pl.ds(0, 1)], rows.at[1 - p, 0, pl.ds(0, 1)], sems.at[1 - p]).wait())


def _inproj(h2, w_in_b, layer, gate_b, entry_ln=None, slots=None, finish=None):
    d = h2.shape[1]
    n = h2.shape[0] if slots is None else slots.size
    cols = w_in_b.shape[2]
    gate_cols = gate_b.size
    assert LRU_WIDTH == SC_WIDTH == PROJ_CHUNK and cols == QKV_COLS + BRANCH_COLS + gate_cols
    widths = (QKV_COLS, 3 * PROJ_CHUNK, gate_cols + LRU_WIDTH)
    dtypes = (BF16, F32, BF16)
    rows = pl.BlockSpec((PROJ_ROWS, d), lambda i: (i, 0))
    weight_specs = [pl.BlockSpec((None, d, cols), lambda i: (layer, 0, 0),
                                 pipeline_mode=pl.Buffered(1)),
                    _const_spec((1, gate_cols))]
    weights = (w_in_b, gate_b.reshape(1, gate_cols).astype(F32))
    out_shape = [jax.ShapeDtypeStruct((n, w), t) for w, t in zip(widths, dtypes)]
    out_specs = [pl.BlockSpec((PROJ_ROWS, w), lambda i: (i, 0)) for w in widths]
    if slots is not None:
        assert DISPATCH_ROWS == PROJ_ROWS
        steps = n // PROJ_ROWS
        slot_spec = lambda index: pl.BlockSpec((1, 1, PROJ_ROWS), index, memory_space=pltpu.SMEM)
        out_shape.insert(0, jax.ShapeDtypeStruct((n, d), F32))
        out_specs.insert(0, rows)
        return pl.pallas_call(
            _gather_inproj_kernel,
            out_shape=tuple(out_shape),
            grid=(steps,),
            in_specs=[slot_spec(lambda i: (i, 0, 0)),
                      slot_spec(lambda i: (jnp.minimum(i + 1, steps - 1), 0, 0)),
                      pl.BlockSpec(memory_space=pl.ANY), rows,
                      _const_spec((1, d)), _const_spec((1, d))] + weight_specs,
            out_specs=tuple(out_specs),
            scratch_shapes=[pltpu.VMEM((2, PROJ_ROWS // SUBLANES, SUBLANES, d), F32),
                            pltpu.SemaphoreType.DMA((2,))],
            compiler_params=_cparams(("arbitrary",)),
            name="in_proj_gather",
        )(slots, slots, h2, finish[0], finish[1].reshape(1, d).astype(F32),
          finish[2].reshape(1, d).astype(F32), *weights)
    if entry_ln is None:
        body, ln_specs, ln_args = _inproj_kernel, [], ()
    else:
        body = _ln_inproj_kernel
        ln_specs = [_const_spec((1, d)), _const_spec((1, d))]
        ln_args = tuple(v.reshape(1, d).astype(F32) for v in entry_ln)
        out_shape.insert(0, jax.ShapeDtypeStruct((n, d), F32))
        out_specs.insert(0, rows)
    return pl.pallas_call(
        body,
        out_shape=tuple(out_shape),
        grid=(n // PROJ_ROWS,),
        in_specs=[rows] + ln_specs + weight_specs,
        out_specs=tuple(out_specs),
        compiler_params=_cparams(("parallel",)),
        name="in_proj",
    )(h2, *ln_args, *weights)


LOG2_E = math.log2(math.e)
PASS_LOG2_FLOOR = -151.0


HEADS_PER_STEP = LANES // SB_HEAD_DIM


def _per_head(x):
    lane = lax.broadcasted_iota(jnp.int32, x.shape, 1)
    return jnp.concatenate(
        [jnp.where((lane >= h * SB_HEAD_DIM) & (lane < (h + 1) * SB_HEAD_DIM), x, jnp.zeros_like(x))
         for h in range(HEADS_PER_STEP)], axis=0)


MASKED_SCORE = -1e30


def _neg_abs(x):
    bits = lax.bitcast_convert_type(x, jnp.uint32) | jnp.uint32(0x80000000)
    return lax.bitcast_convert_type(bits, F32)


def _sb_block(q, k_j, v_j, neg_tri2, carries, mask):
    nq, kk = q.shape[0], k_j.shape[0]
    z = lax.dot_general(q, _per_head(k_j), (((1,), (1,)), ((), ())), preferred_element_type=F32)
    scores, split = [], []
    for h in range(HEADS_PER_STEP):
        z_h = z[:, h * kk:(h + 1) * kk]
        if mask is not None:
            z_h = jnp.where(mask, z_h, MASKED_SCORE)
        drop = jnp.maximum(z_h, 0.0) + jnp.log2(1.0 + jnp.exp2(_neg_abs(z_h)))
        scores.append(z_h)
        split.append(drop.astype(BF16))
    log_pass = jnp.dot(jnp.concatenate(split, axis=0), neg_tri2, preferred_element_type=F32)
    w, new_carries = [], []
    for h in range(HEADS_PER_STEP):
        lp_h = log_pass[h * nq:(h + 1) * nq]
        w.append(jnp.exp2(scores[h] + lp_h + carries[h]).astype(BF16))
        new_carries.append(carries[h] + lp_h[:, 0:1])
    pv = jnp.dot(jnp.concatenate(w, axis=1), _per_head(v_j), preferred_element_type=F32)
    return pv, new_carries


def _attn_kernel(q_ref, k_ref, v_ref, u2_ref, o_ref):
    qi = pl.program_id(2)
    u2 = u2_ref[...]
    row = lax.broadcasted_iota(jnp.int32, (ATT_Q, ATT_K), 0)
    col = lax.broadcasted_iota(jnp.int32, (ATT_Q, ATT_K), 1)
    diag_mask = col < row
    blocks = [slice(g * LANES, (g + 1) * LANES) for g in range(ATT_LANE_BLOCKS)]
    q = [q_ref[0, :, lanes] for lanes in blocks]

    def sweep(j, accs, carries, mask):
        rows = pl.ds(pl.multiple_of(j * ATT_K, ATT_K), ATT_K)
        out = [_sb_block(q_g, k_ref[0, rows, lanes], v_ref[0, rows, lanes], u2, c_g, mask)
               for q_g, lanes, c_g in zip(q, blocks, carries)]
        return [acc + pv for acc, (pv, _) in zip(accs, out)], [c for _, c in out]

    zero = [[jnp.zeros((ATT_Q, 1), F32) for _ in range(HEADS_PER_STEP)] for _ in blocks]
    accs, carries = sweep(qi, [jnp.zeros((ATT_Q, LANES), F32) for _ in blocks], zero, diag_mask)
    no_prev = jnp.where(qi > 0, 0.0, -1e30)
    accs, carries = sweep(jnp.maximum(qi - 1, 0), accs,
                          [[c + no_prev for c in c_g] for c_g in carries], None)

    def largest(carries):
        return jnp.max(functools.reduce(jnp.maximum, [c for c_g in carries for c in c_g]))

    def cond(state):
        j, _, _, live = state
        return jnp.logical_and(j >= 0, live > PASS_LOG2_FLOOR)

    def body(state):
        j, accs, carries, _ = state
        accs, carries = sweep(j, accs, carries, None)
        return j - 1, accs, carries, largest(carries)

    _, accs, _, _ = lax.while_loop(cond, body, (qi - 2, accs, carries, largest(carries)))
    o_ref[0] = jnp.concatenate(accs, axis=1).astype(o_ref.dtype)


def _sb_attention(qkv3):
    b, s, _ = qkv3.shape
    width = ATT_LANE_BLOCKS * LANES
    groups = SB_WIDTH // width
    neg_tri = -(jnp.arange(ATT_K)[:, None] >= jnp.arange(ATT_K)[None, :]).astype(BF16)
    u2 = neg_tri
    return pl.pallas_call(
        _attn_kernel,
        out_shape=jax.ShapeDtypeStruct((b, s, SB_WIDTH), BF16),
        grid=(b, groups, s // ATT_Q),
        in_specs=[pl.BlockSpec((1, ATT_Q, width), lambda bi, p, qi: (bi, qi, p)),
                  pl.BlockSpec((1, s, width), lambda bi, p, qi: (bi, 0, groups + p)),
                  pl.BlockSpec((1, s, width), lambda bi, p, qi: (bi, 0, 2 * groups + p)),
                  _const_spec((ATT_K, ATT_K))],
        out_specs=pl.BlockSpec((1, ATT_Q, width), lambda bi, p, qi: (bi, qi, p)),
        compiler_params=_cparams(("parallel", "parallel", "arbitrary")),
        name="sb_attention",
    )(qkv3, qkv3, qkv3, u2)


def _causal_conv(x, stage, w):
    t = x.shape[0]
    k = w.shape[0]
    stage[SUBLANES:, :] = x
    out = w[k - 1:k] * x
    for j in range(1, k):
        out = out + w[k - 1 - j:k - j] * stage[pl.ds(SUBLANES - j, t), :]
    stage[:SUBLANES, :] = x[t - SUBLANES:]
    return out


def _linear_scan(a, b, h0):
    t = a.shape[0]
    sub = lax.broadcasted_iota(jnp.int32, a.shape, 0) % SUBLANES
    d = 1
    while d < SUBLANES:
        keep = sub >= d
        a_prev = jnp.where(keep, pltpu.roll(a, d, axis=0), 1.0)
        b_prev = jnp.where(keep, pltpu.roll(b, d, axis=0), 0.0)
        b = a * b_prev + b
        a = a * a_prev
        d *= 2
    h = h0
    groups = []
    for g in range(t // SUBLANES):
        rows = slice(g * SUBLANES, (g + 1) * SUBLANES)
        seg = b[rows] + a[rows] * h
        h = seg[SUBLANES - 1:]
        groups.append(seg)
    return jnp.concatenate(groups, axis=0)


def _mix_kernel(h_ref, ysb_ref, lin_ref, scb_ref, scp_ref, gsb_ref, glru_ref, gsc_ref, gelu_ref,
                lconv_w_ref, lconv_b_ref, wa_ref, ba_ref, wx_ref, bx_ref, lam_ref, scw_ref,
                wsb_ref, wlru_ref, wsc_ref, wout_ref, ln_g_ref, ln_b_ref,
                out_ref, lin_stage, sc_stage, h_state):
    t = MIX_ROWS

    @pl.when(pl.program_id(1) == 0)
    def _():
        lin_stage[:SUBLANES, :] = jnp.zeros((SUBLANES, LRU_WIDTH), F32)
        sc_stage[:SUBLANES, :] = jnp.zeros((SUBLANES, SC_WIDTH), F32)
        h_state[...] = jnp.zeros_like(h_state)

    u = _causal_conv(lin_ref[0], lin_stage, lconv_w_ref[...]) + lconv_b_ref[...]
    ub = u.astype(BF16)
    r = _sigmoid(jnp.dot(ub, wa_ref[...], preferred_element_type=F32) + ba_ref[...])
    i = _sigmoid(jnp.dot(ub, wx_ref[...], preferred_element_type=F32) + bx_ref[...])
    lam = lam_ref[...]
    softplus_neg_lam = jnp.maximum(-lam, 0.0) + jnp.log(1.0 + jnp.exp(-jnp.abs(lam)))
    log_a = (-LRU_C) * r * softplus_neg_lam
    a = jnp.exp(log_a)
    gap = 1.0 - a * a
    drive = jnp.where(gap > 0.0, gap * lax.rsqrt(gap), 0.0) * (i * u)
    hs = _linear_scan(a, drive, h_state[...])
    h_state[...] = hs[t - 1:]
    y_lru = gelu_ref[0] * hs.astype(BF16)

    y_sc = (scb_ref[0] * _causal_conv(scp_ref[0], sc_stage, scw_ref[...])).astype(BF16)

    def branch(y, w_ref):
        return jnp.dot(y, w_ref[...], preferred_element_type=F32).astype(BF16)

    merged = (gsb_ref[0] * branch(ysb_ref[0], wsb_ref) + glru_ref[0] * branch(y_lru, wlru_ref)
              + gsc_ref[0] * branch(y_sc, wsc_ref))
    mix = jnp.dot(merged, wout_ref[...], preferred_element_type=F32)
    out_ref[0] = _layer_norm(ALPHA * h_ref[0] + mix, ln_g_ref[...], ln_b_ref[...])


def _block_diag(w):
    hh, ii, jj = w.shape
    eye = jnp.eye(hh, dtype=w.dtype)
    return (eye[:, None, :, None] * w[:, :, None, :]).reshape(hh * ii, hh * jj)


def _mixer_tail(h3, ysb3, conv3, act3, p, l):
    b, s, d = h3.shape
    t = MIX_ROWS
    assert LRU_WIDTH == SC_WIDTH

    def row_spec(width, col):
        return pl.BlockSpec((1, t, width), lambda bi, si: (bi, si, col))

    vec = lambda v: v.reshape(1, -1).astype(F32)
    in_specs = [row_spec(d, 0), row_spec(SB_WIDTH, 0)]
    in_specs += [row_spec(LRU_WIDTH, c) for c in range(3)]
    in_specs += [row_spec(d, g) for g in range(3)]
    in_specs += [row_spec(LRU_WIDTH, 3 * d // LRU_WIDTH)]
    weights = [
        p["lru_conv_w"][l].astype(F32), vec(p["lru_conv_b"][l]),
        _block_diag(p["lru_wa"][l]).astype(BF16), vec(p["lru_ba"][l]),
        _block_diag(p["lru_wx"][l]).astype(BF16), vec(p["lru_bx"][l]),
        vec(p["lru_lambda"][l]), p["sc_conv_w"][l].astype(F32),
        p["w_branch_sb"][l].astype(BF16), p["w_branch_lru"][l].astype(BF16),
        p["w_branch_sc"][l].astype(BF16), p["w_out"][l].astype(BF16),
        vec(p["ln1_g"][l]), vec(p["ln1_b"][l]),
    ]
    in_specs += [_const_spec(w.shape) for w in weights]
    return pl.pallas_call(
        _mix_kernel,
        out_shape=jax.ShapeDtypeStruct((b, s, d), F32),
        grid=(b, s // t),
        in_specs=in_specs,
        out_specs=pl.BlockSpec((1, t, d), lambda bi, si: (bi, si, 0)),
        scratch_shapes=[pltpu.VMEM((SUBLANES + t, LRU_WIDTH), F32),
                        pltpu.VMEM((SUBLANES + t, SC_WIDTH), F32),
                        pltpu.VMEM((1, LRU_WIDTH), F32)],
        compiler_params=_cparams(("parallel", "arbitrary")),
        name="mixer_tail",
    )(h3, ysb3, *([conv3] * 3), *([act3] * 4), *weights)


def _first_argmax(vals, row):
    m = jnp.max(vals, axis=0, keepdims=True)
    idx = jnp.min(jnp.where(vals == m, row, vals.shape[0]), axis=0, keepdims=True)
    return m, idx


def _route_kernel(h_ref, w_ref, bias_ref, before_ref, info_ref, key_ref, counts_ref, running):
    @pl.when(pl.program_id(0) == 0)
    def _():
        running[...] = jnp.zeros_like(running)

    h = h_ref[...]
    h_hi = h.astype(BF16)
    h_lo = (h - h_hi.astype(F32)).astype(BF16)
    w = w_ref[...]
    w_hi = w.astype(BF16)
    w_lo = (w - w_hi.astype(F32)).astype(BF16)
    logits = jnp.dot(jnp.concatenate([h_hi, h_lo, h_hi], axis=1),
                     jnp.concatenate([w_hi, w_hi, w_lo], axis=0), preferred_element_type=F32)
    lt = logits.T
    neg = -jnp.inf
    experts = lt[:N_EXPERTS]
    groups = lt[N_EXPERTS:N_EXPERTS + SUBLANES]
    experts_b = experts + bias_ref[:N_EXPERTS]
    groups_b = groups + bias_ref[N_EXPERTS:N_EXPERTS + SUBLANES]
    row_g = lax.broadcasted_iota(jnp.int32, groups.shape, 0)
    row_e = lax.broadcasted_iota(jnp.int32, experts.shape, 0)
    is_group = row_g < N_GROUPS
    g_max = jnp.max(jnp.where(is_group, groups, neg), axis=0, keepdims=True)
    g_exp = jnp.where(is_group, jnp.exp(groups - g_max), 0.0)
    g_den = jnp.sum(g_exp, axis=0, keepdims=True)
    _, g_sel = _first_argmax(jnp.where(is_group, groups_b, neg), row_g)
    g_prob = jnp.sum(jnp.where(row_g == g_sel, g_exp, 0.0), axis=0, keepdims=True) / g_den
    cand = jnp.where((row_e >> 2) == g_sel, experts_b, neg)
    _, e1 = _first_argmax(cand, row_e)
    _, e2 = _first_argmax(jnp.where(row_e == e1, neg, cand), row_e)
    l1 = jnp.sum(jnp.where(row_e == e1, experts, 0.0), axis=0, keepdims=True)
    l2 = jnp.sum(jnp.where(row_e == e2, experts, 0.0), axis=0, keepdims=True)
    m = jnp.maximum(l1, l2)
    x1 = jnp.exp(l1 - m)
    x2 = jnp.exp(l2 - m)
    scale = g_prob / (x1 + x2)
    w1 = x1 * scale
    w2 = x2 * scale
    first_low = e1 < e2
    a = jnp.where(first_low, e1, e2) - g_sel * EXP_PER_GROUP
    b = jnp.where(first_low, e2, e1) - g_sel * EXP_PER_GROUP
    cls = g_sel * PAIRS_PER_GROUP + ((a * (2 * EXP_PER_GROUP - 1 - a)) >> 1) + (b - a - 1)
    row_c = lax.broadcasted_iota(jnp.int32, (CLASS_ROWS, cls.shape[1]), 0)
    onehot = row_c == cls
    ones = jnp.where(onehot, 1.0, 0.0)
    earlier = jnp.dot(ones, before_ref[...], preferred_element_type=F32) + running[...]
    rank = jnp.sum(jnp.where(onehot, earlier, 0.0), axis=0, keepdims=True)
    running[...] += jnp.sum(ones, axis=1, keepdims=True)
    counts_ref[...] = running[...]
    key_ref[0] = (cls << KEY_SHIFT) + rank.astype(jnp.int32)
    row_w = lax.broadcasted_iota(jnp.int32, (LANES, cls.shape[1]), 0)
    w_rows = jnp.where(row_w == INFO_W_LOW, jnp.where(first_low, w1, w2), 0.0)
    w_rows = jnp.where(row_w == INFO_W_HIGH, jnp.where(first_low, w2, w1), w_rows)
    info_ref[...] = w_rows.T


def _route(h2, w_router, bias):
    n, d = h2.shape
    t = ROUTE_ROWS
    before = (jnp.arange(t)[:, None] < jnp.arange(t)[None, :]).astype(F32)
    return pl.pallas_call(
        _route_kernel,
        out_shape=(jax.ShapeDtypeStruct((n, LANES), F32),
                   jax.ShapeDtypeStruct((n // t, 1, t), jnp.int32),
                   jax.ShapeDtypeStruct((CLASS_ROWS, 1), F32)),
        grid=(n // t,),
        in_specs=[pl.BlockSpec((t, d), lambda i: (i, 0)),
                  _const_spec((d, LANES)), _const_spec((LANES, 1)), _const_spec((t, t))],
        out_specs=(pl.BlockSpec((t, LANES), lambda i: (i, 0)),
                   pl.BlockSpec((1, 1, t), lambda i: (i, 0, 0)),
                   _const_spec((CLASS_ROWS, 1))),
        scratch_shapes=[pltpu.VMEM((CLASS_ROWS, 1), F32)],
        compiler_params=_cparams(("arbitrary",)),
        name="moe_route",
    )(h2, w_router, bias, before)


def _slot_kernel(start_ref, key_ref, slot_ref):
    key = key_ref[...]
    cls = key >> KEY_SHIFT
    slot = key & ((1 << KEY_SHIFT) - 1)
    for c in range(N_CLASSES):
        slot = slot + jnp.where(cls == c, start_ref[c], 0)
    slot_ref[...] = slot


def _slots(class_start, key3):
    g, _, t = key3.shape
    return pl.pallas_call(
        _slot_kernel,
        out_shape=jax.ShapeDtypeStruct(key3.shape, jnp.int32),
        grid_spec=pltpu.PrefetchScalarGridSpec(
            num_scalar_prefetch=1, grid=(1,),
            in_specs=[pl.BlockSpec((g, 1, t), lambda i, s: (0, 0, 0))],
            out_specs=pl.BlockSpec((g, 1, t), lambda i, s: (0, 0, 0))),
        compiler_params=_cparams(("arbitrary",)),
        name="moe_slots",
    )(class_start, key3)


def _for_rows(n_rows, fn):
    def body(g, c):
        for j in range(ROW_COPY_UNROLL):
            fn(g * ROW_COPY_UNROLL + j, j % 2)
        return c

    lax.fori_loop(0, n_rows // ROW_COPY_UNROLL, body, 0)


def _pack_rows(x, info):
    half = x.shape[1] // 2
    bits = lax.bitcast_convert_type(x.astype(BF16).astype(F32), jnp.uint32)
    words = (bits[:, :half] & jnp.uint32(0xFFFF0000)) | (bits[:, half:] >> 16)
    return jnp.concatenate([words, lax.bitcast_convert_type(info, jnp.uint32)], axis=1)


def _unpack_rows(packed):
    half = packed.shape[1] - LANES
    words = packed[:, :half]
    hi = lax.bitcast_convert_type(words & jnp.uint32(0xFFFF0000), F32)
    lo = lax.bitcast_convert_type(words << 16, F32)
    return (jnp.concatenate([hi, lo], axis=1).astype(BF16),
            lax.bitcast_convert_type(packed[:, half:], F32))


def _dispatch_kernel(tile_end_ref, n_used_ref, slot_ref, h_ref, info_ref, xs_ref, rows, sems, zsem):
    i = pl.program_id(0)
    last = pl.num_programs(0) - 1
    p = i % 2
    t = DISPATCH_ROWS
    te = EXPERT_ROWS

    def row_copy(buf, r):
        return pltpu.make_async_copy(rows.at[buf, pl.ds(r, 1)],
                                     xs_ref.at[pl.ds(slot_ref[0, 0, r], 1)], sems.at[buf])

    @pl.when(i == 0)
    def _():
        rows[1, :te] = jnp.zeros((te, rows.shape[2]), jnp.uint32)

        def zero_tile(tile):
            return pltpu.make_async_copy(rows.at[1, pl.ds(0, te)],
                                         xs_ref.at[pl.ds(tile * te, te)], zsem)

        prev_end = 0
        for c in range(N_CLASSES):
            end = tile_end_ref[c]

            @pl.when(end > prev_end)
            def _(end=end):
                cp = zero_tile(end - 1)
                cp.start()
                cp.wait()
            prev_end = end

        def zero_tail(tile, c):
            cp = zero_tile(tile)
            cp.start()
            cp.wait()
            return c

        lax.fori_loop(n_used_ref[0], xs_ref.shape[0] // te, zero_tail, 0)

    rows[p] = _pack_rows(h_ref[...], info_ref[...])
    _for_rows(t, lambda r, q: row_copy(p, r).start(priority=q))

    def row_wait(buf):
        pltpu.make_async_copy(rows.at[buf, pl.ds(0, 1)], xs_ref.at[pl.ds(0, 1)],
                              sems.at[buf]).wait()

    @pl.when(i > 0)
    def _():
        _for_rows(t, lambda r, q: row_wait(1 - p))

    @pl.when(i == last)
    def _():
        _for_rows(t, lambda r, q: row_wait(p))


def _dispatch(tile_end, n_used, slot3, h2, info, n_slots):
    n, d = h2.shape
    t = DISPATCH_ROWS
    return pl.pallas_call(
        _dispatch_kernel,
        out_shape=jax.ShapeDtypeStruct((n_slots, d // 2 + LANES), jnp.uint32),
        grid_spec=pltpu.PrefetchScalarGridSpec(
            num_scalar_prefetch=2, grid=(n // t,),
            in_specs=[pl.BlockSpec((1, 1, t), lambda i, *_: (i, 0, 0), memory_space=pltpu.SMEM),
                      pl.BlockSpec((t, d), lambda i, *_: (i, 0)),
                      pl.BlockSpec((t, LANES), lambda i, *_: (i, 0))],
            out_specs=pl.BlockSpec(memory_space=pl.ANY),
            scratch_shapes=[pltpu.VMEM((2, t, d // 2 + LANES), jnp.uint32),
                            pltpu.SemaphoreType.DMA((2,)), pltpu.SemaphoreType.DMA(())]),
        compiler_params=_cparams(("arbitrary",)),
        name="moe_dispatch",
    )(tile_end, n_used, slot3, h2, info)


def _expert_kernel(n_used_ref, src_ref, lo_ref, hi_ref, xs_ref, wg_lo, wg_hi, wu_lo, wu_hi,
                   wd_lo, wd_hi, ys_ref):
    del src_ref, lo_ref, hi_ref

    @pl.when(pl.program_id(0) < n_used_ref[0])
    def _():
        xb, info = _unpack_rows(xs_ref[...])
        y = None
        for wg, wu, wd, lane in ((wg_lo, wu_lo, wd_lo, INFO_W_LOW), (wg_hi, wu_hi, wd_hi, INFO_W_HIGH)):
            gate = jnp.dot(xb, wg[0], preferred_element_type=F32)
            up = jnp.dot(xb, wu[0], preferred_element_type=F32)
            act = gate * _sigmoid(gate) * up * info[:, lane:lane + 1]
            part = jnp.dot(act.astype(BF16), wd[0], preferred_element_type=F32)
            y = part if y is None else y + part
        ys_ref[...] = y

    @pl.when(pl.program_id(0) >= n_used_ref[0])
    def _():
        ys_ref[...] = jnp.zeros_like(ys_ref)


def _experts(tile_meta, xs, layer, w_gate_b, w_up_b, w_down_b):
    n_slots, cols = xs.shape
    d, f = w_gate_b.shape[2:]
    t = EXPERT_ROWS
    x_map = lambda i, n_used, src, lo, hi: (src[i], 0)
    lo_map = lambda i, n_used, src, lo, hi: (layer, lo[i], 0, 0)
    hi_map = lambda i, n_used, src, lo, hi: (layer, hi[i], 0, 0)
    grid_spec = pltpu.PrefetchScalarGridSpec(
        num_scalar_prefetch=4,
        grid=(n_slots // t,),
        in_specs=[pl.BlockSpec((t, cols), x_map),
                  pl.BlockSpec((None, 1, d, f), lo_map), pl.BlockSpec((None, 1, d, f), hi_map),
                  pl.BlockSpec((None, 1, d, f), lo_map), pl.BlockSpec((None, 1, d, f), hi_map),
                  pl.BlockSpec((None, 1, f, d), lo_map), pl.BlockSpec((None, 1, f, d), hi_map)],
        out_specs=pl.BlockSpec((t, d), lambda i, *_: (i, 0)),
    )
    return pl.pallas_call(
        _expert_kernel,
        out_shape=jax.ShapeDtypeStruct((n_slots, d), F32),
        grid_spec=grid_spec,
        compiler_params=_cparams(("arbitrary",)),
        name="moe_experts",
    )(*tile_meta, xs, w_gate_b, w_gate_b, w_up_b, w_up_b, w_down_b, w_down_b)


def _combine_kernel(slot_ref, ys_ref, h_ref, ln_g_ref, ln_b_ref, o_ref, buf, sems):
    i = pl.program_id(0)
    tiles = pl.num_programs(0) - 1
    p = i % 2
    t = DISPATCH_ROWS

    @pl.when(i < tiles)
    def _():
        _for_rows(t, lambda r, q: pltpu.make_async_copy(
            ys_ref.at[pl.ds(slot_ref[0, 0, r], 1)], buf.at[p, pl.ds(r, 1)],
            sems.at[p]).start(priority=q))

    @pl.when(i == 0)
    def _():
        o_ref[...] = jnp.zeros_like(o_ref)

    @pl.when(i > 0)
    def _():
        _for_rows(t, lambda r, q: pltpu.make_async_copy(
            ys_ref.at[pl.ds(0, 1)], buf.at[1 - p, pl.ds(0, 1)], sems.at[1 - p]).wait())
        o_ref[...] = _layer_norm(ALPHA * h_ref[...] + buf[1 - p], ln_g_ref[...], ln_b_ref[...])


def _combine(slot3, ys, h2, ln_g, ln_b):
    n, d = h2.shape
    t = DISPATCH_ROWS
    tiles = n // t
    done = lambda i: (jnp.maximum(i - 1, 0), 0)
    return pl.pallas_call(
        _combine_kernel,
        out_shape=jax.ShapeDtypeStruct((n, d), F32),
        grid=(tiles + 1,),
        in_specs=[pl.BlockSpec((1, 1, t), lambda i: (jnp.minimum(i, tiles - 1), 0, 0),
                               memory_space=pltpu.SMEM),
                  pl.BlockSpec(memory_space=pl.ANY),
                  pl.BlockSpec((t, d), done), _const_spec((1, d)), _const_spec((1, d))],
        out_specs=pl.BlockSpec((t, d), done),
        scratch_shapes=[pltpu.VMEM((2, t, d), F32), pltpu.SemaphoreType.DMA((2,))],
        compiler_params=_cparams(("arbitrary",)),
        name="moe_combine",
    )(slot3, ys, h2, ln_g.reshape(1, d).astype(F32), ln_b.reshape(1, d).astype(F32))


def _pair_tables():
    lo, hi = [], []
    for g in range(N_GROUPS):
        for a in range(EXP_PER_GROUP):
            for b in range(a + 1, EXP_PER_GROUP):
                lo.append(g * EXP_PER_GROUP + a)
                hi.append(g * EXP_PER_GROUP + b)
    return jnp.asarray(lo, jnp.int32), jnp.asarray(hi, jnp.int32)


def _tile_plan(counts, n_tiles):
    t = EXPERT_ROWS
    counts = counts[:N_CLASSES, 0].astype(jnp.int32)
    tiles_per_class = (counts + t - 1) // t
    tile_end = jnp.cumsum(tiles_per_class)
    class_start = (tile_end - tiles_per_class) * t
    n_used = tile_end[-1:]
    src = jnp.minimum(jnp.arange(n_tiles, dtype=jnp.int32), n_used - 1)
    tile_class = jnp.sum((tile_end[None, :] <= src[:, None]).astype(jnp.int32), axis=1)
    pair_lo, pair_hi = _pair_tables()
    return class_start, tile_end, (n_used, src, pair_lo[tile_class], pair_hi[tile_class])


def kernel(x, ln_in_g, ln_in_b, w_in, gate_b, lru_conv_w, lru_conv_b, lru_wa, lru_ba, lru_wx, lru_bx, lru_lambda, sc_conv_w, w_branch_sb, w_branch_lru, w_branch_sc, w_out, ln1_g, ln1_b, w_group, group_bias, w_expert_router, expert_bias, w_gate, w_up, w_down, ln2_g, ln2_b):
    b, s, d = x.shape
    n = b * s
    params = dict(gate_b=gate_b, lru_conv_w=lru_conv_w, lru_conv_b=lru_conv_b, lru_wa=lru_wa,
                  lru_ba=lru_ba, lru_wx=lru_wx, lru_bx=lru_bx, lru_lambda=lru_lambda,
                  sc_conv_w=sc_conv_w, w_branch_sb=w_branch_sb, w_branch_lru=w_branch_lru,
                  w_branch_sc=w_branch_sc, w_out=w_out, ln1_g=ln1_g, ln1_b=ln1_b)
    w_in_b, w_gate_b, w_up_b, w_down_b = (w.astype(BF16) for w in (w_in, w_gate, w_up, w_down))
    h = x.reshape(n, d)
    for l in range(w_in.shape[0]):
        if l == 0:
            h, qkv, conv_in, act = _inproj(h, w_in_b, l, gate_b[l], entry_ln=(ln_in_g, ln_in_b))
        else:
            h, qkv, conv_in, act = _inproj(ys, w_in_b, l, gate_b[l], slots=slot3,
                                           finish=(h1, ln2_g[l - 1], ln2_b[l - 1]))
        y_sb = _sb_attention(qkv.reshape(b, s, -1))
        h1 = _mixer_tail(h.reshape(b, s, d), y_sb, conv_in.reshape(b, s, -1),
                         act.reshape(b, s, -1), params, l)
        h1 = h1.reshape(n, d)
        pad = LANES - N_EXPERTS - N_GROUPS
        w_router = jnp.concatenate(
            [w_expert_router[l], w_group[l], jnp.zeros((d, pad), F32)], axis=1).astype(F32)
        bias = jnp.concatenate(
            [expert_bias[l], group_bias[l], jnp.zeros((pad,), F32)]).reshape(LANES, 1).astype(F32)
        info, key3, counts = _route(h1, w_router, bias)
        n_tiles = n // EXPERT_ROWS + N_CLASSES
        class_start, tile_end, tile_meta = _tile_plan(counts, n_tiles)
        slot3 = _slots(class_start, key3)
        xs = _dispatch(tile_end, tile_meta[0], slot3, h1, info, n_tiles * EXPERT_ROWS)
        ys = _experts(tile_meta, xs, l, w_gate_b, w_up_b, w_down_b)
    return _combine(slot3, ys, h1, ln2_g[-1], ln2_b[-1]).reshape(b, s, d)
```

```python
import functools
import math

import jax
import jax.numpy as jnp
from jax import lax
from jax.experimental import pallas as pl
from jax.experimental.pallas import tpu as pltpu

F32 = jnp.float32
BF16 = jnp.bfloat16

SB_HEADS = 8
SB_HEAD_DIM = 64
SB_WIDTH = SB_HEADS * SB_HEAD_DIM
LRU_WIDTH = 512
LRU_BLOCKS = 8
LRU_C = 8.0
SC_WIDTH = 512
N_GROUPS = 4
EXP_PER_GROUP = 4
N_EXPERTS = N_GROUPS * EXP_PER_GROUP
D_EXPERT = 512
DEPTH = 2
ALPHA = (2 * DEPTH) ** 0.25
LN_EPS = 1e-5

LANES = 128
SUBLANES = 8
VMEM_LIMIT_BYTES = 56 * 1024 * 1024

PROJ_ROWS = 512
PROJ_CHUNK = 512
ATT_Q = 256
ATT_K = 256
ATT_LANE_BLOCKS = 2
MIX_ROWS = 512
ROUTE_ROWS = 512
DISPATCH_ROWS = 512
EXPERT_ROWS = 256
ROW_COPY_UNROLL = 8

PAIRS_PER_GROUP = EXP_PER_GROUP * (EXP_PER_GROUP - 1) // 2
N_CLASSES = N_GROUPS * PAIRS_PER_GROUP
CLASS_ROWS = 32
INFO_W_LOW, INFO_W_HIGH = 0, 1
KEY_SHIFT = 16


def _cparams(sem):
    return pltpu.CompilerParams(dimension_semantics=sem, vmem_limit_bytes=VMEM_LIMIT_BYTES)


def _const_spec(shape):
    nd = len(shape)
    return pl.BlockSpec(shape, lambda *_: (0,) * nd)


def _layer_norm(y, g, b):
    mu = jnp.mean(y, axis=-1, keepdims=True)
    d = y - mu
    var = jnp.mean(d * d, axis=-1, keepdims=True)
    return d * lax.rsqrt(var + LN_EPS) * g + b


QKV_COLS = 3 * SB_WIDTH
BRANCH_COLS = 2 * LRU_WIDTH + 3 * SC_WIDTH


def _sigmoid(x):
    return 0.5 * jnp.tanh(0.5 * x) + 0.5


def _gelu_tanh(x):
    return 0.5 * x * (1.0 + jnp.tanh(math.sqrt(2.0 / math.pi) * (x + 0.044715 * (x * x * x))))


def _project(h, w_ref, gate_b_ref, qkv_ref, conv_ref, act_ref, after_chunk=lambda k: None):
    hb = h.astype(BF16)
    c = PROJ_CHUNK

    def proj(chunk):
        out = jnp.dot(hb, w_ref[:, chunk * c:(chunk + 1) * c], preferred_element_type=F32)
        after_chunk(chunk)
        return out

    n_qkv = QKV_COLS // c
    for j in range(n_qkv):
        acc = proj(j)
        if j * c < SB_WIDTH:
            acc = acc * (SB_HEAD_DIM ** -0.5 * LOG2_E)
        qkv_ref[:, j * c:(j + 1) * c] = acc.astype(BF16)
    lru_in, lru_gate, sc_b, sc_c, sc_h = range(n_qkv, n_qkv + 5)
    conv_ref[:, 0:c] = proj(lru_in)
    conv_ref[:, c:2 * c] = proj(sc_b)
    conv_ref[:, 2 * c:3 * c] = proj(sc_c) * proj(sc_h)
    for j in range(gate_b_ref.shape[1] // c):
        acc = proj(sc_h + 1 + j) + gate_b_ref[:, j * c:(j + 1) * c]
        act_ref[:, j * c:(j + 1) * c] = _sigmoid(acc).astype(BF16)
    act_ref[:, gate_b_ref.shape[1]:] = _gelu_tanh(proj(lru_gate)).astype(BF16)


def _inproj_kernel(h_ref, w_ref, gate_b_ref, qkv_ref, conv_ref, act_ref):
    _project(h_ref[...], w_ref, gate_b_ref, qkv_ref, conv_ref, act_ref)


def _ln_inproj_kernel(x_ref, g_ref, b_ref, w_ref, gate_b_ref, h_ref, qkv_ref, conv_ref, act_ref):
    h = _layer_norm(x_ref[...], g_ref[...], b_ref[...])
    h_ref[...] = h
    _project(h, w_ref, gate_b_ref, qkv_ref, conv_ref, act_ref)


def _gather_inproj_kernel(slot_ref, slot_next_ref, ys_ref, res_ref, ln_g_ref, ln_b_ref,
                          w_ref, gate_b_ref, h_ref, qkv_ref, conv_ref, act_ref, rows, sems):
    i = pl.program_id(0)
    last = pl.num_programs(0) - 1
    p = i % 2

    def fetch(slots, buf):
        def group(g, c):
            for j in range(SUBLANES):
                pltpu.make_async_copy(ys_ref.at[pl.ds(slots[0, 0, g * SUBLANES + j], 1)],
                                      rows.at[buf, g, pl.ds(j, 1)],
                                      sems.at[buf]).start(priority=j % 2)
            return c

        lax.fori_loop(0, PROJ_ROWS // SUBLANES, group, 0)

    @pl.when(i == 0)
    def _():
        fetch(slot_ref, 0)

    _for_rows(PROJ_ROWS, lambda r, q: pltpu.make_async_copy(
        ys_ref.at[pl.ds(0, 1)], rows.at[p, 0, pl.ds(0, 1)], sems.at[p]).wait())
    h = _layer_norm(ALPHA * res_ref[...] + rows[p].reshape(PROJ_ROWS, rows.shape[-1]),
                    ln_g_ref[...], ln_b_ref[...])
    h_ref[...] = h

    n_groups = PROJ_ROWS // SUBLANES
    n_chunks = w_ref.shape[1] // PROJ_CHUNK

    def start_some(k):
        for g in range(k * n_groups // n_chunks, (k + 1) * n_groups // n_chunks):
            for j in range(SUBLANES):
                pltpu.make_async_copy(
                    ys_ref.at[pl.ds(slot_next_ref[0, 0, g * SUBLANES + j], 1)],
                    rows.at[1 - p, g, pl.ds(j, 1)], sems.at[1 - p]).start(priority=j % 2)

    _project(h, w_ref, gate_b_ref, qkv_ref, conv_ref, act_ref, after_chunk=start_some)

    @pl.when(i == last)
    def _():
        _for_rows(PROJ_ROWS, lambda r, q: pltpu.make_async_copy(
            ys_ref.at[pl.ds(0, 1)], rows.at[1 - p, 0, pl.ds(0, 1)], sems.at[1 - p]).wait())


def _inproj(h2, w_in_b, layer, gate_b, entry_ln=None, slots=None, finish=None):
    d = h2.shape[1]
    n = h2.shape[0] if slots is None else slots.size
    cols = w_in_b.shape[2]
    gate_cols = gate_b.size
    assert LRU_WIDTH == SC_WIDTH == PROJ_CHUNK and cols == QKV_COLS + BRANCH_COLS + gate_cols
    widths = (QKV_COLS, 3 * PROJ_CHUNK, gate_cols + LRU_WIDTH)
    dtypes = (BF16, F32, BF16)
    rows = pl.BlockSpec((PROJ_ROWS, d), lambda i: (i, 0))
    weight_specs = [pl.BlockSpec((None, d, cols), lambda i: (layer, 0, 0),
                                 pipeline_mode=pl.Buffered(1)),
                    _const_spec((1, gate_cols))]
    weights = (w_in_b, gate_b.reshape(1, gate_cols).astype(F32))
    out_shape = [jax.ShapeDtypeStruct((n, w), t) for w, t in zip(widths, dtypes)]
    out_specs = [pl.BlockSpec((PROJ_ROWS, w), lambda i: (i, 0)) for w in widths]
    if slots is not None:
        assert DISPATCH_ROWS == PROJ_ROWS
        steps = n // PROJ_ROWS
        slot_spec = lambda index: pl.BlockSpec((1, 1, PROJ_ROWS), index, memory_space=pltpu.SMEM)
        out_shape.insert(0, jax.ShapeDtypeStruct((n, d), F32))
        out_specs.insert(0, rows)
        return pl.pallas_call(
            _gather_inproj_kernel,
            out_shape=tuple(out_shape),
            grid=(steps,),
            in_specs=[slot_spec(lambda i: (i, 0, 0)),
                      slot_spec(lambda i: (jnp.minimum(i + 1, steps - 1), 0, 0)),
                      pl.BlockSpec(memory_space=pl.ANY), rows,
                      _const_spec((1, d)), _const_spec((1, d))] + weight_specs,
            out_specs=tuple(out_specs),
            scratch_shapes=[pltpu.VMEM((2, PROJ_ROWS // SUBLANES, SUBLANES, d), F32),
                            pltpu.SemaphoreType.DMA((2,))],
            compiler_params=_cparams(("arbitrary",)),
            name="in_proj_gather",
        )(slots, slots, h2, finish[0], finish[1].reshape(1, d).astype(F32),
          finish[2].reshape(1, d).astype(F32), *weights)
    if entry_ln is None:
        body, ln_specs, ln_args = _inproj_kernel, [], ()
    else:
        body = _ln_inproj_kernel
        ln_specs = [_const_spec((1, d)), _const_spec((1, d))]
        ln_args = tuple(v.reshape(1, d).astype(F32) for v in entry_ln)
        out_shape.insert(0, jax.ShapeDtypeStruct((n, d), F32))
        out_specs.insert(0, rows)
    return pl.pallas_call(
        body,
        out_shape=tuple(out_shape),
        grid=(n // PROJ_ROWS,),
        in_specs=[rows] + ln_specs + weight_specs,
        out_specs=tuple(out_specs),
        compiler_params=_cparams(("parallel",)),
        name="in_proj",
    )(h2, *ln_args, *weights)


LOG2_E = math.log2(math.e)
PASS_LOG2_FLOOR = -151.0


HEADS_PER_STEP = LANES // SB_HEAD_DIM


def _per_head(x):
    lane = lax.broadcasted_iota(jnp.int32, x.shape, 1)
    return jnp.concatenate(
        [jnp.where((lane >= h * SB_HEAD_DIM) & (lane < (h + 1) * SB_HEAD_DIM), x, jnp.zeros_like(x))
         for h in range(HEADS_PER_STEP)], axis=0)


MASKED_SCORE = -1e30


def _neg_abs(x):
    bits = lax.bitcast_convert_type(x, jnp.uint32) | jnp.uint32(0x80000000)
    return lax.bitcast_convert_type(bits, F32)


def _sb_block(q, k_j, v_j, neg_tri2, carries, mask):
    nq, kk = q.shape[0], k_j.shape[0]
    z = lax.dot_general(q, _per_head(k_j), (((1,), (1,)), ((), ())), preferred_element_type=F32)
    scores, split = [], []
    for h in range(HEADS_PER_STEP):
        z_h = z[:, h * kk:(h + 1) * kk]
        if mask is not None:
            z_h = jnp.where(mask, z_h, MASKED_SCORE)
        drop = jnp.maximum(z_h, 0.0) + jnp.log2(1.0 + jnp.exp2(_neg_abs(z_h)))
        scores.append(z_h)
        split.append(drop.astype(BF16))
    log_pass = jnp.dot(jnp.concatenate(split, axis=0), neg_tri2, preferred_element_type=F32)
    w, new_carries = [], []
    for h in range(HEADS_PER_STEP):
        lp_h = log_pass[h * nq:(h + 1) * nq]
        w.append(jnp.exp2(scores[h] + lp_h + carries[h]).astype(BF16))
        new_carries.append(carries[h] + lp_h[:, 0:1])
    pv = jnp.dot(jnp.concatenate(w, axis=1), _per_head(v_j), preferred_element_type=F32)
    return pv, new_carries


def _attn_kernel(q_ref, k_ref, v_ref, u2_ref, o_ref):
    qi = pl.program_id(2)
    u2 = u2_ref[...]
    row = lax.broadcasted_iota(jnp.int32, (ATT_Q, ATT_K), 0)
    col = lax.broadcasted_iota(jnp.int32, (ATT_Q, ATT_K), 1)
    diag_mask = col < row
    blocks = [slice(g * LANES, (g + 1) * LANES) for g in range(ATT_LANE_BLOCKS)]
    q = [q_ref[0, :, lanes] for lanes in blocks]

    def sweep(j, accs, carries, mask):
        rows = pl.ds(pl.multiple_of(j * ATT_K, ATT_K), ATT_K)
        out = [_sb_block(q_g, k_ref[0, rows, lanes], v_ref[0, rows, lanes], u2, c_g, mask)
               for q_g, lanes, c_g in zip(q, blocks, carries)]
        return [acc + pv for acc, (pv, _) in zip(accs, out)], [c for _, c in out]

    zero = [[jnp.zeros((ATT_Q, 1), F32) for _ in range(HEADS_PER_STEP)] for _ in blocks]
    accs, carries = sweep(qi, [jnp.zeros((ATT_Q, LANES), F32) for _ in blocks], zero, diag_mask)
    no_prev = jnp.where(qi > 0, 0.0, -1e30)
    accs, carries = sweep(jnp.maximum(qi - 1, 0), accs,
                          [[c + no_prev for c in c_g] for c_g in carries], None)

    def largest(carries):
        return jnp.max(functools.reduce(jnp.maximum, [c for c_g in carries for c in c_g]))

    def cond(state):
        j, _, _, live = state
        return jnp.logical_and(j >= 0, live > PASS_LOG2_FLOOR)

    def body(state):
        j, accs, carries, _ = state
        accs, carries = sweep(j, accs, carries, None)
        return j - 1, accs, carries, largest(carries)

    _, accs, _, _ = lax.while_loop(cond, body, (qi - 2, accs, carries, largest(carries)))
    o_ref[0] = jnp.concatenate(accs, axis=1).astype(o_ref.dtype)


def _sb_attention(qkv3):
    b, s, _ = qkv3.shape
    width = ATT_LANE_BLOCKS * LANES
    groups = SB_WIDTH // width
    neg_tri = -(jnp.arange(ATT_K)[:, None] >= jnp.arange(ATT_K)[None, :]).astype(BF16)
    u2 = neg_tri
    return pl.pallas_call(
        _attn_kernel,
        out_shape=jax.ShapeDtypeStruct((b, s, SB_WIDTH), BF16),
        grid=(b, groups, s // ATT_Q),
        in_specs=[pl.BlockSpec((1, ATT_Q, width), lambda bi, p, qi: (bi, qi, p)),
                  pl.BlockSpec((1, s, width), lambda bi, p, qi: (bi, 0, groups + p)),
                  pl.BlockSpec((1, s, width), lambda bi, p, qi: (bi, 0, 2 * groups + p)),
                  _const_spec((ATT_K, ATT_K))],
        out_specs=pl.BlockSpec((1, ATT_Q, width), lambda bi, p, qi: (bi, qi, p)),
        compiler_params=_cparams(("parallel", "parallel", "arbitrary")),
        name="sb_attention",
    )(qkv3, qkv3, qkv3, u2)


def _causal_conv(x, stage, w):
    t = x.shape[0]
    k = w.shape[0]
    stage[SUBLANES:, :] = x
    out = w[k - 1:k] * x
    for j in range(1, k):
        out = out + w[k - 1 - j:k - j] * stage[pl.ds(SUBLANES - j, t), :]
    stage[:SUBLANES, :] = x[t - SUBLANES:]
    return out


def _linear_scan(a, b, h0):
    t = a.shape[0]
    sub = lax.broadcasted_iota(jnp.int32, a.shape, 0) % SUBLANES
    d = 1
    while d < SUBLANES:
        keep = sub >= d
        a_prev = jnp.where(keep, pltpu.roll(a, d, axis=0), 1.0)
        b_prev = jnp.where(keep, pltpu.roll(b, d, axis=0), 0.0)
        b = a * b_prev + b
        a = a * a_prev
        d *= 2
    h = h0
    groups = []
    for g in range(t // SUBLANES):
        rows = slice(g * SUBLANES, (g + 1) * SUBLANES)
        seg = b[rows] + a[rows] * h
        h = seg[SUBLANES - 1:]
        groups.append(seg)
    return jnp.concatenate(groups, axis=0)


def _mix_kernel(h_ref, ysb_ref, lin_ref, scb_ref, scp_ref, gsb_ref, glru_ref, gsc_ref, gelu_ref,
                lconv_w_ref, lconv_b_ref, wa_ref, ba_ref, wx_ref, bx_ref, lam_ref, scw_ref,
                wsb_ref, wlru_ref, wsc_ref, wout_ref, ln_g_ref, ln_b_ref,
                out_ref, lin_stage, sc_stage, h_state):
    t = MIX_ROWS

    @pl.when(pl.program_id(1) == 0)
    def _():
        lin_stage[:SUBLANES, :] = jnp.zeros((SUBLANES, LRU_WIDTH), F32)
        sc_stage[:SUBLANES, :] = jnp.zeros((SUBLANES, SC_WIDTH), F32)
        h_state[...] = jnp.zeros_like(h_state)

    u = _causal_conv(lin_ref[0], lin_stage, lconv_w_ref[...]) + lconv_b_ref[...]
    ub = u.astype(BF16)
    r = _sigmoid(jnp.dot(ub, wa_ref[...], preferred_element_type=F32) + ba_ref[...])
    i = _sigmoid(jnp.dot(ub, wx_ref[...], preferred_element_type=F32) + bx_ref[...])
    lam = lam_ref[...]
    softplus_neg_lam = jnp.maximum(-lam, 0.0) + jnp.log(1.0 + jnp.exp(-jnp.abs(lam)))
    log_a = (-LRU_C) * r * softplus_neg_lam
    a = jnp.exp(log_a)
    gap = 1.0 - a * a
    drive = jnp.where(gap > 0.0, gap * lax.rsqrt(gap), 0.0) * (i * u)
    hs = _linear_scan(a, drive, h_state[...])
    h_state[...] = hs[t - 1:]
    y_lru = gelu_ref[0] * hs.astype(BF16)

    y_sc = (scb_ref[0] * _causal_conv(scp_ref[0], sc_stage, scw_ref[...])).astype(BF16)

    def branch(y, w_ref):
        return jnp.dot(y, w_ref[...], preferred_element_type=F32).astype(BF16)

    merged = (gsb_ref[0] * branch(ysb_ref[0], wsb_ref) + glru_ref[0] * branch(y_lru, wlru_ref)
              + gsc_ref[0] * branch(y_sc, wsc_ref))
    mix = jnp.dot(merged, wout_ref[...], preferred_element_type=F32)
    out_ref[0] = _layer_norm(ALPHA * h_ref[0] + mix, ln_g_ref[...], ln_b_ref[...])


def _block_diag(w):
    hh, ii, jj = w.shape
    eye = jnp.eye(hh, dtype=w.dtype)
    return (eye[:, None, :, None] * w[:, :, None, :]).reshape(hh * ii, hh * jj)


def _mixer_tail(h3, ysb3, conv3, act3, p, l):
    b, s, d = h3.shape
    t = MIX_ROWS
    assert LRU_WIDTH == SC_WIDTH

    def row_spec(width, col):
        return pl.BlockSpec((1, t, width), lambda bi, si: (bi, si, col))

    vec = lambda v: v.reshape(1, -1).astype(F32)
    in_specs = [row_spec(d, 0), row_spec(SB_WIDTH, 0)]
    in_specs += [row_spec(LRU_WIDTH, c) for c in range(3)]
    in_specs += [row_spec(d, g) for g in range(3)]
    in_specs += [row_spec(LRU_WIDTH, 3 * d // LRU_WIDTH)]
    weights = [
        p["lru_conv_w"][l].astype(F32), vec(p["lru_conv_b"][l]),
        _block_diag(p["lru_wa"][l]).astype(BF16), vec(p["lru_ba"][l]),
        _block_diag(p["lru_wx"][l]).astype(BF16), vec(p["lru_bx"][l]),
        vec(p["lru_lambda"][l]), p["sc_conv_w"][l].astype(F32),
        p["w_branch_sb"][l].astype(BF16), p["w_branch_lru"][l].astype(BF16),
        p["w_branch_sc"][l].astype(BF16), p["w_out"][l].astype(BF16),
        vec(p["ln1_g"][l]), vec(p["ln1_b"][l]),
    ]
    in_specs += [_const_spec(w.shape) for w in weights]
    return pl.pallas_call(
        _mix_kernel,
        out_shape=jax.ShapeDtypeStruct((b, s, d), F32),
        grid=(b, s // t),
        in_specs=in_specs,
        out_specs=pl.BlockSpec((1, t, d), lambda bi, si: (bi, si, 0)),
        scratch_shapes=[pltpu.VMEM((SUBLANES + t, LRU_WIDTH), F32),
                        pltpu.VMEM((SUBLANES + t, SC_WIDTH), F32),
                        pltpu.VMEM((1, LRU_WIDTH), F32)],
        compiler_params=_cparams(("parallel", "arbitrary")),
        name="mixer_tail",
    )(h3, ysb3, *([conv3] * 3), *([act3] * 4), *weights)


def _first_argmax(vals, row):
    m = jnp.max(vals, axis=0, keepdims=True)
    idx = jnp.min(jnp.where(vals == m, row, vals.shape[0]), axis=0, keepdims=True)
    return m, idx


def _route_kernel(h_ref, w_ref, bias_ref, before_ref, info_ref, key_ref, counts_ref, running):
    @pl.when(pl.program_id(0) == 0)
    def _():
        running[...] = jnp.zeros_like(running)

    h = h_ref[...]
    h_hi = h.astype(BF16)
    h_lo = (h - h_hi.astype(F32)).astype(BF16)
    w = w_ref[...]
    w_hi = w.astype(BF16)
    w_lo = (w - w_hi.astype(F32)).astype(BF16)
    logits = jnp.dot(jnp.concatenate([h_hi, h_lo, h_hi], axis=1),
                     jnp.concatenate([w_hi, w_hi, w_lo], axis=0), preferred_element_type=F32)
    lt = logits.T
    neg = -jnp.inf
    experts = lt[:N_EXPERTS]
    groups = lt[N_EXPERTS:N_EXPERTS + SUBLANES]
    experts_b = experts + bias_ref[:N_EXPERTS]
    groups_b = groups + bias_ref[N_EXPERTS:N_EXPERTS + SUBLANES]
    row_g = lax.broadcasted_iota(jnp.int32, groups.shape, 0)
    row_e = lax.broadcasted_iota(jnp.int32, experts.shape, 0)
    is_group = row_g < N_GROUPS
    g_max = jnp.max(jnp.where(is_group, groups, neg), axis=0, keepdims=True)
    g_exp = jnp.where(is_group, jnp.exp(groups - g_max), 0.0)
    g_den = jnp.sum(g_exp, axis=0, keepdims=True)
    _, g_sel = _first_argmax(jnp.where(is_group, groups_b, neg), row_g)
    g_prob = jnp.sum(jnp.where(row_g == g_sel, g_exp, 0.0), axis=0, keepdims=True) / g_den
    cand = jnp.where((row_e >> 2) == g_sel, experts_b, neg)
    _, e1 = _first_argmax(cand, row_e)
    _, e2 = _first_argmax(jnp.where(row_e == e1, neg, cand), row_e)
    l1 = jnp.sum(jnp.where(row_e == e1, experts, 0.0), axis=0, keepdims=True)
    l2 = jnp.sum(jnp.where(row_e == e2, experts, 0.0), axis=0, keepdims=True)
    m = jnp.maximum(l1, l2)
    x1 = jnp.exp(l1 - m)
    x2 = jnp.exp(l2 - m)
    scale = g_prob / (x1 + x2)
    w1 = x1 * scale
    w2 = x2 * scale
    first_low = e1 < e2
    a = jnp.where(first_low, e1, e2) - g_sel * EXP_PER_GROUP
    b = jnp.where(first_low, e2, e1) - g_sel * EXP_PER_GROUP
    cls = g_sel * PAIRS_PER_GROUP + ((a * (2 * EXP_PER_GROUP - 1 - a)) >> 1) + (b - a - 1)
    row_c = lax.broadcasted_iota(jnp.int32, (CLASS_ROWS, cls.shape[1]), 0)
    onehot = row_c == cls
    ones = jnp.where(onehot, 1.0, 0.0)
    earlier = jnp.dot(ones, before_ref[...], preferred_element_type=F32) + running[...]
    rank = jnp.sum(jnp.where(onehot, earlier, 0.0), axis=0, keepdims=True)
    running[...] += jnp.sum(ones, axis=1, keepdims=True)
    counts_ref[...] = running[...]
    key_ref[0] = (cls << KEY_SHIFT) + rank.astype(jnp.int32)
    row_w = lax.broadcasted_iota(jnp.int32, (LANES, cls.shape[1]), 0)
    w_rows = jnp.where(row_w == INFO_W_LOW, jnp.where(first_low, w1, w2), 0.0)
    w_rows = jnp.where(row_w == INFO_W_HIGH, jnp.where(first_low, w2, w1), w_rows)
    info_ref[...] = w_rows.T


def _route(h2, w_router, bias):
    n, d = h2.shape
    t = ROUTE_ROWS
    before = (jnp.arange(t)[:, None] < jnp.arange(t)[None, :]).astype(F32)
    return pl.pallas_call(
        _route_kernel,
        out_shape=(jax.ShapeDtypeStruct((n, LANES), F32),
                   jax.ShapeDtypeStruct((n // t, 1, t), jnp.int32),
                   jax.ShapeDtypeStruct((CLASS_ROWS, 1), F32)),
        grid=(n // t,),
        in_specs=[pl.BlockSpec((t, d), lambda i: (i, 0)),
                  _const_spec((d, LANES)), _const_spec((LANES, 1)), _const_spec((t, t))],
        out_specs=(pl.BlockSpec((t, LANES), lambda i: (i, 0)),
                   pl.BlockSpec((1, 1, t), lambda i: (i, 0, 0)),
                   _const_spec((CLASS_ROWS, 1))),
        scratch_shapes=[pltpu.VMEM((CLASS_ROWS, 1), F32)],
        compiler_params=_cparams(("arbitrary",)),
        name="moe_route",
    )(h2, w_router, bias, before)


def _slot_kernel(start_ref, key_ref, slot_ref):
    key = key_ref[...]
    cls = key >> KEY_SHIFT
    slot = key & ((1 << KEY_SHIFT) - 1)
    for c in range(N_CLASSES):
        slot = slot + jnp.where(cls == c, start_ref[c], 0)
    slot_ref[...] = slot


def _slots(class_start, key3):
    g, _, t = key3.shape
    return pl.pallas_call(
        _slot_kernel,
        out_shape=jax.ShapeDtypeStruct(key3.shape, jnp.int32),
        grid_spec=pltpu.PrefetchScalarGridSpec(
            num_scalar_prefetch=1, grid=(1,),
            in_specs=[pl.BlockSpec((g, 1, t), lambda i, s: (0, 0, 0))],
            out_specs=pl.BlockSpec((g, 1, t), lambda i, s: (0, 0, 0))),
        compiler_params=_cparams(("arbitrary",)),
        name="moe_slots",
    )(class_start, key3)


def _for_rows(n_rows, fn):
    def body(g, c):
        for j in range(ROW_COPY_UNROLL):
            fn(g * ROW_COPY_UNROLL + j, j % 2)
        return c

    lax.fori_loop(0, n_rows // ROW_COPY_UNROLL, body, 0)


def _for_row_groups(n_rows, fn):
    def body(g, c):
        for j in range(SUBLANES):
            fn(g, j)
        return c

    lax.fori_loop(0, n_rows // SUBLANES, body, 0)


def _pack_rows(x, info):
    half = x.shape[1] // 2
    bits = lax.bitcast_convert_type(x.astype(BF16).astype(F32), jnp.uint32)
    words = (bits[:, :half] & jnp.uint32(0xFFFF0000)) | (bits[:, half:] >> 16)
    return jnp.concatenate([words, lax.bitcast_convert_type(info, jnp.uint32)], axis=1)


def _unpack_rows(packed):
    half = packed.shape[1] - LANES
    words = packed[:, :half]
    hi = lax.bitcast_convert_type(words & jnp.uint32(0xFFFF0000), F32)
    lo = lax.bitcast_convert_type(words << 16, F32)
    return (jnp.concatenate([hi, lo], axis=1).astype(BF16),
            lax.bitcast_convert_type(packed[:, half:], F32))


def _dispatch_kernel(tile_end_ref, n_used_ref, slot_ref, h_ref, info_ref, xs_ref,
                     rows, zeros, sems, zsem):
    i = pl.program_id(0)
    last = pl.num_programs(0) - 1
    p = i % 2
    t = DISPATCH_ROWS
    te = EXPERT_ROWS

    def row_copy(buf, g, j):
        return pltpu.make_async_copy(rows.at[buf, g, pl.ds(j, 1)],
                                     xs_ref.at[pl.ds(slot_ref[0, 0, g * SUBLANES + j], 1)],
                                     sems.at[buf])

    @pl.when(i == 0)
    def _():
        zeros[...] = jnp.zeros_like(zeros)

        def zero_tile(tile):
            return pltpu.make_async_copy(zeros, xs_ref.at[pl.ds(tile * te, te)], zsem)

        prev_end = 0
        for c in range(N_CLASSES):
            end = tile_end_ref[c]

            @pl.when(end > prev_end)
            def _(end=end):
                cp = zero_tile(end - 1)
                cp.start()
                cp.wait()
            prev_end = end

        def zero_tail(tile, c):
            cp = zero_tile(tile)
            cp.start()
            cp.wait()
            return c

        lax.fori_loop(n_used_ref[0], xs_ref.shape[0] // te, zero_tail, 0)

    rows[p] = _pack_rows(h_ref[...], info_ref[...]).reshape(rows.shape[1:])
    _for_row_groups(t, lambda g, j: row_copy(p, g, j).start(priority=j % 2))

    def row_wait(buf):
        pltpu.make_async_copy(rows.at[buf, 0, pl.ds(0, 1)], xs_ref.at[pl.ds(0, 1)],
                              sems.at[buf]).wait()

    @pl.when(i > 0)
    def _():
        _for_rows(t, lambda r, q: row_wait(1 - p))

    @pl.when(i == last)
    def _():
        _for_rows(t, lambda r, q: row_wait(p))


def _dispatch(tile_end, n_used, slot3, h2, info, n_slots):
    n, d = h2.shape
    t = DISPATCH_ROWS
    return pl.pallas_call(
        _dispatch_kernel,
        out_shape=jax.ShapeDtypeStruct((n_slots, d // 2 + LANES), jnp.uint32),
        grid_spec=pltpu.PrefetchScalarGridSpec(
            num_scalar_prefetch=2, grid=(n // t,),
            in_specs=[pl.BlockSpec((1, 1, t), lambda i, *_: (i, 0, 0), memory_space=pltpu.SMEM),
                      pl.BlockSpec((t, d), lambda i, *_: (i, 0)),
                      pl.BlockSpec((t, LANES), lambda i, *_: (i, 0))],
            out_specs=pl.BlockSpec(memory_space=pl.ANY),
            scratch_shapes=[pltpu.VMEM((2, t // SUBLANES, SUBLANES, d // 2 + LANES), jnp.uint32),
                            pltpu.VMEM((EXPERT_ROWS, d // 2 + LANES), jnp.uint32),
                            pltpu.SemaphoreType.DMA((2,)), pltpu.SemaphoreType.DMA(())]),
        compiler_params=_cparams(("arbitrary",)),
        name="moe_dispatch",
    )(tile_end, n_used, slot3, h2, info)


def _expert_kernel(n_used_ref, src_ref, lo_ref, hi_ref, xs_ref, wg_lo, wg_hi, wu_lo, wu_hi,
                   wd_lo, wd_hi, ys_ref):
    del src_ref, lo_ref, hi_ref

    @pl.when(pl.program_id(0) < n_used_ref[0])
    def _():
        xb, info = _unpack_rows(xs_ref[...])
        y = None
        for wg, wu, wd, lane in ((wg_lo, wu_lo, wd_lo, INFO_W_LOW), (wg_hi, wu_hi, wd_hi, INFO_W_HIGH)):
            gate = jnp.dot(xb, wg[0], preferred_element_type=F32)
            up = jnp.dot(xb, wu[0], preferred_element_type=F32)
            act = gate * _sigmoid(gate) * up * info[:, lane:lane + 1]
            part = jnp.dot(act.astype(BF16), wd[0], preferred_element_type=F32)
            y = part if y is None else y + part
        ys_ref[...] = y

    @pl.when(pl.program_id(0) >= n_used_ref[0])
    def _():
        ys_ref[...] = jnp.zeros_like(ys_ref)


def _experts(tile_meta, xs, layer, w_gate_b, w_up_b, w_down_b):
    n_slots, cols = xs.shape
    d, f = w_gate_b.shape[2:]
    t = EXPERT_ROWS
    x_map = lambda i, n_used, src, lo, hi: (src[i], 0)
    lo_map = lambda i, n_used, src, lo, hi: (layer, lo[i], 0, 0)
    hi_map = lambda i, n_used, src, lo, hi: (layer, hi[i], 0, 0)
    grid_spec = pltpu.PrefetchScalarGridSpec(
        num_scalar_prefetch=4,
        grid=(n_slots // t,),
        in_specs=[pl.BlockSpec((t, cols), x_map),
                  pl.BlockSpec((None, 1, d, f), lo_map), pl.BlockSpec((None, 1, d, f), hi_map),
                  pl.BlockSpec((None, 1, d, f), lo_map), pl.BlockSpec((None, 1, d, f), hi_map),
                  pl.BlockSpec((None, 1, f, d), lo_map), pl.BlockSpec((None, 1, f, d), hi_map)],
        out_specs=pl.BlockSpec((t, d), lambda i, *_: (i, 0)),
    )
    return pl.pallas_call(
        _expert_kernel,
        out_shape=jax.ShapeDtypeStruct((n_slots, d), F32),
        grid_spec=grid_spec,
        compiler_params=_cparams(("arbitrary",)),
        name="moe_experts",
    )(*tile_meta, xs, w_gate_b, w_gate_b, w_up_b, w_up_b, w_down_b, w_down_b)


def _combine_kernel(slot_ref, ys_ref, h_ref, ln_g_ref, ln_b_ref, o_ref, buf, sems):
    i = pl.program_id(0)
    tiles = pl.num_programs(0) - 1
    p = i % 2
    t = DISPATCH_ROWS

    @pl.when(i < tiles)
    def _():
        _for_row_groups(t, lambda g, j: pltpu.make_async_copy(
            ys_ref.at[pl.ds(slot_ref[0, 0, g * SUBLANES + j], 1)], buf.at[p, g, pl.ds(j, 1)],
            sems.at[p]).start(priority=j % 2))

    @pl.when(i == 0)
    def _():
        o_ref[...] = jnp.zeros_like(o_ref)

    @pl.when(i > 0)
    def _():
        _for_rows(t, lambda r, q: pltpu.make_async_copy(
            ys_ref.at[pl.ds(0, 1)], buf.at[1 - p, 0, pl.ds(0, 1)], sems.at[1 - p]).wait())
        o_ref[...] = _layer_norm(ALPHA * h_ref[...] + buf[1 - p].reshape(o_ref.shape),
                                 ln_g_ref[...], ln_b_ref[...])


def _combine(slot3, ys, h2, ln_g, ln_b):
    n, d = h2.shape
    t = DISPATCH_ROWS
    tiles = n // t
    done = lambda i: (jnp.maximum(i - 1, 0), 0)
    return pl.pallas_call(
        _combine_kernel,
        out_shape=jax.ShapeDtypeStruct((n, d), F32),
        grid=(tiles + 1,),
        in_specs=[pl.BlockSpec((1, 1, t), lambda i: (jnp.minimum(i, tiles - 1), 0, 0),
                               memory_space=pltpu.SMEM),
                  pl.BlockSpec(memory_space=pl.ANY),
                  pl.BlockSpec((t, d), done), _const_spec((1, d)), _const_spec((1, d))],
        out_specs=pl.BlockSpec((t, d), done),
        scratch_shapes=[pltpu.VMEM((2, t // SUBLANES, SUBLANES, d), F32),
                        pltpu.SemaphoreType.DMA((2,))],
        compiler_params=_cparams(("arbitrary",)),
        name="moe_combine",
    )(slot3, ys, h2, ln_g.reshape(1, d).astype(F32), ln_b.reshape(1, d).astype(F32))


def _pair_tables():
    lo, hi = [], []
    for g in range(N_GROUPS):
        for a in range(EXP_PER_GROUP):
            for b in range(a + 1, EXP_PER_GROUP):
                lo.append(g * EXP_PER_GROUP + a)
                hi.append(g * EXP_PER_GROUP + b)
    return jnp.asarray(lo, jnp.int32), jnp.asarray(hi, jnp.int32)


def _tile_plan(counts, n_tiles):
    t = EXPERT_ROWS
    counts = counts[:N_CLASSES, 0].astype(jnp.int32)
    tiles_per_class = (counts + t - 1) // t
    tile_end = jnp.cumsum(tiles_per_class)
    class_start = (tile_end - tiles_per_class) * t
    n_used = tile_end[-1:]
    src = jnp.minimum(jnp.arange(n_tiles, dtype=jnp.int32), n_used - 1)
    tile_class = jnp.sum((tile_end[None, :] <= src[:, None]).astype(jnp.int32), axis=1)
    pair_lo, pair_hi = _pair_tables()
    return class_start, tile_end, (n_used, src, pair_lo[tile_class], pair_hi[tile_class])


def kernel(x, ln_in_g, ln_in_b, w_in, gate_b, lru_conv_w, lru_conv_b, lru_wa, lru_ba, lru_wx, lru_bx, lru_lambda, sc_conv_w, w_branch_sb, w_branch_lru, w_branch_sc, w_out, ln1_g, ln1_b, w_group, group_bias, w_expert_router, expert_bias, w_gate, w_up, w_down, ln2_g, ln2_b):
    b, s, d = x.shape
    n = b * s
    params = dict(gate_b=gate_b, lru_conv_w=lru_conv_w, lru_conv_b=lru_conv_b, lru_wa=lru_wa,
                  lru_ba=lru_ba, lru_wx=lru_wx, lru_bx=lru_bx, lru_lambda=lru_lambda,
                  sc_conv_w=sc_conv_w, w_branch_sb=w_branch_sb, w_branch_lru=w_branch_lru,
                  w_branch_sc=w_branch_sc, w_out=w_out, ln1_g=ln1_g, ln1_b=ln1_b)
    w_in_b, w_gate_b, w_up_b, w_down_b = (w.astype(BF16) for w in (w_in, w_gate, w_up, w_down))
    h = x.reshape(n, d)
    for l in range(w_in.shape[0]):
        if l == 0:
            h, qkv, conv_in, act = _inproj(h, w_in_b, l, gate_b[l], entry_ln=(ln_in_g, ln_in_b))
        else:
            h, qkv, conv_in, act = _inproj(ys, w_in_b, l, gate_b[l], slots=slot3,
                                           finish=(h1, ln2_g[l - 1], ln2_b[l - 1]))
        y_sb = _sb_attention(qkv.reshape(b, s, -1))
        h1 = _mixer_tail(h.reshape(b, s, d), y_sb, conv_in.reshape(b, s, -1),
                         act.reshape(b, s, -1), params, l)
        h1 = h1.reshape(n, d)
        pad = LANES - N_EXPERTS - N_GROUPS
        w_router = jnp.concatenate(
            [w_expert_router[l], w_group[l], jnp.zeros((d, pad), F32)], axis=1).astype(F32)
        bias = jnp.concatenate(
            [expert_bias[l], group_bias[l], jnp.zeros((pad,), F32)]).reshape(LANES, 1).astype(F32)
        info, key3, counts = _route(h1, w_router, bias)
        n_tiles = n // EXPERT_ROWS + N_CLASSES
        class_start, tile_end, tile_meta = _tile_plan(counts, n_tiles)
        slot3 = _slots(class_start, key3)
        xs = _dispatch(tile_end, tile_meta[0], slot3, h1, info, n_tiles * EXPERT_ROWS)
        ys = _experts(tile_meta, xs, l, w_gate_b, w_up_b, w_down_b)
    return _combine(slot3, ys, h1, ln2_g[-1], ln2_b[-1]).reshape(b, s, d)
```

```python
import functools
import math

import jax
import jax.numpy as jnp
from jax import lax
from jax.experimental import pallas as pl
from jax.experimental.pallas import tpu as pltpu

F32 = jnp.float32
BF16 = jnp.bfloat16

SB_HEADS = 8
SB_HEAD_DIM = 64
SB_WIDTH = SB_HEADS * SB_HEAD_DIM
LRU_WIDTH = 512
LRU_BLOCKS = 8
LRU_C = 8.0
SC_WIDTH = 512
N_GROUPS = 4
EXP_PER_GROUP = 4
N_EXPERTS = N_GROUPS * EXP_PER_GROUP
D_EXPERT = 512
DEPTH = 2
ALPHA = (2 * DEPTH) ** 0.25
LN_EPS = 1e-5

LANES = 128
SUBLANES = 8
VMEM_LIMIT_BYTES = 56 * 1024 * 1024
F32_SIGN_BIT = 0x80000000
HIGH_HALF_WORD = 0xFFFF0000

PROJ_ROWS = 512
PROJ_CHUNK = 512
ATT_Q = 256
ATT_K = 256
ATT_LANE_BLOCKS = 2
MIX_ROWS = 512
ROUTE_ROWS = 512
DISPATCH_ROWS = 512
EXPERT_ROWS = 256
ROW_COPY_UNROLL = 8

PAIRS_PER_GROUP = EXP_PER_GROUP * (EXP_PER_GROUP - 1) // 2
N_CLASSES = N_GROUPS * PAIRS_PER_GROUP
CLASS_ROWS = 32
INFO_W_LOW, INFO_W_HIGH = 0, 1
KEY_SHIFT = 16


def _cparams(sem):
    return pltpu.CompilerParams(dimension_semantics=sem, vmem_limit_bytes=VMEM_LIMIT_BYTES)


def _const_spec(shape):
    nd = len(shape)
    return pl.BlockSpec(shape, lambda *_: (0,) * nd)


def _layer_norm(y, g, b):
    mu = jnp.mean(y, axis=-1, keepdims=True)
    d = y - mu
    var = jnp.mean(d * d, axis=-1, keepdims=True)
    return d * lax.rsqrt(var + LN_EPS) * g + b


QKV_COLS = 3 * SB_WIDTH
BRANCH_COLS = 2 * LRU_WIDTH + 3 * SC_WIDTH


def _sigmoid(x):
    return 0.5 * jnp.tanh(0.5 * x) + 0.5


def _gelu_tanh(x):
    return 0.5 * x * (1.0 + jnp.tanh(math.sqrt(2.0 / math.pi) * (x + 0.044715 * (x * x * x))))


def _project(h, w_ref, gate_b_ref, qkv_ref, conv_ref, act_ref, after_chunk=lambda k: None):
    hb = h.astype(BF16)
    c = PROJ_CHUNK

    def proj(chunk):
        out = jnp.dot(hb, w_ref[:, chunk * c:(chunk + 1) * c], preferred_element_type=F32)
        after_chunk(chunk)
        return out

    n_qkv = QKV_COLS // c
    for j in range(n_qkv):
        acc = proj(j)
        if j * c < SB_WIDTH:
            acc = acc * (SB_HEAD_DIM ** -0.5 * LOG2_E)
        qkv_ref[:, j * c:(j + 1) * c] = acc.astype(BF16)
    lru_in, lru_gate, sc_b, sc_c, sc_h = range(n_qkv, n_qkv + 5)
    conv_ref[:, 0:c] = proj(lru_in)
    conv_ref[:, c:2 * c] = proj(sc_b)
    conv_ref[:, 2 * c:3 * c] = proj(sc_c) * proj(sc_h)
    for j in range(gate_b_ref.shape[1] // c):
        acc = proj(sc_h + 1 + j) + gate_b_ref[:, j * c:(j + 1) * c]
        act_ref[:, j * c:(j + 1) * c] = _sigmoid(acc).astype(BF16)
    act_ref[:, gate_b_ref.shape[1]:] = _gelu_tanh(proj(lru_gate)).astype(BF16)


def _inproj_kernel(h_ref, w_ref, gate_b_ref, qkv_ref, conv_ref, act_ref):
    _project(h_ref[...], w_ref, gate_b_ref, qkv_ref, conv_ref, act_ref)


def _ln_inproj_kernel(x_ref, g_ref, b_ref, w_ref, gate_b_ref, h_ref, qkv_ref, conv_ref, act_ref):
    h = _layer_norm(x_ref[...], g_ref[...], b_ref[...])
    h_ref[...] = h
    _project(h, w_ref, gate_b_ref, qkv_ref, conv_ref, act_ref)


def _gather_inproj_kernel(slot_ref, slot_next_ref, ys_ref, res_ref, ln_g_ref, ln_b_ref,
                          w_ref, gate_b_ref, h_ref, qkv_ref, conv_ref, act_ref, rows, sems):
    i = pl.program_id(0)
    last = pl.num_programs(0) - 1
    p = i % 2

    def fetch(slots, buf):
        def group(g, c):
            for j in range(SUBLANES):
                pltpu.make_async_copy(ys_ref.at[pl.ds(slots[0, 0, g * SUBLANES + j], 1)],
                                      rows.at[buf, g, pl.ds(j, 1)],
                                      sems.at[buf]).start(priority=j % 2)
            return c

        lax.fori_loop(0, PROJ_ROWS // SUBLANES, group, 0)

    @pl.when(i == 0)
    def _():
        fetch(slot_ref, 0)

    _for_rows(PROJ_ROWS, lambda r, q: pltpu.make_async_copy(
        ys_ref.at[pl.ds(0, 1)], rows.at[p, 0, pl.ds(0, 1)], sems.at[p]).wait())
    h = _layer_norm(ALPHA * res_ref[...] + rows[p].reshape(PROJ_ROWS, rows.shape[-1]),
                    ln_g_ref[...], ln_b_ref[...])
    h_ref[...] = h

    n_groups = PROJ_ROWS // SUBLANES
    n_chunks = w_ref.shape[1] // PROJ_CHUNK

    def start_some(k):
        for g in range(k * n_groups // n_chunks, (k + 1) * n_groups // n_chunks):
            for j in range(SUBLANES):
                pltpu.make_async_copy(
                    ys_ref.at[pl.ds(slot_next_ref[0, 0, g * SUBLANES + j], 1)],
                    rows.at[1 - p, g, pl.ds(j, 1)], sems.at[1 - p]).start(priority=j % 2)

    _project(h, w_ref, gate_b_ref, qkv_ref, conv_ref, act_ref, after_chunk=start_some)

    @pl.when(i == last)
    def _():
        _for_rows(PROJ_ROWS, lambda r, q: pltpu.make_async_copy(
            ys_ref.at[pl.ds(0, 1)], rows.at[1 - p, 0, pl.ds(0, 1)], sems.at[1 - p]).wait())


def _inproj(h2, w_in_b, layer, gate_b, entry_ln=None, slots=None, finish=None):
    d = h2.shape[1]
    n = h2.shape[0] if slots is None else slots.size
    cols = w_in_b.shape[2]
    gate_cols = gate_b.size
    assert LRU_WIDTH == SC_WIDTH == PROJ_CHUNK and cols == QKV_COLS + BRANCH_COLS + gate_cols
    widths = (QKV_COLS, 3 * PROJ_CHUNK, gate_cols + LRU_WIDTH)
    dtypes = (BF16, F32, BF16)
    rows = pl.BlockSpec((PROJ_ROWS, d), lambda i: (i, 0))
    weight_specs = [pl.BlockSpec((None, d, cols), lambda i: (layer, 0, 0),
                                 pipeline_mode=pl.Buffered(1)),
                    _const_spec((1, gate_cols))]
    weights = (w_in_b, gate_b.reshape(1, gate_cols).astype(F32))
    out_shape = [jax.ShapeDtypeStruct((n, w), t) for w, t in zip(widths, dtypes)]
    out_specs = [pl.BlockSpec((PROJ_ROWS, w), lambda i: (i, 0)) for w in widths]
    if slots is not None:
        assert DISPATCH_ROWS == PROJ_ROWS
        steps = n // PROJ_ROWS
        slot_spec = lambda index: pl.BlockSpec((1, 1, PROJ_ROWS), index, memory_space=pltpu.SMEM)
        out_shape.insert(0, jax.ShapeDtypeStruct((n, d), F32))
        out_specs.insert(0, rows)
        return pl.pallas_call(
            _gather_inproj_kernel,
            out_shape=tuple(out_shape),
            grid=(steps,),
            in_specs=[slot_spec(lambda i: (i, 0, 0)),
                      slot_spec(lambda i: (jnp.minimum(i + 1, steps - 1), 0, 0)),
                      pl.BlockSpec(memory_space=pl.ANY), rows,
                      _const_spec((1, d)), _const_spec((1, d))] + weight_specs,
            out_specs=tuple(out_specs),
            scratch_shapes=[pltpu.VMEM((2, PROJ_ROWS // SUBLANES, SUBLANES, d), F32),
                            pltpu.SemaphoreType.DMA((2,))],
            compiler_params=_cparams(("arbitrary",)),
            name="in_proj_gather",
        )(slots, slots, h2, finish[0], finish[1].reshape(1, d).astype(F32),
          finish[2].reshape(1, d).astype(F32), *weights)
    if entry_ln is None:
        body, ln_specs, ln_args = _inproj_kernel, [], ()
    else:
        body = _ln_inproj_kernel
        ln_specs = [_const_spec((1, d)), _const_spec((1, d))]
        ln_args = tuple(v.reshape(1, d).astype(F32) for v in entry_ln)
        out_shape.insert(0, jax.ShapeDtypeStruct((n, d), F32))
        out_specs.insert(0, rows)
    return pl.pallas_call(
        body,
        out_shape=tuple(out_shape),
        grid=(n // PROJ_ROWS,),
        in_specs=[rows] + ln_specs + weight_specs,
        out_specs=tuple(out_specs),
        compiler_params=_cparams(("parallel",)),
        name="in_proj",
    )(h2, *ln_args, *weights)


LOG2_E = math.log2(math.e)
PASS_LOG2_FLOOR = -151.0


HEADS_PER_STEP = LANES // SB_HEAD_DIM


def _per_head(x):
    lane = lax.broadcasted_iota(jnp.int32, x.shape, 1)
    return jnp.concatenate(
        [jnp.where((lane >= h * SB_HEAD_DIM) & (lane < (h + 1) * SB_HEAD_DIM), x, jnp.zeros_like(x))
         for h in range(HEADS_PER_STEP)], axis=0)


MASKED_SCORE = -1e30


def _neg_abs(x):
    bits = lax.bitcast_convert_type(x, jnp.uint32) | jnp.uint32(F32_SIGN_BIT)
    return lax.bitcast_convert_type(bits, F32)


def _sb_block(q, k_j, v_j, neg_tri2, carries, mask):
    nq, kk = q.shape[0], k_j.shape[0]
    z = lax.dot_general(q, _per_head(k_j), (((1,), (1,)), ((), ())), preferred_element_type=F32)
    scores, split = [], []
    for h in range(HEADS_PER_STEP):
        z_h = z[:, h * kk:(h + 1) * kk]
        if mask is not None:
            z_h = jnp.where(mask, z_h, MASKED_SCORE)
        drop = jnp.maximum(z_h, 0.0) + jnp.log2(1.0 + jnp.exp2(_neg_abs(z_h)))
        scores.append(z_h)
        split.append(drop.astype(BF16))
    log_pass = jnp.dot(jnp.concatenate(split, axis=0), neg_tri2, preferred_element_type=F32)
    w, new_carries = [], []
    for h in range(HEADS_PER_STEP):
        lp_h = log_pass[h * nq:(h + 1) * nq]
        w.append(jnp.exp2(scores[h] + lp_h + carries[h]).astype(BF16))
        new_carries.append(carries[h] + lp_h[:, 0:1])
    pv = jnp.dot(jnp.concatenate(w, axis=1), _per_head(v_j), preferred_element_type=F32)
    return pv, new_carries


def _attn_kernel(q_ref, k_ref, v_ref, u2_ref, o_ref):
    qi = pl.program_id(2)
    u2 = u2_ref[...]
    row = lax.broadcasted_iota(jnp.int32, (ATT_Q, ATT_K), 0)
    col = lax.broadcasted_iota(jnp.int32, (ATT_Q, ATT_K), 1)
    diag_mask = col < row
    blocks = [slice(g * LANES, (g + 1) * LANES) for g in range(ATT_LANE_BLOCKS)]
    q = [q_ref[0, :, lanes] for lanes in blocks]

    def sweep(j, accs, carries, mask):
        rows = pl.ds(pl.multiple_of(j * ATT_K, ATT_K), ATT_K)
        out = [_sb_block(q_g, k_ref[0, rows, lanes], v_ref[0, rows, lanes], u2, c_g, mask)
               for q_g, lanes, c_g in zip(q, blocks, carries)]
        return [acc + pv for acc, (pv, _) in zip(accs, out)], [c for _, c in out]

    zero = [[jnp.zeros((ATT_Q, 1), F32) for _ in range(HEADS_PER_STEP)] for _ in blocks]
    accs, carries = sweep(qi, [jnp.zeros((ATT_Q, LANES), F32) for _ in blocks], zero, diag_mask)
    no_prev = jnp.where(qi > 0, 0.0, MASKED_SCORE)
    accs, carries = sweep(jnp.maximum(qi - 1, 0), accs,
                          [[c + no_prev for c in c_g] for c_g in carries], None)

    def largest(carries):
        return jnp.max(functools.reduce(jnp.maximum, [c for c_g in carries for c in c_g]))

    def cond(state):
        j, _, _, live = state
        return jnp.logical_and(j >= 0, live > PASS_LOG2_FLOOR)

    def body(state):
        j, accs, carries, _ = state
        accs, carries = sweep(j, accs, carries, None)
        return j - 1, accs, carries, largest(carries)

    _, accs, _, _ = lax.while_loop(cond, body, (qi - 2, accs, carries, largest(carries)))
    o_ref[0] = jnp.concatenate(accs, axis=1).astype(o_ref.dtype)


def _sb_attention(qkv3):
    b, s, _ = qkv3.shape
    width = ATT_LANE_BLOCKS * LANES
    groups = SB_WIDTH // width
    neg_tri = -(jnp.arange(ATT_K)[:, None] >= jnp.arange(ATT_K)[None, :]).astype(BF16)
    u2 = neg_tri
    return pl.pallas_call(
        _attn_kernel,
        out_shape=jax.ShapeDtypeStruct((b, s, SB_WIDTH), BF16),
        grid=(b, groups, s // ATT_Q),
        in_specs=[pl.BlockSpec((1, ATT_Q, width), lambda bi, p, qi: (bi, qi, p)),
                  pl.BlockSpec((1, s, width), lambda bi, p, qi: (bi, 0, groups + p)),
                  pl.BlockSpec((1, s, width), lambda bi, p, qi: (bi, 0, 2 * groups + p)),
                  _const_spec((ATT_K, ATT_K))],
        out_specs=pl.BlockSpec((1, ATT_Q, width), lambda bi, p, qi: (bi, qi, p)),
        compiler_params=_cparams(("parallel", "parallel", "arbitrary")),
        name="sb_attention",
    )(qkv3, qkv3, qkv3, u2)


def _causal_conv(x, stage, w):
    t = x.shape[0]
    k = w.shape[0]
    stage[SUBLANES:, :] = x
    out = w[k - 1:k] * x
    for j in range(1, k):
        out = out + w[k - 1 - j:k - j] * stage[pl.ds(SUBLANES - j, t), :]
    stage[:SUBLANES, :] = x[t - SUBLANES:]
    return out


def _linear_scan(a, b, h0):
    t, c = a.shape
    a = a.reshape(t // SUBLANES, SUBLANES, c)
    b = b.reshape(t // SUBLANES, SUBLANES, c)
    sub = lax.broadcasted_iota(jnp.int32, a.shape, 1)
    d = 1
    while d < SUBLANES:
        keep = sub >= d
        a_prev = jnp.where(keep, pltpu.roll(a, d, axis=1), 1.0)
        b_prev = jnp.where(keep, pltpu.roll(b, d, axis=1), 0.0)
        b = a * b_prev + b
        a = a * a_prev
        d *= 2
    h = h0
    groups = []
    for g in range(t // SUBLANES):
        seg = b[g] + a[g] * h
        h = seg[SUBLANES - 1:]
        groups.append(seg)
    return jnp.concatenate(groups, axis=0)


def _mix_kernel(h_ref, ysb_ref, lin_ref, scb_ref, scp_ref, gsb_ref, glru_ref, gsc_ref, gelu_ref,
                lconv_w_ref, lconv_b_ref, wa_ref, ba_ref, wx_ref, bx_ref, lam_ref, scw_ref,
                wsb_ref, wlru_ref, wsc_ref, wout_ref, ln_g_ref, ln_b_ref,
                out_ref, lin_stage, sc_stage, h_state):
    t = MIX_ROWS

    @pl.when(pl.program_id(1) == 0)
    def _():
        lin_stage[:SUBLANES, :] = jnp.zeros((SUBLANES, LRU_WIDTH), F32)
        sc_stage[:SUBLANES, :] = jnp.zeros((SUBLANES, SC_WIDTH), F32)
        h_state[...] = jnp.zeros_like(h_state)

    u = _causal_conv(lin_ref[0], lin_stage, lconv_w_ref[...]) + lconv_b_ref[...]
    ub = u.astype(BF16)
    r = _sigmoid(jnp.dot(ub, wa_ref[...], preferred_element_type=F32) + ba_ref[...])
    i = _sigmoid(jnp.dot(ub, wx_ref[...], preferred_element_type=F32) + bx_ref[...])
    lam = lam_ref[...]
    softplus_neg_lam = jnp.maximum(-lam, 0.0) + jnp.log(1.0 + jnp.exp(-jnp.abs(lam)))
    log_a = (-LRU_C) * r * softplus_neg_lam
    a = jnp.exp(log_a)
    gap = 1.0 - a * a
    drive = jnp.where(gap > 0.0, gap * lax.rsqrt(gap), 0.0) * (i * u)
    hs = _linear_scan(a, drive, h_state[...])
    h_state[...] = hs[t - 1:]
    y_lru = gelu_ref[0] * hs.astype(BF16)

    y_sc = (scb_ref[0] * _causal_conv(scp_ref[0], sc_stage, scw_ref[...])).astype(BF16)

    def branch(y, w_ref):
        return jnp.dot(y, w_ref[...], preferred_element_type=F32).astype(BF16)

    merged = (gsb_ref[0] * branch(ysb_ref[0], wsb_ref) + glru_ref[0] * branch(y_lru, wlru_ref)
              + gsc_ref[0] * branch(y_sc, wsc_ref))
    mix = jnp.dot(merged, wout_ref[...], preferred_element_type=F32)
    out_ref[0] = _layer_norm(ALPHA * h_ref[0] + mix, ln_g_ref[...], ln_b_ref[...])


def _block_diag(w):
    hh, ii, jj = w.shape
    eye = jnp.eye(hh, dtype=w.dtype)
    return (eye[:, None, :, None] * w[:, :, None, :]).reshape(hh * ii, hh * jj)


def _mixer_tail(h3, ysb3, conv3, act3, p, l):
    b, s, d = h3.shape
    t = MIX_ROWS
    assert LRU_WIDTH == SC_WIDTH

    def row_spec(width, col):
        return pl.BlockSpec((1, t, width), lambda bi, si: (bi, si, col))

    vec = lambda v: v.reshape(1, -1).astype(F32)
    in_specs = [row_spec(d, 0), row_spec(SB_WIDTH, 0)]
    in_specs += [row_spec(LRU_WIDTH, c) for c in range(3)]
    in_specs += [row_spec(d, g) for g in range(3)]
    in_specs += [row_spec(LRU_WIDTH, 3 * d // LRU_WIDTH)]
    weights = [
        p["lru_conv_w"][l].astype(F32), vec(p["lru_conv_b"][l]),
        _block_diag(p["lru_wa"][l]).astype(BF16), vec(p["lru_ba"][l]),
        _block_diag(p["lru_wx"][l]).astype(BF16), vec(p["lru_bx"][l]),
        vec(p["lru_lambda"][l]), p["sc_conv_w"][l].astype(F32),
        p["w_branch_sb"][l].astype(BF16), p["w_branch_lru"][l].astype(BF16),
        p["w_branch_sc"][l].astype(BF16), p["w_out"][l].astype(BF16),
        vec(p["ln1_g"][l]), vec(p["ln1_b"][l]),
    ]
    in_specs += [_const_spec(w.shape) for w in weights]
    return pl.pallas_call(
        _mix_kernel,
        out_shape=jax.ShapeDtypeStruct((b, s, d), F32),
        grid=(b, s // t),
        in_specs=in_specs,
        out_specs=pl.BlockSpec((1, t, d), lambda bi, si: (bi, si, 0)),
        scratch_shapes=[pltpu.VMEM((SUBLANES + t, LRU_WIDTH), F32),
                        pltpu.VMEM((SUBLANES + t, SC_WIDTH), F32),
                        pltpu.VMEM((1, LRU_WIDTH), F32)],
        compiler_params=_cparams(("parallel", "arbitrary")),
        name="mixer_tail",
    )(h3, ysb3, *([conv3] * 3), *([act3] * 4), *weights)


def _first_argmax(vals, row):
    m = jnp.max(vals, axis=0, keepdims=True)
    idx = jnp.min(jnp.where(vals == m, row, vals.shape[0]), axis=0, keepdims=True)
    return m, idx


def _route_kernel(h_ref, w_ref, bias_ref, before_ref, info_ref, key_ref, counts_ref, running):
    @pl.when(pl.program_id(0) == 0)
    def _():
        running[...] = jnp.zeros_like(running)

    h = h_ref[...]
    h_hi = h.astype(BF16)
    h_lo = (h - h_hi.astype(F32)).astype(BF16)
    w = w_ref[...]
    w_hi = w.astype(BF16)
    w_lo = (w - w_hi.astype(F32)).astype(BF16)
    logits = jnp.dot(jnp.concatenate([h_hi, h_lo, h_hi], axis=1),
                     jnp.concatenate([w_hi, w_hi, w_lo], axis=0), preferred_element_type=F32)
    lt = logits.T
    neg = -jnp.inf
    experts = lt[:N_EXPERTS]
    groups = lt[N_EXPERTS:N_EXPERTS + SUBLANES]
    experts_b = experts + bias_ref[:N_EXPERTS]
    groups_b = groups + bias_ref[N_EXPERTS:N_EXPERTS + SUBLANES]
    row_g = lax.broadcasted_iota(jnp.int32, groups.shape, 0)
    row_e = lax.broadcasted_iota(jnp.int32, experts.shape, 0)
    is_group = row_g < N_GROUPS
    g_max = jnp.max(jnp.where(is_group, groups, neg), axis=0, keepdims=True)
    g_exp = jnp.where(is_group, jnp.exp(groups - g_max), 0.0)
    g_den = jnp.sum(g_exp, axis=0, keepdims=True)
    _, g_sel = _first_argmax(jnp.where(is_group, groups_b, neg), row_g)
    g_prob = jnp.sum(jnp.where(row_g == g_sel, g_exp, 0.0), axis=0, keepdims=True) / g_den
    cand = jnp.where(row_e // EXP_PER_GROUP == g_sel, experts_b, neg)
    _, e1 = _first_argmax(cand, row_e)
    _, e2 = _first_argmax(jnp.where(row_e == e1, neg, cand), row_e)
    l1 = jnp.sum(jnp.where(row_e == e1, experts, 0.0), axis=0, keepdims=True)
    l2 = jnp.sum(jnp.where(row_e == e2, experts, 0.0), axis=0, keepdims=True)
    m = jnp.maximum(l1, l2)
    x1 = jnp.exp(l1 - m)
    x2 = jnp.exp(l2 - m)
    scale = g_prob / (x1 + x2)
    w1 = x1 * scale
    w2 = x2 * scale
    first_low = e1 < e2
    a = jnp.where(first_low, e1, e2) - g_sel * EXP_PER_GROUP
    b = jnp.where(first_low, e2, e1) - g_sel * EXP_PER_GROUP
    cls = g_sel * PAIRS_PER_GROUP + ((a * (2 * EXP_PER_GROUP - 1 - a)) >> 1) + (b - a - 1)
    row_c = lax.broadcasted_iota(jnp.int32, (CLASS_ROWS, cls.shape[1]), 0)
    onehot = row_c == cls
    ones = jnp.where(onehot, 1.0, 0.0)
    earlier = jnp.dot(ones, before_ref[...], preferred_element_type=F32) + running[...]
    rank = jnp.sum(jnp.where(onehot, earlier, 0.0), axis=0, keepdims=True)
    running[...] += jnp.sum(ones, axis=1, keepdims=True)
    counts_ref[...] = running[...]
    key_ref[0] = (cls << KEY_SHIFT) + rank.astype(jnp.int32)
    row_w = lax.broadcasted_iota(jnp.int32, (LANES, cls.shape[1]), 0)
    w_rows = jnp.where(row_w == INFO_W_LOW, jnp.where(first_low, w1, w2), 0.0)
    w_rows = jnp.where(row_w == INFO_W_HIGH, jnp.where(first_low, w2, w1), w_rows)
    info_ref[...] = w_rows.T


def _route(h2, w_router, bias):
    n, d = h2.shape
    t = ROUTE_ROWS
    before = (jnp.arange(t)[:, None] < jnp.arange(t)[None, :]).astype(F32)
    return pl.pallas_call(
        _route_kernel,
        out_shape=(jax.ShapeDtypeStruct((n, LANES), F32),
                   jax.ShapeDtypeStruct((n // t, 1, t), jnp.int32),
                   jax.ShapeDtypeStruct((CLASS_ROWS, 1), F32)),
        grid=(n // t,),
        in_specs=[pl.BlockSpec((t, d), lambda i: (i, 0)),
                  _const_spec((d, LANES)), _const_spec((LANES, 1)), _const_spec((t, t))],
        out_specs=(pl.BlockSpec((t, LANES), lambda i: (i, 0)),
                   pl.BlockSpec((1, 1, t), lambda i: (i, 0, 0)),
                   _const_spec((CLASS_ROWS, 1))),
        scratch_shapes=[pltpu.VMEM((CLASS_ROWS, 1), F32)],
        compiler_params=_cparams(("arbitrary",)),
        name="moe_route",
    )(h2, w_router, bias, before)


def _slot_kernel(start_ref, key_ref, slot_ref):
    key = key_ref[...]
    cls = key >> KEY_SHIFT
    slot = key & ((1 << KEY_SHIFT) - 1)
    for c in range(N_CLASSES):
        slot = slot + jnp.where(cls == c, start_ref[c], 0)
    slot_ref[...] = slot


def _slots(class_start, key3):
    g, _, t = key3.shape
    return pl.pallas_call(
        _slot_kernel,
        out_shape=jax.ShapeDtypeStruct(key3.shape, jnp.int32),
        grid_spec=pltpu.PrefetchScalarGridSpec(
            num_scalar_prefetch=1, grid=(1,),
            in_specs=[pl.BlockSpec((g, 1, t), lambda i, s: (0, 0, 0))],
            out_specs=pl.BlockSpec((g, 1, t), lambda i, s: (0, 0, 0))),
        compiler_params=_cparams(("arbitrary",)),
        name="moe_slots",
    )(class_start, key3)


def _for_rows(n_rows, fn):
    def body(g, c):
        for j in range(ROW_COPY_UNROLL):
            fn(g * ROW_COPY_UNROLL + j, j % 2)
        return c

    lax.fori_loop(0, n_rows // ROW_COPY_UNROLL, body, 0)


def _for_row_groups(n_rows, fn):
    def body(g, c):
        for j in range(SUBLANES):
            fn(g, j)
        return c

    lax.fori_loop(0, n_rows // SUBLANES, body, 0)


def _pack_rows(x, info):
    half = x.shape[1] // 2
    bits = lax.bitcast_convert_type(x.astype(BF16).astype(F32), jnp.uint32)
    words = (bits[:, :half] & jnp.uint32(HIGH_HALF_WORD)) | (bits[:, half:] >> 16)
    return jnp.concatenate([words, lax.bitcast_convert_type(info, jnp.uint32)], axis=1)


def _unpack_rows(packed):
    half = packed.shape[1] - LANES
    words = packed[:, :half]
    hi = lax.bitcast_convert_type(words & jnp.uint32(HIGH_HALF_WORD), F32)
    lo = lax.bitcast_convert_type(words << 16, F32)
    return (jnp.concatenate([hi, lo], axis=1).astype(BF16),
            lax.bitcast_convert_type(packed[:, half:], F32))


def _dispatch_kernel(tile_end_ref, n_used_ref, slot_ref, h_ref, info_ref, xs_ref,
                     rows, zeros, sems, zsem):
    i = pl.program_id(0)
    last = pl.num_programs(0) - 1
    p = i % 2
    t = DISPATCH_ROWS
    te = EXPERT_ROWS

    def row_copy(buf, g, j):
        return pltpu.make_async_copy(rows.at[buf, g, pl.ds(j, 1)],
                                     xs_ref.at[pl.ds(slot_ref[0, 0, g * SUBLANES + j], 1)],
                                     sems.at[buf])

    @pl.when(i == 0)
    def _():
        zeros[...] = jnp.zeros_like(zeros)

        def zero_tile(tile):
            return pltpu.make_async_copy(zeros, xs_ref.at[pl.ds(tile * te, te)], zsem)

        prev_end = 0
        for c in range(N_CLASSES):
            end = tile_end_ref[c]

            @pl.when(end > prev_end)
            def _(end=end):
                cp = zero_tile(end - 1)
                cp.start()
                cp.wait()
            prev_end = end

        def zero_tail(tile, c):
            cp = zero_tile(tile)
            cp.start()
            cp.wait()
            return c

        lax.fori_loop(n_used_ref[0], xs_ref.shape[0] // te, zero_tail, 0)

    rows[p] = _pack_rows(h_ref[...], info_ref[...]).reshape(rows.shape[1:])
    _for_row_groups(t, lambda g, j: row_copy(p, g, j).start(priority=j % 2))

    def row_wait(buf):
        pltpu.make_async_copy(rows.at[buf, 0, pl.ds(0, 1)], xs_ref.at[pl.ds(0, 1)],
                              sems.at[buf]).wait()

    @pl.when(i > 0)
    def _():
        _for_rows(t, lambda r, q: row_wait(1 - p))

    @pl.when(i == last)
    def _():
        _for_rows(t, lambda r, q: row_wait(p))


def _dispatch(tile_end, n_used, slot3, h2, info, n_slots):
    n, d = h2.shape
    t = DISPATCH_ROWS
    return pl.pallas_call(
        _dispatch_kernel,
        out_shape=jax.ShapeDtypeStruct((n_slots, d // 2 + LANES), jnp.uint32),
        grid_spec=pltpu.PrefetchScalarGridSpec(
            num_scalar_prefetch=2, grid=(n // t,),
            in_specs=[pl.BlockSpec((1, 1, t), lambda i, *_: (i, 0, 0), memory_space=pltpu.SMEM),
                      pl.BlockSpec((t, d), lambda i, *_: (i, 0)),
                      pl.BlockSpec((t, LANES), lambda i, *_: (i, 0))],
            out_specs=pl.BlockSpec(memory_space=pl.ANY),
            scratch_shapes=[pltpu.VMEM((2, t // SUBLANES, SUBLANES, d // 2 + LANES), jnp.uint32),
                            pltpu.VMEM((EXPERT_ROWS, d // 2 + LANES), jnp.uint32),
                            pltpu.SemaphoreType.DMA((2,)), pltpu.SemaphoreType.DMA(())]),
        compiler_params=_cparams(("arbitrary",)),
        name="moe_dispatch",
    )(tile_end, n_used, slot3, h2, info)


def _expert_kernel(n_used_ref, src_ref, lo_ref, hi_ref, xs_ref, wg_lo, wg_hi, wu_lo, wu_hi,
                   wd_lo, wd_hi, ys_ref):
    del src_ref, lo_ref, hi_ref

    @pl.when(pl.program_id(0) < n_used_ref[0])
    def _():
        xb, info = _unpack_rows(xs_ref[...])
        y = None
        for wg, wu, wd, lane in ((wg_lo, wu_lo, wd_lo, INFO_W_LOW), (wg_hi, wu_hi, wd_hi, INFO_W_HIGH)):
            gate = jnp.dot(xb, wg[0], preferred_element_type=F32)
            up = jnp.dot(xb, wu[0], preferred_element_type=F32)
            act = gate * _sigmoid(gate) * up * info[:, lane:lane + 1]
            part = jnp.dot(act.astype(BF16), wd[0], preferred_element_type=F32)
            y = part if y is None else y + part
        ys_ref[...] = y

    @pl.when(pl.program_id(0) >= n_used_ref[0])
    def _():
        ys_ref[...] = jnp.zeros_like(ys_ref)


def _experts(tile_meta, xs, layer, w_gate_b, w_up_b, w_down_b):
    n_slots, cols = xs.shape
    d, f = w_gate_b.shape[2:]
    t = EXPERT_ROWS
    x_map = lambda i, n_used, src, lo, hi: (src[i], 0)
    lo_map = lambda i, n_used, src, lo, hi: (layer, lo[i], 0, 0)
    hi_map = lambda i, n_used, src, lo, hi: (layer, hi[i], 0, 0)
    grid_spec = pltpu.PrefetchScalarGridSpec(
        num_scalar_prefetch=4,
        grid=(n_slots // t,),
        in_specs=[pl.BlockSpec((t, cols), x_map),
                  pl.BlockSpec((None, 1, d, f), lo_map), pl.BlockSpec((None, 1, d, f), hi_map),
                  pl.BlockSpec((None, 1, d, f), lo_map), pl.BlockSpec((None, 1, d, f), hi_map),
                  pl.BlockSpec((None, 1, f, d), lo_map), pl.BlockSpec((None, 1, f, d), hi_map)],
        out_specs=pl.BlockSpec((t, d), lambda i, *_: (i, 0)),
    )
    return pl.pallas_call(
        _expert_kernel,
        out_shape=jax.ShapeDtypeStruct((n_slots, d), F32),
        grid_spec=grid_spec,
        compiler_params=_cparams(("arbitrary",)),
        name="moe_experts",
    )(*tile_meta, xs, w_gate_b, w_gate_b, w_up_b, w_up_b, w_down_b, w_down_b)


def _combine_kernel(slot_ref, ys_ref, h_ref, ln_g_ref, ln_b_ref, o_ref, buf, sems):
    i = pl.program_id(0)
    tiles = pl.num_programs(0) - 1
    p = i % 2
    t = DISPATCH_ROWS

    @pl.when(i < tiles)
    def _():
        _for_row_groups(t, lambda g, j: pltpu.make_async_copy(
            ys_ref.at[pl.ds(slot_ref[0, 0, g * SUBLANES + j], 1)], buf.at[p, g, pl.ds(j, 1)],
            sems.at[p]).start(priority=j % 2))

    @pl.when(i == 0)
    def _():
        o_ref[...] = jnp.zeros_like(o_ref)

    @pl.when(i > 0)
    def _():
        _for_rows(t, lambda r, q: pltpu.make_async_copy(
            ys_ref.at[pl.ds(0, 1)], buf.at[1 - p, 0, pl.ds(0, 1)], sems.at[1 - p]).wait())
        o_ref[...] = _layer_norm(ALPHA * h_ref[...] + buf[1 - p].reshape(o_ref.shape),
                                 ln_g_ref[...], ln_b_ref[...])


def _combine(slot3, ys, h2, ln_g, ln_b):
    n, d = h2.shape
    t = DISPATCH_ROWS
    tiles = n // t
    done = lambda i: (jnp.maximum(i - 1, 0), 0)
    return pl.pallas_call(
        _combine_kernel,
        out_shape=jax.ShapeDtypeStruct((n, d), F32),
        grid=(tiles + 1,),
        in_specs=[pl.BlockSpec((1, 1, t), lambda i: (jnp.minimum(i, tiles - 1), 0, 0),
                               memory_space=pltpu.SMEM),
                  pl.BlockSpec(memory_space=pl.ANY),
                  pl.BlockSpec((t, d), done), _const_spec((1, d)), _const_spec((1, d))],
        out_specs=pl.BlockSpec((t, d), done),
        scratch_shapes=[pltpu.VMEM((2, t // SUBLANES, SUBLANES, d), F32),
                        pltpu.SemaphoreType.DMA((2,))],
        compiler_params=_cparams(("arbitrary",)),
        name="moe_combine",
    )(slot3, ys, h2, ln_g.reshape(1, d).astype(F32), ln_b.reshape(1, d).astype(F32))


def _pair_tables():
    lo, hi = [], []
    for g in range(N_GROUPS):
        for a in range(EXP_PER_GROUP):
            for b in range(a + 1, EXP_PER_GROUP):
                lo.append(g * EXP_PER_GROUP + a)
                hi.append(g * EXP_PER_GROUP + b)
    return jnp.asarray(lo, jnp.int32), jnp.asarray(hi, jnp.int32)


def _tile_plan(counts, n_tiles):
    t = EXPERT_ROWS
    counts = counts[:N_CLASSES, 0].astype(jnp.int32)
    tiles_per_class = (counts + t - 1) // t
    tile_end = jnp.cumsum(tiles_per_class)
    class_start = (tile_end - tiles_per_class) * t
    n_used = tile_end[-1:]
    src = jnp.minimum(jnp.arange(n_tiles, dtype=jnp.int32), n_used - 1)
    tile_class = jnp.sum((tile_end[None, :] <= src[:, None]).astype(jnp.int32), axis=1)
    pair_lo, pair_hi = _pair_tables()
    return class_start, tile_end, (n_used, src, pair_lo[tile_class], pair_hi[tile_class])


def kernel(x, ln_in_g, ln_in_b, w_in, gate_b, lru_conv_w, lru_conv_b, lru_wa, lru_ba, lru_wx, lru_bx, lru_lambda, sc_conv_w, w_branch_sb, w_branch_lru, w_branch_sc, w_out, ln1_g, ln1_b, w_group, group_bias, w_expert_router, expert_bias, w_gate, w_up, w_down, ln2_g, ln2_b):
    b, s, d = x.shape
    n = b * s
    params = dict(gate_b=gate_b, lru_conv_w=lru_conv_w, lru_conv_b=lru_conv_b, lru_wa=lru_wa,
                  lru_ba=lru_ba, lru_wx=lru_wx, lru_bx=lru_bx, lru_lambda=lru_lambda,
                  sc_conv_w=sc_conv_w, w_branch_sb=w_branch_sb, w_branch_lru=w_branch_lru,
                  w_branch_sc=w_branch_sc, w_out=w_out, ln1_g=ln1_g, ln1_b=ln1_b)
    w_in_b, w_gate_b, w_up_b, w_down_b = (w.astype(BF16) for w in (w_in, w_gate, w_up, w_down))
    h = x.reshape(n, d)
    for l in range(w_in.shape[0]):
        if l == 0:
            h, qkv, conv_in, act = _inproj(h, w_in_b, l, gate_b[l], entry_ln=(ln_in_g, ln_in_b))
        else:
            h, qkv, conv_in, act = _inproj(ys, w_in_b, l, gate_b[l], slots=slot3,
                                           finish=(h1, ln2_g[l - 1], ln2_b[l - 1]))
        y_sb = _sb_attention(qkv.reshape(b, s, -1))
        h1 = _mixer_tail(h.reshape(b, s, d), y_sb, conv_in.reshape(b, s, -1),
                         act.reshape(b, s, -1), params, l)
        h1 = h1.reshape(n, d)
        pad = LANES - N_EXPERTS - N_GROUPS
        w_router = jnp.concatenate(
            [w_expert_router[l], w_group[l], jnp.zeros((d, pad), F32)], axis=1).astype(F32)
        bias = jnp.concatenate(
            [expert_bias[l], group_bias[l], jnp.zeros((pad,), F32)]).reshape(LANES, 1).astype(F32)
        info, key3, counts = _route(h1, w_router, bias)
        n_tiles = n // EXPERT_ROWS + N_CLASSES
        class_start, tile_end, tile_meta = _tile_plan(counts, n_tiles)
        slot3 = _slots(class_start, key3)
        xs = _dispatch(tile_end, tile_meta[0], slot3, h1, info, n_tiles * EXPERT_ROWS)
        ys = _experts(tile_meta, xs, l, w_gate_b, w_up_b, w_down_b)
    return _combine(slot3, ys, h1, ln2_g[-1], ln2_b[-1]).reshape(b, s, d)
```

```python
import functools
import math

import jax
import jax.numpy as jnp
from jax import lax
from jax.experimental import pallas as pl
from jax.experimental.pallas import tpu as pltpu

F32 = jnp.float32
BF16 = jnp.bfloat16

SB_HEADS = 8
SB_HEAD_DIM = 64
SB_WIDTH = SB_HEADS * SB_HEAD_DIM
LRU_WIDTH = 512
LRU_BLOCKS = 8
LRU_C = 8.0
SC_WIDTH = 512
N_GROUPS = 4
EXP_PER_GROUP = 4
N_EXPERTS = N_GROUPS * EXP_PER_GROUP
D_EXPERT = 512
DEPTH = 2
ALPHA = (2 * DEPTH) ** 0.25
LN_EPS = 1e-5

LANES = 128
SUBLANES = 8
VMEM_LIMIT_BYTES = 56 * 1024 * 1024
F32_SIGN_BIT = 0x80000000
HIGH_HALF_WORD = 0xFFFF0000

PROJ_ROWS = 512
PROJ_CHUNK = 512
ATT_Q = 256
ATT_K = 256
ATT_LANE_BLOCKS = 4
MIX_ROWS = 512
ROUTE_ROWS = 512
DISPATCH_ROWS = 512
EXPERT_ROWS = 256
ROW_COPY_UNROLL = 8

PAIRS_PER_GROUP = EXP_PER_GROUP * (EXP_PER_GROUP - 1) // 2
N_CLASSES = N_GROUPS * PAIRS_PER_GROUP
CLASS_ROWS = 32
INFO_W_LOW, INFO_W_HIGH = 0, 1
KEY_SHIFT = 16


def _cparams(sem):
    return pltpu.CompilerParams(dimension_semantics=sem, vmem_limit_bytes=VMEM_LIMIT_BYTES)


def _const_spec(shape):
    nd = len(shape)
    return pl.BlockSpec(shape, lambda *_: (0,) * nd)


def _layer_norm(y, g, b):
    mu = jnp.mean(y, axis=-1, keepdims=True)
    d = y - mu
    var = jnp.mean(d * d, axis=-1, keepdims=True)
    return d * lax.rsqrt(var + LN_EPS) * g + b


QKV_COLS = 3 * SB_WIDTH
BRANCH_COLS = 2 * LRU_WIDTH + 3 * SC_WIDTH


def _sigmoid(x):
    return 0.5 * jnp.tanh(0.5 * x) + 0.5


def _gelu_tanh(x):
    return 0.5 * x * (1.0 + jnp.tanh(math.sqrt(2.0 / math.pi) * (x + 0.044715 * (x * x * x))))


def _project(h, w_ref, gate_b_ref, qkv_ref, conv_ref, act_ref, after_chunk=lambda k: None):
    hb = h.astype(BF16)
    c = PROJ_CHUNK

    def proj(chunk):
        out = jnp.dot(hb, w_ref[:, chunk * c:(chunk + 1) * c], preferred_element_type=F32)
        after_chunk(chunk)
        return out

    n_qkv = QKV_COLS // c
    for j in range(n_qkv):
        acc = proj(j)
        if j * c < SB_WIDTH:
            acc = acc * (SB_HEAD_DIM ** -0.5 * LOG2_E)
        qkv_ref[:, j * c:(j + 1) * c] = acc.astype(BF16)
    lru_in, lru_gate, sc_b, sc_c, sc_h = range(n_qkv, n_qkv + 5)
    conv_ref[:, 0:c] = proj(lru_in)
    conv_ref[:, c:2 * c] = proj(sc_b)
    conv_ref[:, 2 * c:3 * c] = proj(sc_c) * proj(sc_h)
    for j in range(gate_b_ref.shape[1] // c):
        acc = proj(sc_h + 1 + j) + gate_b_ref[:, j * c:(j + 1) * c]
        act_ref[:, j * c:(j + 1) * c] = _sigmoid(acc).astype(BF16)
    act_ref[:, gate_b_ref.shape[1]:] = _gelu_tanh(proj(lru_gate)).astype(BF16)


def _inproj_kernel(h_ref, w_ref, gate_b_ref, qkv_ref, conv_ref, act_ref):
    _project(h_ref[...], w_ref, gate_b_ref, qkv_ref, conv_ref, act_ref)


def _ln_inproj_kernel(x_ref, g_ref, b_ref, w_ref, gate_b_ref, h_ref, qkv_ref, conv_ref, act_ref):
    h = _layer_norm(x_ref[...], g_ref[...], b_ref[...])
    h_ref[...] = h
    _project(h, w_ref, gate_b_ref, qkv_ref, conv_ref, act_ref)


def _gather_inproj_kernel(slot_ref, slot_next_ref, ys_ref, res_ref, ln_g_ref, ln_b_ref,
                          w_ref, gate_b_ref, h_ref, qkv_ref, conv_ref, act_ref, rows, sems):
    i = pl.program_id(0)
    last = pl.num_programs(0) - 1
    p = i % 2

    def fetch(slots, buf):
        def group(g, c):
            for j in range(SUBLANES):
                pltpu.make_async_copy(ys_ref.at[pl.ds(slots[0, 0, g * SUBLANES + j], 1)],
                                      rows.at[buf, g, pl.ds(j, 1)],
                                      sems.at[buf]).start(priority=j % 2)
            return c

        lax.fori_loop(0, PROJ_ROWS // SUBLANES, group, 0)

    @pl.when(i == 0)
    def _():
        fetch(slot_ref, 0)

    _for_rows(PROJ_ROWS, lambda r, q: pltpu.make_async_copy(
        ys_ref.at[pl.ds(0, 1)], rows.at[p, 0, pl.ds(0, 1)], sems.at[p]).wait())
    h = _layer_norm(ALPHA * res_ref[...] + rows[p].reshape(PROJ_ROWS, rows.shape[-1]),
                    ln_g_ref[...], ln_b_ref[...])
    h_ref[...] = h

    n_groups = PROJ_ROWS // SUBLANES
    n_chunks = w_ref.shape[1] // PROJ_CHUNK

    def start_some(k):
        for g in range(k * n_groups // n_chunks, (k + 1) * n_groups // n_chunks):
            for j in range(SUBLANES):
                pltpu.make_async_copy(
                    ys_ref.at[pl.ds(slot_next_ref[0, 0, g * SUBLANES + j], 1)],
                    rows.at[1 - p, g, pl.ds(j, 1)], sems.at[1 - p]).start(priority=j % 2)

    _project(h, w_ref, gate_b_ref, qkv_ref, conv_ref, act_ref, after_chunk=start_some)

    @pl.when(i == last)
    def _():
        _for_rows(PROJ_ROWS, lambda r, q: pltpu.make_async_copy(
            ys_ref.at[pl.ds(0, 1)], rows.at[1 - p, 0, pl.ds(0, 1)], sems.at[1 - p]).wait())


def _inproj(h2, w_in_b, layer, gate_b, entry_ln=None, slots=None, finish=None):
    d = h2.shape[1]
    n = h2.shape[0] if slots is None else slots.size
    cols = w_in_b.shape[2]
    gate_cols = gate_b.size
    assert LRU_WIDTH == SC_WIDTH == PROJ_CHUNK and cols == QKV_COLS + BRANCH_COLS + gate_cols
    widths = (QKV_COLS, 3 * PROJ_CHUNK, gate_cols + LRU_WIDTH)
    dtypes = (BF16, F32, BF16)
    rows = pl.BlockSpec((PROJ_ROWS, d), lambda i: (i, 0))
    weight_specs = [pl.BlockSpec((None, d, cols), lambda i: (layer, 0, 0),
                                 pipeline_mode=pl.Buffered(1)),
                    _const_spec((1, gate_cols))]
    weights = (w_in_b, gate_b.reshape(1, gate_cols).astype(F32))
    out_shape = [jax.ShapeDtypeStruct((n, w), t) for w, t in zip(widths, dtypes)]
    out_specs = [pl.BlockSpec((PROJ_ROWS, w), lambda i: (i, 0)) for w in widths]
    if slots is not None:
        assert DISPATCH_ROWS == PROJ_ROWS
        steps = n // PROJ_ROWS
        slot_spec = lambda index: pl.BlockSpec((1, 1, PROJ_ROWS), index, memory_space=pltpu.SMEM)
        out_shape.insert(0, jax.ShapeDtypeStruct((n, d), F32))
        out_specs.insert(0, rows)
        return pl.pallas_call(
            _gather_inproj_kernel,
            out_shape=tuple(out_shape),
            grid=(steps,),
            in_specs=[slot_spec(lambda i: (i, 0, 0)),
                      slot_spec(lambda i: (jnp.minimum(i + 1, steps - 1), 0, 0)),
                      pl.BlockSpec(memory_space=pl.ANY), rows,
                      _const_spec((1, d)), _const_spec((1, d))] + weight_specs,
            out_specs=tuple(out_specs),
            scratch_shapes=[pltpu.VMEM((2, PROJ_ROWS // SUBLANES, SUBLANES, d), F32),
                            pltpu.SemaphoreType.DMA((2,))],
            compiler_params=_cparams(("arbitrary",)),
            name="in_proj_gather",
        )(slots, slots, h2, finish[0], finish[1].reshape(1, d).astype(F32),
          finish[2].reshape(1, d).astype(F32), *weights)
    if entry_ln is None:
        body, ln_specs, ln_args = _inproj_kernel, [], ()
    else:
        body = _ln_inproj_kernel
        ln_specs = [_const_spec((1, d)), _const_spec((1, d))]
        ln_args = tuple(v.reshape(1, d).astype(F32) for v in entry_ln)
        out_shape.insert(0, jax.ShapeDtypeStruct((n, d), F32))
        out_specs.insert(0, rows)
    return pl.pallas_call(
        body,
        out_shape=tuple(out_shape),
        grid=(n // PROJ_ROWS,),
        in_specs=[rows] + ln_specs + weight_specs,
        out_specs=tuple(out_specs),
        compiler_params=_cparams(("parallel",)),
        name="in_proj",
    )(h2, *ln_args, *weights)


LOG2_E = math.log2(math.e)
PASS_LOG2_FLOOR = -151.0


HEADS_PER_STEP = LANES // SB_HEAD_DIM


def _per_head(x):
    lane = lax.broadcasted_iota(jnp.int32, x.shape, 1)
    return jnp.concatenate(
        [jnp.where((lane >= h * SB_HEAD_DIM) & (lane < (h + 1) * SB_HEAD_DIM), x, jnp.zeros_like(x))
         for h in range(HEADS_PER_STEP)], axis=0)


MASKED_SCORE = -1e30


def _neg_abs(x):
    bits = lax.bitcast_convert_type(x, jnp.uint32) | jnp.uint32(F32_SIGN_BIT)
    return lax.bitcast_convert_type(bits, F32)


def _sb_block(q, k_j, v_j, neg_tri2, carries, mask):
    nq, kk = q.shape[0], k_j.shape[0]
    z = lax.dot_general(q, _per_head(k_j), (((1,), (1,)), ((), ())), preferred_element_type=F32)
    scores, split = [], []
    for h in range(HEADS_PER_STEP):
        z_h = z[:, h * kk:(h + 1) * kk]
        if mask is not None:
            z_h = jnp.where(mask, z_h, MASKED_SCORE)
        drop = jnp.maximum(z_h, 0.0) + jnp.log2(1.0 + jnp.exp2(_neg_abs(z_h)))
        scores.append(z_h)
        split.append(drop.astype(BF16))
    log_pass = jnp.dot(jnp.concatenate(split, axis=0), neg_tri2, preferred_element_type=F32)
    w, new_carries = [], []
    for h in range(HEADS_PER_STEP):
        lp_h = log_pass[h * nq:(h + 1) * nq]
        w.append(jnp.exp2(scores[h] + lp_h + carries[h]).astype(BF16))
        new_carries.append(carries[h] + lp_h[:, 0:1])
    pv = jnp.dot(jnp.concatenate(w, axis=1), _per_head(v_j), preferred_element_type=F32)
    return pv, new_carries


def _attn_kernel(q_ref, k_ref, v_ref, u2_ref, o_ref):
    qi = pl.program_id(2)
    u2 = u2_ref[...]
    row = lax.broadcasted_iota(jnp.int32, (ATT_Q, ATT_K), 0)
    col = lax.broadcasted_iota(jnp.int32, (ATT_Q, ATT_K), 1)
    diag_mask = col < row
    blocks = [slice(g * LANES, (g + 1) * LANES) for g in range(ATT_LANE_BLOCKS)]
    q = [q_ref[0, :, lanes] for lanes in blocks]

    def sweep(j, accs, carries, mask):
        rows = pl.ds(pl.multiple_of(j * ATT_K, ATT_K), ATT_K)
        out = [_sb_block(q_g, k_ref[0, rows, lanes], v_ref[0, rows, lanes], u2, c_g, mask)
               for q_g, lanes, c_g in zip(q, blocks, carries)]
        return [acc + pv for acc, (pv, _) in zip(accs, out)], [c for _, c in out]

    zero = [[jnp.zeros((ATT_Q, 1), F32) for _ in range(HEADS_PER_STEP)] for _ in blocks]
    accs, carries = sweep(qi, [jnp.zeros((ATT_Q, LANES), F32) for _ in blocks], zero, diag_mask)
    no_prev = jnp.where(qi > 0, 0.0, MASKED_SCORE)
    accs, carries = sweep(jnp.maximum(qi - 1, 0), accs,
                          [[c + no_prev for c in c_g] for c_g in carries], None)

    def largest(carries):
        return jnp.max(functools.reduce(jnp.maximum, [c for c_g in carries for c in c_g]))

    def cond(state):
        j, _, _, live = state
        return jnp.logical_and(j >= 0, live > PASS_LOG2_FLOOR)

    def body(state):
        j, accs, carries, _ = state
        accs, carries = sweep(j, accs, carries, None)
        return j - 1, accs, carries, largest(carries)

    _, accs, _, _ = lax.while_loop(cond, body, (qi - 2, accs, carries, largest(carries)))
    o_ref[0] = jnp.concatenate(accs, axis=1).astype(o_ref.dtype)


def _sb_attention(qkv3):
    b, s, _ = qkv3.shape
    width = ATT_LANE_BLOCKS * LANES
    groups = SB_WIDTH // width
    neg_tri = -(jnp.arange(ATT_K)[:, None] >= jnp.arange(ATT_K)[None, :]).astype(BF16)
    u2 = neg_tri
    return pl.pallas_call(
        _attn_kernel,
        out_shape=jax.ShapeDtypeStruct((b, s, SB_WIDTH), BF16),
        grid=(b, groups, s // ATT_Q),
        in_specs=[pl.BlockSpec((1, ATT_Q, width), lambda bi, p, qi: (bi, qi, p)),
                  pl.BlockSpec((1, s, width), lambda bi, p, qi: (bi, 0, groups + p)),
                  pl.BlockSpec((1, s, width), lambda bi, p, qi: (bi, 0, 2 * groups + p)),
                  _const_spec((ATT_K, ATT_K))],
        out_specs=pl.BlockSpec((1, ATT_Q, width), lambda bi, p, qi: (bi, qi, p)),
        compiler_params=_cparams(("parallel", "parallel", "arbitrary")),
        name="sb_attention",
    )(qkv3, qkv3, qkv3, u2)


def _causal_conv(x, stage, w):
    t = x.shape[0]
    k = w.shape[0]
    stage[SUBLANES:, :] = x
    out = w[k - 1:k] * x
    for j in range(1, k):
        out = out + w[k - 1 - j:k - j] * stage[pl.ds(SUBLANES - j, t), :]
    stage[:SUBLANES, :] = x[t - SUBLANES:]
    return out


def _linear_scan(a, b, h0):
    t, c = a.shape
    a = a.reshape(t // SUBLANES, SUBLANES, c)
    b = b.reshape(t // SUBLANES, SUBLANES, c)
    sub = lax.broadcasted_iota(jnp.int32, a.shape, 1)
    d = 1
    while d < SUBLANES:
        keep = sub >= d
        a_prev = jnp.where(keep, pltpu.roll(a, d, axis=1), 1.0)
        b_prev = jnp.where(keep, pltpu.roll(b, d, axis=1), 0.0)
        b = a * b_prev + b
        a = a * a_prev
        d *= 2
    h = h0
    groups = []
    for g in range(t // SUBLANES):
        seg = b[g] + a[g] * h
        h = seg[SUBLANES - 1:]
        groups.append(seg)
    return jnp.concatenate(groups, axis=0)


def _mix_kernel(h_ref, ysb_ref, lin_ref, scb_ref, scp_ref, gsb_ref, glru_ref, gsc_ref, gelu_ref,
                lconv_w_ref, lconv_b_ref, wa_ref, ba_ref, wx_ref, bx_ref, lam_ref, scw_ref,
                wsb_ref, wlru_ref, wsc_ref, wout_ref, ln_g_ref, ln_b_ref,
                out_ref, lin_stage, sc_stage, h_state):
    t = MIX_ROWS

    @pl.when(pl.program_id(1) == 0)
    def _():
        lin_stage[:SUBLANES, :] = jnp.zeros((SUBLANES, LRU_WIDTH), F32)
        sc_stage[:SUBLANES, :] = jnp.zeros((SUBLANES, SC_WIDTH), F32)
        h_state[...] = jnp.zeros_like(h_state)

    u = _causal_conv(lin_ref[0], lin_stage, lconv_w_ref[...]) + lconv_b_ref[...]
    ub = u.astype(BF16)
    r = _sigmoid(jnp.dot(ub, wa_ref[...], preferred_element_type=F32) + ba_ref[...])
    i = _sigmoid(jnp.dot(ub, wx_ref[...], preferred_element_type=F32) + bx_ref[...])
    lam = lam_ref[...]
    softplus_neg_lam = jnp.maximum(-lam, 0.0) + jnp.log(1.0 + jnp.exp(-jnp.abs(lam)))
    log_a = (-LRU_C) * r * softplus_neg_lam
    a = jnp.exp(log_a)
    gap = 1.0 - a * a
    drive = jnp.where(gap > 0.0, gap * lax.rsqrt(gap), 0.0) * (i * u)
    hs = _linear_scan(a, drive, h_state[...])
    h_state[...] = hs[t - 1:]
    y_lru = gelu_ref[0] * hs.astype(BF16)

    y_sc = (scb_ref[0] * _causal_conv(scp_ref[0], sc_stage, scw_ref[...])).astype(BF16)

    def branch(y, w_ref):
        return jnp.dot(y, w_ref[...], preferred_element_type=F32).astype(BF16)

    merged = (gsb_ref[0] * branch(ysb_ref[0], wsb_ref) + glru_ref[0] * branch(y_lru, wlru_ref)
              + gsc_ref[0] * branch(y_sc, wsc_ref))
    mix = jnp.dot(merged, wout_ref[...], preferred_element_type=F32)
    out_ref[0] = _layer_norm(ALPHA * h_ref[0] + mix, ln_g_ref[...], ln_b_ref[...])


def _block_diag(w):
    hh, ii, jj = w.shape
    eye = jnp.eye(hh, dtype=w.dtype)
    return (eye[:, None, :, None] * w[:, :, None, :]).reshape(hh * ii, hh * jj)


def _mixer_tail(h3, ysb3, conv3, act3, p, l):
    b, s, d = h3.shape
    t = MIX_ROWS
    assert LRU_WIDTH == SC_WIDTH

    def row_spec(width, col):
        return pl.BlockSpec((1, t, width), lambda bi, si: (bi, si, col))

    vec = lambda v: v.reshape(1, -1).astype(F32)
    in_specs = [row_spec(d, 0), row_spec(SB_WIDTH, 0)]
    in_specs += [row_spec(LRU_WIDTH, c) for c in range(3)]
    in_specs += [row_spec(d, g) for g in range(3)]
    in_specs += [row_spec(LRU_WIDTH, 3 * d // LRU_WIDTH)]
    weights = [
        p["lru_conv_w"][l].astype(F32), vec(p["lru_conv_b"][l]),
        _block_diag(p["lru_wa"][l]).astype(BF16), vec(p["lru_ba"][l]),
        _block_diag(p["lru_wx"][l]).astype(BF16), vec(p["lru_bx"][l]),
        vec(p["lru_lambda"][l]), p["sc_conv_w"][l].astype(F32),
        p["w_branch_sb"][l].astype(BF16), p["w_branch_lru"][l].astype(BF16),
        p["w_branch_sc"][l].astype(BF16), p["w_out"][l].astype(BF16),
        vec(p["ln1_g"][l]), vec(p["ln1_b"][l]),
    ]
    in_specs += [_const_spec(w.shape) for w in weights]
    return pl.pallas_call(
        _mix_kernel,
        out_shape=jax.ShapeDtypeStruct((b, s, d), F32),
        grid=(b, s // t),
        in_specs=in_specs,
        out_specs=pl.BlockSpec((1, t, d), lambda bi, si: (bi, si, 0)),
        scratch_shapes=[pltpu.VMEM((SUBLANES + t, LRU_WIDTH), F32),
                        pltpu.VMEM((SUBLANES + t, SC_WIDTH), F32),
                        pltpu.VMEM((1, LRU_WIDTH), F32)],
        compiler_params=_cparams(("parallel", "arbitrary")),
        name="mixer_tail",
    )(h3, ysb3, *([conv3] * 3), *([act3] * 4), *weights)


def _first_argmax(vals, row):
    m = jnp.max(vals, axis=0, keepdims=True)
    idx = jnp.min(jnp.where(vals == m, row, vals.shape[0]), axis=0, keepdims=True)
    return m, idx


def _route_kernel(h_ref, w_ref, bias_ref, before_ref, info_ref, key_ref, counts_ref, running):
    @pl.when(pl.program_id(0) == 0)
    def _():
        running[...] = jnp.zeros_like(running)

    h = h_ref[...]
    h_hi = h.astype(BF16)
    h_lo = (h - h_hi.astype(F32)).astype(BF16)
    w = w_ref[...]
    w_hi = w.astype(BF16)
    w_lo = (w - w_hi.astype(F32)).astype(BF16)
    logits = jnp.dot(jnp.concatenate([h_hi, h_lo, h_hi], axis=1),
                     jnp.concatenate([w_hi, w_hi, w_lo], axis=0), preferred_element_type=F32)
    lt = logits.T
    neg = -jnp.inf
    experts = lt[:N_EXPERTS]
    groups = lt[N_EXPERTS:N_EXPERTS + SUBLANES]
    experts_b = experts + bias_ref[:N_EXPERTS]
    groups_b = groups + bias_ref[N_EXPERTS:N_EXPERTS + SUBLANES]
    row_g = lax.broadcasted_iota(jnp.int32, groups.shape, 0)
    row_e = lax.broadcasted_iota(jnp.int32, experts.shape, 0)
    is_group = row_g < N_GROUPS
    g_max = jnp.max(jnp.where(is_group, groups, neg), axis=0, keepdims=True)
    g_exp = jnp.where(is_group, jnp.exp(groups - g_max), 0.0)
    g_den = jnp.sum(g_exp, axis=0, keepdims=True)
    _, g_sel = _first_argmax(jnp.where(is_group, groups_b, neg), row_g)
    g_prob = jnp.sum(jnp.where(row_g == g_sel, g_exp, 0.0), axis=0, keepdims=True) / g_den
    cand = jnp.where(row_e // EXP_PER_GROUP == g_sel, experts_b, neg)
    _, e1 = _first_argmax(cand, row_e)
    _, e2 = _first_argmax(jnp.where(row_e == e1, neg, cand), row_e)
    l1 = jnp.sum(jnp.where(row_e == e1, experts, 0.0), axis=0, keepdims=True)
    l2 = jnp.sum(jnp.where(row_e == e2, experts, 0.0), axis=0, keepdims=True)
    m = jnp.maximum(l1, l2)
    x1 = jnp.exp(l1 - m)
    x2 = jnp.exp(l2 - m)
    scale = g_prob / (x1 + x2)
    w1 = x1 * scale
    w2 = x2 * scale
    first_low = e1 < e2
    a = jnp.where(first_low, e1, e2) - g_sel * EXP_PER_GROUP
    b = jnp.where(first_low, e2, e1) - g_sel * EXP_PER_GROUP
    cls = g_sel * PAIRS_PER_GROUP + ((a * (2 * EXP_PER_GROUP - 1 - a)) >> 1) + (b - a - 1)
    row_c = lax.broadcasted_iota(jnp.int32, (CLASS_ROWS, cls.shape[1]), 0)
    onehot = row_c == cls
    ones = jnp.where(onehot, 1.0, 0.0)
    earlier = jnp.dot(ones, before_ref[...], preferred_element_type=F32) + running[...]
    rank = jnp.sum(jnp.where(onehot, earlier, 0.0), axis=0, keepdims=True)
    running[...] += jnp.sum(ones, axis=1, keepdims=True)
    counts_ref[...] = running[...]
    key_ref[0] = (cls << KEY_SHIFT) + rank.astype(jnp.int32)
    row_w = lax.broadcasted_iota(jnp.int32, (LANES, cls.shape[1]), 0)
    w_rows = jnp.where(row_w == INFO_W_LOW, jnp.where(first_low, w1, w2), 0.0)
    w_rows = jnp.where(row_w == INFO_W_HIGH, jnp.where(first_low, w2, w1), w_rows)
    info_ref[...] = w_rows.T


def _route(h2, w_router, bias):
    n, d = h2.shape
    t = ROUTE_ROWS
    assert n <= 1 << KEY_SHIFT
    before =(jnp.arange(t)[:, None] < jnp.arange(t)[None, :]).astype(F32)
    return pl.pallas_call(
        _route_kernel,
        out_shape=(jax.ShapeDtypeStruct((n, LANES), F32),
                   jax.ShapeDtypeStruct((n // t, 1, t), jnp.int32),
                   jax.ShapeDtypeStruct((CLASS_ROWS, 1), F32)),
        grid=(n // t,),
        in_specs=[pl.BlockSpec((t, d), lambda i: (i, 0)),
                  _const_spec((d, LANES)), _const_spec((LANES, 1)), _const_spec((t, t))],
        out_specs=(pl.BlockSpec((t, LANES), lambda i: (i, 0)),
                   pl.BlockSpec((1, 1, t), lambda i: (i, 0, 0)),
                   _const_spec((CLASS_ROWS, 1))),
        scratch_shapes=[pltpu.VMEM((CLASS_ROWS, 1), F32)],
        compiler_params=_cparams(("arbitrary",)),
        name="moe_route",
    )(h2, w_router, bias, before)


def _slot_kernel(start_ref, key_ref, slot_ref):
    key = key_ref[...]
    cls = key >> KEY_SHIFT
    slot = key & ((1 << KEY_SHIFT) - 1)
    for c in range(N_CLASSES):
        slot = slot + jnp.where(cls == c, start_ref[c], 0)
    slot_ref[...] = slot


def _slots(class_start, key3):
    g, _, t = key3.shape
    return pl.pallas_call(
        _slot_kernel,
        out_shape=jax.ShapeDtypeStruct(key3.shape, jnp.int32),
        grid_spec=pltpu.PrefetchScalarGridSpec(
            num_scalar_prefetch=1, grid=(1,),
            in_specs=[pl.BlockSpec((g, 1, t), lambda i, s: (0, 0, 0))],
            out_specs=pl.BlockSpec((g, 1, t), lambda i, s: (0, 0, 0))),
        compiler_params=_cparams(("arbitrary",)),
        name="moe_slots",
    )(class_start, key3)


def _for_rows(n_rows, fn):
    def body(g, c):
        for j in range(ROW_COPY_UNROLL):
            fn(g * ROW_COPY_UNROLL + j, j % 2)
        return c

    lax.fori_loop(0, n_rows // ROW_COPY_UNROLL, body, 0)


def _for_row_groups(n_rows, fn):
    def body(g, c):
        for j in range(SUBLANES):
            fn(g, j)
        return c

    lax.fori_loop(0, n_rows // SUBLANES, body, 0)


def _pack_rows(x, info):
    half = x.shape[1] // 2
    bits = lax.bitcast_convert_type(x.astype(BF16).astype(F32), jnp.uint32)
    words = (bits[:, :half] & jnp.uint32(HIGH_HALF_WORD)) | (bits[:, half:] >> 16)
    return jnp.concatenate([words, lax.bitcast_convert_type(info, jnp.uint32)], axis=1)


def _unpack_rows(packed):
    half = packed.shape[1] - LANES
    words = packed[:, :half]
    hi = lax.bitcast_convert_type(words & jnp.uint32(HIGH_HALF_WORD), F32)
    lo = lax.bitcast_convert_type(words << 16, F32)
    return (jnp.concatenate([hi, lo], axis=1).astype(BF16),
            lax.bitcast_convert_type(packed[:, half:], F32))


def _dispatch_kernel(tile_end_ref, n_used_ref, slot_ref, h_ref, info_ref, xs_ref,
                     rows, zeros, sems, zsem):
    i = pl.program_id(0)
    last = pl.num_programs(0) - 1
    p = i % 2
    t = DISPATCH_ROWS
    te = EXPERT_ROWS

    def row_copy(buf, g, j):
        return pltpu.make_async_copy(rows.at[buf, g, pl.ds(j, 1)],
                                     xs_ref.at[pl.ds(slot_ref[0, 0, g * SUBLANES + j], 1)],
                                     sems.at[buf])

    @pl.when(i == 0)
    def _():
        zeros[...] = jnp.zeros_like(zeros)

        def zero_tile(tile):
            return pltpu.make_async_copy(zeros, xs_ref.at[pl.ds(tile * te, te)], zsem)

        prev_end = 0
        for c in range(N_CLASSES):
            end = tile_end_ref[c]

            @pl.when(end > prev_end)
            def _(end=end):
                cp = zero_tile(end - 1)
                cp.start()
                cp.wait()
            prev_end = end

        def zero_tail(tile, c):
            cp = zero_tile(tile)
            cp.start()
            cp.wait()
            return c

        lax.fori_loop(n_used_ref[0], xs_ref.shape[0] // te, zero_tail, 0)

    rows[p] = _pack_rows(h_ref[...], info_ref[...]).reshape(rows.shape[1:])
    _for_row_groups(t, lambda g, j: row_copy(p, g, j).start(priority=j % 2))

    def row_wait(buf):
        pltpu.make_async_copy(rows.at[buf, 0, pl.ds(0, 1)], xs_ref.at[pl.ds(0, 1)],
                              sems.at[buf]).wait()

    @pl.when(i > 0)
    def _():
        _for_rows(t, lambda r, q: row_wait(1 - p))

    @pl.when(i == last)
    def _():
        _for_rows(t, lambda r, q: row_wait(p))


def _dispatch(tile_end, n_used, slot3, h2, info, n_slots):
    n, d = h2.shape
    t = DISPATCH_ROWS
    return pl.pallas_call(
        _dispatch_kernel,
        out_shape=jax.ShapeDtypeStruct((n_slots, d // 2 + LANES), jnp.uint32),
        grid_spec=pltpu.PrefetchScalarGridSpec(
            num_scalar_prefetch=2, grid=(n // t,),
            in_specs=[pl.BlockSpec((1, 1, t), lambda i, *_: (i, 0, 0), memory_space=pltpu.SMEM),
                      pl.BlockSpec((t, d), lambda i, *_: (i, 0)),
                      pl.BlockSpec((t, LANES), lambda i, *_: (i, 0))],
            out_specs=pl.BlockSpec(memory_space=pl.ANY),
            scratch_shapes=[pltpu.VMEM((2, t // SUBLANES, SUBLANES, d // 2 + LANES), jnp.uint32),
                            pltpu.VMEM((EXPERT_ROWS, d // 2 + LANES), jnp.uint32),
                            pltpu.SemaphoreType.DMA((2,)), pltpu.SemaphoreType.DMA(())]),
        compiler_params=_cparams(("arbitrary",)),
        name="moe_dispatch",
    )(tile_end, n_used, slot3, h2, info)


def _expert_kernel(n_used_ref, src_ref, lo_ref, hi_ref, xs_ref, wg_lo, wg_hi, wu_lo, wu_hi,
                   wd_lo, wd_hi, ys_ref):
    del src_ref, lo_ref, hi_ref

    @pl.when(pl.program_id(0) < n_used_ref[0])
    def _():
        xb, info = _unpack_rows(xs_ref[...])
        y = None
        for wg, wu, wd, lane in ((wg_lo, wu_lo, wd_lo, INFO_W_LOW), (wg_hi, wu_hi, wd_hi, INFO_W_HIGH)):
            gate = jnp.dot(xb, wg[0], preferred_element_type=F32)
            up = jnp.dot(xb, wu[0], preferred_element_type=F32)
            act = gate * _sigmoid(gate) * up * info[:, lane:lane + 1]
            part = jnp.dot(act.astype(BF16), wd[0], preferred_element_type=F32)
            y = part if y is None else y + part
        ys_ref[...] = y

    @pl.when(pl.program_id(0) >= n_used_ref[0])
    def _():
        ys_ref[...] = jnp.zeros_like(ys_ref)


def _experts(tile_meta, xs, layer, w_gate_b, w_up_b, w_down_b):
    n_slots, cols = xs.shape
    d, f = w_gate_b.shape[2:]
    t = EXPERT_ROWS
    x_map = lambda i, n_used, src, lo, hi: (src[i], 0)
    lo_map = lambda i, n_used, src, lo, hi: (layer, lo[i], 0, 0)
    hi_map = lambda i, n_used, src, lo, hi: (layer, hi[i], 0, 0)
    grid_spec = pltpu.PrefetchScalarGridSpec(
        num_scalar_prefetch=4,
        grid=(n_slots // t,),
        in_specs=[pl.BlockSpec((t, cols), x_map),
                  pl.BlockSpec((None, 1, d, f), lo_map), pl.BlockSpec((None, 1, d, f), hi_map),
                  pl.BlockSpec((None, 1, d, f), lo_map), pl.BlockSpec((None, 1, d, f), hi_map),
                  pl.BlockSpec((None, 1, f, d), lo_map), pl.BlockSpec((None, 1, f, d), hi_map)],
        out_specs=pl.BlockSpec((t, d), lambda i, *_: (i, 0)),
    )
    return pl.pallas_call(
        _expert_kernel,
        out_shape=jax.ShapeDtypeStruct((n_slots, d), F32),
        grid_spec=grid_spec,
        compiler_params=_cparams(("arbitrary",)),
        name="moe_experts",
    )(*tile_meta, xs, w_gate_b, w_gate_b, w_up_b, w_up_b, w_down_b, w_down_b)


def _combine_kernel(slot_ref, ys_ref, h_ref, ln_g_ref, ln_b_ref, o_ref, buf, sems):
    i = pl.program_id(0)
    tiles = pl.num_programs(0) - 1
    p = i % 2
    t = DISPATCH_ROWS

    @pl.when(i < tiles)
    def _():
        _for_row_groups(t, lambda g, j: pltpu.make_async_copy(
            ys_ref.at[pl.ds(slot_ref[0, 0, g * SUBLANES + j], 1)], buf.at[p, g, pl.ds(j, 1)],
            sems.at[p]).start(priority=j % 2))

    @pl.when(i == 0)
    def _():
        o_ref[...] = jnp.zeros_like(o_ref)

    @pl.when(i > 0)
    def _():
        _for_rows(t, lambda r, q: pltpu.make_async_copy(
            ys_ref.at[pl.ds(0, 1)], buf.at[1 - p, 0, pl.ds(0, 1)], sems.at[1 - p]).wait())
        o_ref[...] = _layer_norm(ALPHA * h_ref[...] + buf[1 - p].reshape(o_ref.shape),
                                 ln_g_ref[...], ln_b_ref[...])


def _combine(slot3, ys, h2, ln_g, ln_b):
    n, d = h2.shape
    t = DISPATCH_ROWS
    tiles = n // t
    done = lambda i: (jnp.maximum(i - 1, 0), 0)
    return pl.pallas_call(
        _combine_kernel,
        out_shape=jax.ShapeDtypeStruct((n, d), F32),
        grid=(tiles + 1,),
        in_specs=[pl.BlockSpec((1, 1, t), lambda i: (jnp.minimum(i, tiles - 1), 0, 0),
                               memory_space=pltpu.SMEM),
                  pl.BlockSpec(memory_space=pl.ANY),
                  pl.BlockSpec((t, d), done), _const_spec((1, d)), _const_spec((1, d))],
        out_specs=pl.BlockSpec((t, d), done),
        scratch_shapes=[pltpu.VMEM((2, t // SUBLANES, SUBLANES, d), F32),
                        pltpu.SemaphoreType.DMA((2,))],
        compiler_params=_cparams(("arbitrary",)),
        name="moe_combine",
    )(slot3, ys, h2, ln_g.reshape(1, d).astype(F32), ln_b.reshape(1, d).astype(F32))


def _pair_tables():
    lo, hi = [], []
    for g in range(N_GROUPS):
        for a in range(EXP_PER_GROUP):
            for b in range(a + 1, EXP_PER_GROUP):
                lo.append(g * EXP_PER_GROUP + a)
                hi.append(g * EXP_PER_GROUP + b)
    return jnp.asarray(lo, jnp.int32), jnp.asarray(hi, jnp.int32)


def _tile_plan(counts, n_tiles):
    t = EXPERT_ROWS
    counts = counts[:N_CLASSES, 0].astype(jnp.int32)
    tiles_per_class = (counts + t - 1) // t
    tile_end = jnp.cumsum(tiles_per_class)
    class_start = (tile_end - tiles_per_class) * t
    n_used = tile_end[-1:]
    src = jnp.minimum(jnp.arange(n_tiles, dtype=jnp.int32), n_used - 1)
    tile_class = jnp.sum((tile_end[None, :] <= src[:, None]).astype(jnp.int32), axis=1)
    pair_lo, pair_hi = _pair_tables()
    return class_start, tile_end, (n_used, src, pair_lo[tile_class], pair_hi[tile_class])


def kernel(x, ln_in_g, ln_in_b, w_in, gate_b, lru_conv_w, lru_conv_b, lru_wa, lru_ba, lru_wx, lru_bx, lru_lambda, sc_conv_w, w_branch_sb, w_branch_lru, w_branch_sc, w_out, ln1_g, ln1_b, w_group, group_bias, w_expert_router, expert_bias, w_gate, w_up, w_down, ln2_g, ln2_b):
    b, s, d = x.shape
    n = b * s
    params = dict(gate_b=gate_b, lru_conv_w=lru_conv_w, lru_conv_b=lru_conv_b, lru_wa=lru_wa,
                  lru_ba=lru_ba, lru_wx=lru_wx, lru_bx=lru_bx, lru_lambda=lru_lambda,
                  sc_conv_w=sc_conv_w, w_branch_sb=w_branch_sb, w_branch_lru=w_branch_lru,
                  w_branch_sc=w_branch_sc, w_out=w_out, ln1_g=ln1_g, ln1_b=ln1_b)
    w_in_b, w_gate_b, w_up_b, w_down_b = (w.astype(BF16) for w in (w_in, w_gate, w_up, w_down))
    h = x.reshape(n, d)
    for l in range(w_in.shape[0]):
        if l == 0:
            h, qkv, conv_in, act = _inproj(h, w_in_b, l, gate_b[l], entry_ln=(ln_in_g, ln_in_b))
        else:
            h, qkv, conv_in, act = _inproj(ys, w_in_b, l, gate_b[l], slots=slot3,
                                           finish=(h1, ln2_g[l - 1], ln2_b[l - 1]))
        y_sb = _sb_attention(qkv.reshape(b, s, -1))
        h1 = _mixer_tail(h.reshape(b, s, d), y_sb, conv_in.reshape(b, s, -1),
                         act.reshape(b, s, -1), params, l)
        h1 = h1.reshape(n, d)
        pad = LANES - N_EXPERTS - N_GROUPS
        w_router = jnp.concatenate(
            [w_expert_router[l], w_group[l], jnp.zeros((d, pad), F32)], axis=1).astype(F32)
        bias = jnp.concatenate(
            [expert_bias[l], group_bias[l], jnp.zeros((pad,), F32)]).reshape(LANES, 1).astype(F32)
        info, key3, counts = _route(h1, w_router, bias)
        n_tiles = n // EXPERT_ROWS + N_CLASSES
        class_start, tile_end, tile_meta = _tile_plan(counts, n_tiles)
        slot3 = _slots(class_start, key3)
        xs = _dispatch(tile_end, tile_meta[0], slot3, h1, info, n_tiles * EXPERT_ROWS)
        ys = _experts(tile_meta, xs, l, w_gate_b, w_up_b, w_down_b)
    return _combine(slot3, ys, h1, ln2_g[-1], ln2_b[-1]).reshape(b, s, d)
```

```python
import functools
import math

import jax
import jax.numpy as jnp
from jax import lax
from jax.experimental import pallas as pl
from jax.experimental.pallas import tpu as pltpu

F32 = jnp.float32
BF16 = jnp.bfloat16

SB_HEADS = 8
SB_HEAD_DIM = 64
SB_WIDTH = SB_HEADS * SB_HEAD_DIM
LRU_WIDTH = 512
LRU_BLOCKS = 8
LRU_C = 8.0
SC_WIDTH = 512
N_GROUPS = 4
EXP_PER_GROUP = 4
N_EXPERTS = N_GROUPS * EXP_PER_GROUP
D_EXPERT = 512
DEPTH = 2
ALPHA = (2 * DEPTH) ** 0.25
LN_EPS = 1e-5

LANES = 128
SUBLANES = 8
VMEM_LIMIT_BYTES = 56 * 1024 * 1024
F32_SIGN_BIT = 0x80000000

PROJ_ROWS = 512
PROJ_CHUNK = 512
ATT_Q = 256
ATT_K = 256
ATT_LANE_BLOCKS = 4
MIX_ROWS = 512
ROUTE_ROWS = 512
DISPATCH_ROWS = 512
EXPERT_ROWS = 256
ROW_COPY_UNROLL = 8

PAIRS_PER_GROUP = EXP_PER_GROUP * (EXP_PER_GROUP - 1) // 2
N_CLASSES = N_GROUPS * PAIRS_PER_GROUP
CLASS_ROWS = 32
INFO_W_LOW, INFO_W_HIGH = 0, 1
KEY_SHIFT = 16


def _cparams(sem):
    return pltpu.CompilerParams(dimension_semantics=sem, vmem_limit_bytes=VMEM_LIMIT_BYTES)


def _const_spec(shape):
    nd = len(shape)
    return pl.BlockSpec(shape, lambda *_: (0,) * nd)


def _layer_norm(y, g, b):
    mu = jnp.mean(y, axis=-1, keepdims=True)
    d = y - mu
    var = jnp.mean(d * d, axis=-1, keepdims=True)
    return d * lax.rsqrt(var + LN_EPS) * g + b


QKV_COLS = 3 * SB_WIDTH
BRANCH_COLS = 2 * LRU_WIDTH + 3 * SC_WIDTH


def _sigmoid(x):
    return 0.5 * jnp.tanh(0.5 * x) + 0.5


def _gelu_tanh(x):
    return 0.5 * x * (1.0 + jnp.tanh(math.sqrt(2.0 / math.pi) * (x + 0.044715 * (x * x * x))))


def _project(h, w_ref, gate_b_ref, qkv_ref, conv_ref, act_ref, after_chunk=lambda k: None):
    hb = h.astype(BF16)
    c = PROJ_CHUNK

    def proj(chunk):
        out = jnp.dot(hb, w_ref[:, chunk * c:(chunk + 1) * c], preferred_element_type=F32)
        after_chunk(chunk)
        return out

    n_qkv = QKV_COLS // c
    for j in range(n_qkv):
        acc = proj(j)
        if j * c < SB_WIDTH:
            acc = acc * (SB_HEAD_DIM ** -0.5 * LOG2_E)
        qkv_ref[:, j * c:(j + 1) * c] = acc.astype(BF16)
    lru_in, lru_gate, sc_b, sc_c, sc_h = range(n_qkv, n_qkv + 5)
    conv_ref[:, 0:c] = proj(lru_in)
    conv_ref[:, c:2 * c] = proj(sc_b)
    conv_ref[:, 2 * c:3 * c] = proj(sc_c) * proj(sc_h)
    for j in range(gate_b_ref.shape[1] // c):
        acc = proj(sc_h + 1 + j) + gate_b_ref[:, j * c:(j + 1) * c]
        act_ref[:, j * c:(j + 1) * c] = _sigmoid(acc).astype(BF16)
    act_ref[:, gate_b_ref.shape[1]:] = _gelu_tanh(proj(lru_gate)).astype(BF16)


def _inproj_kernel(h_ref, w_ref, gate_b_ref, qkv_ref, conv_ref, act_ref):
    _project(h_ref[...], w_ref, gate_b_ref, qkv_ref, conv_ref, act_ref)


def _ln_inproj_kernel(x_ref, g_ref, b_ref, w_ref, gate_b_ref, h_ref, qkv_ref, conv_ref, act_ref):
    h = _layer_norm(x_ref[...], g_ref[...], b_ref[...])
    h_ref[...] = h
    _project(h, w_ref, gate_b_ref, qkv_ref, conv_ref, act_ref)


def _gather_inproj_kernel(slot_ref, slot_next_ref, ys_ref, w_ref, gate_b_ref,
                          h_ref, qkv_ref, conv_ref, act_ref, rows, sems):
    i = pl.program_id(0)
    last = pl.num_programs(0) - 1
    p = i % 2

    def fetch(slots, buf):
        def group(g, c):
            for j in range(SUBLANES):
                pltpu.make_async_copy(ys_ref.at[pl.ds(slots[0, 0, g * SUBLANES + j], 1)],
                                      rows.at[buf, g, pl.ds(j, 1)],
                                      sems.at[buf]).start(priority=j % 2)
            return c

        lax.fori_loop(0, PROJ_ROWS // SUBLANES, group, 0)

    @pl.when(i == 0)
    def _():
        fetch(slot_ref, 0)

    _for_rows(PROJ_ROWS, lambda r, q: pltpu.make_async_copy(
        ys_ref.at[pl.ds(0, 1)], rows.at[p, 0, pl.ds(0, 1)], sems.at[p]).wait())
    h = rows[p].reshape(PROJ_ROWS, rows.shape[-1])
    h_ref[...] = h

    n_groups = PROJ_ROWS // SUBLANES
    n_chunks = w_ref.shape[1] // PROJ_CHUNK

    def start_some(k):
        for g in range(k * n_groups // n_chunks, (k + 1) * n_groups // n_chunks):
            for j in range(SUBLANES):
                pltpu.make_async_copy(
                    ys_ref.at[pl.ds(slot_next_ref[0, 0, g * SUBLANES + j], 1)],
                    rows.at[1 - p, g, pl.ds(j, 1)], sems.at[1 - p]).start(priority=j % 2)

    _project(h, w_ref, gate_b_ref, qkv_ref, conv_ref, act_ref, after_chunk=start_some)

    @pl.when(i == last)
    def _():
        _for_rows(PROJ_ROWS, lambda r, q: pltpu.make_async_copy(
            ys_ref.at[pl.ds(0, 1)], rows.at[1 - p, 0, pl.ds(0, 1)], sems.at[1 - p]).wait())


def _inproj(h2, w_in_b, layer, gate_b, entry_ln=None, slots=None):
    d = h2.shape[1]
    n = h2.shape[0] if slots is None else slots.size
    cols = w_in_b.shape[2]
    gate_cols = gate_b.size
    assert LRU_WIDTH == SC_WIDTH == PROJ_CHUNK and cols == QKV_COLS + BRANCH_COLS + gate_cols
    widths = (QKV_COLS, 3 * PROJ_CHUNK, gate_cols + LRU_WIDTH)
    dtypes = (BF16, F32, BF16)
    rows = pl.BlockSpec((PROJ_ROWS, d), lambda i: (i, 0))
    weight_specs = [pl.BlockSpec((None, d, cols), lambda i: (layer, 0, 0),
                                 pipeline_mode=pl.Buffered(1)),
                    _const_spec((1, gate_cols))]
    weights = (w_in_b, gate_b.reshape(1, gate_cols).astype(F32))
    out_shape = [jax.ShapeDtypeStruct((n, w), t) for w, t in zip(widths, dtypes)]
    out_specs = [pl.BlockSpec((PROJ_ROWS, w), lambda i: (i, 0)) for w in widths]
    if slots is not None:
        assert DISPATCH_ROWS == PROJ_ROWS
        steps = n // PROJ_ROWS
        slot_spec = lambda index: pl.BlockSpec((1, 1, PROJ_ROWS), index, memory_space=pltpu.SMEM)
        out_shape.insert(0, jax.ShapeDtypeStruct((n, d), F32))
        out_specs.insert(0, rows)
        return pl.pallas_call(
            _gather_inproj_kernel,
            out_shape=tuple(out_shape),
            grid=(steps,),
            in_specs=[slot_spec(lambda i: (i, 0, 0)),
                      slot_spec(lambda i: (jnp.minimum(i + 1, steps - 1), 0, 0)),
                      pl.BlockSpec(memory_space=pl.ANY)] + weight_specs,
            out_specs=tuple(out_specs),
            scratch_shapes=[pltpu.VMEM((2, PROJ_ROWS // SUBLANES, SUBLANES, d), F32),
                            pltpu.SemaphoreType.DMA((2,))],
            compiler_params=_cparams(("arbitrary",)),
            name="in_proj_gather",
        )(slots, slots, h2, *weights)
    if entry_ln is None:
        body, ln_specs, ln_args = _inproj_kernel, [], ()
    else:
        body = _ln_inproj_kernel
        ln_specs = [_const_spec((1, d)), _const_spec((1, d))]
        ln_args = tuple(v.reshape(1, d).astype(F32) for v in entry_ln)
        out_shape.insert(0, jax.ShapeDtypeStruct((n, d), F32))
        out_specs.insert(0, rows)
    return pl.pallas_call(
        body,
        out_shape=tuple(out_shape),
        grid=(n // PROJ_ROWS,),
        in_specs=[rows] + ln_specs + weight_specs,
        out_specs=tuple(out_specs),
        compiler_params=_cparams(("parallel",)),
        name="in_proj",
    )(h2, *ln_args, *weights)


LOG2_E = math.log2(math.e)
PASS_LOG2_FLOOR = -151.0


HEADS_PER_STEP = LANES // SB_HEAD_DIM


def _per_head(x):
    lane = lax.broadcasted_iota(jnp.int32, x.shape, 1)
    return jnp.concatenate(
        [jnp.where((lane >= h * SB_HEAD_DIM) & (lane < (h + 1) * SB_HEAD_DIM), x, jnp.zeros_like(x))
         for h in range(HEADS_PER_STEP)], axis=0)


MASKED_SCORE = -1e30


def _neg_abs(x):
    bits = lax.bitcast_convert_type(x, jnp.uint32) | jnp.uint32(F32_SIGN_BIT)
    return lax.bitcast_convert_type(bits, F32)


def _sb_block(q, k_j, v_j, neg_tri2, carries, mask):
    nq, kk = q.shape[0], k_j.shape[0]
    z = lax.dot_general(q, _per_head(k_j), (((1,), (1,)), ((), ())), preferred_element_type=F32)
    scores, split = [], []
    for h in range(HEADS_PER_STEP):
        z_h = z[:, h * kk:(h + 1) * kk]
        if mask is not None:
            z_h = jnp.where(mask, z_h, MASKED_SCORE)
        drop = jnp.maximum(z_h, 0.0) + jnp.log2(1.0 + jnp.exp2(_neg_abs(z_h)))
        scores.append(z_h)
        split.append(drop.astype(BF16))
    log_pass = jnp.dot(jnp.concatenate(split, axis=0), neg_tri2, preferred_element_type=F32)
    w, new_carries = [], []
    for h in range(HEADS_PER_STEP):
        lp_h = log_pass[h * nq:(h + 1) * nq]
        w.append(jnp.exp2(scores[h] + lp_h + carries[h]).astype(BF16))
        new_carries.append(carries[h] + lp_h[:, 0:1])
    pv = jnp.dot(jnp.concatenate(w, axis=1), _per_head(v_j), preferred_element_type=F32)
    return pv, new_carries


def _attn_kernel(q_ref, k_ref, v_ref, u2_ref, o_ref):
    qi = pl.program_id(2)
    u2 = u2_ref[...]
    row = lax.broadcasted_iota(jnp.int32, (ATT_Q, ATT_K), 0)
    col = lax.broadcasted_iota(jnp.int32, (ATT_Q, ATT_K), 1)
    diag_mask = col < row
    blocks = [slice(g * LANES, (g + 1) * LANES) for g in range(ATT_LANE_BLOCKS)]
    q = [q_ref[0, :, lanes] for lanes in blocks]

    def sweep(j, accs, carries, mask):
        rows = pl.ds(pl.multiple_of(j * ATT_K, ATT_K), ATT_K)
        out = [_sb_block(q_g, k_ref[0, rows, lanes], v_ref[0, rows, lanes], u2, c_g, mask)
               for q_g, lanes, c_g in zip(q, blocks, carries)]
        return [acc + pv for acc, (pv, _) in zip(accs, out)], [c for _, c in out]

    zero = [[jnp.zeros((ATT_Q, 1), F32) for _ in range(HEADS_PER_STEP)] for _ in blocks]
    accs, carries = sweep(qi, [jnp.zeros((ATT_Q, LANES), F32) for _ in blocks], zero, diag_mask)
    no_prev = jnp.where(qi > 0, 0.0, MASKED_SCORE)
    accs, carries = sweep(jnp.maximum(qi - 1, 0), accs,
                          [[c + no_prev for c in c_g] for c_g in carries], None)

    def largest(carries):
        return jnp.max(functools.reduce(jnp.maximum, [c for c_g in carries for c in c_g]))

    def cond(state):
        j, _, _, live = state
        return jnp.logical_and(j >= 0, live > PASS_LOG2_FLOOR)

    def body(state):
        j, accs, carries, _ = state
        accs, carries = sweep(j, accs, carries, None)
        return j - 1, accs, carries, largest(carries)

    _, accs, _, _ = lax.while_loop(cond, body, (qi - 2, accs, carries, largest(carries)))
    o_ref[0] = jnp.concatenate(accs, axis=1).astype(o_ref.dtype)


def _sb_attention(qkv3):
    b, s, _ = qkv3.shape
    width = ATT_LANE_BLOCKS * LANES
    groups = SB_WIDTH // width
    neg_tri = -(jnp.arange(ATT_K)[:, None] >= jnp.arange(ATT_K)[None, :]).astype(BF16)
    u2 = neg_tri
    return pl.pallas_call(
        _attn_kernel,
        out_shape=jax.ShapeDtypeStruct((b, s, SB_WIDTH), BF16),
        grid=(b, groups, s // ATT_Q),
        in_specs=[pl.BlockSpec((1, ATT_Q, width), lambda bi, p, qi: (bi, qi, p)),
                  pl.BlockSpec((1, s, width), lambda bi, p, qi: (bi, 0, groups + p)),
                  pl.BlockSpec((1, s, width), lambda bi, p, qi: (bi, 0, 2 * groups + p)),
                  _const_spec((ATT_K, ATT_K))],
        out_specs=pl.BlockSpec((1, ATT_Q, width), lambda bi, p, qi: (bi, qi, p)),
        compiler_params=_cparams(("parallel", "parallel", "arbitrary")),
        name="sb_attention",
    )(qkv3, qkv3, qkv3, u2)


def _causal_conv(x, stage, w):
    t = x.shape[0]
    k = w.shape[0]
    stage[SUBLANES:, :] = x
    out = w[k - 1:k] * x
    for j in range(1, k):
        out = out + w[k - 1 - j:k - j] * stage[pl.ds(SUBLANES - j, t), :]
    stage[:SUBLANES, :] = x[t - SUBLANES:]
    return out


def _linear_scan(a, b, h0):
    t, c = a.shape
    a = a.reshape(t // SUBLANES, SUBLANES, c)
    b = b.reshape(t // SUBLANES, SUBLANES, c)
    sub = lax.broadcasted_iota(jnp.int32, a.shape, 1)
    d = 1
    while d < SUBLANES:
        keep = sub >= d
        a_prev = jnp.where(keep, pltpu.roll(a, d, axis=1), 1.0)
        b_prev = jnp.where(keep, pltpu.roll(b, d, axis=1), 0.0)
        b = a * b_prev + b
        a = a * a_prev
        d *= 2
    h = h0
    groups = []
    for g in range(t // SUBLANES):
        seg = b[g] + a[g] * h
        h = seg[SUBLANES - 1:]
        groups.append(seg)
    return jnp.concatenate(groups, axis=0)


def _mix_kernel(h_ref, ysb_ref, lin_ref, scb_ref, scp_ref, gsb_ref, glru_ref, gsc_ref, gelu_ref,
                lconv_w_ref, lconv_b_ref, wa_ref, ba_ref, wx_ref, bx_ref, lam_ref, scw_ref,
                wsb_ref, wlru_ref, wsc_ref, wout_ref, ln_g_ref, ln_b_ref,
                out_ref, lin_stage, sc_stage, h_state):
    t = MIX_ROWS

    @pl.when(pl.program_id(1) == 0)
    def _():
        lin_stage[:SUBLANES, :] = jnp.zeros((SUBLANES, LRU_WIDTH), F32)
        sc_stage[:SUBLANES, :] = jnp.zeros((SUBLANES, SC_WIDTH), F32)
        h_state[...] = jnp.zeros_like(h_state)

    u = _causal_conv(lin_ref[0], lin_stage, lconv_w_ref[...]) + lconv_b_ref[...]
    ub = u.astype(BF16)
    r = _sigmoid(jnp.dot(ub, wa_ref[...], preferred_element_type=F32) + ba_ref[...])
    i = _sigmoid(jnp.dot(ub, wx_ref[...], preferred_element_type=F32) + bx_ref[...])
    lam = lam_ref[...]
    softplus_neg_lam = jnp.maximum(-lam, 0.0) + jnp.log(1.0 + jnp.exp(-jnp.abs(lam)))
    log_a = (-LRU_C) * r * softplus_neg_lam
    a = jnp.exp(log_a)
    gap = 1.0 - a * a
    drive = jnp.where(gap > 0.0, gap * lax.rsqrt(gap), 0.0) * (i * u)
    hs = _linear_scan(a, drive, h_state[...])
    h_state[...] = hs[t - 1:]
    y_lru = gelu_ref[0] * hs.astype(BF16)

    y_sc = (scb_ref[0] * _causal_conv(scp_ref[0], sc_stage, scw_ref[...])).astype(BF16)

    def branch(y, w_ref):
        return jnp.dot(y, w_ref[...], preferred_element_type=F32).astype(BF16)

    merged = (gsb_ref[0] * branch(ysb_ref[0], wsb_ref) + glru_ref[0] * branch(y_lru, wlru_ref)
              + gsc_ref[0] * branch(y_sc, wsc_ref))
    mix = jnp.dot(merged, wout_ref[...], preferred_element_type=F32)
    out_ref[0] = _layer_norm(ALPHA * h_ref[0] + mix, ln_g_ref[...], ln_b_ref[...])


def _block_diag(w):
    hh, ii, jj = w.shape
    eye = jnp.eye(hh, dtype=w.dtype)
    return (eye[:, None, :, None] * w[:, :, None, :]).reshape(hh * ii, hh * jj)


def _mixer_tail(h3, ysb3, conv3, act3, p, l):
    b, s, d = h3.shape
    t = MIX_ROWS
    assert LRU_WIDTH == SC_WIDTH

    def row_spec(width, col):
        return pl.BlockSpec((1, t, width), lambda bi, si: (bi, si, col))

    vec = lambda v: v.reshape(1, -1).astype(F32)
    in_specs = [row_spec(d, 0), row_spec(SB_WIDTH, 0)]
    in_specs += [row_spec(LRU_WIDTH, c) for c in range(3)]
    in_specs += [row_spec(d, g) for g in range(3)]
    in_specs += [row_spec(LRU_WIDTH, 3 * d // LRU_WIDTH)]
    weights = [
        p["lru_conv_w"][l].astype(F32), vec(p["lru_conv_b"][l]),
        _block_diag(p["lru_wa"][l]).astype(BF16), vec(p["lru_ba"][l]),
        _block_diag(p["lru_wx"][l]).astype(BF16), vec(p["lru_bx"][l]),
        vec(p["lru_lambda"][l]), p["sc_conv_w"][l].astype(F32),
        p["w_branch_sb"][l].astype(BF16), p["w_branch_lru"][l].astype(BF16),
        p["w_branch_sc"][l].astype(BF16), p["w_out"][l].astype(BF16),
        vec(p["ln1_g"][l]), vec(p["ln1_b"][l]),
    ]
    in_specs += [_const_spec(w.shape) for w in weights]
    return pl.pallas_call(
        _mix_kernel,
        out_shape=jax.ShapeDtypeStruct((b, s, d), F32),
        grid=(b, s // t),
        in_specs=in_specs,
        out_specs=pl.BlockSpec((1, t, d), lambda bi, si: (bi, si, 0)),
        scratch_shapes=[pltpu.VMEM((SUBLANES + t, LRU_WIDTH), F32),
                        pltpu.VMEM((SUBLANES + t, SC_WIDTH), F32),
                        pltpu.VMEM((1, LRU_WIDTH), F32)],
        compiler_params=_cparams(("parallel", "arbitrary")),
        name="mixer_tail",
    )(h3, ysb3, *([conv3] * 3), *([act3] * 4), *weights)


def _first_argmax(vals, row):
    m = jnp.max(vals, axis=0, keepdims=True)
    idx = jnp.min(jnp.where(vals == m, row, vals.shape[0]), axis=0, keepdims=True)
    return m, idx


def _route_kernel(h_ref, w_ref, bias_ref, before_ref, info_ref, key_ref, counts_ref, running):
    @pl.when(pl.program_id(0) == 0)
    def _():
        running[...] = jnp.zeros_like(running)

    h = h_ref[...]
    h_hi = h.astype(BF16)
    h_lo = (h - h_hi.astype(F32)).astype(BF16)
    w = w_ref[...]
    w_hi = w.astype(BF16)
    w_lo = (w - w_hi.astype(F32)).astype(BF16)
    logits = jnp.dot(jnp.concatenate([h_hi, h_lo, h_hi], axis=1),
                     jnp.concatenate([w_hi, w_hi, w_lo], axis=0), preferred_element_type=F32)
    lt = logits.T
    neg = -jnp.inf
    experts = lt[:N_EXPERTS]
    groups = lt[N_EXPERTS:N_EXPERTS + SUBLANES]
    experts_b = experts + bias_ref[:N_EXPERTS]
    groups_b = groups + bias_ref[N_EXPERTS:N_EXPERTS + SUBLANES]
    row_g = lax.broadcasted_iota(jnp.int32, groups.shape, 0)
    row_e = lax.broadcasted_iota(jnp.int32, experts.shape, 0)
    is_group = row_g < N_GROUPS
    g_max = jnp.max(jnp.where(is_group, groups, neg), axis=0, keepdims=True)
    g_exp = jnp.where(is_group, jnp.exp(groups - g_max), 0.0)
    g_den = jnp.sum(g_exp, axis=0, keepdims=True)
    _, g_sel = _first_argmax(jnp.where(is_group, groups_b, neg), row_g)
    g_prob = jnp.sum(jnp.where(row_g == g_sel, g_exp, 0.0), axis=0, keepdims=True) / g_den
    cand = jnp.where(row_e // EXP_PER_GROUP == g_sel, experts_b, neg)
    _, e1 = _first_argmax(cand, row_e)
    _, e2 = _first_argmax(jnp.where(row_e == e1, neg, cand), row_e)
    l1 = jnp.sum(jnp.where(row_e == e1, experts, 0.0), axis=0, keepdims=True)
    l2 = jnp.sum(jnp.where(row_e == e2, experts, 0.0), axis=0, keepdims=True)
    m = jnp.maximum(l1, l2)
    x1 = jnp.exp(l1 - m)
    x2 = jnp.exp(l2 - m)
    scale = g_prob / (x1 + x2)
    w1 = x1 * scale
    w2 = x2 * scale
    first_low = e1 < e2
    a = jnp.where(first_low, e1, e2) - g_sel * EXP_PER_GROUP
    b = jnp.where(first_low, e2, e1) - g_sel * EXP_PER_GROUP
    cls = g_sel * PAIRS_PER_GROUP + ((a * (2 * EXP_PER_GROUP - 1 - a)) >> 1) + (b - a - 1)
    row_c = lax.broadcasted_iota(jnp.int32, (CLASS_ROWS, cls.shape[1]), 0)
    onehot = row_c == cls
    ones = jnp.where(onehot, 1.0, 0.0)
    earlier = jnp.dot(ones, before_ref[...], preferred_element_type=F32) + running[...]
    rank = jnp.sum(jnp.where(onehot, earlier, 0.0), axis=0, keepdims=True)
    running[...] += jnp.sum(ones, axis=1, keepdims=True)
    counts_ref[...] = running[...]
    key_ref[0] = (cls << KEY_SHIFT) + rank.astype(jnp.int32)
    row_w = lax.broadcasted_iota(jnp.int32, (LANES, cls.shape[1]), 0)
    w_rows = jnp.where(row_w == INFO_W_LOW, jnp.where(first_low, w1, w2), 0.0)
    w_rows = jnp.where(row_w == INFO_W_HIGH, jnp.where(first_low, w2, w1), w_rows)
    info_ref[...] = w_rows.T


def _route(h2, w_router, bias):
    n, d = h2.shape
    t = ROUTE_ROWS
    assert n <= 1 << KEY_SHIFT
    before = (jnp.arange(t)[:, None] < jnp.arange(t)[None, :]).astype(F32)
    return pl.pallas_call(
        _route_kernel,
        out_shape=(jax.ShapeDtypeStruct((n, LANES), F32),
                   jax.ShapeDtypeStruct((n // t, 1, t), jnp.int32),
                   jax.ShapeDtypeStruct((CLASS_ROWS, 1), F32)),
        grid=(n // t,),
        in_specs=[pl.BlockSpec((t, d), lambda i: (i, 0)),
                  _const_spec((d, LANES)), _const_spec((LANES, 1)), _const_spec((t, t))],
        out_specs=(pl.BlockSpec((t, LANES), lambda i: (i, 0)),
                   pl.BlockSpec((1, 1, t), lambda i: (i, 0, 0)),
                   _const_spec((CLASS_ROWS, 1))),
        scratch_shapes=[pltpu.VMEM((CLASS_ROWS, 1), F32)],
        compiler_params=_cparams(("arbitrary",)),
        name="moe_route",
    )(h2, w_router, bias, before)


def _slot_kernel(start_ref, key_ref, slot_ref):
    key = key_ref[...]
    cls = key >> KEY_SHIFT
    slot = key & ((1 << KEY_SHIFT) - 1)
    for c in range(N_CLASSES):
        slot = slot + jnp.where(cls == c, start_ref[c], 0)
    slot_ref[...] = slot


def _slots(class_start, key3):
    g, _, t = key3.shape
    return pl.pallas_call(
        _slot_kernel,
        out_shape=jax.ShapeDtypeStruct(key3.shape, jnp.int32),
        grid_spec=pltpu.PrefetchScalarGridSpec(
            num_scalar_prefetch=1, grid=(1,),
            in_specs=[pl.BlockSpec((g, 1, t), lambda i, s: (0, 0, 0))],
            out_specs=pl.BlockSpec((g, 1, t), lambda i, s: (0, 0, 0))),
        compiler_params=_cparams(("arbitrary",)),
        name="moe_slots",
    )(class_start, key3)


def _for_rows(n_rows, fn):
    def body(g, c):
        for j in range(ROW_COPY_UNROLL):
            fn(g * ROW_COPY_UNROLL + j, j % 2)
        return c

    lax.fori_loop(0, n_rows // ROW_COPY_UNROLL, body, 0)


def _for_row_groups(n_rows, fn):
    def body(g, c):
        for j in range(SUBLANES):
            fn(g, j)
        return c

    lax.fori_loop(0, n_rows // SUBLANES, body, 0)


def _dispatch_kernel(tile_end_ref, n_used_ref, slot_ref, h_ref, info_ref, xs_ref,
                     rows, zeros, sems, zsem):
    i = pl.program_id(0)
    last = pl.num_programs(0) - 1
    p = i % 2
    t = DISPATCH_ROWS
    te = EXPERT_ROWS

    def row_copy(buf, g, j):
        return pltpu.make_async_copy(rows.at[buf, g, pl.ds(j, 1)],
                                     xs_ref.at[pl.ds(slot_ref[0, 0, g * SUBLANES + j], 1)],
                                     sems.at[buf])

    @pl.when(i == 0)
    def _():
        zeros[...] = jnp.zeros_like(zeros)

        def zero_tile(tile):
            return pltpu.make_async_copy(zeros, xs_ref.at[pl.ds(tile * te, te)], zsem)

        prev_end = 0
        for c in range(N_CLASSES):
            end = tile_end_ref[c]

            @pl.when(end > prev_end)
            def _(end=end):
                cp = zero_tile(end - 1)
                cp.start()
                cp.wait()
            prev_end = end

        def zero_tail(tile, c):
            cp = zero_tile(tile)
            cp.start()
            cp.wait()
            return c

        lax.fori_loop(n_used_ref[0], xs_ref.shape[0] // te, zero_tail, 0)

    rows[p] = jnp.concatenate([h_ref[...], info_ref[...]], axis=1).reshape(rows.shape[1:])
    _for_row_groups(t, lambda g, j: row_copy(p, g, j).start(priority=j % 2))

    def row_wait(buf):
        pltpu.make_async_copy(rows.at[buf, 0, pl.ds(0, 1)], xs_ref.at[pl.ds(0, 1)],
                              sems.at[buf]).wait()

    @pl.when(i > 0)
    def _():
        _for_rows(t, lambda r, q: row_wait(1 - p))

    @pl.when(i == last)
    def _():
        _for_rows(t, lambda r, q: row_wait(p))


def _dispatch(tile_end, n_used, slot3, h2, info, n_slots):
    n, d = h2.shape
    t = DISPATCH_ROWS
    return pl.pallas_call(
        _dispatch_kernel,
        out_shape=jax.ShapeDtypeStruct((n_slots, d + LANES), F32),
        grid_spec=pltpu.PrefetchScalarGridSpec(
            num_scalar_prefetch=2, grid=(n // t,),
            in_specs=[pl.BlockSpec((1, 1, t), lambda i, *_: (i, 0, 0), memory_space=pltpu.SMEM),
                      pl.BlockSpec((t, d), lambda i, *_: (i, 0)),
                      pl.BlockSpec((t, LANES), lambda i, *_: (i, 0))],
            out_specs=pl.BlockSpec(memory_space=pl.ANY),
            scratch_shapes=[pltpu.VMEM((2, t // SUBLANES, SUBLANES, d + LANES), F32),
                            pltpu.VMEM((EXPERT_ROWS, d + LANES), F32),
                            pltpu.SemaphoreType.DMA((2,)), pltpu.SemaphoreType.DMA(())]),
        compiler_params=_cparams(("arbitrary",)),
        name="moe_dispatch",
    )(tile_end, n_used, slot3, h2, info)


def _expert_kernel(n_used_ref, src_ref, lo_ref, hi_ref, xs_ref, wg_lo, wg_hi, wu_lo, wu_hi,
                   wd_lo, wd_hi, ln_g_ref, ln_b_ref, ys_ref):
    del src_ref, lo_ref, hi_ref

    @pl.when(pl.program_id(0) < n_used_ref[0])
    def _():
        d = ys_ref.shape[1]
        x = xs_ref[:, :d]
        info = xs_ref[:, d:]
        xb = x.astype(BF16)
        y = None
        for wg, wu, wd, lane in ((wg_lo, wu_lo, wd_lo, INFO_W_LOW), (wg_hi, wu_hi, wd_hi, INFO_W_HIGH)):
            gate = jnp.dot(xb, wg[0], preferred_element_type=F32)
            up = jnp.dot(xb, wu[0], preferred_element_type=F32)
            act = gate * _sigmoid(gate) * up * info[:, lane:lane + 1]
            part = jnp.dot(act.astype(BF16), wd[0], preferred_element_type=F32)
            y = part if y is None else y + part
        ys_ref[...] = _layer_norm(ALPHA * x + y, ln_g_ref[...], ln_b_ref[...])

    @pl.when(pl.program_id(0) >= n_used_ref[0])
    def _():
        ys_ref[...] = jnp.zeros_like(ys_ref)


def _experts(tile_meta, xs, layer, w_gate_b, w_up_b, w_down_b, ln_g, ln_b):
    n_slots, cols = xs.shape
    d, f = w_gate_b.shape[2:]
    t = EXPERT_ROWS
    x_map = lambda i, n_used, src, lo, hi: (src[i], 0)
    lo_map = lambda i, n_used, src, lo, hi: (layer, lo[i], 0, 0)
    hi_map = lambda i, n_used, src, lo, hi: (layer, hi[i], 0, 0)
    grid_spec = pltpu.PrefetchScalarGridSpec(
        num_scalar_prefetch=4,
        grid=(n_slots // t,),
        in_specs=[pl.BlockSpec((t, cols), x_map),
                  pl.BlockSpec((None, 1, d, f), lo_map), pl.BlockSpec((None, 1, d, f), hi_map),
                  pl.BlockSpec((None, 1, d, f), lo_map), pl.BlockSpec((None, 1, d, f), hi_map),
                  pl.BlockSpec((None, 1, f, d), lo_map), pl.BlockSpec((None, 1, f, d), hi_map),
                  pl.BlockSpec((1, d), lambda i, *_: (0, 0)), pl.BlockSpec((1, d), lambda i, *_: (0, 0))],
        out_specs=pl.BlockSpec((t, d), lambda i, *_: (i, 0)),
    )
    return pl.pallas_call(
        _expert_kernel,
        out_shape=jax.ShapeDtypeStruct((n_slots, d), F32),
        grid_spec=grid_spec,
        compiler_params=_cparams(("arbitrary",)),
        name="moe_experts",
    )(*tile_meta, xs, w_gate_b, w_gate_b, w_up_b, w_up_b, w_down_b, w_down_b,
      ln_g.reshape(1, d).astype(F32), ln_b.reshape(1, d).astype(F32))


def _combine_kernel(slot_ref, ys_ref, o_ref, buf, sems):
    i = pl.program_id(0)
    tiles = pl.num_programs(0) - 1
    p = i % 2
    t = DISPATCH_ROWS

    @pl.when(i < tiles)
    def _():
        _for_row_groups(t, lambda g, j: pltpu.make_async_copy(
            ys_ref.at[pl.ds(slot_ref[0, 0, g * SUBLANES + j], 1)], buf.at[p, g, pl.ds(j, 1)],
            sems.at[p]).start(priority=j % 2))

    @pl.when(i == 0)
    def _():
        o_ref[...] = jnp.zeros_like(o_ref)

    @pl.when(i > 0)
    def _():
        _for_rows(t, lambda r, q: pltpu.make_async_copy(
            ys_ref.at[pl.ds(0, 1)], buf.at[1 - p, 0, pl.ds(0, 1)], sems.at[1 - p]).wait())
        o_ref[...] = buf[1 - p].reshape(o_ref.shape)


def _combine(slot3, ys, n):
    d = ys.shape[1]
    t = DISPATCH_ROWS
    tiles = n // t
    done = lambda i: (jnp.maximum(i - 1, 0), 0)
    return pl.pallas_call(
        _combine_kernel,
        out_shape=jax.ShapeDtypeStruct((n, d), F32),
        grid=(tiles + 1,),
        in_specs=[pl.BlockSpec((1, 1, t), lambda i: (jnp.minimum(i, tiles - 1), 0, 0),
                               memory_space=pltpu.SMEM),
                  pl.BlockSpec(memory_space=pl.ANY)],
        out_specs=pl.BlockSpec((t, d), done),
        scratch_shapes=[pltpu.VMEM((2, t // SUBLANES, SUBLANES, d), F32),
                        pltpu.SemaphoreType.DMA((2,))],
        compiler_params=_cparams(("arbitrary",)),
        name="moe_combine",
    )(slot3, ys)


def _pair_tables():
    lo, hi = [], []
    for g in range(N_GROUPS):
        for a in range(EXP_PER_GROUP):
            for b in range(a + 1, EXP_PER_GROUP):
                lo.append(g * EXP_PER_GROUP + a)
                hi.append(g * EXP_PER_GROUP + b)
    return jnp.asarray(lo, jnp.int32), jnp.asarray(hi, jnp.int32)


def _tile_plan(counts, n_tiles):
    t = EXPERT_ROWS
    counts = counts[:N_CLASSES, 0].astype(jnp.int32)
    tiles_per_class = (counts + t - 1) // t
    tile_end = jnp.cumsum(tiles_per_class)
    class_start = (tile_end - tiles_per_class) * t
    n_used = tile_end[-1:]
    src = jnp.minimum(jnp.arange(n_tiles, dtype=jnp.int32), n_used - 1)
    tile_class = jnp.sum((tile_end[None, :] <= src[:, None]).astype(jnp.int32), axis=1)
    pair_lo, pair_hi = _pair_tables()
    return class_start, tile_end, (n_used, src, pair_lo[tile_class], pair_hi[tile_class])


def kernel(x, ln_in_g, ln_in_b, w_in, gate_b, lru_conv_w, lru_conv_b, lru_wa, lru_ba, lru_wx, lru_bx, lru_lambda, sc_conv_w, w_branch_sb, w_branch_lru, w_branch_sc, w_out, ln1_g, ln1_b, w_group, group_bias, w_expert_router, expert_bias, w_gate, w_up, w_down, ln2_g, ln2_b):
    b, s, d = x.shape
    n = b * s
    params = dict(gate_b=gate_b, lru_conv_w=lru_conv_w, lru_conv_b=lru_conv_b, lru_wa=lru_wa,
                  lru_ba=lru_ba, lru_wx=lru_wx, lru_bx=lru_bx, lru_lambda=lru_lambda,
                  sc_conv_w=sc_conv_w, w_branch_sb=w_branch_sb, w_branch_lru=w_branch_lru,
                  w_branch_sc=w_branch_sc, w_out=w_out, ln1_g=ln1_g, ln1_b=ln1_b)
    w_in_b, w_gate_b, w_up_b, w_down_b = (w.astype(BF16) for w in (w_in, w_gate, w_up, w_down))
    h = x.reshape(n, d)
    for l in range(w_in.shape[0]):
        if l == 0:
            h, qkv, conv_in, act = _inproj(h, w_in_b, l, gate_b[l], entry_ln=(ln_in_g, ln_in_b))
        else:
            h, qkv, conv_in, act = _inproj(ys, w_in_b, l, gate_b[l], slots=slot3)
        y_sb = _sb_attention(qkv.reshape(b, s, -1))
        h1 = _mixer_tail(h.reshape(b, s, d), y_sb, conv_in.reshape(b, s, -1),
                         act.reshape(b, s, -1), params, l)
        h1 = h1.reshape(n, d)
        pad = LANES - N_EXPERTS - N_GROUPS
        w_router = jnp.concatenate(
            [w_expert_router[l], w_group[l], jnp.zeros((d, pad), F32)], axis=1).astype(F32)
        bias = jnp.concatenate(
            [expert_bias[l], group_bias[l], jnp.zeros((pad,), F32)]).reshape(LANES, 1).astype(F32)
        info, key3, counts = _route(h1, w_router, bias)
        n_tiles = n // EXPERT_ROWS + N_CLASSES
        class_start, tile_end, tile_meta = _tile_plan(counts, n_tiles)
        slot3 = _slots(class_start, key3)
        xs = _dispatch(tile_end, tile_meta[0], slot3, h1, info, n_tiles * EXPERT_ROWS)
        ys = _experts(tile_meta, xs, l, w_gate_b, w_up_b, w_down_b, ln2_g[l], ln2_b[l])
    return _combine(slot3, ys, n).reshape(b, s, d)
```

```python
import functools
import math

import jax
import jax.numpy as jnp
from jax import lax
from jax.experimental import pallas as pl
from jax.experimental.pallas import tpu as pltpu

F32 = jnp.float32
BF16 = jnp.bfloat16

SB_HEADS = 8
SB_HEAD_DIM = 64
SB_WIDTH = SB_HEADS * SB_HEAD_DIM
LRU_WIDTH = 512
LRU_BLOCKS = 8
LRU_C = 8.0
SC_WIDTH = 512
N_GROUPS = 4
EXP_PER_GROUP = 4
N_EXPERTS = N_GROUPS * EXP_PER_GROUP
D_EXPERT = 512
DEPTH = 2
ALPHA = (2 * DEPTH) ** 0.25
LN_EPS = 1e-5

LANES = 128
SUBLANES = 8
VMEM_LIMIT_BYTES = 56 * 1024 * 1024

PROJ_ROWS = 512
PROJ_CHUNK = 512
ATT_Q = 256
ATT_K = 256
ATT_LANE_BLOCKS = 4
MIX_ROWS = 512
ROUTE_ROWS = 512
DISPATCH_ROWS = 512
EXPERT_ROWS = 256
ROW_COPY_UNROLL = 8

PAIRS_PER_GROUP = EXP_PER_GROUP * (EXP_PER_GROUP - 1) // 2
N_CLASSES = N_GROUPS * PAIRS_PER_GROUP
CLASS_ROWS = 32
INFO_W_LOW, INFO_W_HIGH = 0, 1
KEY_SHIFT = 16


def _cparams(sem):
    return pltpu.CompilerParams(dimension_semantics=sem, vmem_limit_bytes=VMEM_LIMIT_BYTES)


def _const_spec(shape):
    nd = len(shape)
    return pl.BlockSpec(shape, lambda *_: (0,) * nd)


def _layer_norm(y, g, b):
    mu = jnp.mean(y, axis=-1, keepdims=True)
    d = y - mu
    var = jnp.mean(d * d, axis=-1, keepdims=True)
    return d * lax.rsqrt(var + LN_EPS) * g + b


QKV_COLS = 3 * SB_WIDTH
BRANCH_COLS = 2 * LRU_WIDTH + 3 * SC_WIDTH


def _sigmoid(x):
    return 0.5 * jnp.tanh(0.5 * x) + 0.5


def _gelu_tanh(x):
    return 0.5 * x * (1.0 + jnp.tanh(math.sqrt(2.0 / math.pi) * (x + 0.044715 * (x * x * x))))


def _project(h, w_ref, gate_b_ref, qkv_ref, conv_ref, act_ref, after_chunk=lambda k: None):
    hb = h.astype(BF16)
    c = PROJ_CHUNK

    def proj(chunk):
        out = jnp.dot(hb, w_ref[:, chunk * c:(chunk + 1) * c], preferred_element_type=F32)
        after_chunk(chunk)
        return out

    n_qkv = QKV_COLS // c
    for j in range(n_qkv):
        acc = proj(j)
        if j * c < SB_WIDTH:
            acc = acc * (SB_HEAD_DIM ** -0.5 * LOG2_E)
        qkv_ref[:, j * c:(j + 1) * c] = acc.astype(BF16)
    lru_in, lru_gate, sc_b, sc_c, sc_h = range(n_qkv, n_qkv + 5)
    conv_ref[:, 0:c] = proj(lru_in)
    conv_ref[:, c:2 * c] = proj(sc_b)
    conv_ref[:, 2 * c:3 * c] = proj(sc_c) * proj(sc_h)
    for j in range(gate_b_ref.shape[1] // c):
        acc = proj(sc_h + 1 + j) + gate_b_ref[:, j * c:(j + 1) * c]
        act_ref[:, j * c:(j + 1) * c] = _sigmoid(acc).astype(BF16)
    act_ref[:, gate_b_ref.shape[1]:] = _gelu_tanh(proj(lru_gate)).astype(BF16)


def _inproj_kernel(h_ref, w_ref, gate_b_ref, qkv_ref, conv_ref, act_ref):
    _project(h_ref[...], w_ref, gate_b_ref, qkv_ref, conv_ref, act_ref)


def _ln_inproj_kernel(x_ref, g_ref, b_ref, w_ref, gate_b_ref, h_ref, qkv_ref, conv_ref, act_ref):
    h = _layer_norm(x_ref[...], g_ref[...], b_ref[...])
    h_ref[...] = h
    _project(h, w_ref, gate_b_ref, qkv_ref, conv_ref, act_ref)


def _gather_inproj_kernel(slot_ref, slot_next_ref, ys_ref, w_ref, gate_b_ref,
                          h_ref, qkv_ref, conv_ref, act_ref, rows, sems):
    i = pl.program_id(0)
    last = pl.num_programs(0) - 1
    p = i % 2

    def fetch(slots, buf):
        def group(g, c):
            for j in range(SUBLANES):
                pltpu.make_async_copy(ys_ref.at[pl.ds(slots[0, 0, g * SUBLANES + j], 1)],
                                      rows.at[buf, g, pl.ds(j, 1)],
                                      sems.at[buf]).start(priority=j % 2)
            return c

        lax.fori_loop(0, PROJ_ROWS // SUBLANES, group, 0)

    @pl.when(i == 0)
    def _():
        fetch(slot_ref, 0)

    _for_rows(PROJ_ROWS, lambda r, q: pltpu.make_async_copy(
        ys_ref.at[pl.ds(0, 1)], rows.at[p, 0, pl.ds(0, 1)], sems.at[p]).wait())
    h = rows[p].reshape(PROJ_ROWS, rows.shape[-1])
    h_ref[...] = h

    n_groups = PROJ_ROWS // SUBLANES
    n_chunks = w_ref.shape[1] // PROJ_CHUNK

    def start_some(k):
        for g in range(k * n_groups // n_chunks, (k + 1) * n_groups // n_chunks):
            for j in range(SUBLANES):
                pltpu.make_async_copy(
                    ys_ref.at[pl.ds(slot_next_ref[0, 0, g * SUBLANES + j], 1)],
                    rows.at[1 - p, g, pl.ds(j, 1)], sems.at[1 - p]).start(priority=j % 2)

    _project(h, w_ref, gate_b_ref, qkv_ref, conv_ref, act_ref, after_chunk=start_some)

    @pl.when(i == last)
    def _():
        _for_rows(PROJ_ROWS, lambda r, q: pltpu.make_async_copy(
            ys_ref.at[pl.ds(0, 1)], rows.at[1 - p, 0, pl.ds(0, 1)], sems.at[1 - p]).wait())


def _inproj(h2, w_in_b, layer, gate_b, entry_ln=None, slots=None):
    d = h2.shape[1]
    n = h2.shape[0] if slots is None else slots.size
    cols = w_in_b.shape[2]
    gate_cols = gate_b.size
    assert LRU_WIDTH == SC_WIDTH == PROJ_CHUNK and cols == QKV_COLS + BRANCH_COLS + gate_cols
    widths = (QKV_COLS, 3 * PROJ_CHUNK, gate_cols + LRU_WIDTH)
    dtypes = (BF16, F32, BF16)
    rows = pl.BlockSpec((PROJ_ROWS, d), lambda i: (i, 0))
    weight_specs = [pl.BlockSpec((None, d, cols), lambda i: (layer, 0, 0),
                                 pipeline_mode=pl.Buffered(1)),
                    _const_spec((1, gate_cols))]
    weights = (w_in_b, gate_b.reshape(1, gate_cols).astype(F32))
    out_shape = [jax.ShapeDtypeStruct((n, w), t) for w, t in zip(widths, dtypes)]
    out_specs = [pl.BlockSpec((PROJ_ROWS, w), lambda i: (i, 0)) for w in widths]
    if slots is not None:
        assert DISPATCH_ROWS == PROJ_ROWS
        steps = n // PROJ_ROWS
        slot_spec = lambda index: pl.BlockSpec((1, 1, PROJ_ROWS), index, memory_space=pltpu.SMEM)
        out_shape.insert(0, jax.ShapeDtypeStruct((n, d), F32))
        out_specs.insert(0, rows)
        return pl.pallas_call(
            _gather_inproj_kernel,
            out_shape=tuple(out_shape),
            grid=(steps,),
            in_specs=[slot_spec(lambda i: (i, 0, 0)),
                      slot_spec(lambda i: (jnp.minimum(i + 1, steps - 1), 0, 0)),
                      pl.BlockSpec(memory_space=pl.ANY)] + weight_specs,
            out_specs=tuple(out_specs),
            scratch_shapes=[pltpu.VMEM((2, PROJ_ROWS // SUBLANES, SUBLANES, d), F32),
                            pltpu.SemaphoreType.DMA((2,))],
            compiler_params=_cparams(("arbitrary",)),
            name="in_proj_gather",
        )(slots, slots, h2, *weights)
    if entry_ln is None:
        body, ln_specs, ln_args = _inproj_kernel, [], ()
    else:
        body = _ln_inproj_kernel
        ln_specs = [_const_spec((1, d)), _const_spec((1, d))]
        ln_args = tuple(v.reshape(1, d).astype(F32) for v in entry_ln)
        out_shape.insert(0, jax.ShapeDtypeStruct((n, d), F32))
        out_specs.insert(0, rows)
    return pl.pallas_call(
        body,
        out_shape=tuple(out_shape),
        grid=(n // PROJ_ROWS,),
        in_specs=[rows] + ln_specs + weight_specs,
        out_specs=tuple(out_specs),
        compiler_params=_cparams(("parallel",)),
        name="in_proj",
    )(h2, *ln_args, *weights)


LOG2_E = math.log2(math.e)
PASS_LOG2_FLOOR = -151.0


HEADS_PER_STEP = LANES // SB_HEAD_DIM


def _per_head(x):
    lane = lax.broadcasted_iota(jnp.int32, x.shape, 1)
    return jnp.concatenate(
        [jnp.where((lane >= h * SB_HEAD_DIM) & (lane < (h + 1) * SB_HEAD_DIM), x, jnp.zeros_like(x))
         for h in range(HEADS_PER_STEP)], axis=0)


MASKED_SCORE = -1e30


def _neg_abs(x):
    return -jnp.abs(x)


def _sb_block(q, k_j, v_j, neg_tri2, carries, mask):
    nq, kk = q.shape[0], k_j.shape[0]
    z = lax.dot_general(q, _per_head(k_j), (((1,), (1,)), ((), ())), preferred_element_type=F32)
    scores, split = [], []
    for h in range(HEADS_PER_STEP):
        z_h = z[:, h * kk:(h + 1) * kk]
        if mask is not None:
            z_h = jnp.where(mask, z_h, MASKED_SCORE)
        drop = jnp.maximum(z_h, 0.0) + jnp.log2(1.0 + jnp.exp2(_neg_abs(z_h)))
        scores.append(z_h)
        split.append(drop.astype(BF16))
    log_pass = jnp.dot(jnp.concatenate(split, axis=0), neg_tri2, preferred_element_type=F32)
    w, new_carries = [], []
    for h in range(HEADS_PER_STEP):
        lp_h = log_pass[h * nq:(h + 1) * nq]
        w.append(jnp.exp2(scores[h] + lp_h + carries[h]).astype(BF16))
        new_carries.append(carries[h] + lp_h[:, 0:1])
    pv = jnp.dot(jnp.concatenate(w, axis=1), _per_head(v_j), preferred_element_type=F32)
    return pv, new_carries


def _attn_kernel(q_ref, k_ref, v_ref, u2_ref, o_ref):
    qi = pl.program_id(2)
    u2 = u2_ref[...]
    row = lax.broadcasted_iota(jnp.int32, (ATT_Q, ATT_K), 0)
    col = lax.broadcasted_iota(jnp.int32, (ATT_Q, ATT_K), 1)
    diag_mask = col < row
    blocks = [slice(g * LANES, (g + 1) * LANES) for g in range(ATT_LANE_BLOCKS)]
    q = [q_ref[0, :, lanes] for lanes in blocks]

    def sweep(j, accs, carries, mask):
        rows = pl.ds(pl.multiple_of(j * ATT_K, ATT_K), ATT_K)
        out = [_sb_block(q_g, k_ref[0, rows, lanes], v_ref[0, rows, lanes], u2, c_g, mask)
               for q_g, lanes, c_g in zip(q, blocks, carries)]
        return [acc + pv for acc, (pv, _) in zip(accs, out)], [c for _, c in out]

    zero = [[jnp.zeros((ATT_Q, 1), F32) for _ in range(HEADS_PER_STEP)] for _ in blocks]
    accs, carries = sweep(qi, [jnp.zeros((ATT_Q, LANES), F32) for _ in blocks], zero, diag_mask)
    no_prev = jnp.where(qi > 0, 0.0, MASKED_SCORE)
    accs, carries = sweep(jnp.maximum(qi - 1, 0), accs,
                          [[c + no_prev for c in c_g] for c_g in carries], None)

    def largest(carries):
        return jnp.max(functools.reduce(jnp.maximum, [c for c_g in carries for c in c_g]))

    def cond(state):
        j, _, _, live = state
        return jnp.logical_and(j >= 0, live > PASS_LOG2_FLOOR)

    def body(state):
        j, accs, carries, _ = state
        accs, carries = sweep(j, accs, carries, None)
        return j - 1, accs, carries, largest(carries)

    _, accs, _, _ = lax.while_loop(cond, body, (qi - 2, accs, carries, largest(carries)))
    o_ref[0] = jnp.concatenate(accs, axis=1).astype(o_ref.dtype)


def _sb_attention(qkv3):
    b, s, _ = qkv3.shape
    width = ATT_LANE_BLOCKS * LANES
    groups = SB_WIDTH // width
    neg_tri = -(jnp.arange(ATT_K)[:, None] >= jnp.arange(ATT_K)[None, :]).astype(BF16)
    u2 = neg_tri
    return pl.pallas_call(
        _attn_kernel,
        out_shape=jax.ShapeDtypeStruct((b, s, SB_WIDTH), BF16),
        grid=(b, groups, s // ATT_Q),
        in_specs=[pl.BlockSpec((1, ATT_Q, width), lambda bi, p, qi: (bi, qi, p)),
                  pl.BlockSpec((1, s, width), lambda bi, p, qi: (bi, 0, groups + p)),
                  pl.BlockSpec((1, s, width), lambda bi, p, qi: (bi, 0, 2 * groups + p)),
                  _const_spec((ATT_K, ATT_K))],
        out_specs=pl.BlockSpec((1, ATT_Q, width), lambda bi, p, qi: (bi, qi, p)),
        compiler_params=_cparams(("parallel", "parallel", "arbitrary")),
        name="sb_attention",
    )(qkv3, qkv3, qkv3, u2)


def _causal_conv(x, stage, w):
    t = x.shape[0]
    k = w.shape[0]
    stage[SUBLANES:, :] = x
    out = w[k - 1:k] * x
    for j in range(1, k):
        out = out + w[k - 1 - j:k - j] * stage[pl.ds(SUBLANES - j, t), :]
    stage[:SUBLANES, :] = x[t - SUBLANES:]
    return out


def _linear_scan(a, b, h0):
    t, c = a.shape
    a = a.reshape(t // SUBLANES, SUBLANES, c)
    b = b.reshape(t // SUBLANES, SUBLANES, c)
    sub = lax.broadcasted_iota(jnp.int32, a.shape, 1)
    d = 1
    while d < SUBLANES:
        keep = sub >= d
        a_prev = jnp.where(keep, pltpu.roll(a, d, axis=1), 1.0)
        b_prev = jnp.where(keep, pltpu.roll(b, d, axis=1), 0.0)
        b = a * b_prev + b
        a = a * a_prev
        d *= 2
    h = h0
    groups = []
    for g in range(t // SUBLANES):
        seg = b[g] + a[g] * h
        h = seg[SUBLANES - 1:]
        groups.append(seg)
    return jnp.concatenate(groups, axis=0)


def _mix_kernel(h_ref, ysb_ref, lin_ref, scb_ref, scp_ref, gsb_ref, glru_ref, gsc_ref, gelu_ref,
                lconv_w_ref, lconv_b_ref, wa_ref, ba_ref, wx_ref, bx_ref, lam_ref, scw_ref,
                wsb_ref, wlru_ref, wsc_ref, wout_ref, ln_g_ref, ln_b_ref,
                out_ref, lin_stage, sc_stage, h_state):
    t = MIX_ROWS

    @pl.when(pl.program_id(1) == 0)
    def _():
        lin_stage[:SUBLANES, :] = jnp.zeros((SUBLANES, LRU_WIDTH), F32)
        sc_stage[:SUBLANES, :] = jnp.zeros((SUBLANES, SC_WIDTH), F32)
        h_state[...] = jnp.zeros_like(h_state)

    u = _causal_conv(lin_ref[0], lin_stage, lconv_w_ref[...]) + lconv_b_ref[...]
    ub = u.astype(BF16)
    r = _sigmoid(jnp.dot(ub, wa_ref[...], preferred_element_type=F32) + ba_ref[...])
    i = _sigmoid(jnp.dot(ub, wx_ref[...], preferred_element_type=F32) + bx_ref[...])
    lam = lam_ref[...]
    softplus_neg_lam = jnp.maximum(-lam, 0.0) + jnp.log(1.0 + jnp.exp(-jnp.abs(lam)))
    log_a = (-LRU_C) * r * softplus_neg_lam
    a = jnp.exp(log_a)
    gap = 1.0 - a * a
    drive = jnp.where(gap > 0.0, gap * lax.rsqrt(gap), 0.0) * (i * u)
    hs = _linear_scan(a, drive, h_state[...])
    h_state[...] = hs[t - 1:]
    y_lru = gelu_ref[0] * hs.astype(BF16)

    y_sc = (scb_ref[0] * _causal_conv(scp_ref[0], sc_stage, scw_ref[...])).astype(BF16)

    def branch(y, w_ref):
        return jnp.dot(y, w_ref[...], preferred_element_type=F32).astype(BF16)

    merged = (gsb_ref[0] * branch(ysb_ref[0], wsb_ref) + glru_ref[0] * branch(y_lru, wlru_ref)
              + gsc_ref[0] * branch(y_sc, wsc_ref))
    mix = jnp.dot(merged, wout_ref[...], preferred_element_type=F32)
    out_ref[0] = _layer_norm(ALPHA * h_ref[0] + mix, ln_g_ref[...], ln_b_ref[...])


def _block_diag(w):
    hh, ii, jj = w.shape
    eye = jnp.eye(hh, dtype=w.dtype)
    return (eye[:, None, :, None] * w[:, :, None, :]).reshape(hh * ii, hh * jj)


def _mixer_tail(h3, ysb3, conv3, act3, p, l):
    b, s, d = h3.shape
    t = MIX_ROWS
    assert LRU_WIDTH == SC_WIDTH

    def row_spec(width, col):
        return pl.BlockSpec((1, t, width), lambda bi, si: (bi, si, col))

    vec = lambda v: v.reshape(1, -1).astype(F32)
    in_specs = [row_spec(d, 0), row_spec(SB_WIDTH, 0)]
    in_specs += [row_spec(LRU_WIDTH, c) for c in range(3)]
    in_specs += [row_spec(d, g) for g in range(3)]
    in_specs += [row_spec(LRU_WIDTH, 3 * d // LRU_WIDTH)]
    weights = [
        p["lru_conv_w"][l].astype(F32), vec(p["lru_conv_b"][l]),
        _block_diag(p["lru_wa"][l]).astype(BF16), vec(p["lru_ba"][l]),
        _block_diag(p["lru_wx"][l]).astype(BF16), vec(p["lru_bx"][l]),
        vec(p["lru_lambda"][l]), p["sc_conv_w"][l].astype(F32),
        p["w_branch_sb"][l].astype(BF16), p["w_branch_lru"][l].astype(BF16),
        p["w_branch_sc"][l].astype(BF16), p["w_out"][l].astype(BF16),
        vec(p["ln1_g"][l]), vec(p["ln1_b"][l]),
    ]
    in_specs += [_const_spec(w.shape) for w in weights]
    return pl.pallas_call(
        _mix_kernel,
        out_shape=jax.ShapeDtypeStruct((b, s, d), F32),
        grid=(b, s // t),
        in_specs=in_specs,
        out_specs=pl.BlockSpec((1, t, d), lambda bi, si: (bi, si, 0)),
        scratch_shapes=[pltpu.VMEM((SUBLANES + t, LRU_WIDTH), F32),
                        pltpu.VMEM((SUBLANES + t, SC_WIDTH), F32),
                        pltpu.VMEM((1, LRU_WIDTH), F32)],
        compiler_params=_cparams(("parallel", "arbitrary")),
        name="mixer_tail",
    )(h3, ysb3, *([conv3] * 3), *([act3] * 4), *weights)


def _first_argmax(vals, row):
    m = jnp.max(vals, axis=0, keepdims=True)
    idx = jnp.min(jnp.where(vals == m, row, vals.shape[0]), axis=0, keepdims=True)
    return m, idx


def _route_kernel(h_ref, w_ref, bias_ref, before_ref, info_ref, key_ref, counts_ref, running):
    @pl.when(pl.program_id(0) == 0)
    def _():
        running[...] = jnp.zeros_like(running)

    h = h_ref[...]
    h_hi = h.astype(BF16)
    h_lo = (h - h_hi.astype(F32)).astype(BF16)
    w = w_ref[...]
    w_hi = w.astype(BF16)
    w_lo = (w - w_hi.astype(F32)).astype(BF16)
    logits = jnp.dot(jnp.concatenate([h_hi, h_lo, h_hi], axis=1),
                     jnp.concatenate([w_hi, w_hi, w_lo], axis=0), preferred_element_type=F32)
    lt = logits.T
    neg = -jnp.inf
    experts = lt[:N_EXPERTS]
    groups = lt[N_EXPERTS:N_EXPERTS + SUBLANES]
    experts_b = experts + bias_ref[:N_EXPERTS]
    groups_b = groups + bias_ref[N_EXPERTS:N_EXPERTS + SUBLANES]
    row_g = lax.broadcasted_iota(jnp.int32, groups.shape, 0)
    row_e = lax.broadcasted_iota(jnp.int32, experts.shape, 0)
    is_group = row_g < N_GROUPS
    g_max = jnp.max(jnp.where(is_group, groups, neg), axis=0, keepdims=True)
    g_exp = jnp.where(is_group, jnp.exp(groups - g_max), 0.0)
    g_den = jnp.sum(g_exp, axis=0, keepdims=True)
    _, g_sel = _first_argmax(jnp.where(is_group, groups_b, neg), row_g)
    g_prob = jnp.sum(jnp.where(row_g == g_sel, g_exp, 0.0), axis=0, keepdims=True) / g_den
    cand = jnp.where(row_e // EXP_PER_GROUP == g_sel, experts_b, neg)
    _, e1 = _first_argmax(cand, row_e)
    _, e2 = _first_argmax(jnp.where(row_e == e1, neg, cand), row_e)
    l1 = jnp.sum(jnp.where(row_e == e1, experts, 0.0), axis=0, keepdims=True)
    l2 = jnp.sum(jnp.where(row_e == e2, experts, 0.0), axis=0, keepdims=True)
    m = jnp.maximum(l1, l2)
    x1 = jnp.exp(l1 - m)
    x2 = jnp.exp(l2 - m)
    scale = g_prob / (x1 + x2)
    w1 = x1 * scale
    w2 = x2 * scale
    first_low = e1 < e2
    a = jnp.where(first_low, e1, e2) - g_sel * EXP_PER_GROUP
    b = jnp.where(first_low, e2, e1) - g_sel * EXP_PER_GROUP
    cls = g_sel * PAIRS_PER_GROUP + ((a * (2 * EXP_PER_GROUP - 1 - a)) >> 1) + (b - a - 1)
    row_c = lax.broadcasted_iota(jnp.int32, (CLASS_ROWS, cls.shape[1]), 0)
    onehot = row_c == cls
    ones = jnp.where(onehot, 1.0, 0.0)
    earlier = jnp.dot(ones, before_ref[...], preferred_element_type=F32) + running[...]
    rank = jnp.sum(jnp.where(onehot, earlier, 0.0), axis=0, keepdims=True)
    running[...] += jnp.sum(ones, axis=1, keepdims=True)
    counts_ref[...] = running[...]
    key_ref[0] = (cls << KEY_SHIFT) + rank.astype(jnp.int32)
    row_w = lax.broadcasted_iota(jnp.int32, (LANES, cls.shape[1]), 0)
    w_rows = jnp.where(row_w == INFO_W_LOW, jnp.where(first_low, w1, w2), 0.0)
    w_rows = jnp.where(row_w == INFO_W_HIGH, jnp.where(first_low, w2, w1), w_rows)
    info_ref[...] = w_rows.T


def _route(h2, w_router, bias):
    n, d = h2.shape
    t = ROUTE_ROWS
    assert n <= 1 << KEY_SHIFT
    before = (jnp.arange(t)[:, None] < jnp.arange(t)[None, :]).astype(F32)
    return pl.pallas_call(
        _route_kernel,
        out_shape=(jax.ShapeDtypeStruct((n, LANES), F32),
                   jax.ShapeDtypeStruct((n // t, 1, t), jnp.int32),
                   jax.ShapeDtypeStruct((CLASS_ROWS, 1), F32)),
        grid=(n // t,),
        in_specs=[pl.BlockSpec((t, d), lambda i: (i, 0)),
                  _const_spec((d, LANES)), _const_spec((LANES, 1)), _const_spec((t, t))],
        out_specs=(pl.BlockSpec((t, LANES), lambda i: (i, 0)),
                   pl.BlockSpec((1, 1, t), lambda i: (i, 0, 0)),
                   _const_spec((CLASS_ROWS, 1))),
        scratch_shapes=[pltpu.VMEM((CLASS_ROWS, 1), F32)],
        compiler_params=_cparams(("arbitrary",)),
        name="moe_route",
    )(h2, w_router, bias, before)


def _slot_kernel(start_ref, key_ref, slot_ref):
    key = key_ref[...]
    cls = key >> KEY_SHIFT
    slot = key & ((1 << KEY_SHIFT) - 1)
    for c in range(N_CLASSES):
        slot = slot + jnp.where(cls == c, start_ref[c], 0)
    slot_ref[...] = slot


def _slots(class_start, key3):
    g, _, t = key3.shape
    return pl.pallas_call(
        _slot_kernel,
        out_shape=jax.ShapeDtypeStruct(key3.shape, jnp.int32),
        grid_spec=pltpu.PrefetchScalarGridSpec(
            num_scalar_prefetch=1, grid=(1,),
            in_specs=[pl.BlockSpec((g, 1, t), lambda i, s: (0, 0, 0))],
            out_specs=pl.BlockSpec((g, 1, t), lambda i, s: (0, 0, 0))),
        compiler_params=_cparams(("arbitrary",)),
        name="moe_slots",
    )(class_start, key3)


def _for_rows(n_rows, fn):
    def body(g, c):
        for j in range(ROW_COPY_UNROLL):
            fn(g * ROW_COPY_UNROLL + j, j % 2)
        return c

    lax.fori_loop(0, n_rows // ROW_COPY_UNROLL, body, 0)


def _for_row_groups(n_rows, fn):
    def body(g, c):
        for j in range(SUBLANES):
            fn(g, j)
        return c

    lax.fori_loop(0, n_rows // SUBLANES, body, 0)


def _dispatch_kernel(tile_end_ref, n_used_ref, slot_ref, h_ref, info_ref, xs_ref,
                     rows, zeros, sems, zsem):
    i = pl.program_id(0)
    last = pl.num_programs(0) - 1
    p = i % 2
    t = DISPATCH_ROWS
    te = EXPERT_ROWS

    def row_copy(buf, g, j):
        return pltpu.make_async_copy(rows.at[buf, g, pl.ds(j, 1)],
                                     xs_ref.at[pl.ds(slot_ref[0, 0, g * SUBLANES + j], 1)],
                                     sems.at[buf])

    @pl.when(i == 0)
    def _():
        zeros[...] = jnp.zeros_like(zeros)

        def zero_tile(tile):
            return pltpu.make_async_copy(zeros, xs_ref.at[pl.ds(tile * te, te)], zsem)

        prev_end = 0
        for c in range(N_CLASSES):
            end = tile_end_ref[c]

            @pl.when(end > prev_end)
            def _(end=end):
                cp = zero_tile(end - 1)
                cp.start()
                cp.wait()
            prev_end = end

        def zero_tail(tile, c):
            cp = zero_tile(tile)
            cp.start()
            cp.wait()
            return c

        lax.fori_loop(n_used_ref[0], xs_ref.shape[0] // te, zero_tail, 0)

    rows[p] = jnp.concatenate([h_ref[...], info_ref[...]], axis=1).reshape(rows.shape[1:])
    _for_row_groups(t, lambda g, j: row_copy(p, g, j).start(priority=j % 2))

    def row_wait(buf):
        pltpu.make_async_copy(rows.at[buf, 0, pl.ds(0, 1)], xs_ref.at[pl.ds(0, 1)],
                              sems.at[buf]).wait()

    @pl.when(i > 0)
    def _():
        _for_rows(t, lambda r, q: row_wait(1 - p))

    @pl.when(i == last)
    def _():
        _for_rows(t, lambda r, q: row_wait(p))


def _dispatch(tile_end, n_used, slot3, h2, info, n_slots):
    n, d = h2.shape
    t = DISPATCH_ROWS
    return pl.pallas_call(
        _dispatch_kernel,
        out_shape=jax.ShapeDtypeStruct((n_slots, d + LANES), F32),
        grid_spec=pltpu.PrefetchScalarGridSpec(
            num_scalar_prefetch=2, grid=(n // t,),
            in_specs=[pl.BlockSpec((1, 1, t), lambda i, *_: (i, 0, 0), memory_space=pltpu.SMEM),
                      pl.BlockSpec((t, d), lambda i, *_: (i, 0)),
                      pl.BlockSpec((t, LANES), lambda i, *_: (i, 0))],
            out_specs=pl.BlockSpec(memory_space=pl.ANY),
            scratch_shapes=[pltpu.VMEM((2, t // SUBLANES, SUBLANES, d + LANES), F32),
                            pltpu.VMEM((EXPERT_ROWS, d + LANES), F32),
                            pltpu.SemaphoreType.DMA((2,)), pltpu.SemaphoreType.DMA(())]),
        compiler_params=_cparams(("arbitrary",)),
        name="moe_dispatch",
    )(tile_end, n_used, slot3, h2, info)


def _expert_kernel(n_used_ref, src_ref, lo_ref, hi_ref, xs_ref, wg_lo, wg_hi, wu_lo, wu_hi,
                   wd_lo, wd_hi, ln_g_ref, ln_b_ref, ys_ref):
    del src_ref, lo_ref, hi_ref

    @pl.when(pl.program_id(0) < n_used_ref[0])
    def _():
        d = ys_ref.shape[1]
        x = xs_ref[:, :d]
        info = xs_ref[:, d:]
        xb = x.astype(BF16)
        y = None
        for wg, wu, wd, lane in ((wg_lo, wu_lo, wd_lo, INFO_W_LOW), (wg_hi, wu_hi, wd_hi, INFO_W_HIGH)):
            gate = jnp.dot(xb, wg[0], preferred_element_type=F32)
            up = jnp.dot(xb, wu[0], preferred_element_type=F32)
            act = gate * _sigmoid(gate) * up * info[:, lane:lane + 1]
            part = jnp.dot(act.astype(BF16), wd[0], preferred_element_type=F32)
            y = part if y is None else y + part
        ys_ref[...] = _layer_norm(ALPHA * x + y, ln_g_ref[...], ln_b_ref[...])

    @pl.when(pl.program_id(0) >= n_used_ref[0])
    def _():
        ys_ref[...] = jnp.zeros_like(ys_ref)


def _experts(tile_meta, xs, layer, w_gate_b, w_up_b, w_down_b, ln_g, ln_b):
    n_slots, cols = xs.shape
    d, f = w_gate_b.shape[2:]
    t = EXPERT_ROWS
    x_map = lambda i, n_used, src, lo, hi: (src[i], 0)
    lo_map = lambda i, n_used, src, lo, hi: (layer, lo[i], 0, 0)
    hi_map = lambda i, n_used, src, lo, hi: (layer, hi[i], 0, 0)
    grid_spec = pltpu.PrefetchScalarGridSpec(
        num_scalar_prefetch=4,
        grid=(n_slots // t,),
        in_specs=[pl.BlockSpec((t, cols), x_map),
                  pl.BlockSpec((None, 1, d, f), lo_map), pl.BlockSpec((None, 1, d, f), hi_map),
                  pl.BlockSpec((None, 1, d, f), lo_map), pl.BlockSpec((None, 1, d, f), hi_map),
                  pl.BlockSpec((None, 1, f, d), lo_map), pl.BlockSpec((None, 1, f, d), hi_map),
                  pl.BlockSpec((1, d), lambda i, *_: (0, 0)), pl.BlockSpec((1, d), lambda i, *_: (0, 0))],
        out_specs=pl.BlockSpec((t, d), lambda i, *_: (i, 0)),
    )
    return pl.pallas_call(
        _expert_kernel,
        out_shape=jax.ShapeDtypeStruct((n_slots, d), F32),
        grid_spec=grid_spec,
        compiler_params=_cparams(("arbitrary",)),
        name="moe_experts",
    )(*tile_meta, xs, w_gate_b, w_gate_b, w_up_b, w_up_b, w_down_b, w_down_b,
      ln_g.reshape(1, d).astype(F32), ln_b.reshape(1, d).astype(F32))


def _combine_kernel(slot_ref, ys_ref, o_ref, buf, sems):
    i = pl.program_id(0)
    tiles = pl.num_programs(0) - 1
    p = i % 2
    t = DISPATCH_ROWS

    @pl.when(i < tiles)
    def _():
        _for_row_groups(t, lambda g, j: pltpu.make_async_copy(
            ys_ref.at[pl.ds(slot_ref[0, 0, g * SUBLANES + j], 1)], buf.at[p, g, pl.ds(j, 1)],
            sems.at[p]).start(priority=j % 2))

    @pl.when(i == 0)
    def _():
        o_ref[...] = jnp.zeros_like(o_ref)

    @pl.when(i > 0)
    def _():
        _for_rows(t, lambda r, q: pltpu.make_async_copy(
            ys_ref.at[pl.ds(0, 1)], buf.at[1 - p, 0, pl.ds(0, 1)], sems.at[1 - p]).wait())
        o_ref[...] = buf[1 - p].reshape(o_ref.shape)


def _combine(slot3, ys, n):
    d = ys.shape[1]
    t = DISPATCH_ROWS
    tiles = n // t
    done = lambda i: (jnp.maximum(i - 1, 0), 0)
    return pl.pallas_call(
        _combine_kernel,
        out_shape=jax.ShapeDtypeStruct((n, d), F32),
        grid=(tiles + 1,),
        in_specs=[pl.BlockSpec((1, 1, t), lambda i: (jnp.minimum(i, tiles - 1), 0, 0),
                               memory_space=pltpu.SMEM),
                  pl.BlockSpec(memory_space=pl.ANY)],
        out_specs=pl.BlockSpec((t, d), done),
        scratch_shapes=[pltpu.VMEM((2, t // SUBLANES, SUBLANES, d), F32),
                        pltpu.SemaphoreType.DMA((2,))],
        compiler_params=_cparams(("arbitrary",)),
        name="moe_combine",
    )(slot3, ys)


def _pair_tables():
    lo, hi = [], []
    for g in range(N_GROUPS):
        for a in range(EXP_PER_GROUP):
            for b in range(a + 1, EXP_PER_GROUP):
                lo.append(g * EXP_PER_GROUP + a)
                hi.append(g * EXP_PER_GROUP + b)
    return jnp.asarray(lo, jnp.int32), jnp.asarray(hi, jnp.int32)


def _tile_plan(counts, n_tiles):
    t = EXPERT_ROWS
    counts = counts[:N_CLASSES, 0].astype(jnp.int32)
    tiles_per_class = (counts + t - 1) // t
    tile_end = jnp.cumsum(tiles_per_class)
    class_start = (tile_end - tiles_per_class) * t
    n_used = tile_end[-1:]
    src = jnp.minimum(jnp.arange(n_tiles, dtype=jnp.int32), n_used - 1)
    tile_class = jnp.sum((tile_end[None, :] <= src[:, None]).astype(jnp.int32), axis=1)
    pair_lo, pair_hi = _pair_tables()
    return class_start, tile_end, (n_used, src, pair_lo[tile_class], pair_hi[tile_class])


def kernel(x, ln_in_g, ln_in_b, w_in, gate_b, lru_conv_w, lru_conv_b, lru_wa, lru_ba, lru_wx, lru_bx, lru_lambda, sc_conv_w, w_branch_sb, w_branch_lru, w_branch_sc, w_out, ln1_g, ln1_b, w_group, group_bias, w_expert_router, expert_bias, w_gate, w_up, w_down, ln2_g, ln2_b):
    b, s, d = x.shape
    n = b * s
    params = dict(gate_b=gate_b, lru_conv_w=lru_conv_w, lru_conv_b=lru_conv_b, lru_wa=lru_wa,
                  lru_ba=lru_ba, lru_wx=lru_wx, lru_bx=lru_bx, lru_lambda=lru_lambda,
                  sc_conv_w=sc_conv_w, w_branch_sb=w_branch_sb, w_branch_lru=w_branch_lru,
                  w_branch_sc=w_branch_sc, w_out=w_out, ln1_g=ln1_g, ln1_b=ln1_b)
    w_in_b, w_gate_b, w_up_b, w_down_b = (w.astype(BF16) for w in (w_in, w_gate, w_up, w_down))
    h = x.reshape(n, d)
    for l in range(w_in.shape[0]):
        if l == 0:
            h, qkv, conv_in, act = _inproj(h, w_in_b, l, gate_b[l], entry_ln=(ln_in_g, ln_in_b))
        else:
            h, qkv, conv_in, act = _inproj(ys, w_in_b, l, gate_b[l], slots=slot3)
        y_sb = _sb_attention(qkv.reshape(b, s, -1))
        h1 = _mixer_tail(h.reshape(b, s, d), y_sb, conv_in.reshape(b, s, -1),
                         act.reshape(b, s, -1), params, l)
        h1 = h1.reshape(n, d)
        pad = LANES - N_EXPERTS - N_GROUPS
        w_router = jnp.concatenate(
            [w_expert_router[l], w_group[l], jnp.zeros((d, pad), F32)], axis=1).astype(F32)
        bias = jnp.concatenate(
            [expert_bias[l], group_bias[l], jnp.zeros((pad,), F32)]).reshape(LANES, 1).astype(F32)
        info, key3, counts = _route(h1, w_router, bias)
        n_tiles = n // EXPERT_ROWS + N_CLASSES
        class_start, tile_end, tile_meta = _tile_plan(counts, n_tiles)
        slot3 = _slots(class_start, key3)
        xs = _dispatch(tile_end, tile_meta[0], slot3, h1, info, n_tiles * EXPERT_ROWS)
        ys = _experts(tile_meta, xs, l, w_gate_b, w_up_b, w_down_b, ln2_g[l], ln2_b[l])
    return _combine(slot3, ys, n).reshape(b, s, d)
```
